```python
import math
import jax, jax.numpy as jnp
from jax import lax
import numpy as np

D_MODEL = 1024
BATCH = 4
SEQ = 4096
DEPTH = 4

GRID_W = 64
CTX_LEN = 256
N_MIXERS = 4
Q_BLOCK = 128
NORM_EPS = 1e-6
NEG_INF = -1e30
ROPE_BASE = 10000.0
ROPE_DIM = 64

MLA_HEADS = 8
MLA_Q_RANK = 384
MLA_KV_RANK = 256
MLA_NOPE = 128
MLA_ROPE = ROPE_DIM
MLA_V = 128
MLA_WIDTH = MLA_HEADS * MLA_V

HY_WIDTH = D_MODEL
HY_ORDER = 2
HY_SHORT = 3
HY_EMB = 33
HY_BANDS = (HY_EMB - 1) // 2
HY_FILT_HIDDEN = 64
HY_DECAY_TARGET = 1e-2
HY_FAST_DECAY = 0.3
HY_SLOW_DECAY = 1.5

SWA_Q_HEADS = 16
SWA_KV_HEADS = 4
SWA_GROUP = SWA_Q_HEADS // SWA_KV_HEADS
SWA_HEAD_DIM = ROPE_DIM
SWA_WINDOW = 128
SWA_WIDTH = SWA_Q_HEADS * SWA_HEAD_DIM

CF_WIDTH = D_MODEL
CF_KERNEL = 31

kernel_name = "hybrid_mla_hyena_swa_conformer_flow_trunk"


def n_layers_of(m):
    return len(range(m, DEPTH, N_MIXERS))


def rmsnorm(x, g):
    xf = x.astype(jnp.float32)
    y = xf * lax.rsqrt(jnp.mean(xf * xf, axis=-1, keepdims=True) + NORM_EPS)
    return (y * g.astype(jnp.float32)).astype(x.dtype)


def layernorm(x, g, b):
    xf = x.astype(jnp.float32)
    mu = jnp.mean(xf, axis=-1, keepdims=True)
    xc = xf - mu
    var = jnp.mean(xc * xc, axis=-1, keepdims=True)
    return (xc * lax.rsqrt(var + NORM_EPS) * g.astype(jnp.float32) + b.astype(jnp.float32)).astype(x.dtype)


def modulate(h, shift, scale):
    return h * (1 + scale) + shift


def grid_rope(L, dim, dtype):
    rows = L // GRID_W
    row = jnp.repeat(jnp.arange(rows, dtype=jnp.float32), GRID_W)
    col = jnp.tile(jnp.arange(GRID_W, dtype=jnp.float32), rows)
    n_freq = dim // 4
    inv = ROPE_BASE ** (-jnp.arange(n_freq, dtype=jnp.float32) / n_freq)
    ang = jnp.concatenate([row[:, None] * inv, col[:, None] * inv], axis=-1)
    return jnp.cos(ang).astype(dtype), jnp.sin(ang).astype(dtype)


def apply_rope(x, cos, sin):
    half = x.shape[-1] // 2
    x1, x2 = x[..., :half], x[..., half:]
    return jnp.concatenate([x1 * cos - x2 * sin, x1 * sin + x2 * cos], axis=-1)


def depthwise_conv(u, w, b):
    K, C = w.shape
    pad = (K - 1) // 2
    y = lax.conv_general_dilated(u, w[:, None, :].astype(u.dtype), window_strides=(1,),
                                 padding=[(pad, pad)], dimension_numbers=('NWC', 'WIO', 'NWC'),
                                 feature_group_count=C)
    return y + b


def to_blocks(t):
    B, L = t.shape[:2]
    return jnp.moveaxis(t.reshape(B, L // Q_BLOCK, Q_BLOCK, *t.shape[2:]), 1, 0)


def from_blocks(o):
    nb, B, Q = o.shape[:3]
    return jnp.moveaxis(o, 0, 1).reshape(B, nb * Q, *o.shape[3:])


def mla_mixer(a, ac, cos, sin, w_in, q_norm_g, kv_norm_g, w_uq, w_ukv, w_out, ctx_out):
    B, L, _ = a.shape
    splits = [MLA_Q_RANK, MLA_Q_RANK + MLA_KV_RANK, MLA_Q_RANK + MLA_KV_RANK + MLA_ROPE]

    def project(t):
        n = t.shape[1]
        cq, ckv, k_rope, gate = jnp.split(t @ w_in, splits, axis=-1)
        q = (rmsnorm(cq, q_norm_g) @ w_uq).reshape(B, n, MLA_HEADS, MLA_NOPE + MLA_ROPE)
        kv = (rmsnorm(ckv, kv_norm_g) @ w_ukv).reshape(B, n, MLA_HEADS, MLA_NOPE + MLA_V)
        return q[..., :MLA_NOPE], q[..., MLA_NOPE:], kv[..., :MLA_NOPE], k_rope, kv[..., MLA_NOPE:], gate

    qn, qr, kn, kr, v, gate = project(a)
    qr = apply_rope(qr, cos[:, None, :], sin[:, None, :])
    kr = apply_rope(kr, cos, sin)
    qnc, qrc, knc, krc, vc, gatec = project(ac)
    scale = (MLA_NOPE + MLA_ROPE) ** -0.5

    def scores(qn_, qr_, kn_, kr_):
        s = jnp.einsum('bqhd,bkhd->bhqk', qn_, kn_) + jnp.einsum('bqhr,bkr->bhqk', qr_, kr_)
        return s.astype(jnp.float32) * scale

    def attend(blk):
        qn_b, qr_b = blk
        s = jnp.concatenate([scores(qn_b, qr_b, kn, kr), scores(qn_b, qr_b, knc, krc)], axis=-1)
        p = jax.nn.softmax(s, axis=-1).astype(v.dtype)
        return (jnp.einsum('bhqk,bkhd->bqhd', p[..., :L], v)
                + jnp.einsum('bhqk,bkhd->bqhd', p[..., L:], vc))

    o = from_blocks(lax.map(attend, (to_blocks(qn), to_blocks(qr)))).reshape(B, L, MLA_WIDTH)
    y = (o * jax.nn.silu(gate)) @ w_out
    yc = None
    if ctx_out:
        pc = jax.nn.softmax(scores(qnc, qrc, knc, krc), axis=-1).astype(vc.dtype)
        oc = jnp.einsum('bhqk,bkhd->bqhd', pc, vc).reshape(B, -1, MLA_WIDTH)
        yc = (oc * jax.nn.silu(gatec)) @ w_out
    return y, yc


def hyena_filter_spectra(L, w_in, w_hid, b, freq, w_out):
    f32 = jnp.float32
    t = jnp.linspace(0.0, 1.0, L, dtype=f32)[:, None]
    wpos = (2.0 * math.pi / L) * jnp.arange(L, dtype=f32)[:, None]
    bands = jnp.linspace(1e-4, HY_BANDS - 1, HY_BANDS, dtype=f32)[None, :]
    hdn = jnp.concatenate([t, jnp.cos(bands * wpos), -jnp.sin(bands * wpos)], axis=-1)
    b, freq = b.astype(f32), freq.astype(f32)
    for k, w in enumerate((w_in, w_hid[0], w_hid[1])):
        hdn = jnp.sin(freq[k] * (hdn @ w.astype(f32) + b[k]))
    h = (hdn @ w_out.astype(f32)).reshape(L, HY_ORDER, 2, HY_WIDTH)
    deltas = jnp.abs(jnp.linspace(math.log(HY_DECAY_TARGET) / HY_FAST_DECAY,
                                  math.log(HY_DECAY_TARGET) / HY_SLOW_DECAY, HY_WIDTH, dtype=f32))
    h = h * jnp.exp(-t * deltas)[:, None, None, :]
    hf, hb = h[:, :, 0], h[:, :, 1]
    k_circ = jnp.concatenate([hf[:1] + hb[:1], hf[1:], jnp.zeros_like(hf[:1]), hb[:0:-1]], axis=0)
    return jnp.fft.rfft(k_circ, axis=0)


def fft_long_conv(u, spec):
    L = u.shape[1]
    U = jnp.fft.rfft(u.astype(jnp.float32), n=2 * L, axis=1)
    return jnp.fft.irfft(U * spec, n=2 * L, axis=1)[:, :L].astype(u.dtype)


def hyena_seq(t, w_in, conv_w, conv_b, spec, bias, w_out):
    u, gate = jnp.split(t @ w_in, [(HY_ORDER + 1) * HY_WIDTH], axis=-1)
    u = depthwise_conv(u, conv_w, conv_b)
    v, *gates = jnp.split(u, HY_ORDER + 1, axis=-1)
    z = v
    for o in range(HY_ORDER):
        z = gates[o] * (fft_long_conv(z, spec[:, o]) + z * bias[o])
    return (z * jax.nn.silu(gate)) @ w_out


def swa_mixer(a, ac, cos, sin, w_in, sink, w_out, ctx_out):
    B, L, _ = a.shape
    nq = SWA_Q_HEADS * SWA_HEAD_DIM
    nkv = SWA_KV_HEADS * SWA_HEAD_DIM

    def project(t):
        n = t.shape[1]
        q, k, v, gate = jnp.split(t @ w_in, [nq, nq + nkv, nq + 2 * nkv], axis=-1)
        return (q.reshape(B, n, SWA_KV_HEADS, SWA_GROUP, SWA_HEAD_DIM),
                k.reshape(B, n, SWA_KV_HEADS, SWA_HEAD_DIM),
                v.reshape(B, n, SWA_KV_HEADS, SWA_HEAD_DIM), gate)

    q, k, v, gate = project(a)
    q = apply_rope(q, cos[:, None, None, :], sin[:, None, None, :])
    k = apply_rope(k, cos[:, None, :], sin[:, None, :])
    qc, kc, vc, gatec = project(ac)
    scale = SWA_HEAD_DIM ** -0.5
    sink_logit = sink.astype(jnp.float32).reshape(SWA_KV_HEADS, SWA_GROUP, 1, 1)

    def softmax_with_sink(parts, n_q):
        s_sink = jnp.broadcast_to(sink_logit, (B, SWA_KV_HEADS, SWA_GROUP, n_q, 1))
        p = jax.nn.softmax(jnp.concatenate([*parts, s_sink], axis=-1), axis=-1)
        return p[..., :-1].astype(v.dtype)

    span = Q_BLOCK + 2 * SWA_WINDOW
    pad = ((0, 0), (SWA_WINDOW, SWA_WINDOW), (0, 0), (0, 0))
    kp, vp = jnp.pad(k, pad), jnp.pad(v, pad)
    rel = jnp.arange(span)[None, :] - jnp.arange(Q_BLOCK)[:, None]
    band = (rel >= 0) & (rel <= 2 * SWA_WINDOW)

    def attend(args):
        j, q_b = args
        start = j * Q_BLOCK
        k_b = lax.dynamic_slice_in_dim(kp, start, span, axis=1)
        v_b = lax.dynamic_slice_in_dim(vp, start, span, axis=1)
        kpos = start - SWA_WINDOW + jnp.arange(span)
        valid = band & ((kpos >= 0) & (kpos < L))[None, :]
        s_win = jnp.einsum('bqhgd,bkhd->bhgqk', q_b, k_b).astype(jnp.float32) * scale
        s_win = jnp.where(valid, s_win, NEG_INF)
        s_ctx = jnp.einsum('bqhgd,bkhd->bhgqk', q_b, kc).astype(jnp.float32) * scale
        p = softmax_with_sink([s_win, s_ctx], Q_BLOCK)
        return (jnp.einsum('bhgqk,bkhd->bqhgd', p[..., :span], v_b)
                + jnp.einsum('bhgqk,bkhd->bqhgd', p[..., span:], vc))

    o = from_blocks(lax.map(attend, (jnp.arange(L // Q_BLOCK), to_blocks(q)))).reshape(B, L, SWA_WIDTH)
    y = (o * jax.nn.silu(gate)) @ w_out
    yc = None
    if ctx_out:
        s = jnp.einsum('bqhgd,bkhd->bhgqk', qc, kc).astype(jnp.float32) * scale
        pc = softmax_with_sink([s], qc.shape[1])
        oc = jnp.einsum('bhgqk,bkhd->bqhgd', pc, vc).reshape(B, -1, SWA_WIDTH)
        yc = (oc * jax.nn.silu(gatec)) @ w_out
    return y, yc


def conformer_seq(t, w_in, dw_w, dw_b, ln_g, ln_b, w_out):
    a, b, gate = jnp.split(t @ w_in, 3, axis=-1)
    u = a * jax.nn.sigmoid(b)
    u = depthwise_conv(u, dw_w, dw_b)
    u = jax.nn.silu(layernorm(u, ln_g, ln_b))
    return (u * jax.nn.silu(gate)) @ w_out


def setup_inputs(seed: int = 0) -> dict:
    key = jax.random.key(seed)
    ks = iter(jax.random.split(key, 40))
    D = D_MODEL
    f32 = jnp.float32

    def nrm(shape, std):
        return jax.random.normal(next(ks), shape, f32) * std

    def gain(shape):
        return 1.0 + nrm(shape, 0.02)

    nA, nB, nC, nD = (n_layers_of(m) for m in range(N_MIXERS))
    mla_in = MLA_Q_RANK + MLA_KV_RANK + MLA_ROPE + MLA_WIDTH
    swa_in = SWA_Q_HEADS * SWA_HEAD_DIM + 2 * SWA_KV_HEADS * SWA_HEAD_DIM + SWA_WIDTH
    return {
        "x": nrm((BATCH, SEQ, D), 1.0),
        "c": nrm((BATCH, D), 1.0),
        "ctx": nrm((BATCH, CTX_LEN, D), 1.0),
        "c_ctx": nrm((D,), 1.0),
        "norm_g": gain((DEPTH, D)),
        "ada_w": nrm((DEPTH, D, 3 * D), 0.5 * D ** -0.5),
        "ada_b": nrm((DEPTH, 3 * D), 0.02),
        "final_g": gain((D,)),
        "mla_w_in": nrm((nA, D, mla_in), D ** -0.5),
        "mla_q_norm_g": gain((nA, MLA_Q_RANK)),
        "mla_kv_norm_g": gain((nA, MLA_KV_RANK)),
        "mla_w_uq": nrm((nA, MLA_Q_RANK, MLA_HEADS * (MLA_NOPE + MLA_ROPE)), MLA_Q_RANK ** -0.5),
        "mla_w_ukv": nrm((nA, MLA_KV_RANK, MLA_HEADS * (MLA_NOPE + MLA_V)), MLA_KV_RANK ** -0.5),
        "mla_w_out": nrm((nA, MLA_WIDTH, D), MLA_WIDTH ** -0.5),
        "hy_w_in": nrm((nB, D, (HY_ORDER + 2) * HY_WIDTH), D ** -0.5),
        "hy_conv_w": nrm((nB, HY_SHORT, (HY_ORDER + 1) * HY_WIDTH), HY_SHORT ** -0.5),
        "hy_conv_b": nrm((nB, (HY_ORDER + 1) * HY_WIDTH), 0.02),
        "hy_filt_w_in": nrm((nB, HY_EMB, HY_FILT_HIDDEN), HY_EMB ** -0.5),
        "hy_filt_w_hid": nrm((nB, 2, HY_FILT_HIDDEN, HY_FILT_HIDDEN), HY_FILT_HIDDEN ** -0.5),
        "hy_filt_b": nrm((nB, 3, HY_FILT_HIDDEN), 0.2),
        "hy_filt_freq": gain((nB, 3, HY_FILT_HIDDEN)),
        "hy_filt_w_out": nrm((nB, HY_FILT_HIDDEN, HY_ORDER * 2 * HY_WIDTH), 0.005),
        "hy_bias": nrm((nB, HY_ORDER, HY_WIDTH), 0.5),
        "hy_w_out": nrm((nB, HY_WIDTH, D), HY_WIDTH ** -0.5),
        "swa_w_in": nrm((nC, D, swa_in), D ** -0.5),
        "swa_sink": nrm((nC, SWA_Q_HEADS), 0.5),
        "swa_w_out": nrm((nC, SWA_WIDTH, D), SWA_WIDTH ** -0.5),
        "cf_w_in": nrm((nD, D, 3 * CF_WIDTH), D ** -0.5),
        "cf_dw_w": nrm((nD, CF_KERNEL, CF_WIDTH), CF_KERNEL ** -0.5),
        "cf_dw_b": nrm((nD, CF_WIDTH), 0.02),
        "cf_ln_g": gain((nD, CF_WIDTH)),
        "cf_ln_b": nrm((nD, CF_WIDTH), 0.02),
        "cf_w_out": nrm((nD, CF_WIDTH, D), CF_WIDTH ** -0.5),
    }


def reference(x, c, ctx, c_ctx, norm_g, ada_w, ada_b, final_g,
              mla_w_in, mla_q_norm_g, mla_kv_norm_g, mla_w_uq, mla_w_ukv, mla_w_out,
              hy_w_in, hy_conv_w, hy_conv_b, hy_filt_w_in, hy_filt_w_hid, hy_filt_b, hy_filt_freq,
              hy_filt_w_out, hy_bias, hy_w_out,
              swa_w_in, swa_sink, swa_w_out,
              cf_w_in, cf_dw_w, cf_dw_b, cf_ln_g, cf_ln_b, cf_w_out):
    B, L, _ = x.shape
    Lc = ctx.shape[1]
    cos, sin = grid_rope(L, ROPE_DIM, x.dtype)
    silu_c = jax.nn.silu(c)
    silu_cc = jax.nn.silu(c_ctx)
    xc = ctx
    for i in range(DEPTH):
        m, j = i % N_MIXERS, i // N_MIXERS
        ctx_out = i < DEPTH - 1
        shift, scale, gate = jnp.split((silu_c @ ada_w[i] + ada_b[i])[:, None, :], 3, axis=-1)
        a = modulate(rmsnorm(x, norm_g[i]), shift, scale)
        if ctx_out or m in (0, 2):
            shift_c, scale_c, gate_c = jnp.split(silu_cc @ ada_w[i] + ada_b[i], 3, axis=-1)
            ac = modulate(rmsnorm(xc, norm_g[i]), shift_c, scale_c)
        yc = None
        if m == 0:
            y, yc = mla_mixer(a, ac, cos, sin, mla_w_in[j], mla_q_norm_g[j], mla_kv_norm_g[j],
                              mla_w_uq[j], mla_w_ukv[j], mla_w_out[j], ctx_out)
        elif m == 1:
            filt = (hy_filt_w_in[j], hy_filt_w_hid[j], hy_filt_b[j], hy_filt_freq[j], hy_filt_w_out[j])
            y = hyena_seq(a, hy_w_in[j], hy_conv_w[j], hy_conv_b[j],
                          hyena_filter_spectra(L, *filt), hy_bias[j], hy_w_out[j])
            if ctx_out:
                yc = hyena_seq(ac, hy_w_in[j], hy_conv_w[j], hy_conv_b[j],
                               hyena_filter_spectra(Lc, *filt), hy_bias[j], hy_w_out[j])
        elif m == 2:
            y, yc = swa_mixer(a, ac, cos, sin, swa_w_in[j], swa_sink[j], swa_w_out[j], ctx_out)
        else:
            y = conformer_seq(a, cf_w_in[j], cf_dw_w[j], cf_dw_b[j], cf_ln_g[j], cf_ln_b[j], cf_w_out[j])
            if ctx_out:
                yc = conformer_seq(ac, cf_w_in[j], cf_dw_w[j], cf_dw_b[j], cf_ln_g[j], cf_ln_b[j], cf_w_out[j])
        x = x + gate * y
        if ctx_out:
            xc = xc + gate_c * yc
    return rmsnorm(x, final_g)
```

```python
import functools
import math

import numpy as np
import jax
import jax.numpy as jnp
from jax import lax
from jax.experimental import pallas as pl
from jax.experimental.pallas import tpu as pltpu

F32 = jnp.float32
BF16 = jnp.bfloat16

GRID_W = 64
NORM_EPS = 1e-6
NEG_INF = -1e30
ROPE_BASE = 10000.0
ROPE_DIM = 64

MLA_HEADS = 8
MLA_Q_RANK = 384
MLA_KV_RANK = 256
MLA_NOPE = 128
MLA_ROPE = ROPE_DIM
MLA_V = 128
MLA_QK = MLA_NOPE + MLA_ROPE

HY_ORDER = 2
HY_EMB = 33
HY_EMB_PAD = 64
HY_BANDS = (HY_EMB - 1) // 2
HY_DECAY_TARGET = 1e-2
HY_FAST_DECAY = 0.3
HY_SLOW_DECAY = 1.5

SWA_Q_HEADS = 16
SWA_KV_HEADS = 4
SWA_GROUP = SWA_Q_HEADS // SWA_KV_HEADS
SWA_HEAD_DIM = ROPE_DIM
SWA_WINDOW = 128

CF_KERNEL = 31
CF_HALO = 16

ROW_TILE = 256
SUBLANES = 8
LANES = 128
FFT_N1 = 64
FFT_N1H = FFT_N1 // 2
FFT_K1 = FFT_N1 // 2 + 1
VMEM_LIMIT = 48 * 1024 * 1024


def _cparams(n_axes):
    return pltpu.CompilerParams(dimension_semantics=("arbitrary",) * n_axes,
                                vmem_limit_bytes=VMEM_LIMIT)


def _dot(a, b):
    return jnp.dot(a, b, preferred_element_type=F32)


def _dot_nt(a, b):
    return lax.dot_general(a, b, (((1,), (1,)), ((), ())), preferred_element_type=F32)


def _rms(x, g):
    return x * lax.rsqrt(jnp.mean(x * x, axis=-1, keepdims=True) + NORM_EPS) * g


def _norm_mod(x, g, mod):
    return _rms(x, g) * (1.0 + mod[1:2, :]) + mod[0:1, :]


def _silu(x):
    return x * jax.nn.sigmoid(x)


def _adaln_kernel(c_ref, w_ref, b_ref, o_ref):
    s = _silu(c_ref[...]).astype(BF16)
    o_ref[...] = _dot(s, w_ref[...].astype(BF16)) + b_ref[...]


def _adaln(cs, ada_w, ada_b):
    depth, d, d3 = ada_w.shape
    tn = 512
    return pl.pallas_call(
        _adaln_kernel,
        out_shape=jax.ShapeDtypeStruct((depth, cs.shape[0], d3), F32),
        grid=(depth, d3 // tn),
        in_specs=[pl.BlockSpec(cs.shape, lambda i, j: (0, 0)),
                  pl.BlockSpec((None, d, tn), lambda i, j: (i, 0, j)),
                  pl.BlockSpec((None, 1, tn), lambda i, j: (i, 0, j))],
        out_specs=pl.BlockSpec((None, cs.shape[0], tn), lambda i, j: (i, 0, j)),
        compiler_params=_cparams(2), name="adaln",
    )(cs, ada_w, ada_b.reshape(depth, 1, d3))


def _outproj_kernel(o_ref, sg_ref, x_ref, mod_ref, w_ref, xo_ref):
    og = (o_ref[...].astype(F32) * sg_ref[...].astype(F32)).astype(BF16)
    xo_ref[...] = x_ref[...] + mod_ref[2:3, :] * _dot(og, w_ref[...])


def _outproj(o, sg, x, mod, w_out, n_lat_tiles):
    b, t, d = x.shape
    width = o.shape[-1]
    nt = t // ROW_TILE
    row = lambda bi, i: (bi, i, 0)
    return pl.pallas_call(
        _outproj_kernel,
        out_shape=jax.ShapeDtypeStruct(x.shape, F32),
        grid=(b, nt),
        in_specs=[pl.BlockSpec((None, ROW_TILE, width), row),
                  pl.BlockSpec((None, ROW_TILE, width), row),
                  pl.BlockSpec((None, ROW_TILE, d), row),
                  pl.BlockSpec((None, None, 3, d), lambda bi, i: (bi, i // n_lat_tiles, 0, 0)),
                  pl.BlockSpec(w_out.shape, lambda bi, i: (0, 0))],
        out_specs=pl.BlockSpec((None, ROW_TILE, d), row),
        compiler_params=_cparams(2), name="outproj",
    )(o, sg, x, mod, w_out)


def _rope_swap_matrix():
    half = ROPE_DIM // 2
    p = np.zeros((ROPE_DIM, ROPE_DIM), np.float32)
    for j in range(half):
        p[j + half, j] = -1.0
        p[j, j + half] = 1.0
    return p


def _mla_proj_kernel(x_ref, g_ref, mod_ref, w_ref, qg_ref, kvg_ref, wuq_ref, wukv_ref,
                     cos_ref, sin_ref, p_ref, q_ref, k_ref, v_ref, sg_ref):
    h = _norm_mod(x_ref[...], g_ref[...], mod_ref[...]).astype(BF16)
    res = _dot(h, w_ref[...])
    c0, c1 = MLA_Q_RANK, MLA_Q_RANK + MLA_KV_RANK
    c2 = c1 + sg_ref.shape[-1]
    cqn = _rms(res[:, :c0], qg_ref[...]).astype(BF16)
    ckvn = _rms(res[:, c0:c1], kvg_ref[...]).astype(BF16)
    sg_ref[...] = _silu(res[:, c1:c2]).astype(sg_ref.dtype)
    cos, sin, swap = cos_ref[...], sin_ref[...], p_ref[...]

    def rope(t):
        return t * cos + _dot(t.astype(BF16), swap) * sin

    k_rope = rope(res[:, c2:c2 + MLA_ROPE]).astype(k_ref.dtype)
    scale = MLA_QK ** -0.5
    for hd in range(MLA_HEADS):
        q = _dot(cqn, wuq_ref[hd])
        q_ref[hd, :, :MLA_NOPE] = (q[:, :MLA_NOPE] * scale).astype(q_ref.dtype)
        q_ref[hd, :, MLA_NOPE:] = (rope(q[:, MLA_NOPE:]) * scale).astype(q_ref.dtype)
        kv = _dot(ckvn, wukv_ref[hd])
        k_ref[hd, :, :MLA_NOPE] = kv[:, :MLA_NOPE].astype(k_ref.dtype)
        k_ref[hd, :, MLA_NOPE:] = k_rope
        v_ref[hd] = kv[:, MLA_NOPE:].astype(v_ref.dtype)


def _mla_proj(x, g, mod, w, qg, kvg, wuq, wukv, cos, sin, n_lat_tiles):
    b, t, d = x.shape
    nt = t // ROW_TILE
    width = MLA_HEADS * MLA_V
    swap = jnp.asarray(_rope_swap_matrix()).astype(BF16)
    full = lambda a: pl.BlockSpec(a.shape, lambda bi, i: (0,) * a.ndim)
    head_out = lambda n: pl.BlockSpec((None, MLA_HEADS, ROW_TILE, n), lambda bi, i: (bi, 0, i, 0))
    return pl.pallas_call(
        _mla_proj_kernel,
        out_shape=(jax.ShapeDtypeStruct((b, MLA_HEADS, t, MLA_QK), BF16),
                   jax.ShapeDtypeStruct((b, MLA_HEADS, t, MLA_QK), BF16),
                   jax.ShapeDtypeStruct((b, MLA_HEADS, t, MLA_V), BF16),
                   jax.ShapeDtypeStruct((b, t, width), BF16)),
        grid=(b, nt),
        in_specs=[pl.BlockSpec((None, ROW_TILE, d), lambda bi, i: (bi, i, 0)),
                  full(g),
                  pl.BlockSpec((None, None, 3, d), lambda bi, i: (bi, i // n_lat_tiles, 0, 0)),
                  full(w), full(qg), full(kvg), full(wuq), full(wukv),
                  pl.BlockSpec((ROW_TILE, ROPE_DIM), lambda bi, i: (i, 0)),
                  pl.BlockSpec((ROW_TILE, ROPE_DIM), lambda bi, i: (i, 0)),
                  full(swap)],
        out_specs=(head_out(MLA_QK), head_out(MLA_QK), head_out(MLA_V),
                   pl.BlockSpec((None, ROW_TILE, width), lambda bi, i: (bi, i, 0))),
        compiler_params=_cparams(2), name="mla_proj",
    )(x, g, mod, w, qg, kvg, wuq, wukv, cos, sin, swap)


def _softmax_pv(s, v):
    m = jnp.max(s, axis=-1, keepdims=True)
    p = jnp.exp(s - m)
    l = jnp.sum(p, axis=-1, keepdims=True)
    return _dot(p.astype(BF16), v) / l


def _mla_attn_kernel(q_ref, k_ref, v_ref, o_ref, *, n_lat_tiles, n_lat, n_ctx):
    qi = pl.program_id(2)
    q = q_ref[...]

    @pl.when(qi < n_lat_tiles)
    def _():
        o_ref[...] = _softmax_pv(_dot_nt(q, k_ref[...]), v_ref[...]).astype(o_ref.dtype)

    @pl.when(qi >= n_lat_tiles)
    def _():
        k = k_ref[pl.ds(n_lat, n_ctx), :]
        v = v_ref[pl.ds(n_lat, n_ctx), :]
        o_ref[...] = _softmax_pv(_dot_nt(q, k), v).astype(o_ref.dtype)


def _mla_attn(q, k, v, n_lat, n_ctx):
    b, hds, t, _ = q.shape
    nt = t // ROW_TILE
    kern = functools.partial(_mla_attn_kernel, n_lat_tiles=n_lat // ROW_TILE, n_lat=n_lat, n_ctx=n_ctx)
    return pl.pallas_call(
        kern,
        out_shape=jax.ShapeDtypeStruct((b, t, hds * MLA_V), BF16),
        grid=(b, hds, nt),
        in_specs=[pl.BlockSpec((None, None, ROW_TILE, MLA_QK), lambda bi, h, i: (bi, h, i, 0)),
                  pl.BlockSpec((None, None, t, MLA_QK), lambda bi, h, i: (bi, h, 0, 0)),
                  pl.BlockSpec((None, None, t, MLA_V), lambda bi, h, i: (bi, h, 0, 0))],
        out_specs=pl.BlockSpec((None, ROW_TILE, MLA_V), lambda bi, h, i: (bi, i, h)),
        compiler_params=_cparams(3), name="mla_attn",
    )(q, k, v)


def _hy_proj_kernel(xp_ref, x_ref, xn_ref, g_ref, mod_ref, w_ref, cw_ref, cb_ref,
                    z_ref, g0_ref, g1_ref, sg_ref, *, n_lat_tiles):
    i = pl.program_id(1)
    nt = pl.num_programs(1)
    tm = x_ref.shape[0]
    c = z_ref.shape[-1]
    xa = jnp.concatenate([xp_ref[...], x_ref[...], xn_ref[...]], axis=0)
    h = _norm_mod(xa, g_ref[...], mod_ref[...]).astype(BF16)
    res = _dot(h, w_ref[...])
    nu = 3 * c
    first = jnp.logical_or(i == 0, i == n_lat_tiles)
    last = jnp.logical_or(i == n_lat_tiles - 1, i == nt - 1)
    rows = lax.broadcasted_iota(jnp.int32, (tm + 2 * SUBLANES, 1), 0)
    valid = jnp.logical_and(jnp.logical_or(rows >= SUBLANES, jnp.logical_not(first)),
                            jnp.logical_or(rows < tm + SUBLANES, jnp.logical_not(last)))
    u = jnp.where(valid, res[:, :nu], 0.0)
    n = tm + 2 * SUBLANES
    up = pltpu.roll(u, 1, 0)
    dn = pltpu.roll(u, n - 1, 0)
    cw = cw_ref[...]
    cv = (cw[0:1, :] * up + cw[1:2, :] * u + cw[2:3, :] * dn)[SUBLANES:SUBLANES + tm] + cb_ref[...]
    z_ref[...] = cv[:, :c]
    g0_ref[...] = cv[:, c:2 * c]
    g1_ref[...] = cv[:, 2 * c:]
    sg_ref[...] = _silu(res[SUBLANES:SUBLANES + tm, nu:]).astype(sg_ref.dtype)


def _hy_proj(x, g, mod, w, cw, cb, n_lat_tiles):
    b, t, d = x.shape
    nt = t // ROW_TILE
    c = w.shape[1] // (HY_ORDER + 2)
    per = ROW_TILE // SUBLANES
    last_halo = t // SUBLANES - 1
    full = lambda a: pl.BlockSpec(a.shape, lambda bi, i: (0,) * a.ndim)
    row = lambda bi, i: (bi, i, 0)
    kern = functools.partial(_hy_proj_kernel, n_lat_tiles=n_lat_tiles)
    return pl.pallas_call(
        kern,
        out_shape=(jax.ShapeDtypeStruct((b, t, c), F32),) * 3 + (jax.ShapeDtypeStruct((b, t, c), BF16),),
        grid=(b, nt),
        in_specs=[pl.BlockSpec((None, SUBLANES, d), lambda bi, i: (bi, jnp.maximum(i * per - 1, 0), 0)),
                  pl.BlockSpec((None, ROW_TILE, d), row),
                  pl.BlockSpec((None, SUBLANES, d), lambda bi, i: (bi, jnp.minimum((i + 1) * per, last_halo), 0)),
                  full(g),
                  pl.BlockSpec((None, None, 3, d), lambda bi, i: (bi, i // n_lat_tiles, 0, 0)),
                  full(w), full(cw), full(cb)],
        out_specs=(pl.BlockSpec((None, ROW_TILE, c), row),) * 4,
        compiler_params=_cparams(2), name="hy_proj",
    )(x, x, x, g, mod, w, cw, cb)


def _hy_filter_kernel(ef_ref, eb_ref, w0_ref, wh_ref, b_ref, fr_ref, wf_ref, wb_ref, dl_ref, o_ref):
    i = pl.program_id(0)
    c = dl_ref.shape[-1]

    def taps(e, wo):
        hdn = e
        ws = (w0_ref[...], wh_ref[0], wh_ref[1])
        for k in range(3):
            hdn = jnp.sin(fr_ref[k:k + 1, :] * (_dot(hdn.astype(BF16), ws[k].astype(BF16)) + b_ref[k:k + 1, :]))
        decay = jnp.exp(-e[:, 0:1] * dl_ref[...])
        hh = _dot(hdn.astype(BF16), wo.astype(BF16))
        return [hh[:, o * c:(o + 1) * c] * decay for o in range(HY_ORDER)]

    hf = taps(ef_ref[...], wf_ref[...])
    hb = taps(eb_ref[...], wb_ref[...])
    rows = lax.broadcasted_iota(jnp.int32, (ef_ref.shape[0], 1), 0)
    lag0 = jnp.logical_and(rows == 0, i == 0)
    for o in range(HY_ORDER):
        o_ref[o, 0] = hf[o] + jnp.where(lag0, hb[o], 0.0)
        o_ref[o, 1] = jnp.where(lag0, 0.0, hb[o])


def _hy_position_features(n):
    t = jnp.linspace(0.0, 1.0, n, dtype=F32)[:, None]
    wpos = (2.0 * math.pi / n) * jnp.arange(n, dtype=F32)[:, None]
    bands = jnp.linspace(1e-4, HY_BANDS - 1, HY_BANDS, dtype=F32)[None, :]
    hdn = jnp.concatenate([t, jnp.cos(bands * wpos), -jnp.sin(bands * wpos)], axis=-1)
    return jnp.pad(hdn, ((0, 0), (0, HY_EMB_PAD - HY_EMB)))


def _hy_filter_taps(n, w_in, w_hid, bias, freq, w_out):
    c = w_out.shape[1] // (2 * HY_ORDER)
    ef = _hy_position_features(n)
    rev = (n - jnp.arange(n)) % n
    eb = ef[rev]
    w0 = jnp.pad(w_in, ((0, HY_EMB_PAD - HY_EMB), (0, 0)))
    wo = w_out.reshape(w_out.shape[0], HY_ORDER, 2, c)
    wf = wo[:, :, 0, :].reshape(w_out.shape[0], HY_ORDER * c)
    wb = wo[:, :, 1, :].reshape(w_out.shape[0], HY_ORDER * c)
    deltas = jnp.abs(jnp.linspace(math.log(HY_DECAY_TARGET) / HY_FAST_DECAY,
                                  math.log(HY_DECAY_TARGET) / HY_SLOW_DECAY, c, dtype=F32))[None, :]
    tl = min(n, 512)
    full = lambda a: pl.BlockSpec(a.shape, lambda i: (0,) * a.ndim)
    return pl.pallas_call(
        _hy_filter_kernel,
        out_shape=jax.ShapeDtypeStruct((HY_ORDER, 2, n, c), F32),
        grid=(n // tl,),
        in_specs=[pl.BlockSpec((tl, HY_EMB_PAD), lambda i: (i, 0)),
                  pl.BlockSpec((tl, HY_EMB_PAD), lambda i: (i, 0)),
                  full(w0), full(w_hid), full(bias), full(freq), full(wf), full(wb), full(deltas)],
        out_specs=pl.BlockSpec((HY_ORDER, 2, tl, c), lambda i: (0, 0, i, 0)),
        compiler_params=_cparams(1), name="hy_filter",
    )(ef, eb, w0, w_hid, bias, freq, wf, wb, deltas)


def _fft_matrices(n_lat):
    n = 2 * n_lat
    n2 = n_lat // FFT_N1H
    k1 = np.arange(FFT_K1)
    n1 = np.arange(FFT_N1H)
    ang = 2.0 * np.pi * np.outer(k1, n1) / FFT_N1
    eye = np.eye(SUBLANES)
    f1 = np.kron(np.concatenate([np.cos(ang), -np.sin(ang)], axis=0), eye)
    wgt = np.full(FFT_K1, 2.0)
    wgt[0] = wgt[-1] = 1.0
    g1 = np.kron(np.concatenate([wgt * np.cos(ang.T), -wgt * np.sin(ang.T)], axis=1) / n, eye)
    idx = np.arange(n2)
    f2 = np.zeros((FFT_K1, 2 * n2, 2 * n2))
    f2i = np.zeros((FFT_K1, 2 * n2, 2 * n2))
    for a in range(FFT_K1):
        ph = 2.0 * np.pi * (np.outer(idx, idx) / n2 + (a * idx)[None, :] / n)
        f2[a] = np.block([[np.cos(ph), np.sin(ph)], [-np.sin(ph), np.cos(ph)]])
        f2i[a] = np.block([[np.cos(ph.T), -np.sin(ph.T)], [np.sin(ph.T), np.cos(ph.T)]])
    as_bf16 = lambda m: jnp.asarray(m.astype(np.float32)).astype(BF16)
    return as_bf16(f1), as_bf16(f2), as_bf16(f2i), as_bf16(g1)


def _dense_dft_matrices(n_ctx):
    n = 2 * n_ctx
    nf = n_ctx + 1
    nfp = -(-nf // SUBLANES) * SUBLANES
    k = np.arange(nf)
    ang = 2.0 * np.pi * np.outer(k, np.arange(n)) / n
    fwd = np.zeros((2 * nfp, n))
    fwd[:nf] = np.cos(ang)
    fwd[nfp:nfp + nf] = -np.sin(ang)
    wgt = np.full(nf, 2.0)
    wgt[0] = wgt[-1] = 1.0
    angi = 2.0 * np.pi * np.outer(np.arange(n_ctx), k) / n
    inv = np.zeros((n_ctx, 2 * nfp))
    inv[:, :nf] = wgt * np.cos(angi) / n
    inv[:, nfp:nfp + nf] = -wgt * np.sin(angi) / n
    as_bf16 = lambda m: jnp.asarray(m.astype(np.float32)).astype(BF16)
    return as_bf16(fwd), as_bf16(inv), nfp


def _hy_s1f_kernel(z_ref, f_ref, a_ref):
    k, s, c = z_ref.shape
    res = _dot(f_ref[...], z_ref[...].reshape(k * s, c).astype(BF16))
    a_ref[...] = res.reshape(a_ref.shape)


def _hy_s1f(z, f1, n_lat):
    bz, t, c = z.shape
    n2 = n_lat // FFT_N1H
    zv = z.reshape(bz, t // n2, n2 // SUBLANES, SUBLANES, c)
    a = pl.pallas_call(
        _hy_s1f_kernel,
        out_shape=jax.ShapeDtypeStruct((bz, 2, FFT_K1, n2 // SUBLANES, SUBLANES, c), F32),
        grid=(bz, n2 // SUBLANES),
        in_specs=[pl.BlockSpec((None, FFT_N1H, None, SUBLANES, c), lambda bi, j: (bi, 0, j, 0, 0)),
                  pl.BlockSpec(f1.shape, lambda bi, j: (0, 0))],
        out_specs=pl.BlockSpec((None, 2, FFT_K1, None, SUBLANES, c), lambda bi, j: (bi, 0, 0, j, 0, 0)),
        compiler_params=_cparams(2), name="hy_s1f",
    )(zv, f1)
    return a.reshape(bz, 2, FFT_K1, n2, c)


def _hy_spec_kernel(af_ref, ab_ref, f2_ref, h_ref):
    sign = (1 - 2 * (pl.program_id(1) % 2)).astype(F32)
    _, n2, c = af_ref.shape
    a = (af_ref[...] + sign * ab_ref[...]).reshape(2 * n2, c).astype(BF16)
    h_ref[...] = _dot(f2_ref[...], a).reshape(h_ref.shape)


def _hy_spec(a, f2):
    _, _, _, n2, c = a.shape
    blk = (None, 2, None, n2, c)
    return pl.pallas_call(
        _hy_spec_kernel,
        out_shape=jax.ShapeDtypeStruct((HY_ORDER, 2, FFT_K1, n2, c), F32),
        grid=(HY_ORDER, FFT_K1),
        in_specs=[pl.BlockSpec(blk, lambda o, k: (2 * o, 0, k, 0, 0)),
                  pl.BlockSpec(blk, lambda o, k: (2 * o + 1, 0, k, 0, 0)),
                  pl.BlockSpec((None, 2 * n2, 2 * n2), lambda o, k: (k, 0, 0))],
        out_specs=pl.BlockSpec(blk, lambda o, k: (o, 0, k, 0, 0)),
        compiler_params=_cparams(2), name="hy_spec",
    )(a, a, f2)


def _hy_s2_kernel(a_ref, h_ref, f2_ref, f2i_ref, b_ref):
    _, n2, c = a_ref.shape
    x = _dot(f2_ref[...], a_ref[...].reshape(2 * n2, c).astype(BF16))
    xr, xi = x[:n2], x[n2:]
    hr, hi = h_ref[0], h_ref[1]
    y = jnp.concatenate([xr * hr - xi * hi, xr * hi + xi * hr], axis=0).astype(BF16)
    b_ref[...] = _dot(f2i_ref[...], y).reshape(b_ref.shape)


def _hy_s2(a, h, order, f2, f2i):
    bz, _, _, n2, c = a.shape
    blk = (None, 2, None, n2, c)
    mat = pl.BlockSpec((None, 2 * n2, 2 * n2), lambda bi, k: (k, 0, 0))
    return pl.pallas_call(
        _hy_s2_kernel,
        out_shape=jax.ShapeDtypeStruct(a.shape, F32),
        grid=(bz, FFT_K1),
        in_specs=[pl.BlockSpec(blk, lambda bi, k: (bi, 0, k, 0, 0)),
                  pl.BlockSpec(blk, lambda bi, k: (order, 0, k, 0, 0)),
                  mat, mat],
        out_specs=pl.BlockSpec(blk, lambda bi, k: (bi, 0, k, 0, 0)),
        compiler_params=_cparams(2), name="hy_s2",
    )(a, h, f2, f2i)


def _hy_s1i_kernel(b_ref, g1_ref, z_ref, gt_ref, bias_ref, o_ref):
    _, _, s, c = b_ref.shape
    y = _dot(g1_ref[...], b_ref[...].reshape(2 * FFT_K1 * s, c).astype(BF16)).reshape(o_ref.shape)
    o_ref[...] = gt_ref[...] * (y + z_ref[...] * bias_ref[...])


def _hy_s1i(bp, g1, z, gate, bias, n_lat):
    bz, t, c = z.shape
    n2 = n_lat // FFT_N1H
    view = (bz, t // n2, n2 // SUBLANES, SUBLANES, c)
    tile = pl.BlockSpec((None, FFT_N1H, None, SUBLANES, c), lambda bi, j: (bi, 0, j, 0, 0))
    out = pl.pallas_call(
        _hy_s1i_kernel,
        out_shape=jax.ShapeDtypeStruct(view, F32),
        grid=(bz, n2 // SUBLANES),
        in_specs=[pl.BlockSpec((None, 2, FFT_K1, None, SUBLANES, c), lambda bi, j: (bi, 0, 0, j, 0, 0)),
                  pl.BlockSpec(g1.shape, lambda bi, j: (0, 0)),
                  tile, tile,
                  pl.BlockSpec((1, c), lambda bi, j: (0, 0))],
        out_specs=tile,
        compiler_params=_cparams(2), name="hy_s1i",
    )(bp.reshape(bz, 2, FFT_K1, n2 // SUBLANES, SUBLANES, c), g1, z.reshape(view), gate.reshape(view), bias)
    return out.reshape(bz, t, c)


def _hy_ctx_spec_kernel(t_ref, f_ref, h_ref):
    sig = jnp.concatenate([t_ref[0], t_ref[1]], axis=0).astype(BF16)
    h_ref[...] = _dot(f_ref[...], sig)


def _hy_ctx_spec(taps, fwd):
    _, _, n_ctx, c = taps.shape
    return pl.pallas_call(
        _hy_ctx_spec_kernel,
        out_shape=jax.ShapeDtypeStruct((HY_ORDER, fwd.shape[0], c), F32),
        grid=(HY_ORDER,),
        in_specs=[pl.BlockSpec((None, 2, n_ctx, c), lambda o: (o, 0, 0, 0)),
                  pl.BlockSpec(fwd.shape, lambda o: (0, 0))],
        out_specs=pl.BlockSpec((None, fwd.shape[0], c), lambda o: (o, 0, 0)),
        compiler_params=_cparams(1), name="hy_ctx_spec",
    )(taps, fwd)


def _hy_ctx_conv_kernel(zin_ref, gt_ref, h_ref, f_ref, fi_ref, bias_ref, dst_ref, o_ref, *, nfp):
    del dst_ref
    z = zin_ref[...]
    x = _dot(f_ref[...], z.astype(BF16))
    xr, xi = x[:nfp], x[nfp:]
    hr, hi = h_ref[:nfp, :], h_ref[nfp:, :]
    y = jnp.concatenate([xr * hr - xi * hi, xr * hi + xi * hr], axis=0).astype(BF16)
    o_ref[...] = gt_ref[...] * (_dot(fi_ref[...], y) + z * bias_ref[...])


def _hy_ctx_conv(z, gate, h, order, fwd_half, inv, nfp, bias, dst, n_lat):
    bz, t, c = z.shape
    n_ctx = t - n_lat
    blk = pl.BlockSpec((None, n_ctx, c), lambda bi: (bi, n_lat // n_ctx, 0))
    kern = functools.partial(_hy_ctx_conv_kernel, nfp=nfp)
    return pl.pallas_call(
        kern,
        out_shape=jax.ShapeDtypeStruct(dst.shape, F32),
        grid=(bz,),
        in_specs=[blk, blk,
                  pl.BlockSpec((None, 2 * nfp, c), lambda bi: (order, 0, 0)),
                  pl.BlockSpec(fwd_half.shape, lambda bi: (0, 0)),
                  pl.BlockSpec(inv.shape, lambda bi: (0, 0)),
                  pl.BlockSpec((1, c), lambda bi: (0, 0)),
                  pl.BlockSpec(memory_space=pl.ANY)],
        out_specs=blk,
        input_output_aliases={6: 0},
        compiler_params=_cparams(1), name="hy_ctx_conv",
    )(z, gate, h, fwd_half, inv, bias, dst)


def _rope_lanes(x, cos, sin):
    half = ROPE_DIM // 2
    lane = lax.broadcasted_iota(jnp.int32, x.shape, 1) % ROPE_DIM
    rot = jnp.where(lane < half, -pltpu.roll(x, LANES - half, 1), pltpu.roll(x, half, 1))
    return x * cos + rot * sin


def _swa_proj_kernel(x_ref, g_ref, mod_ref, w_ref, cos_ref, sin_ref, q_ref, k_ref, v_ref, sg_ref):
    h = _norm_mod(x_ref[...], g_ref[...], mod_ref[...]).astype(BF16)
    res = _dot(h, w_ref[...])
    cos, sin = cos_ref[...], sin_ref[...]
    nq = q_ref.shape[-1]
    nk = SWA_KV_HEADS * LANES
    scale = SWA_HEAD_DIM ** -0.5
    for cb in range(nq // LANES):
        sl = slice(cb * LANES, (cb + 1) * LANES)
        q_ref[:, sl] = (_rope_lanes(res[:, sl], cos, sin) * scale).astype(q_ref.dtype)
    for hd in range(SWA_KV_HEADS):
        k_ref[hd] = _rope_lanes(res[:, nq + hd * LANES: nq + (hd + 1) * LANES], cos, sin).astype(k_ref.dtype)
        v_ref[hd] = res[:, nq + nk + hd * LANES: nq + nk + (hd + 1) * LANES].astype(v_ref.dtype)
    sg_ref[...] = _silu(res[:, nq + 2 * nk:]).astype(sg_ref.dtype)


def _swa_proj(x, g, mod, w, cos, sin, n_lat_tiles):
    b, t, d = x.shape
    nt = t // ROW_TILE
    nq = SWA_Q_HEADS * SWA_HEAD_DIM
    full = lambda a: pl.BlockSpec(a.shape, lambda bi, i: (0,) * a.ndim)
    row = lambda bi, i: (bi, i, 0)
    kv = pl.BlockSpec((None, SWA_KV_HEADS, ROW_TILE, LANES), lambda bi, i: (bi, 0, i, 0))
    return pl.pallas_call(
        _swa_proj_kernel,
        out_shape=(jax.ShapeDtypeStruct((b, t, nq), BF16),
                   jax.ShapeDtypeStruct((b, SWA_KV_HEADS, t, LANES), BF16),
                   jax.ShapeDtypeStruct((b, SWA_KV_HEADS, t, LANES), BF16),
                   jax.ShapeDtypeStruct((b, t, nq), BF16)),
        grid=(b, nt),
        in_specs=[pl.BlockSpec((None, ROW_TILE, d), row), full(g),
                  pl.BlockSpec((None, None, 3, d), lambda bi, i: (bi, i // n_lat_tiles, 0, 0)),
                  full(w),
                  pl.BlockSpec((ROW_TILE, LANES), lambda bi, i: (i, 0)),
                  pl.BlockSpec((ROW_TILE, LANES), lambda bi, i: (i, 0))],
        out_specs=(pl.BlockSpec((None, ROW_TILE, nq), row), kv, kv, pl.BlockSpec((None, ROW_TILE, nq), row)),
        compiler_params=_cparams(2), name="swa_proj",
    )(x, g, mod, w, cos, sin)


def _swa_attn_kernel(sink_ref, q_ref, k_ref, v_ref, o_ref, *, n_lat_tiles, n_lat, n_ctx):
    g = pl.program_id(1)
    qi = pl.program_id(2)
    tq = q_ref.shape[0]
    span = tq + 2 * SWA_WINDOW
    lane_half = lax.broadcasted_iota(jnp.int32, (tq, LANES), 1) // SWA_HEAD_DIM
    kc = k_ref[pl.ds(n_lat, n_ctx), :]
    vc = v_ref[pl.ds(n_lat, n_ctx), :]

    def head_query(hh):
        qcol = q_ref[:, (hh // 2) * LANES:(hh // 2 + 1) * LANES]
        return jnp.where(lane_half == hh % 2, qcol, jnp.zeros_like(qcol))

    def finish(outs):
        for col in range(SWA_GROUP // 2):
            o_ref[:, col * LANES:(col + 1) * LANES] = jnp.where(
                lane_half == 0, outs[2 * col], outs[2 * col + 1]).astype(o_ref.dtype)

    @pl.when(qi < n_lat_tiles)
    def _():
        start = pl.multiple_of(jnp.clip(qi * tq - SWA_WINDOW, 0, n_lat - span), SWA_WINDOW)
        kw = k_ref[pl.ds(start, span), :]
        vw = v_ref[pl.ds(start, span), :]
        qpos = qi * tq + lax.broadcasted_iota(jnp.int32, (tq, 1), 0)
        kpos = start + lax.broadcasted_iota(jnp.int32, (1, span), 1)
        band = jnp.abs(kpos - qpos) <= SWA_WINDOW
        outs = []
        for hh in range(SWA_GROUP):
            qm = head_query(hh)
            sink = sink_ref[g * SWA_GROUP + hh]
            sw = jnp.where(band, _dot_nt(qm, kw), NEG_INF)
            sc = _dot_nt(qm, kc)
            m = jnp.maximum(jnp.maximum(jnp.max(sw, axis=-1, keepdims=True),
                                        jnp.max(sc, axis=-1, keepdims=True)), sink)
            pw = jnp.exp(sw - m)
            pc = jnp.exp(sc - m)
            l = jnp.sum(pw, axis=-1, keepdims=True) + jnp.sum(pc, axis=-1, keepdims=True) + jnp.exp(sink - m)
            outs.append((_dot(pw.astype(BF16), vw) + _dot(pc.astype(BF16), vc)) / l)
        finish(outs)

    @pl.when(qi >= n_lat_tiles)
    def _():
        outs = []
        for hh in range(SWA_GROUP):
            sink = sink_ref[g * SWA_GROUP + hh]
            sc = _dot_nt(head_query(hh), kc)
            m = jnp.maximum(jnp.max(sc, axis=-1, keepdims=True), sink)
            pc = jnp.exp(sc - m)
            l = jnp.sum(pc, axis=-1, keepdims=True) + jnp.exp(sink - m)
            outs.append(_dot(pc.astype(BF16), vc) / l)
        finish(outs)


def _swa_attn(sink, q, k, v, n_lat, n_ctx):
    b, t, nq = q.shape
    nt = t // ROW_TILE
    gw = SWA_GROUP * SWA_HEAD_DIM
    kern = functools.partial(_swa_attn_kernel, n_lat_tiles=n_lat // ROW_TILE, n_lat=n_lat, n_ctx=n_ctx)
    kv = pl.BlockSpec((None, None, t, LANES), lambda bi, gi, i: (bi, gi, 0, 0))
    return pl.pallas_call(
        kern,
        out_shape=jax.ShapeDtypeStruct((b, t, nq), BF16),
        grid=(b, SWA_KV_HEADS, nt),
        in_specs=[pl.BlockSpec(memory_space=pltpu.SMEM),
                  pl.BlockSpec((None, ROW_TILE, gw), lambda bi, gi, i: (bi, i, gi)), kv, kv],
        out_specs=pl.BlockSpec((None, ROW_TILE, gw), lambda bi, gi, i: (bi, i, gi)),
        compiler_params=_cparams(3), name="swa_attn",
    )(sink, q, k, v)


def _cf_proj_kernel(x_ref, g_ref, mod_ref, w_ref, u_ref, sg_ref):
    h = _norm_mod(x_ref[...], g_ref[...], mod_ref[...]).astype(BF16)
    res = _dot(h, w_ref[...])
    c = u_ref.shape[-1]
    u_ref[...] = res[:, :c] * jax.nn.sigmoid(res[:, c:2 * c])
    sg_ref[...] = _silu(res[:, 2 * c:]).astype(sg_ref.dtype)


def _cf_proj(x, g, mod, w, n_lat):
    b, _, d = x.shape
    c = w.shape[1] // 3
    full = lambda a: pl.BlockSpec(a.shape, lambda bi, i: (0,) * a.ndim)
    row = lambda bi, i: (bi, i, 0)
    return pl.pallas_call(
        _cf_proj_kernel,
        out_shape=(jax.ShapeDtypeStruct((b, n_lat, c), F32), jax.ShapeDtypeStruct((b, n_lat, c), BF16)),
        grid=(b, n_lat // ROW_TILE),
        in_specs=[pl.BlockSpec((None, ROW_TILE, d), row), full(g),
                  pl.BlockSpec((None, None, 3, d), lambda bi, i: (bi, 0, 0, 0)), full(w)],
        out_specs=(pl.BlockSpec((None, ROW_TILE, c), row),) * 2,
        compiler_params=_cparams(2), name="cf_proj",
    )(x, g, mod, w)


def _cf_tail_kernel(up_ref, u_ref, un_ref, sg_ref, x_ref, mod_ref, dw_ref, db_ref, lg_ref, lb_ref,
                    w_ref, fg_ref, o_ref, ext_ref):
    i = pl.program_id(1)
    nt = pl.num_programs(1)
    tm = u_ref.shape[0]
    zero = jnp.zeros(up_ref.shape, F32)
    ext_ref[0:CF_HALO, :] = jnp.where(i > 0, up_ref[...], zero)
    ext_ref[CF_HALO:CF_HALO + tm, :] = u_ref[...]
    ext_ref[CF_HALO + tm:, :] = jnp.where(i < nt - 1, un_ref[...], zero)
    pad = (CF_KERNEL - 1) // 2
    acc = jnp.zeros(u_ref.shape, F32) + db_ref[...]
    for k in range(CF_KERNEL):
        acc = acc + dw_ref[k:k + 1, :] * ext_ref[pl.ds(CF_HALO - pad + k, tm), :]
    mu = jnp.mean(acc, axis=-1, keepdims=True)
    xc = acc - mu
    var = jnp.mean(xc * xc, axis=-1, keepdims=True)
    ln = xc * lax.rsqrt(var + NORM_EPS) * lg_ref[...] + lb_ref[...]
    og = (_silu(ln) * sg_ref[...].astype(F32)).astype(BF16)
    xo = x_ref[...] + mod_ref[2:3, :] * _dot(og, w_ref[...])
    o_ref[...] = _rms(xo, fg_ref[...])


def _cf_tail(u, sg, x, mod, dw, db, lg, lb, w_out, fg):
    b, n_lat, c = u.shape
    d = x.shape[-1]
    nt = n_lat // ROW_TILE
    per = ROW_TILE // CF_HALO
    last_halo = n_lat // CF_HALO - 1
    full = lambda a: pl.BlockSpec(a.shape, lambda bi, i: (0,) * a.ndim)
    row = lambda bi, i: (bi, i, 0)
    return pl.pallas_call(
        _cf_tail_kernel,
        out_shape=jax.ShapeDtypeStruct((b, n_lat, d), F32),
        grid=(b, nt),
        in_specs=[pl.BlockSpec((None, CF_HALO, c), lambda bi, i: (bi, jnp.maximum(i * per - 1, 0), 0)),
                  pl.BlockSpec((None, ROW_TILE, c), row),
                  pl.BlockSpec((None, CF_HALO, c), lambda bi, i: (bi, jnp.minimum((i + 1) * per, last_halo), 0)),
                  pl.BlockSpec((None, ROW_TILE, c), row),
                  pl.BlockSpec((None, ROW_TILE, d), row),
                  pl.BlockSpec((None, None, 3, d), lambda bi, i: (bi, 0, 0, 0)),
                  full(dw), full(db), full(lg), full(lb), full(w_out), full(fg)],
        out_specs=pl.BlockSpec((None, ROW_TILE, d), row),
        scratch_shapes=[pltpu.VMEM((ROW_TILE + 2 * CF_HALO, c), F32)],
        compiler_params=_cparams(2), name="cf_tail",
    )(u, u, u, sg, x, mod, dw, db, lg, lb, w_out, fg)


def _rope_tables(n_lat, n_ctx):
    rows = n_lat // GRID_W
    row = jnp.repeat(jnp.arange(rows, dtype=F32), GRID_W)
    col = jnp.tile(jnp.arange(GRID_W, dtype=F32), rows)
    n_freq = ROPE_DIM // 4
    inv = ROPE_BASE ** (-jnp.arange(n_freq, dtype=F32) / n_freq)
    ang = jnp.concatenate([row[:, None] * inv, col[:, None] * inv], axis=-1)
    cos = jnp.concatenate([jnp.cos(ang), jnp.ones((n_ctx, ROPE_DIM // 2), F32)], axis=0)
    sin = jnp.concatenate([jnp.sin(ang), jnp.zeros((n_ctx, ROPE_DIM // 2), F32)], axis=0)
    return jnp.tile(cos, (1, 2)), jnp.tile(sin, (1, 2))


def kernel(x, c, ctx, c_ctx, norm_g, ada_w, ada_b, final_g, mla_w_in, mla_q_norm_g, mla_kv_norm_g, mla_w_uq, mla_w_ukv, mla_w_out, hy_w_in, hy_conv_w, hy_conv_b, hy_filt_w_in, hy_filt_w_hid, hy_filt_b, hy_filt_freq, hy_filt_w_out, hy_bias, hy_w_out, swa_w_in, swa_sink, swa_w_out, cf_w_in, cf_dw_w, cf_dw_b, cf_ln_g, cf_ln_b, cf_w_out):
    b, n_lat, d = x.shape
    n_ctx = ctx.shape[1]
    depth = norm_g.shape[0]
    assert depth == 4 and n_lat % ROW_TILE == 0 and n_ctx == ROW_TILE
    assert n_lat % (FFT_N1H * SUBLANES) == 0 and n_lat % GRID_W == 0
    n_lat_tiles = n_lat // ROW_TILE
    bf = lambda a: a.astype(BF16)

    xs = jnp.concatenate([x, ctx], axis=1)
    cs = jnp.concatenate([c, c_ctx[None, :], jnp.zeros((SUBLANES - b - 1, d), F32)], axis=0)
    mods = _adaln(cs, ada_w, ada_b)[:, :b + 1].reshape(depth, b + 1, 3, d)
    mods = jnp.stack([mods[:, :b], jnp.broadcast_to(mods[:, b:], (depth, b, 3, d))], axis=2)
    cos64, sin64 = _rope_tables(n_lat, n_ctx)

    c0, c1, c2 = MLA_Q_RANK, MLA_Q_RANK + MLA_KV_RANK, MLA_Q_RANK + MLA_KV_RANK + MLA_ROPE
    w_in = mla_w_in[0]
    w0 = bf(jnp.concatenate([w_in[:, :c1], w_in[:, c2:], w_in[:, c1:c2]], axis=1))
    wuq = bf(mla_w_uq[0].reshape(MLA_Q_RANK, MLA_HEADS, MLA_QK).transpose(1, 0, 2))
    wukv = bf(mla_w_ukv[0].reshape(MLA_KV_RANK, MLA_HEADS, MLA_NOPE + MLA_V).transpose(1, 0, 2))
    q, k, v, sg = _mla_proj(xs, norm_g[0:1], mods[0], w0, mla_q_norm_g[0:1], mla_kv_norm_g[0:1],
                            wuq, wukv, cos64, sin64, n_lat_tiles)
    o = _mla_attn(q, k, v, n_lat, n_ctx)
    xs = _outproj(o, sg, xs, mods[0], bf(mla_w_out[0]), n_lat_tiles)

    z0, g0, g1, sg = _hy_proj(xs, norm_g[1:2], mods[1], bf(hy_w_in[0]), hy_conv_w[0], hy_conv_b[0][None, :],
                              n_lat_tiles)
    cch = z0.shape[-1]
    filt = (hy_filt_w_in[0], hy_filt_w_hid[0], hy_filt_b[0], hy_filt_freq[0], hy_filt_w_out[0])
    f1, f2, f2i, g1m = _fft_matrices(n_lat)
    taps = _hy_filter_taps(n_lat, *filt)
    spec = _hy_spec(_hy_s1f(taps.reshape(2 * HY_ORDER, n_lat, cch), f1, n_lat), f2)
    cfwd, cinv, nfp = _dense_dft_matrices(n_ctx)
    cspec = _hy_ctx_spec(_hy_filter_taps(n_ctx, *filt), cfwd)
    gates = (g0, g1)
    z = z0
    for order in range(HY_ORDER):
        bias = hy_bias[0][order][None, :]
        bp = _hy_s2(_hy_s1f(z, f1, n_lat), spec, order, f2, f2i)
        zn = _hy_s1i(bp, g1m, z, gates[order], bias, n_lat)
        z = _hy_ctx_conv(z, gates[order], cspec, order, cfwd[:, :n_ctx], cinv, nfp, bias, zn, n_lat)
    xs = _outproj(z, sg, xs, mods[1], bf(hy_w_out[0]), n_lat_tiles)

    nq = SWA_Q_HEADS * SWA_HEAD_DIM
    nkv = SWA_KV_HEADS * SWA_HEAD_DIM
    w_in = swa_w_in[0]
    dup = lambda w: jnp.tile(w.reshape(d, SWA_KV_HEADS, 1, SWA_HEAD_DIM), (1, 1, 2, 1)).reshape(d, 2 * nkv)
    w2 = bf(jnp.concatenate([w_in[:, :nq], dup(w_in[:, nq:nq + nkv]), dup(w_in[:, nq + nkv:nq + 2 * nkv]),
                             w_in[:, nq + 2 * nkv:]], axis=1))
    cos128, sin128 = jnp.tile(cos64, (1, 2)), jnp.tile(sin64, (1, 2))
    q, k, v, sg = _swa_proj(xs, norm_g[2:3], mods[2], w2, cos128, sin128, n_lat_tiles)
    o = _swa_attn(swa_sink[0], q, k, v, n_lat, n_ctx)
    xs = _outproj(o, sg, xs, mods[2], bf(swa_w_out[0]), n_lat_tiles)

    u, sg = _cf_proj(xs, norm_g[3:4], mods[3], bf(cf_w_in[0]), n_lat)
    return _cf_tail(u, sg, xs, mods[3], cf_dw_w[0], cf_dw_b[0][None, :], cf_ln_g[0][None, :], cf_ln_b[0][None, :],
                    bf(cf_w_out[0]), final_g[None, :])
```

```python
import functools
import math

import numpy as np
import jax
import jax.numpy as jnp
from jax import lax
from jax.experimental import pallas as pl
from jax.experimental.pallas import tpu as pltpu

F32 = jnp.float32
BF16 = jnp.bfloat16

GRID_W = 64
NORM_EPS = 1e-6
NEG_INF = -1e30
ROPE_BASE = 10000.0
ROPE_DIM = 64

MLA_HEADS = 8
MLA_Q_RANK = 384
MLA_KV_RANK = 256
MLA_NOPE = 128
MLA_ROPE = ROPE_DIM
MLA_V = 128
MLA_QK = MLA_NOPE + MLA_ROPE

HY_ORDER = 2
HY_EMB = 33
HY_EMB_PAD = 64
HY_BANDS = (HY_EMB - 1) // 2
HY_DECAY_TARGET = 1e-2
HY_FAST_DECAY = 0.3
HY_SLOW_DECAY = 1.5

SWA_Q_HEADS = 16
SWA_KV_HEADS = 4
SWA_GROUP = SWA_Q_HEADS // SWA_KV_HEADS
SWA_HEAD_DIM = ROPE_DIM
SWA_WINDOW = 128

CF_KERNEL = 31
CF_HALO = 16

ROW_TILE = 256
SUBLANES = 8
LANES = 128
FFT_N1 = 64
FFT_N1H = FFT_N1 // 2
FFT_K1 = FFT_N1 // 2 + 1
HY_SLAB = 256
VMEM_LIMIT = 48 * 1024 * 1024
VMEM_LIMIT_BIG = 56 * 1024 * 1024


def _cparams(n_axes):
    return pltpu.CompilerParams(dimension_semantics=("arbitrary",) * n_axes,
                                vmem_limit_bytes=VMEM_LIMIT)


def _cparams_big(n_axes):
    return pltpu.CompilerParams(dimension_semantics=("arbitrary",) * n_axes,
                                vmem_limit_bytes=VMEM_LIMIT_BIG)


def _dot(a, b):
    return jnp.dot(a, b, preferred_element_type=F32)


def _dot_nt(a, b):
    return lax.dot_general(a, b, (((1,), (1,)), ((), ())), preferred_element_type=F32)


def _rms(x, g):
    return x * lax.rsqrt(jnp.mean(x * x, axis=-1, keepdims=True) + NORM_EPS) * g


def _norm_mod(x, g, mod):
    return _rms(x, g) * (1.0 + mod[1:2, :]) + mod[0:1, :]


def _silu(x):
    return x * jax.nn.sigmoid(x)


def _adaln_kernel(c_ref, w_ref, b_ref, o_ref):
    s = _silu(c_ref[...]).astype(BF16)
    o_ref[...] = _dot(s, w_ref[...].astype(BF16)) + b_ref[...]


def _adaln(cs, ada_w, ada_b):
    depth, d, d3 = ada_w.shape
    tn = 512
    return pl.pallas_call(
        _adaln_kernel,
        out_shape=jax.ShapeDtypeStruct((depth, cs.shape[0], d3), F32),
        grid=(depth, d3 // tn),
        in_specs=[pl.BlockSpec(cs.shape, lambda i, j: (0, 0)),
                  pl.BlockSpec((None, d, tn), lambda i, j: (i, 0, j)),
                  pl.BlockSpec((None, 1, tn), lambda i, j: (i, 0, j))],
        out_specs=pl.BlockSpec((None, cs.shape[0], tn), lambda i, j: (i, 0, j)),
        compiler_params=_cparams(2), name="adaln",
    )(cs, ada_w, ada_b.reshape(depth, 1, d3))


def _outproj_kernel(o_ref, sg_ref, x_ref, mod_ref, w_ref, xo_ref):
    og = (o_ref[...].astype(F32) * sg_ref[...].astype(F32)).astype(BF16)
    xo_ref[...] = x_ref[...] + mod_ref[2:3, :] * _dot(og, w_ref[...])


def _outproj(o, sg, x, mod, w_out, n_lat_tiles):
    b, t, d = x.shape
    width = o.shape[-1]
    nt = t // ROW_TILE
    row = lambda bi, i: (bi, i, 0)
    return pl.pallas_call(
        _outproj_kernel,
        out_shape=jax.ShapeDtypeStruct(x.shape, F32),
        grid=(b, nt),
        in_specs=[pl.BlockSpec((None, ROW_TILE, width), row),
                  pl.BlockSpec((None, ROW_TILE, width), row),
                  pl.BlockSpec((None, ROW_TILE, d), row),
                  pl.BlockSpec((None, None, 3, d), lambda bi, i: (bi, i // n_lat_tiles, 0, 0)),
                  pl.BlockSpec(w_out.shape, lambda bi, i: (0, 0))],
        out_specs=pl.BlockSpec((None, ROW_TILE, d), row),
        compiler_params=_cparams(2), name="outproj",
    )(o, sg, x, mod, w_out)


def _rope_swap_matrix():
    half = ROPE_DIM // 2
    p = np.zeros((ROPE_DIM, ROPE_DIM), np.float32)
    for j in range(half):
        p[j + half, j] = -1.0
        p[j, j + half] = 1.0
    return p


def _mla_proj_kernel(x_ref, g_ref, mod_ref, w_ref, qg_ref, kvg_ref, wuq_ref, wukv_ref,
                     cos_ref, sin_ref, p_ref, q_ref, k_ref, v_ref, sg_ref):
    h = _norm_mod(x_ref[...], g_ref[...], mod_ref[...]).astype(BF16)
    res = _dot(h, w_ref[...])
    c0, c1 = MLA_Q_RANK, MLA_Q_RANK + MLA_KV_RANK
    c2 = c1 + sg_ref.shape[-1]
    cqn = _rms(res[:, :c0], qg_ref[...]).astype(BF16)
    ckvn = _rms(res[:, c0:c1], kvg_ref[...]).astype(BF16)
    sg_ref[...] = _silu(res[:, c1:c2]).astype(sg_ref.dtype)
    cos, sin, swap = cos_ref[...], sin_ref[...], p_ref[...]

    def rope(t):
        return t * cos + _dot(t.astype(BF16), swap) * sin

    k_rope_t = rope(res[:, c2:c2 + MLA_ROPE]).T.astype(k_ref.dtype)
    scale = MLA_QK ** -0.5
    for hd in range(MLA_HEADS):
        q = _dot(cqn, wuq_ref[hd])
        q_ref[hd, :, :MLA_NOPE] = (q[:, :MLA_NOPE] * scale).astype(q_ref.dtype)
        q_ref[hd, :, MLA_NOPE:] = (rope(q[:, MLA_NOPE:]) * scale).astype(q_ref.dtype)
        kv = _dot(ckvn, wukv_ref[hd])
        k_ref[hd, :MLA_NOPE, :] = kv[:, :MLA_NOPE].T.astype(k_ref.dtype)
        k_ref[hd, MLA_NOPE:, :] = k_rope_t
        v_ref[hd, :, :MLA_V] = kv[:, MLA_NOPE:].astype(v_ref.dtype)
        v_ref[hd, :, MLA_V:] = jnp.ones((v_ref.shape[1], MLA_V), v_ref.dtype)


def _mla_proj(x, g, mod, w, qg, kvg, wuq, wukv, cos, sin, n_lat_tiles):
    b, t, d = x.shape
    nt = t // ROW_TILE
    width = MLA_HEADS * MLA_V
    swap = jnp.asarray(_rope_swap_matrix()).astype(BF16)
    full = lambda a: pl.BlockSpec(a.shape, lambda bi, i: (0,) * a.ndim)
    head_out = lambda n: pl.BlockSpec((None, MLA_HEADS, ROW_TILE, n), lambda bi, i: (bi, 0, i, 0))
    return pl.pallas_call(
        _mla_proj_kernel,
        out_shape=(jax.ShapeDtypeStruct((b, MLA_HEADS, t, MLA_QK), BF16),
                   jax.ShapeDtypeStruct((b, MLA_HEADS, MLA_QK, t), BF16),
                   jax.ShapeDtypeStruct((b, MLA_HEADS, t, 2 * MLA_V), BF16),
                   jax.ShapeDtypeStruct((b, t, width), BF16)),
        grid=(b, nt),
        in_specs=[pl.BlockSpec((None, ROW_TILE, d), lambda bi, i: (bi, i, 0)),
                  full(g),
                  pl.BlockSpec((None, None, 3, d), lambda bi, i: (bi, i // n_lat_tiles, 0, 0)),
                  full(w), full(qg), full(kvg), full(wuq), full(wukv),
                  pl.BlockSpec((ROW_TILE, ROPE_DIM), lambda bi, i: (i, 0)),
                  pl.BlockSpec((ROW_TILE, ROPE_DIM), lambda bi, i: (i, 0)),
                  full(swap)],
        out_specs=(head_out(MLA_QK),
                   pl.BlockSpec((None, MLA_HEADS, MLA_QK, ROW_TILE), lambda bi, i: (bi, 0, 0, i)),
                   head_out(2 * MLA_V),
                   pl.BlockSpec((None, ROW_TILE, width), lambda bi, i: (bi, i, 0))),
        compiler_params=_cparams(2), name="mla_proj",
    )(x, g, mod, w, qg, kvg, wuq, wukv, cos, sin, swap)


def _attend(q, k_parts, v_parts):
    s = [_dot(q, k) for k in k_parts]
    m = functools.reduce(jnp.maximum, [jnp.max(x, axis=-1, keepdims=True) for x in s])
    ov = functools.reduce(jnp.add, [_dot(jnp.exp(x - m).astype(BF16), v) for x, v in zip(s, v_parts)])
    n = ov.shape[-1] // 2
    return ov[:, :n] / ov[:, n:]


MLA_HEADS_PER_STEP = 2


def _mla_attn_kernel(q_ref, k_ref, v_ref, o_ref, *, n_lat_tiles, n_lat, n_ctx):
    qi = pl.program_id(2)

    @pl.when(qi < n_lat_tiles)
    def _():
        for hd in range(MLA_HEADS_PER_STEP):
            o = _attend(q_ref[hd], [k_ref[hd]], [v_ref[hd]])
            o_ref[:, hd * MLA_V:(hd + 1) * MLA_V] = o.astype(o_ref.dtype)

    @pl.when(qi >= n_lat_tiles)
    def _():
        for hd in range(MLA_HEADS_PER_STEP):
            k = k_ref[hd, :, pl.ds(n_lat, n_ctx)]
            v = v_ref[hd, pl.ds(n_lat, n_ctx), :]
            o_ref[:, hd * MLA_V:(hd + 1) * MLA_V] = _attend(q_ref[hd], [k], [v]).astype(o_ref.dtype)


def _mla_attn(q, k, v, n_lat, n_ctx):
    b, hds, t, _ = q.shape
    nt = t // ROW_TILE
    hps = MLA_HEADS_PER_STEP
    kern = functools.partial(_mla_attn_kernel, n_lat_tiles=n_lat // ROW_TILE, n_lat=n_lat, n_ctx=n_ctx)
    return pl.pallas_call(
        kern,
        out_shape=jax.ShapeDtypeStruct((b, t, hds * MLA_V), BF16),
        grid=(b, hds // hps, nt),
        in_specs=[pl.BlockSpec((None, hps, ROW_TILE, MLA_QK), lambda bi, h, i: (bi, h, i, 0)),
                  pl.BlockSpec((None, hps, MLA_QK, t), lambda bi, h, i: (bi, h, 0, 0)),
                  pl.BlockSpec((None, hps, t, 2 * MLA_V), lambda bi, h, i: (bi, h, 0, 0))],
        out_specs=pl.BlockSpec((None, ROW_TILE, hps * MLA_V), lambda bi, h, i: (bi, i, h)),
        compiler_params=_cparams(3), name="mla_attn",
    )(q, k, v)


def _hy_proj_kernel(xp_ref, x_ref, xn_ref, g_ref, mod_ref, w_ref, cw_ref, cb_ref,
                    z_ref, g0_ref, g1_ref, sg_ref, *, n_lat_tiles):
    i = pl.program_id(1)
    nt = pl.num_programs(1)
    tm = x_ref.shape[0]
    c = z_ref.shape[-1]
    xa = jnp.concatenate([xp_ref[...], x_ref[...], xn_ref[...]], axis=0)
    h = _norm_mod(xa, g_ref[...], mod_ref[...]).astype(BF16)
    res = _dot(h, w_ref[...])
    nu = 3 * c
    first = jnp.logical_or(i == 0, i == n_lat_tiles)
    last = jnp.logical_or(i == n_lat_tiles - 1, i == nt - 1)
    rows = lax.broadcasted_iota(jnp.int32, (tm + 2 * SUBLANES, 1), 0)
    valid = jnp.logical_and(jnp.logical_or(rows >= SUBLANES, jnp.logical_not(first)),
                            jnp.logical_or(rows < tm + SUBLANES, jnp.logical_not(last)))
    u = jnp.where(valid, res[:, :nu], 0.0)
    n = tm + 2 * SUBLANES
    up = pltpu.roll(u, 1, 0)
    dn = pltpu.roll(u, n - 1, 0)
    cw = cw_ref[...]
    cv = (cw[0:1, :] * up + cw[1:2, :] * u + cw[2:3, :] * dn)[SUBLANES:SUBLANES + tm] + cb_ref[...]
    z_ref[...] = cv[:, :c]
    g0_ref[...] = cv[:, c:2 * c]
    g1_ref[...] = cv[:, 2 * c:]
    sg_ref[...] = _silu(res[SUBLANES:SUBLANES + tm, nu:]).astype(sg_ref.dtype)


def _hy_proj(x, g, mod, w, cw, cb, n_lat_tiles):
    b, t, d = x.shape
    nt = t // ROW_TILE
    c = w.shape[1] // (HY_ORDER + 2)
    per = ROW_TILE // SUBLANES
    last_halo = t // SUBLANES - 1
    full = lambda a: pl.BlockSpec(a.shape, lambda bi, i: (0,) * a.ndim)
    row = lambda bi, i: (bi, i, 0)
    kern = functools.partial(_hy_proj_kernel, n_lat_tiles=n_lat_tiles)
    return pl.pallas_call(
        kern,
        out_shape=(jax.ShapeDtypeStruct((b, t, c), F32),) * 3 + (jax.ShapeDtypeStruct((b, t, c), BF16),),
        grid=(b, nt),
        in_specs=[pl.BlockSpec((None, SUBLANES, d), lambda bi, i: (bi, jnp.maximum(i * per - 1, 0), 0)),
                  pl.BlockSpec((None, ROW_TILE, d), row),
                  pl.BlockSpec((None, SUBLANES, d), lambda bi, i: (bi, jnp.minimum((i + 1) * per, last_halo), 0)),
                  full(g),
                  pl.BlockSpec((None, None, 3, d), lambda bi, i: (bi, i // n_lat_tiles, 0, 0)),
                  full(w), full(cw), full(cb)],
        out_specs=(pl.BlockSpec((None, ROW_TILE, c), row),) * 4,
        compiler_params=_cparams(2), name="hy_proj",
    )(x, x, x, g, mod, w, cw, cb)


def _hy_filter_kernel(ef_ref, eb_ref, w0_ref, wh_ref, b_ref, fr_ref, wf_ref, wb_ref, dl_ref, o_ref):
    i = pl.program_id(0)
    c = dl_ref.shape[-1]

    def taps(e, wo):
        hdn = e
        ws = (w0_ref[...], wh_ref[0], wh_ref[1])
        for k in range(3):
            hdn = jnp.sin(fr_ref[k:k + 1, :] * (_dot(hdn.astype(BF16), ws[k].astype(BF16)) + b_ref[k:k + 1, :]))
        decay = jnp.exp(-e[:, 0:1] * dl_ref[...])
        hh = _dot(hdn.astype(BF16), wo.astype(BF16))
        return [hh[:, o * c:(o + 1) * c] * decay for o in range(HY_ORDER)]

    hf = taps(ef_ref[...], wf_ref[...])
    hb = taps(eb_ref[...], wb_ref[...])
    rows = lax.broadcasted_iota(jnp.int32, (ef_ref.shape[0], 1), 0)
    lag0 = jnp.logical_and(rows == 0, i == 0)
    for o in range(HY_ORDER):
        o_ref[o, 0] = hf[o] + jnp.where(lag0, hb[o], 0.0)
        o_ref[o, 1] = jnp.where(lag0, 0.0, hb[o])


def _hy_position_features(n):
    t = jnp.linspace(0.0, 1.0, n, dtype=F32)[:, None]
    wpos = (2.0 * math.pi / n) * jnp.arange(n, dtype=F32)[:, None]
    bands = jnp.linspace(1e-4, HY_BANDS - 1, HY_BANDS, dtype=F32)[None, :]
    hdn = jnp.concatenate([t, jnp.cos(bands * wpos), -jnp.sin(bands * wpos)], axis=-1)
    return jnp.pad(hdn, ((0, 0), (0, HY_EMB_PAD - HY_EMB)))


def _hy_filter_taps(n, w_in, w_hid, bias, freq, w_out):
    c = w_out.shape[1] // (2 * HY_ORDER)
    ef = _hy_position_features(n)
    rev = (n - jnp.arange(n)) % n
    eb = ef[rev]
    w0 = jnp.pad(w_in, ((0, HY_EMB_PAD - HY_EMB), (0, 0)))
    wo = w_out.reshape(w_out.shape[0], HY_ORDER, 2, c)
    wf = wo[:, :, 0, :].reshape(w_out.shape[0], HY_ORDER * c)
    wb = wo[:, :, 1, :].reshape(w_out.shape[0], HY_ORDER * c)
    deltas = jnp.abs(jnp.linspace(math.log(HY_DECAY_TARGET) / HY_FAST_DECAY,
                                  math.log(HY_DECAY_TARGET) / HY_SLOW_DECAY, c, dtype=F32))[None, :]
    tl = min(n, 512)
    full = lambda a: pl.BlockSpec(a.shape, lambda i: (0,) * a.ndim)
    return pl.pallas_call(
        _hy_filter_kernel,
        out_shape=jax.ShapeDtypeStruct((HY_ORDER, 2, n, c), F32),
        grid=(n // tl,),
        in_specs=[pl.BlockSpec((tl, HY_EMB_PAD), lambda i: (i, 0)),
                  pl.BlockSpec((tl, HY_EMB_PAD), lambda i: (i, 0)),
                  full(w0), full(w_hid), full(bias), full(freq), full(wf), full(wb), full(deltas)],
        out_specs=pl.BlockSpec((HY_ORDER, 2, tl, c), lambda i: (0, 0, i, 0)),
        compiler_params=_cparams(1), name="hy_filter",
    )(ef, eb, w0, w_hid, bias, freq, wf, wb, deltas)


def _fft_matrices(n_lat):
    n = 2 * n_lat
    n2 = n_lat // FFT_N1H
    k1 = np.arange(FFT_K1)
    n1 = np.arange(FFT_N1H)
    ang = 2.0 * np.pi * np.outer(k1, n1) / FFT_N1
    eye = np.eye(SUBLANES)
    f1 = np.kron(np.concatenate([np.cos(ang), -np.sin(ang)], axis=0), eye)
    wgt = np.full(FFT_K1, 2.0)
    wgt[0] = wgt[-1] = 1.0
    g1 = np.kron(np.concatenate([wgt * np.cos(ang.T), -wgt * np.sin(ang.T)], axis=1) / n, eye)
    idx = np.arange(n2)
    f2 = np.zeros((FFT_K1, 2 * n2, 2 * n2))
    f2i = np.zeros((FFT_K1, 2 * n2, 2 * n2))
    for a in range(FFT_K1):
        ph = 2.0 * np.pi * (np.outer(idx, idx) / n2 + (a * idx)[None, :] / n)
        f2[a] = np.block([[np.cos(ph), np.sin(ph)], [-np.sin(ph), np.cos(ph)]])
        f2i[a] = np.block([[np.cos(ph.T), -np.sin(ph.T)], [np.sin(ph.T), np.cos(ph.T)]])
    sign = np.kron(np.tile((-1.0) ** k1, 2), np.ones(SUBLANES))[:, None]
    f1s = np.concatenate([f1, sign * f1], axis=1)
    as_bf16 = lambda m: jnp.asarray(m.astype(np.float32)).astype(BF16)
    return as_bf16(f1), as_bf16(f1s), as_bf16(f2), as_bf16(f2i), as_bf16(g1)


def _dense_dft_matrices(n_ctx):
    n = 2 * n_ctx
    nf = n_ctx + 1
    nfp = -(-nf // SUBLANES) * SUBLANES
    k = np.arange(nf)
    ang = 2.0 * np.pi * np.outer(k, np.arange(n)) / n
    fwd = np.zeros((2 * nfp, n))
    fwd[:nf] = np.cos(ang)
    fwd[nfp:nfp + nf] = -np.sin(ang)
    wgt = np.full(nf, 2.0)
    wgt[0] = wgt[-1] = 1.0
    angi = 2.0 * np.pi * np.outer(np.arange(n_ctx), k) / n
    inv = np.zeros((n_ctx, 2 * nfp))
    inv[:, :nf] = wgt * np.cos(angi) / n
    inv[:, nfp:nfp + nf] = -wgt * np.sin(angi) / n
    as_bf16 = lambda m: jnp.asarray(m.astype(np.float32)).astype(BF16)
    return as_bf16(fwd), as_bf16(inv), nfp


def _once(shape, index_map):
    return pl.BlockSpec(shape, index_map, pipeline_mode=pl.Buffered(1))


def _hy_spec_kernel(t_ref, f1s_ref, f2_ref, h_ref, a_ref):
    _, n1h, nj, s, tc = t_ref.shape
    n2 = nj * s

    def stage1(j, carry):
        sig = jnp.concatenate([t_ref[0, :, j].reshape(n1h * s, tc), t_ref[1, :, j].reshape(n1h * s, tc)], axis=0)
        a_ref[:, :, j] = _dot(f1s_ref[...], sig.astype(BF16)).reshape(2, FFT_K1, s, tc)
        return carry

    lax.fori_loop(0, nj, stage1, 0, unroll=2)

    def stage2(k, carry):
        a = a_ref[:, k].reshape(2 * n2, tc).astype(BF16)
        h_ref[:, k] = _dot(f2_ref[k], a).reshape(2, n2, tc).astype(h_ref.dtype)
        return carry

    lax.fori_loop(0, FFT_K1, stage2, 0, unroll=3)


def _hy_spec(taps, f1s, f2):
    _, _, n_lat, c = taps.shape
    n2 = n_lat // FFT_N1H
    nj = n2 // SUBLANES
    tc = HY_SLAB
    return pl.pallas_call(
        _hy_spec_kernel,
        out_shape=jax.ShapeDtypeStruct((HY_ORDER, 2, FFT_K1, n2, c), BF16),
        grid=(c // tc, HY_ORDER),
        in_specs=[pl.BlockSpec((None, 2, FFT_N1H, nj, SUBLANES, tc), lambda ci, o: (o, 0, 0, 0, 0, ci)),
                  _once(f1s.shape, lambda ci, o: (0, 0)),
                  _once(f2.shape, lambda ci, o: (0, 0, 0))],
        out_specs=pl.BlockSpec((None, 2, FFT_K1, n2, tc), lambda ci, o: (o, 0, 0, 0, ci)),
        scratch_shapes=[pltpu.VMEM((2, FFT_K1, nj, SUBLANES, tc), F32)],
        compiler_params=_cparams_big(2), name="hy_spec",
    )(taps.reshape(HY_ORDER, 2, FFT_N1H, nj, SUBLANES, c), f1s, f2)


def _hy_conv_kernel(z_ref, gt_ref, h_ref, f1_ref, f2_ref, f2i_ref, g1_ref, bias_ref, o_ref, a_ref):
    n1h, nj, s, tc = z_ref.shape
    n2 = nj * s

    def stage1(j, carry):
        zz = z_ref[:, j].reshape(n1h * s, tc).astype(BF16)
        a_ref[:, :, j] = _dot(f1_ref[...], zz).reshape(2, FFT_K1, s, tc)
        return carry

    lax.fori_loop(0, nj, stage1, 0, unroll=2)

    def stage2(k, carry):
        x = _dot(f2_ref[k], a_ref[:, k].reshape(2 * n2, tc).astype(BF16))
        xr, xi = x[:n2], x[n2:]
        hr, hi = h_ref[0, k].astype(F32), h_ref[1, k].astype(F32)
        y = jnp.concatenate([xr * hr - xi * hi, xr * hi + xi * hr], axis=0).astype(BF16)
        a_ref[:, k] = _dot(f2i_ref[k], y).reshape(2, nj, s, tc)
        return carry

    lax.fori_loop(0, FFT_K1, stage2, 0, unroll=3)

    def stage3(j, carry):
        bp = a_ref[:, :, j].reshape(2 * FFT_K1 * s, tc).astype(BF16)
        y = _dot(g1_ref[...], bp).reshape(n1h, s, tc)
        o_ref[:, j] = gt_ref[:, j] * (y + z_ref[:, j] * bias_ref[...])
        return carry

    lax.fori_loop(0, nj, stage3, 0, unroll=2)


def _hy_conv(z, gate, h, order, mats, bias, n_lat):
    f1, _, f2, f2i, g1 = mats
    bz, t, c = z.shape
    n2 = n_lat // FFT_N1H
    nj = n2 // SUBLANES
    tc = HY_SLAB
    view = (bz, t // n2, nj, SUBLANES, c)
    slab = pl.BlockSpec((None, FFT_N1H, nj, SUBLANES, tc), lambda ci, bi: (bi, 0, 0, 0, ci))
    out = pl.pallas_call(
        _hy_conv_kernel,
        out_shape=jax.ShapeDtypeStruct(view, F32),
        grid=(c // tc, bz),
        in_specs=[slab, slab,
                  _once((None, 2, FFT_K1, n2, tc), lambda ci, bi: (order, 0, 0, 0, ci)),
                  _once(f1.shape, lambda ci, bi: (0, 0)),
                  _once(f2.shape, lambda ci, bi: (0, 0, 0)),
                  _once(f2i.shape, lambda ci, bi: (0, 0, 0)),
                  _once(g1.shape, lambda ci, bi: (0, 0)),
                  pl.BlockSpec((1, tc), lambda ci, bi: (0, ci))],
        out_specs=slab,
        scratch_shapes=[pltpu.VMEM((2, FFT_K1, nj, SUBLANES, tc), F32)],
        compiler_params=_cparams_big(2), name="hy_conv",
    )(z.reshape(view), gate.reshape(view), h, f1, f2, f2i, g1, bias)
    return out.reshape(bz, t, c)


def _hy_ctx_spec_kernel(t_ref, f_ref, h_ref):
    sig = jnp.concatenate([t_ref[0], t_ref[1]], axis=0).astype(BF16)
    h_ref[...] = _dot(f_ref[...], sig)


def _hy_ctx_spec(taps, fwd):
    _, _, n_ctx, c = taps.shape
    return pl.pallas_call(
        _hy_ctx_spec_kernel,
        out_shape=jax.ShapeDtypeStruct((HY_ORDER, fwd.shape[0], c), F32),
        grid=(HY_ORDER,),
        in_specs=[pl.BlockSpec((None, 2, n_ctx, c), lambda o: (o, 0, 0, 0)),
                  pl.BlockSpec(fwd.shape, lambda o: (0, 0))],
        out_specs=pl.BlockSpec((None, fwd.shape[0], c), lambda o: (o, 0, 0)),
        compiler_params=_cparams(1), name="hy_ctx_spec",
    )(taps, fwd)


def _hy_ctx_conv_kernel(zin_ref, gt_ref, h_ref, f_ref, fi_ref, bias_ref, dst_ref, o_ref, *, nfp):
    del dst_ref
    z = zin_ref[...]
    x = _dot(f_ref[...], z.astype(BF16))
    xr, xi = x[:nfp], x[nfp:]
    hr, hi = h_ref[:nfp, :], h_ref[nfp:, :]
    y = jnp.concatenate([xr * hr - xi * hi, xr * hi + xi * hr], axis=0).astype(BF16)
    o_ref[...] = gt_ref[...] * (_dot(fi_ref[...], y) + z * bias_ref[...])


def _hy_ctx_conv(z, gate, h, order, fwd_half, inv, nfp, bias, dst, n_lat):
    bz, t, c = z.shape
    n_ctx = t - n_lat
    blk = pl.BlockSpec((None, n_ctx, c), lambda bi: (bi, n_lat // n_ctx, 0))
    kern = functools.partial(_hy_ctx_conv_kernel, nfp=nfp)
    return pl.pallas_call(
        kern,
        out_shape=jax.ShapeDtypeStruct(dst.shape, F32),
        grid=(bz,),
        in_specs=[blk, blk,
                  pl.BlockSpec((None, 2 * nfp, c), lambda bi: (order, 0, 0)),
                  pl.BlockSpec(fwd_half.shape, lambda bi: (0, 0)),
                  pl.BlockSpec(inv.shape, lambda bi: (0, 0)),
                  pl.BlockSpec((1, c), lambda bi: (0, 0)),
                  pl.BlockSpec(memory_space=pl.ANY)],
        out_specs=blk,
        input_output_aliases={6: 0},
        compiler_params=_cparams(1), name="hy_ctx_conv",
    )(z, gate, h, fwd_half, inv, bias, dst)


def _rope_lanes(x, cos, sin):
    half = ROPE_DIM // 2
    lane = lax.broadcasted_iota(jnp.int32, x.shape, 1) % ROPE_DIM
    rot = jnp.where(lane < half, -pltpu.roll(x, LANES - half, 1), pltpu.roll(x, half, 1))
    return x * cos + rot * sin


def _swa_proj_kernel(x_ref, g_ref, mod_ref, w_ref, cos_ref, sin_ref, q_ref, k_ref, v_ref, sg_ref):
    h = _norm_mod(x_ref[...], g_ref[...], mod_ref[...]).astype(BF16)
    res = _dot(h, w_ref[...])
    cos, sin = cos_ref[...], sin_ref[...]
    nq = q_ref.shape[-1]
    nk = SWA_KV_HEADS * LANES
    scale = SWA_HEAD_DIM ** -0.5
    for cb in range(nq // LANES):
        sl = slice(cb * LANES, (cb + 1) * LANES)
        q_ref[:, sl] = (_rope_lanes(res[:, sl], cos, sin) * scale).astype(q_ref.dtype)
    for hd in range(SWA_KV_HEADS):
        k_ref[hd] = _rope_lanes(res[:, nq + hd * LANES: nq + (hd + 1) * LANES], cos, sin).astype(k_ref.dtype)
        v_ref[hd, :, :LANES] = res[:, nq + nk + hd * LANES: nq + nk + (hd + 1) * LANES].astype(v_ref.dtype)
        v_ref[hd, :, LANES:] = jnp.ones((v_ref.shape[1], LANES), v_ref.dtype)
    sg_ref[...] = _silu(res[:, nq + 2 * nk:]).astype(sg_ref.dtype)


def _swa_proj(x, g, mod, w, cos, sin, n_lat_tiles):
    b, t, d = x.shape
    nt = t // ROW_TILE
    nq = SWA_Q_HEADS * SWA_HEAD_DIM
    full = lambda a: pl.BlockSpec(a.shape, lambda bi, i: (0,) * a.ndim)
    row = lambda bi, i: (bi, i, 0)
    kv = lambda n: pl.BlockSpec((None, SWA_KV_HEADS, ROW_TILE, n), lambda bi, i: (bi, 0, i, 0))
    return pl.pallas_call(
        _swa_proj_kernel,
        out_shape=(jax.ShapeDtypeStruct((b, t, nq), BF16),
                   jax.ShapeDtypeStruct((b, SWA_KV_HEADS, t, LANES), BF16),
                   jax.ShapeDtypeStruct((b, SWA_KV_HEADS, t, 2 * LANES), BF16),
                   jax.ShapeDtypeStruct((b, t, nq), BF16)),
        grid=(b, nt),
        in_specs=[pl.BlockSpec((None, ROW_TILE, d), row), full(g),
                  pl.BlockSpec((None, None, 3, d), lambda bi, i: (bi, i // n_lat_tiles, 0, 0)),
                  full(w),
                  pl.BlockSpec((ROW_TILE, LANES), lambda bi, i: (i, 0)),
                  pl.BlockSpec((ROW_TILE, LANES), lambda bi, i: (i, 0))],
        out_specs=(pl.BlockSpec((None, ROW_TILE, nq), row), kv(LANES), kv(2 * LANES),
                   pl.BlockSpec((None, ROW_TILE, nq), row)),
        compiler_params=_cparams(2), name="swa_proj",
    )(x, g, mod, w, cos, sin)


def _swa_attn_kernel(sink_ref, q_ref, k_ref, v_ref, o_ref, *, n_lat_tiles, n_lat, n_ctx):
    g = pl.program_id(1)
    qi = pl.program_id(2)
    tq = q_ref.shape[0]
    span = tq + 2 * SWA_WINDOW
    lane_half = lax.broadcasted_iota(jnp.int32, (tq, LANES), 1) // SWA_HEAD_DIM
    first = lax.broadcasted_iota(jnp.int32, (2 * tq, 1), 0) < tq
    kc = k_ref[pl.ds(n_lat, n_ctx), :]
    vc = v_ref[pl.ds(n_lat, n_ctx), :]

    def pair_query(col):
        qcol = q_ref[:, col * LANES:(col + 1) * LANES]
        zero = jnp.zeros_like(qcol)
        return jnp.concatenate([jnp.where(lane_half == 0, qcol, zero), jnp.where(lane_half == 1, qcol, zero)], axis=0)

    def pair_sink(col):
        base = g * SWA_GROUP + 2 * col
        return jnp.where(first, sink_ref[base], sink_ref[base + 1])

    def finish(col, ov, m, sink):
        o = ov[:, :LANES] / (ov[:, LANES:] + jnp.exp(sink - m))
        o_ref[:, col * LANES:(col + 1) * LANES] = jnp.where(lane_half == 0, o[:tq], o[tq:]).astype(o_ref.dtype)

    @pl.when(qi < n_lat_tiles)
    def _():
        start = pl.multiple_of(jnp.clip(qi * tq - SWA_WINDOW, 0, n_lat - span), SWA_WINDOW)
        kw = k_ref[pl.ds(start, span), :]
        vw = v_ref[pl.ds(start, span), :]
        qpos = qi * tq + lax.broadcasted_iota(jnp.int32, (tq, 1), 0)
        kpos = start + lax.broadcasted_iota(jnp.int32, (1, span), 1)
        band = jnp.abs(kpos - qpos) <= SWA_WINDOW
        for col in range(SWA_GROUP // 2):
            q2, sink = pair_query(col), pair_sink(col)
            sw = _dot_nt(q2, kw)
            sw = jnp.concatenate([jnp.where(band, sw[:tq], NEG_INF), jnp.where(band, sw[tq:], NEG_INF)], axis=0)
            sc = _dot_nt(q2, kc)
            m = jnp.maximum(jnp.maximum(jnp.max(sw, axis=-1, keepdims=True),
                                        jnp.max(sc, axis=-1, keepdims=True)), sink)
            pw = jnp.exp(sw - m).astype(BF16)
            pc = jnp.exp(sc - m).astype(BF16)
            finish(col, _dot(pw, vw) + _dot(pc, vc), m, sink)

    @pl.when(qi >= n_lat_tiles)
    def _():
        for col in range(SWA_GROUP // 2):
            q2, sink = pair_query(col), pair_sink(col)
            sc = _dot_nt(q2, kc)
            m = jnp.maximum(jnp.max(sc, axis=-1, keepdims=True), sink)
            finish(col, _dot(jnp.exp(sc - m).astype(BF16), vc), m, sink)


def _swa_attn(sink, q, k, v, n_lat, n_ctx):
    b, t, nq = q.shape
    nt = t // ROW_TILE
    gw = SWA_GROUP * SWA_HEAD_DIM
    kern = functools.partial(_swa_attn_kernel, n_lat_tiles=n_lat // ROW_TILE, n_lat=n_lat, n_ctx=n_ctx)
    kv = lambda n: pl.BlockSpec((None, None, t, n), lambda bi, gi, i: (bi, gi, 0, 0))
    return pl.pallas_call(
        kern,
        out_shape=jax.ShapeDtypeStruct((b, t, nq), BF16),
        grid=(b, SWA_KV_HEADS, nt),
        in_specs=[pl.BlockSpec(memory_space=pltpu.SMEM),
                  pl.BlockSpec((None, ROW_TILE, gw), lambda bi, gi, i: (bi, i, gi)), kv(LANES), kv(2 * LANES)],
        out_specs=pl.BlockSpec((None, ROW_TILE, gw), lambda bi, gi, i: (bi, i, gi)),
        compiler_params=_cparams(3), name="swa_attn",
    )(sink, q, k, v)


def _cf_proj_kernel(x_ref, g_ref, mod_ref, w_ref, u_ref, sg_ref):
    h = _norm_mod(x_ref[...], g_ref[...], mod_ref[...]).astype(BF16)
    res = _dot(h, w_ref[...])
    c = u_ref.shape[-1]
    u_ref[...] = res[:, :c] * jax.nn.sigmoid(res[:, c:2 * c])
    sg_ref[...] = _silu(res[:, 2 * c:]).astype(sg_ref.dtype)


def _cf_proj(x, g, mod, w, n_lat):
    b, _, d = x.shape
    c = w.shape[1] // 3
    full = lambda a: pl.BlockSpec(a.shape, lambda bi, i: (0,) * a.ndim)
    row = lambda bi, i: (bi, i, 0)
    return pl.pallas_call(
        _cf_proj_kernel,
        out_shape=(jax.ShapeDtypeStruct((b, n_lat, c), F32), jax.ShapeDtypeStruct((b, n_lat, c), BF16)),
        grid=(b, n_lat // ROW_TILE),
        in_specs=[pl.BlockSpec((None, ROW_TILE, d), row), full(g),
                  pl.BlockSpec((None, None, 3, d), lambda bi, i: (bi, 0, 0, 0)), full(w)],
        out_specs=(pl.BlockSpec((None, ROW_TILE, c), row),) * 2,
        compiler_params=_cparams(2), name="cf_proj",
    )(x, g, mod, w)


def _cf_tail_kernel(up_ref, u_ref, un_ref, sg_ref, x_ref, mod_ref, dw_ref, db_ref, lg_ref, lb_ref,
                    w_ref, fg_ref, o_ref, ext_ref, win_ref):
    i = pl.program_id(1)
    nt = pl.num_programs(1)
    tm = u_ref.shape[0]
    zero = jnp.zeros(up_ref.shape, F32)
    ext_ref[0:CF_HALO, :] = jnp.where(i > 0, up_ref[...], zero)
    ext_ref[CF_HALO:CF_HALO + tm, :] = u_ref[...]
    ext_ref[CF_HALO + tm:, :] = jnp.where(i < nt - 1, un_ref[...], zero)
    pad = (CF_KERNEL - 1) // 2
    acc = jnp.zeros(u_ref.shape, F32) + db_ref[...]
    top = CF_HALO - pad + CF_KERNEL - 1
    span = tm + (top // SUBLANES) * SUBLANES
    for r in range(SUBLANES):
        taps = [k for k in range(CF_KERNEL) if (CF_HALO - pad + k) % SUBLANES == r]
        if not taps:
            continue
        if r:
            win_ref[...] = ext_ref[pl.ds(r, span), :]
        src = win_ref if r else ext_ref
        for k in taps:
            acc = acc + dw_ref[k:k + 1, :] * src[pl.ds(CF_HALO - pad + k - r, tm), :]
    mu = jnp.mean(acc, axis=-1, keepdims=True)
    xc = acc - mu
    var = jnp.mean(xc * xc, axis=-1, keepdims=True)
    ln = xc * lax.rsqrt(var + NORM_EPS) * lg_ref[...] + lb_ref[...]
    og = (_silu(ln) * sg_ref[...].astype(F32)).astype(BF16)
    xo = x_ref[...] + mod_ref[2:3, :] * _dot(og, w_ref[...])
    o_ref[...] = _rms(xo, fg_ref[...])


def _cf_tail(u, sg, x, mod, dw, db, lg, lb, w_out, fg):
    b, n_lat, c = u.shape
    d = x.shape[-1]
    nt = n_lat // ROW_TILE
    per = ROW_TILE // CF_HALO
    last_halo = n_lat // CF_HALO - 1
    full = lambda a: pl.BlockSpec(a.shape, lambda bi, i: (0,) * a.ndim)
    row = lambda bi, i: (bi, i, 0)
    return pl.pallas_call(
        _cf_tail_kernel,
        out_shape=jax.ShapeDtypeStruct((b, n_lat, d), F32),
        grid=(b, nt),
        in_specs=[pl.BlockSpec((None, CF_HALO, c), lambda bi, i: (bi, jnp.maximum(i * per - 1, 0), 0)),
                  pl.BlockSpec((None, ROW_TILE, c), row),
                  pl.BlockSpec((None, CF_HALO, c), lambda bi, i: (bi, jnp.minimum((i + 1) * per, last_halo), 0)),
                  pl.BlockSpec((None, ROW_TILE, c), row),
                  pl.BlockSpec((None, ROW_TILE, d), row),
                  pl.BlockSpec((None, None, 3, d), lambda bi, i: (bi, 0, 0, 0)),
                  full(dw), full(db), full(lg), full(lb), full(w_out), full(fg)],
        out_specs=pl.BlockSpec((None, ROW_TILE, d), row),
        scratch_shapes=[pltpu.VMEM((ROW_TILE + 2 * CF_HALO, c), F32),
                        pltpu.VMEM((ROW_TILE + 2 * CF_HALO - SUBLANES, c), F32)],
        compiler_params=_cparams(2), name="cf_tail",
    )(u, u, u, sg, x, mod, dw, db, lg, lb, w_out, fg)


def _rope_tables(n_lat, n_ctx):
    rows = n_lat // GRID_W
    row = jnp.repeat(jnp.arange(rows, dtype=F32), GRID_W)
    col = jnp.tile(jnp.arange(GRID_W, dtype=F32), rows)
    n_freq = ROPE_DIM // 4
    inv = ROPE_BASE ** (-jnp.arange(n_freq, dtype=F32) / n_freq)
    ang = jnp.concatenate([row[:, None] * inv, col[:, None] * inv], axis=-1)
    cos = jnp.concatenate([jnp.cos(ang), jnp.ones((n_ctx, ROPE_DIM // 2), F32)], axis=0)
    sin = jnp.concatenate([jnp.sin(ang), jnp.zeros((n_ctx, ROPE_DIM // 2), F32)], axis=0)
    return jnp.tile(cos, (1, 2)), jnp.tile(sin, (1, 2))


def kernel(x, c, ctx, c_ctx, norm_g, ada_w, ada_b, final_g, mla_w_in, mla_q_norm_g, mla_kv_norm_g, mla_w_uq, mla_w_ukv, mla_w_out, hy_w_in, hy_conv_w, hy_conv_b, hy_filt_w_in, hy_filt_w_hid, hy_filt_b, hy_filt_freq, hy_filt_w_out, hy_bias, hy_w_out, swa_w_in, swa_sink, swa_w_out, cf_w_in, cf_dw_w, cf_dw_b, cf_ln_g, cf_ln_b, cf_w_out):
    b, n_lat, d = x.shape
    n_ctx = ctx.shape[1]
    depth = norm_g.shape[0]
    assert depth == 4 and n_lat % ROW_TILE == 0 and n_ctx == ROW_TILE
    assert n_lat % (FFT_N1H * SUBLANES) == 0 and n_lat % GRID_W == 0
    n_lat_tiles = n_lat // ROW_TILE
    bf = lambda a: a.astype(BF16)

    xs = jnp.concatenate([x, ctx], axis=1)
    cs = jnp.concatenate([c, c_ctx[None, :], jnp.zeros((SUBLANES - b - 1, d), F32)], axis=0)
    mods = _adaln(cs, ada_w, ada_b)[:, :b + 1].reshape(depth, b + 1, 3, d)
    mods = jnp.stack([mods[:, :b], jnp.broadcast_to(mods[:, b:], (depth, b, 3, d))], axis=2)
    cos64, sin64 = _rope_tables(n_lat, n_ctx)

    c0, c1, c2 = MLA_Q_RANK, MLA_Q_RANK + MLA_KV_RANK, MLA_Q_RANK + MLA_KV_RANK + MLA_ROPE
    w_in = mla_w_in[0]
    w0 = bf(jnp.concatenate([w_in[:, :c1], w_in[:, c2:], w_in[:, c1:c2]], axis=1))
    wuq = bf(mla_w_uq[0].reshape(MLA_Q_RANK, MLA_HEADS, MLA_QK).transpose(1, 0, 2))
    wukv = bf(mla_w_ukv[0].reshape(MLA_KV_RANK, MLA_HEADS, MLA_NOPE + MLA_V).transpose(1, 0, 2))
    q, k, v, sg = _mla_proj(xs, norm_g[0:1], mods[0], w0, mla_q_norm_g[0:1], mla_kv_norm_g[0:1],
                            wuq, wukv, cos64, sin64, n_lat_tiles)
    o = _mla_attn(q, k, v, n_lat, n_ctx)
    xs = _outproj(o, sg, xs, mods[0], bf(mla_w_out[0]), n_lat_tiles)

    z0, g0, g1, sg = _hy_proj(xs, norm_g[1:2], mods[1], bf(hy_w_in[0]), hy_conv_w[0], hy_conv_b[0][None, :],
                              n_lat_tiles)
    cch = z0.shape[-1]
    filt = (hy_filt_w_in[0], hy_filt_w_hid[0], hy_filt_b[0], hy_filt_freq[0], hy_filt_w_out[0])
    assert cch % HY_SLAB == 0
    mats = _fft_matrices(n_lat)
    spec = _hy_spec(_hy_filter_taps(n_lat, *filt), mats[1], mats[2])
    cfwd, cinv, nfp = _dense_dft_matrices(n_ctx)
    cspec = _hy_ctx_spec(_hy_filter_taps(n_ctx, *filt), cfwd)
    gates = (g0, g1)
    z = z0
    for order in range(HY_ORDER):
        bias = hy_bias[0][order][None, :]
        zn = _hy_conv(z, gates[order], spec, order, mats, bias, n_lat)
        z = _hy_ctx_conv(z, gates[order], cspec, order, cfwd[:, :n_ctx], cinv, nfp, bias, zn, n_lat)
    xs = _outproj(z, sg, xs, mods[1], bf(hy_w_out[0]), n_lat_tiles)

    nq = SWA_Q_HEADS * SWA_HEAD_DIM
    nkv = SWA_KV_HEADS * SWA_HEAD_DIM
    w_in = swa_w_in[0]
    dup = lambda w: jnp.tile(w.reshape(d, SWA_KV_HEADS, 1, SWA_HEAD_DIM), (1, 1, 2, 1)).reshape(d, 2 * nkv)
    w2 = bf(jnp.concatenate([w_in[:, :nq], dup(w_in[:, nq:nq + nkv]), dup(w_in[:, nq + nkv:nq + 2 * nkv]),
                             w_in[:, nq + 2 * nkv:]], axis=1))
    cos128, sin128 = jnp.tile(cos64, (1, 2)), jnp.tile(sin64, (1, 2))
    q, k, v, sg = _swa_proj(xs, norm_g[2:3], mods[2], w2, cos128, sin128, n_lat_tiles)
    o = _swa_attn(swa_sink[0], q, k, v, n_lat, n_ctx)
    xs = _outproj(o, sg, xs, mods[2], bf(swa_w_out[0]), n_lat_tiles)

    u, sg = _cf_proj(xs, norm_g[3:4], mods[3], bf(cf_w_in[0]), n_lat)
    return _cf_tail(u, sg, xs, mods[3], cf_dw_w[0], cf_dw_b[0][None, :], cf_ln_g[0][None, :], cf_ln_b[0][None, :],
                    bf(cf_w_out[0]), final_g[None, :])
```

```python
import functools
import math

import numpy as np
import jax
import jax.numpy as jnp
from jax import lax
from jax.experimental import pallas as pl
from jax.experimental.pallas import tpu as pltpu

F32 = jnp.float32
BF16 = jnp.bfloat16

GRID_W = 64
NORM_EPS = 1e-6
NEG_INF = -1e30
ROPE_BASE = 10000.0
ROPE_DIM = 64

MLA_HEADS = 8
MLA_Q_RANK = 384
MLA_KV_RANK = 256
MLA_NOPE = 128
MLA_ROPE = ROPE_DIM
MLA_V = 128
MLA_QK = MLA_NOPE + MLA_ROPE

HY_ORDER = 2
HY_EMB = 33
HY_EMB_PAD = 64
HY_BANDS = (HY_EMB - 1) // 2
HY_DECAY_TARGET = 1e-2
HY_FAST_DECAY = 0.3
HY_SLOW_DECAY = 1.5

SWA_Q_HEADS = 16
SWA_KV_HEADS = 4
SWA_GROUP = SWA_Q_HEADS // SWA_KV_HEADS
SWA_HEAD_DIM = ROPE_DIM
SWA_WINDOW = 128

CF_KERNEL = 31
CF_HALO = 16

ROW_TILE = 512
ATTN_TILE = 256
SUBLANES = 8
LANES = 128
FFT_N1 = 64
FFT_N1H = FFT_N1 // 2
FFT_K1 = FFT_N1 // 2 + 1
HY_SLAB = 256
VMEM_LIMIT = 48 * 1024 * 1024
VMEM_LIMIT_BIG = 56 * 1024 * 1024


def _cparams(n_axes):
    return pltpu.CompilerParams(dimension_semantics=("arbitrary",) * n_axes,
                                vmem_limit_bytes=VMEM_LIMIT)


def _cparams_big(n_axes):
    return pltpu.CompilerParams(dimension_semantics=("arbitrary",) * n_axes,
                                vmem_limit_bytes=VMEM_LIMIT_BIG)


def _once(shape, index_map):
    return pl.BlockSpec(shape, index_map, pipeline_mode=pl.Buffered(1))


def _dot(a, b):
    return jnp.dot(a, b, preferred_element_type=F32)


def _dot_nt(a, b):
    return lax.dot_general(a, b, (((1,), (1,)), ((), ())), preferred_element_type=F32)


def _rms(x, g):
    return x * lax.rsqrt(jnp.mean(x * x, axis=-1, keepdims=True) + NORM_EPS) * g


def _norm_mod(x, g, mod):
    return _rms(x, g) * (1.0 + mod[1:2, :]) + mod[0:1, :]


def _silu(x):
    return x * jax.nn.sigmoid(x)


def _adaln_kernel(c_ref, w_ref, b_ref, o_ref):
    s = _silu(c_ref[...]).astype(BF16)
    o_ref[...] = _dot(s, w_ref[...].astype(BF16)) + b_ref[...]


def _adaln(cs, ada_w, ada_b):
    depth, d, d3 = ada_w.shape
    tn = 512
    return pl.pallas_call(
        _adaln_kernel,
        out_shape=jax.ShapeDtypeStruct((depth, cs.shape[0], d3), F32),
        grid=(depth, d3 // tn),
        in_specs=[pl.BlockSpec(cs.shape, lambda i, j: (0, 0)),
                  pl.BlockSpec((None, d, tn), lambda i, j: (i, 0, j)),
                  pl.BlockSpec((None, 1, tn), lambda i, j: (i, 0, j))],
        out_specs=pl.BlockSpec((None, cs.shape[0], tn), lambda i, j: (i, 0, j)),
        compiler_params=_cparams(2), name="adaln",
    )(cs, ada_w, ada_b.reshape(depth, 1, d3))


def _outproj_kernel(o_ref, sg_ref, x_ref, mod_ref, w_ref, xo_ref):
    og = (o_ref[...].astype(F32) * sg_ref[...].astype(F32)).astype(BF16)
    xo_ref[...] = x_ref[...] + mod_ref[2:3, :] * _dot(og, w_ref[...])


def _outproj(o, sg, x, mod, w_out, n_lat_tiles):
    b, t, d = x.shape
    width = o.shape[-1]
    nt = t // ROW_TILE
    row = lambda bi, i: (bi, i, 0)
    return pl.pallas_call(
        _outproj_kernel,
        out_shape=jax.ShapeDtypeStruct(x.shape, F32),
        grid=(b, nt),
        in_specs=[pl.BlockSpec((None, ROW_TILE, width), row),
                  pl.BlockSpec((None, ROW_TILE, width), row),
                  pl.BlockSpec((None, ROW_TILE, d), row),
                  pl.BlockSpec((None, None, 3, d), lambda bi, i: (bi, i // n_lat_tiles, 0, 0)),
                  pl.BlockSpec(w_out.shape, lambda bi, i: (0, 0))],
        out_specs=pl.BlockSpec((None, ROW_TILE, d), row),
        compiler_params=_cparams(2), name="outproj",
    )(o, sg, x, mod, w_out)


def _rope_swap_matrix():
    half = ROPE_DIM // 2
    p = np.zeros((ROPE_DIM, ROPE_DIM), np.float32)
    for j in range(half):
        p[j + half, j] = -1.0
        p[j, j + half] = 1.0
    return p


def _mla_proj_kernel(x_ref, g_ref, mod_ref, wl_ref, wg_ref, qg_ref, kvg_ref, wuq_ref, wukv_ref,
                     cos_ref, sin_ref, p_ref, q_ref, k_ref, v_ref, sg_ref):
    h = _norm_mod(x_ref[...], g_ref[...], mod_ref[...]).astype(BF16)
    sg_ref[...] = _silu(_dot(h, wg_ref[...])).astype(sg_ref.dtype)
    res = _dot(h, wl_ref[...])
    c0, c1 = MLA_Q_RANK, MLA_Q_RANK + MLA_KV_RANK
    cqn = _rms(res[:, :c0], qg_ref[...]).astype(BF16)
    ckvn = _rms(res[:, c0:c1], kvg_ref[...]).astype(BF16)
    cos, sin, swap = cos_ref[...], sin_ref[...], p_ref[...]

    def rope(t):
        return t * cos + _dot(t.astype(BF16), swap) * sin

    k_rope_t = rope(res[:, c1:c1 + MLA_ROPE]).T.astype(k_ref.dtype)
    scale = MLA_QK ** -0.5
    for hd in range(MLA_HEADS):
        q = _dot(cqn, wuq_ref[hd])
        q_ref[hd, :, :MLA_NOPE] = (q[:, :MLA_NOPE] * scale).astype(q_ref.dtype)
        q_ref[hd, :, MLA_NOPE:] = (rope(q[:, MLA_NOPE:]) * scale).astype(q_ref.dtype)
        kv = _dot(ckvn, wukv_ref[hd])
        k_ref[hd, :MLA_NOPE, :] = kv[:, :MLA_NOPE].T.astype(k_ref.dtype)
        k_ref[hd, MLA_NOPE:, :] = k_rope_t
        v_ref[hd, :, :MLA_V] = kv[:, MLA_NOPE:].astype(v_ref.dtype)
        v_ref[hd, :, MLA_V:] = jnp.ones((v_ref.shape[1], MLA_V), v_ref.dtype)


def _mla_proj(x, g, mod, wl, wg, qg, kvg, wuq, wukv, cos, sin, n_lat_tiles):
    b, t, d = x.shape
    nt = t // ROW_TILE
    width = MLA_HEADS * MLA_V
    swap = jnp.asarray(_rope_swap_matrix()).astype(BF16)
    full = lambda a: _once(a.shape, lambda bi, i: (0,) * a.ndim)
    head_out = lambda n: pl.BlockSpec((None, MLA_HEADS, ROW_TILE, n), lambda bi, i: (bi, 0, i, 0))
    return pl.pallas_call(
        _mla_proj_kernel,
        out_shape=(jax.ShapeDtypeStruct((b, MLA_HEADS, t, MLA_QK), BF16),
                   jax.ShapeDtypeStruct((b, MLA_HEADS, MLA_QK, t), BF16),
                   jax.ShapeDtypeStruct((b, MLA_HEADS, t, 2 * MLA_V), BF16),
                   jax.ShapeDtypeStruct((b, t, width), BF16)),
        grid=(b, nt),
        in_specs=[pl.BlockSpec((None, ROW_TILE, d), lambda bi, i: (bi, i, 0)),
                  full(g),
                  pl.BlockSpec((None, None, 3, d), lambda bi, i: (bi, i // n_lat_tiles, 0, 0)),
                  full(wl), full(wg), full(qg), full(kvg), full(wuq), full(wukv),
                  pl.BlockSpec((ROW_TILE, ROPE_DIM), lambda bi, i: (i, 0)),
                  pl.BlockSpec((ROW_TILE, ROPE_DIM), lambda bi, i: (i, 0)),
                  full(swap)],
        out_specs=(head_out(MLA_QK),
                   pl.BlockSpec((None, MLA_HEADS, MLA_QK, ROW_TILE), lambda bi, i: (bi, 0, 0, i)),
                   head_out(2 * MLA_V),
                   pl.BlockSpec((None, ROW_TILE, width), lambda bi, i: (bi, i, 0))),
        compiler_params=_cparams(2), name="mla_proj",
    )(x, g, mod, wl, wg, qg, kvg, wuq, wukv, cos, sin, swap)


def _attend(q, k_parts, v_parts):
    s = [_dot(q, k) for k in k_parts]
    m = functools.reduce(jnp.maximum, [jnp.max(x, axis=-1, keepdims=True) for x in s])
    ov = functools.reduce(jnp.add, [_dot(jnp.exp(x - m).astype(BF16), v) for x, v in zip(s, v_parts)])
    n = ov.shape[-1] // 2
    return ov[:, :n] / ov[:, n:]


MLA_HEADS_PER_STEP = 2


def _mla_attn_kernel(q_ref, k_ref, v_ref, o_ref, *, n_lat_tiles, n_lat, n_ctx):
    qi = pl.program_id(2)

    n_keys = n_lat + n_ctx

    @pl.when(qi < n_lat_tiles)
    def _():
        for hd in range(MLA_HEADS_PER_STEP):
            k, v = k_ref[hd, :, 0:n_keys], v_ref[hd, 0:n_keys, :]
            for r0 in range(0, q_ref.shape[1], ATTN_TILE):
                o = _attend(q_ref[hd, r0:r0 + ATTN_TILE, :], [k], [v])
                o_ref[r0:r0 + ATTN_TILE, hd * MLA_V:(hd + 1) * MLA_V] = o.astype(o_ref.dtype)

    @pl.when(qi >= n_lat_tiles)
    def _():
        o_ref[n_ctx:, :] = jnp.zeros((o_ref.shape[0] - n_ctx, o_ref.shape[1]), o_ref.dtype)
        for hd in range(MLA_HEADS_PER_STEP):
            k = k_ref[hd, :, n_lat:n_keys]
            v = v_ref[hd, n_lat:n_keys, :]
            o = _attend(q_ref[hd, 0:n_ctx, :], [k], [v])
            o_ref[0:n_ctx, hd * MLA_V:(hd + 1) * MLA_V] = o.astype(o_ref.dtype)


def _mla_attn(q, k, v, n_lat, n_ctx):
    b, hds, t, _ = q.shape
    nt = t // ROW_TILE
    hps = MLA_HEADS_PER_STEP
    kern = functools.partial(_mla_attn_kernel, n_lat_tiles=n_lat // ROW_TILE, n_lat=n_lat, n_ctx=n_ctx)
    return pl.pallas_call(
        kern,
        out_shape=jax.ShapeDtypeStruct((b, t, hds * MLA_V), BF16),
        grid=(b, hds // hps, nt),
        in_specs=[pl.BlockSpec((None, hps, ROW_TILE, MLA_QK), lambda bi, h, i: (bi, h, i, 0)),
                  pl.BlockSpec((None, hps, MLA_QK, t), lambda bi, h, i: (bi, h, 0, 0)),
                  pl.BlockSpec((None, hps, t, 2 * MLA_V), lambda bi, h, i: (bi, h, 0, 0))],
        out_specs=pl.BlockSpec((None, ROW_TILE, hps * MLA_V), lambda bi, h, i: (bi, i, h)),
        compiler_params=_cparams_big(3), name="mla_attn",
    )(q, k, v)


def _hy_proj_kernel(xp_ref, x_ref, xn_ref, g_ref, mod_ref, w_ref, cw_ref, cb_ref,
                    z_ref, g0_ref, g1_ref, sg_ref, *, n_lat_tiles, n_ctx):
    i = pl.program_id(1)
    tm = x_ref.shape[0]
    c = z_ref.shape[-1]
    n = tm + 2 * SUBLANES
    xa = jnp.concatenate([xp_ref[...], x_ref[...], xn_ref[...]], axis=0)
    hf = _norm_mod(xa, g_ref[...], mod_ref[...])
    h = hf.astype(BF16)
    ctx_tile = i == n_lat_tiles
    first = jnp.logical_or(i == 0, ctx_tile)
    last = jnp.logical_or(i == n_lat_tiles - 1, ctx_tile)
    rows = lax.broadcasted_iota(jnp.int32, (n, 1), 0)
    end = jnp.where(ctx_tile, n_ctx + SUBLANES, jnp.where(last, tm + SUBLANES, n))
    valid = jnp.logical_and(jnp.logical_or(rows >= SUBLANES, jnp.logical_not(first)), rows < end)
    outs = (z_ref, g0_ref, g1_ref)
    for j in range(len(outs)):
        u = jnp.where(valid, _dot(h, w_ref[:, j * c:(j + 1) * c]), 0.0)
        cw = cw_ref[:, j * c:(j + 1) * c]
        cv = cw[0:1, :] * pltpu.roll(u, 1, 0) + cw[1:2, :] * u + cw[2:3, :] * pltpu.roll(u, n - 1, 0)
        outs[j][...] = cv[SUBLANES:SUBLANES + tm] + cb_ref[:, j * c:(j + 1) * c]
    h_mid = hf[SUBLANES:SUBLANES + tm].astype(BF16)
    sg_ref[...] = _silu(_dot(h_mid, w_ref[:, len(outs) * c:])).astype(sg_ref.dtype)


def _hy_proj(x, g, mod, w, cw, cb, n_lat_tiles, n_ctx):
    b, t, d = x.shape
    nt = t // ROW_TILE
    c = w.shape[1] // (HY_ORDER + 2)
    per = ROW_TILE // SUBLANES
    last_halo = t // SUBLANES - 1
    full = lambda a: _once(a.shape, lambda bi, i: (0,) * a.ndim)
    row = lambda bi, i: (bi, i, 0)
    kern = functools.partial(_hy_proj_kernel, n_lat_tiles=n_lat_tiles, n_ctx=n_ctx)
    return pl.pallas_call(
        kern,
        out_shape=(jax.ShapeDtypeStruct((b, t, c), F32),) * 3 + (jax.ShapeDtypeStruct((b, t, c), BF16),),
        grid=(b, nt),
        in_specs=[pl.BlockSpec((None, SUBLANES, d), lambda bi, i: (bi, jnp.maximum(i * per - 1, 0), 0)),
                  pl.BlockSpec((None, ROW_TILE, d), row),
                  pl.BlockSpec((None, SUBLANES, d), lambda bi, i: (bi, jnp.minimum((i + 1) * per, last_halo), 0)),
                  full(g),
                  pl.BlockSpec((None, None, 3, d), lambda bi, i: (bi, i // n_lat_tiles, 0, 0)),
                  full(w), full(cw), full(cb)],
        out_specs=(pl.BlockSpec((None, ROW_TILE, c), row),) * 4,
        compiler_params=_cparams(2), name="hy_proj",
    )(x, x, x, g, mod, w, cw, cb)


def _hy_filter_kernel(ef_ref, eb_ref, w0_ref, wh_ref, b_ref, fr_ref, wf_ref, wb_ref, dl_ref, o_ref):
    i = pl.program_id(0)
    c = dl_ref.shape[-1]

    def taps(e, wo):
        hdn = e
        ws = (w0_ref[...], wh_ref[0], wh_ref[1])
        for k in range(3):
            hdn = jnp.sin(fr_ref[k:k + 1, :] * (_dot(hdn.astype(BF16), ws[k].astype(BF16)) + b_ref[k:k + 1, :]))
        decay = jnp.exp(-e[:, 0:1] * dl_ref[...])
        hh = _dot(hdn.astype(BF16), wo.astype(BF16))
        return [hh[:, o * c:(o + 1) * c] * decay for o in range(HY_ORDER)]

    hf = taps(ef_ref[...], wf_ref[...])
    hb = taps(eb_ref[...], wb_ref[...])
    rows = lax.broadcasted_iota(jnp.int32, (ef_ref.shape[0], 1), 0)
    lag0 = jnp.logical_and(rows == 0, i == 0)
    for o in range(HY_ORDER):
        o_ref[o, 0] = hf[o] + jnp.where(lag0, hb[o], 0.0)
        o_ref[o, 1] = jnp.where(lag0, 0.0, hb[o])


def _hy_position_features(n):
    t = jnp.linspace(0.0, 1.0, n, dtype=F32)[:, None]
    wpos = (2.0 * math.pi / n) * jnp.arange(n, dtype=F32)[:, None]
    bands = jnp.linspace(1e-4, HY_BANDS - 1, HY_BANDS, dtype=F32)[None, :]
    hdn = jnp.concatenate([t, jnp.cos(bands * wpos), -jnp.sin(bands * wpos)], axis=-1)
    return jnp.pad(hdn, ((0, 0), (0, HY_EMB_PAD - HY_EMB)))


def _hy_filter_taps(n, w_in, w_hid, bias, freq, w_out):
    c = w_out.shape[1] // (2 * HY_ORDER)
    ef = _hy_position_features(n)
    rev = (n - jnp.arange(n)) % n
    eb = ef[rev]
    w0 = jnp.pad(w_in, ((0, HY_EMB_PAD - HY_EMB), (0, 0)))
    wo = w_out.reshape(w_out.shape[0], HY_ORDER, 2, c)
    wf = wo[:, :, 0, :].reshape(w_out.shape[0], HY_ORDER * c)
    wb = wo[:, :, 1, :].reshape(w_out.shape[0], HY_ORDER * c)
    deltas = jnp.abs(jnp.linspace(math.log(HY_DECAY_TARGET) / HY_FAST_DECAY,
                                  math.log(HY_DECAY_TARGET) / HY_SLOW_DECAY, c, dtype=F32))[None, :]
    tl = min(n, 512)
    full = lambda a: pl.BlockSpec(a.shape, lambda i: (0,) * a.ndim)
    return pl.pallas_call(
        _hy_filter_kernel,
        out_shape=jax.ShapeDtypeStruct((HY_ORDER, 2, n, c), F32),
        grid=(n // tl,),
        in_specs=[pl.BlockSpec((tl, HY_EMB_PAD), lambda i: (i, 0)),
                  pl.BlockSpec((tl, HY_EMB_PAD), lambda i: (i, 0)),
                  full(w0), full(w_hid), full(bias), full(freq), full(wf), full(wb), full(deltas)],
        out_specs=pl.BlockSpec((HY_ORDER, 2, tl, c), lambda i: (0, 0, i, 0)),
        compiler_params=_cparams(1), name="hy_filter",
    )(ef, eb, w0, w_hid, bias, freq, wf, wb, deltas)


def _fft_matrices(n_lat):
    n = 2 * n_lat
    n2 = n_lat // FFT_N1H
    k1 = np.arange(FFT_K1)
    n1 = np.arange(FFT_N1H)
    ang = 2.0 * np.pi * np.outer(k1, n1) / FFT_N1
    eye = np.eye(SUBLANES)
    f1 = np.kron(np.concatenate([np.cos(ang), -np.sin(ang)], axis=0), eye)
    wgt = np.full(FFT_K1, 2.0)
    wgt[0] = wgt[-1] = 1.0
    g1 = np.kron(np.concatenate([wgt * np.cos(ang.T), -wgt * np.sin(ang.T)], axis=1) / n, eye)
    idx = np.arange(n2)
    f2 = np.zeros((FFT_K1, 2 * n2, 2 * n2))
    f2i = np.zeros((FFT_K1, 2 * n2, 2 * n2))
    for a in range(FFT_K1):
        ph = 2.0 * np.pi * (np.outer(idx, idx) / n2 + (a * idx)[None, :] / n)
        f2[a] = np.block([[np.cos(ph), np.sin(ph)], [-np.sin(ph), np.cos(ph)]])
        f2i[a] = np.block([[np.cos(ph.T), -np.sin(ph.T)], [np.sin(ph.T), np.cos(ph.T)]])
    sign = np.kron(np.tile((-1.0) ** k1, 2), np.ones(SUBLANES))[:, None]
    f1s = np.concatenate([f1, sign * f1], axis=1)
    as_bf16 = lambda m: jnp.asarray(m.astype(np.float32)).astype(BF16)
    return as_bf16(f1), as_bf16(f1s), as_bf16(f2), as_bf16(f2i), as_bf16(g1)


def _dense_dft_matrices(n_ctx):
    n = 2 * n_ctx
    nf = n_ctx + 1
    nfp = -(-nf // SUBLANES) * SUBLANES
    k = np.arange(nf)
    ang = 2.0 * np.pi * np.outer(k, np.arange(n)) / n
    fwd = np.zeros((2 * nfp, n))
    fwd[:nf] = np.cos(ang)
    fwd[nfp:nfp + nf] = -np.sin(ang)
    wgt = np.full(nf, 2.0)
    wgt[0] = wgt[-1] = 1.0
    angi = 2.0 * np.pi * np.outer(np.arange(n_ctx), k) / n
    inv = np.zeros((n_ctx, 2 * nfp))
    inv[:, :nf] = wgt * np.cos(angi) / n
    inv[:, nfp:nfp + nf] = -wgt * np.sin(angi) / n
    as_bf16 = lambda m: jnp.asarray(m.astype(np.float32)).astype(BF16)
    return as_bf16(fwd), as_bf16(inv), nfp


def _hy_spec_kernel(t_ref, f1s_ref, f2_ref, h_ref, a_ref):
    _, n1h, nj, s, tc = t_ref.shape
    n2 = nj * s

    def stage1(j, carry):
        sig = jnp.concatenate([t_ref[0, :, j].reshape(n1h * s, tc), t_ref[1, :, j].reshape(n1h * s, tc)], axis=0)
        a_ref[:, :, j] = _dot(f1s_ref[...], sig.astype(BF16)).reshape(2, FFT_K1, s, tc)
        return carry

    lax.fori_loop(0, nj, stage1, 0, unroll=min(nj, 4))

    def stage2(k, carry):
        a = a_ref[:, k].reshape(2 * n2, tc).astype(BF16)
        h_ref[:, k] = _dot(f2_ref[k], a).reshape(2, n2, tc).astype(h_ref.dtype)
        return carry

    lax.fori_loop(0, FFT_K1, stage2, 0, unroll=11)


def _hy_spec(taps, f1s, f2):
    _, _, n_lat, c = taps.shape
    n2 = n_lat // FFT_N1H
    nj = n2 // SUBLANES
    tc = HY_SLAB
    return pl.pallas_call(
        _hy_spec_kernel,
        out_shape=jax.ShapeDtypeStruct((HY_ORDER, 2, FFT_K1, n2, c), BF16),
        grid=(c // tc, HY_ORDER),
        in_specs=[pl.BlockSpec((None, 2, FFT_N1H, nj, SUBLANES, tc), lambda ci, o: (o, 0, 0, 0, 0, ci)),
                  _once(f1s.shape, lambda ci, o: (0, 0)),
                  _once(f2.shape, lambda ci, o: (0, 0, 0))],
        out_specs=pl.BlockSpec((None, 2, FFT_K1, n2, tc), lambda ci, o: (o, 0, 0, 0, ci)),
        scratch_shapes=[pltpu.VMEM((2, FFT_K1, nj, SUBLANES, tc), F32)],
        compiler_params=_cparams_big(2), name="hy_spec",
    )(taps.reshape(HY_ORDER, 2, FFT_N1H, nj, SUBLANES, c), f1s, f2)


def _hy_conv_kernel(z_ref, gt_ref, h_ref, f1_ref, f2_ref, f2i_ref, g1_ref, bias_ref, o_ref, a_ref):
    n1h, nj, s, tc = z_ref.shape
    n2 = nj * s

    def stage1(j, carry):
        zz = z_ref[:, j].reshape(n1h * s, tc).astype(BF16)
        a_ref[:, :, j] = _dot(f1_ref[...], zz).reshape(2, FFT_K1, s, tc)
        return carry

    lax.fori_loop(0, nj, stage1, 0, unroll=min(nj, 4))

    def stage2(k, carry):
        x = _dot(f2_ref[k], a_ref[:, k].reshape(2 * n2, tc).astype(BF16))
        xr, xi = x[:n2], x[n2:]
        hr, hi = h_ref[0, k].astype(F32), h_ref[1, k].astype(F32)
        y = jnp.concatenate([xr * hr - xi * hi, xr * hi + xi * hr], axis=0).astype(BF16)
        a_ref[:, k] = _dot(f2i_ref[k], y).reshape(2, nj, s, tc)
        return carry

    lax.fori_loop(0, FFT_K1, stage2, 0, unroll=11)

    def stage3(j, carry):
        bp = a_ref[:, :, j].reshape(2 * FFT_K1 * s, tc).astype(BF16)
        y = _dot(g1_ref[...], bp).reshape(n1h, s, tc)
        o_ref[:, j] = gt_ref[:, j] * (y + z_ref[:, j] * bias_ref[...])
        return carry

    lax.fori_loop(0, nj, stage3, 0, unroll=min(nj, 4))


def _hy_conv(z, gate, h, order, mats, bias, n_lat):
    f1, _, f2, f2i, g1 = mats
    bz, t, c = z.shape
    n2 = n_lat // FFT_N1H
    nj = n2 // SUBLANES
    tc = HY_SLAB
    view = (bz, t // n2, nj, SUBLANES, c)
    slab = pl.BlockSpec((None, FFT_N1H, nj, SUBLANES, tc), lambda ci, bi: (bi, 0, 0, 0, ci))
    out = pl.pallas_call(
        _hy_conv_kernel,
        out_shape=jax.ShapeDtypeStruct(view, F32),
        grid=(c // tc, bz),
        in_specs=[slab, slab,
                  _once((None, 2, FFT_K1, n2, tc), lambda ci, bi: (order, 0, 0, 0, ci)),
                  _once(f1.shape, lambda ci, bi: (0, 0)),
                  _once(f2.shape, lambda ci, bi: (0, 0, 0)),
                  _once(f2i.shape, lambda ci, bi: (0, 0, 0)),
                  _once(g1.shape, lambda ci, bi: (0, 0)),
                  pl.BlockSpec((1, tc), lambda ci, bi: (0, ci))],
        out_specs=slab,
        scratch_shapes=[pltpu.VMEM((2, FFT_K1, nj, SUBLANES, tc), F32)],
        compiler_params=_cparams_big(2), name="hy_conv",
    )(z.reshape(view), gate.reshape(view), h, f1, f2, f2i, g1, bias)
    return out.reshape(bz, t, c)


def _hy_ctx_spec_kernel(t_ref, f_ref, h_ref):
    sig = jnp.concatenate([t_ref[0], t_ref[1]], axis=0).astype(BF16)
    h_ref[...] = _dot(f_ref[...], sig)


def _hy_ctx_spec(taps, fwd):
    _, _, n_ctx, c = taps.shape
    return pl.pallas_call(
        _hy_ctx_spec_kernel,
        out_shape=jax.ShapeDtypeStruct((HY_ORDER, fwd.shape[0], c), F32),
        grid=(HY_ORDER,),
        in_specs=[pl.BlockSpec((None, 2, n_ctx, c), lambda o: (o, 0, 0, 0)),
                  pl.BlockSpec(fwd.shape, lambda o: (0, 0))],
        out_specs=pl.BlockSpec((None, fwd.shape[0], c), lambda o: (o, 0, 0)),
        compiler_params=_cparams(1), name="hy_ctx_spec",
    )(taps, fwd)


def _hy_ctx_conv_kernel(zin_ref, gt_ref, h_ref, f_ref, fi_ref, bias_ref, dst_ref, o_ref, *, nfp):
    del dst_ref
    z = zin_ref[...]
    x = _dot(f_ref[...], z.astype(BF16))
    xr, xi = x[:nfp], x[nfp:]
    hr, hi = h_ref[:nfp, :], h_ref[nfp:, :]
    y = jnp.concatenate([xr * hr - xi * hi, xr * hi + xi * hr], axis=0).astype(BF16)
    n_ctx = z.shape[0]
    o_ref[0:n_ctx, :] = gt_ref[...] * (_dot(fi_ref[...], y) + z * bias_ref[...])
    o_ref[n_ctx:, :] = jnp.zeros((o_ref.shape[0] - n_ctx, o_ref.shape[1]), o_ref.dtype)


def _hy_ctx_conv(z, gate, h, order, fwd_half, inv, nfp, bias, dst, n_lat, n_ctx):
    bz, t, c = z.shape
    blk = pl.BlockSpec((None, n_ctx, c), lambda bi: (bi, n_lat // n_ctx, 0))
    kern = functools.partial(_hy_ctx_conv_kernel, nfp=nfp)
    return pl.pallas_call(
        kern,
        out_shape=jax.ShapeDtypeStruct(dst.shape, F32),
        grid=(bz,),
        in_specs=[blk, blk,
                  pl.BlockSpec((None, 2 * nfp, c), lambda bi: (order, 0, 0)),
                  pl.BlockSpec(fwd_half.shape, lambda bi: (0, 0)),
                  pl.BlockSpec(inv.shape, lambda bi: (0, 0)),
                  pl.BlockSpec((1, c), lambda bi: (0, 0)),
                  pl.BlockSpec(memory_space=pl.ANY)],
        out_specs=pl.BlockSpec((None, t - n_lat, c), lambda bi: (bi, n_lat // (t - n_lat), 0)),
        input_output_aliases={6: 0},
        compiler_params=_cparams(1), name="hy_ctx_conv",
    )(z, gate, h, fwd_half, inv, bias, dst)


def _rope_lanes(x, cos, sin):
    half = ROPE_DIM // 2
    lane = lax.broadcasted_iota(jnp.int32, x.shape, 1) % ROPE_DIM
    rot = jnp.where(lane < half, -pltpu.roll(x, LANES - half, 1), pltpu.roll(x, half, 1))
    return x * cos + rot * sin


def _swa_proj_kernel(x_ref, g_ref, mod_ref, w_ref, cos_ref, sin_ref, q_ref, k_ref, v_ref, sg_ref):
    h = _norm_mod(x_ref[...], g_ref[...], mod_ref[...]).astype(BF16)
    cos, sin = cos_ref[...], sin_ref[...]
    nq = q_ref.shape[-1]
    nk = SWA_KV_HEADS * LANES
    scale = SWA_HEAD_DIM ** -0.5
    for c0 in range(0, nq, nk):
        res = _dot(h, w_ref[:, c0:c0 + nk])
        for cb in range(nk // LANES):
            sl = slice(cb * LANES, (cb + 1) * LANES)
            q_ref[:, c0 + cb * LANES:c0 + (cb + 1) * LANES] = (_rope_lanes(res[:, sl], cos, sin) * scale).astype(q_ref.dtype)
    res = _dot(h, w_ref[:, nq:nq + nk])
    for hd in range(SWA_KV_HEADS):
        k_ref[hd] = _rope_lanes(res[:, hd * LANES:(hd + 1) * LANES], cos, sin).astype(k_ref.dtype)
    res = _dot(h, w_ref[:, nq + nk:nq + 2 * nk])
    for hd in range(SWA_KV_HEADS):
        v_ref[hd, :, :LANES] = res[:, hd * LANES:(hd + 1) * LANES].astype(v_ref.dtype)
        v_ref[hd, :, LANES:] = jnp.ones((v_ref.shape[1], LANES), v_ref.dtype)
    for c0 in range(0, nq, nk):
        sg_ref[:, c0:c0 + nk] = _silu(_dot(h, w_ref[:, nq + 2 * nk + c0:nq + 2 * nk + c0 + nk])).astype(sg_ref.dtype)


def _swa_proj(x, g, mod, w, cos, sin, n_lat_tiles):
    b, t, d = x.shape
    nt = t // ROW_TILE
    nq = SWA_Q_HEADS * SWA_HEAD_DIM
    full = lambda a: _once(a.shape, lambda bi, i: (0,) * a.ndim)
    row = lambda bi, i: (bi, i, 0)
    kv = lambda n: pl.BlockSpec((None, SWA_KV_HEADS, ROW_TILE, n), lambda bi, i: (bi, 0, i, 0))
    return pl.pallas_call(
        _swa_proj_kernel,
        out_shape=(jax.ShapeDtypeStruct((b, t, nq), BF16),
                   jax.ShapeDtypeStruct((b, SWA_KV_HEADS, t, LANES), BF16),
                   jax.ShapeDtypeStruct((b, SWA_KV_HEADS, t, 2 * LANES), BF16),
                   jax.ShapeDtypeStruct((b, t, nq), BF16)),
        grid=(b, nt),
        in_specs=[pl.BlockSpec((None, ROW_TILE, d), row), full(g),
                  pl.BlockSpec((None, None, 3, d), lambda bi, i: (bi, i // n_lat_tiles, 0, 0)),
                  full(w),
                  pl.BlockSpec((ROW_TILE, LANES), lambda bi, i: (i, 0)),
                  pl.BlockSpec((ROW_TILE, LANES), lambda bi, i: (i, 0))],
        out_specs=(pl.BlockSpec((None, ROW_TILE, nq), row), kv(LANES), kv(2 * LANES),
                   pl.BlockSpec((None, ROW_TILE, nq), row)),
        compiler_params=_cparams(2), name="swa_proj",
    )(x, g, mod, w, cos, sin)


def _swa_attn_kernel(sink_ref, q_ref, k_ref, v_ref, o_ref, *, n_lat_tiles, n_lat, n_ctx):
    g = pl.program_id(1)
    qi = pl.program_id(2)
    tq = ATTN_TILE
    span = tq + 2 * SWA_WINDOW
    lane_half = lax.broadcasted_iota(jnp.int32, (tq, LANES), 1) // SWA_HEAD_DIM
    first = lax.broadcasted_iota(jnp.int32, (2 * tq, 1), 0) < tq
    kc = k_ref[pl.ds(n_lat, n_ctx), :]
    vc = v_ref[pl.ds(n_lat, n_ctx), :]

    def pair_query(r0, col):
        qcol = q_ref[r0:r0 + tq, col * LANES:(col + 1) * LANES]
        zero = jnp.zeros_like(qcol)
        return jnp.concatenate([jnp.where(lane_half == 0, qcol, zero), jnp.where(lane_half == 1, qcol, zero)], axis=0)

    def pair_sink(col):
        base = g * SWA_GROUP + 2 * col
        return jnp.where(first, sink_ref[base], sink_ref[base + 1])

    def finish(r0, col, ov, m, sink):
        o = ov[:, :LANES] / (ov[:, LANES:] + jnp.exp(sink - m))
        o_ref[r0:r0 + tq, col * LANES:(col + 1) * LANES] = jnp.where(lane_half == 0, o[:tq], o[tq:]).astype(o_ref.dtype)

    @pl.when(qi < n_lat_tiles)
    def _():
        for r0 in range(0, q_ref.shape[0], tq):
            q0 = qi * q_ref.shape[0] + r0
            start = pl.multiple_of(jnp.clip(q0 - SWA_WINDOW, 0, n_lat - span), SWA_WINDOW)
            kw = k_ref[pl.ds(start, span), :]
            vw = v_ref[pl.ds(start, span), :]
            qpos = q0 + lax.broadcasted_iota(jnp.int32, (tq, 1), 0)
            kpos = start + lax.broadcasted_iota(jnp.int32, (1, span), 1)
            band = jnp.abs(kpos - qpos) <= SWA_WINDOW
            for col in range(SWA_GROUP // 2):
                q2, sink = pair_query(r0, col), pair_sink(col)
                sw = _dot_nt(q2, kw)
                sw = jnp.concatenate([jnp.where(band, sw[:tq], NEG_INF), jnp.where(band, sw[tq:], NEG_INF)], axis=0)
                sc = _dot_nt(q2, kc)
                m = jnp.maximum(jnp.maximum(jnp.max(sw, axis=-1, keepdims=True),
                                            jnp.max(sc, axis=-1, keepdims=True)), sink)
                pw = jnp.exp(sw - m).astype(BF16)
                pc = jnp.exp(sc - m).astype(BF16)
                finish(r0, col, _dot(pw, vw) + _dot(pc, vc), m, sink)

    @pl.when(qi >= n_lat_tiles)
    def _():
        o_ref[n_ctx:, :] = jnp.zeros((o_ref.shape[0] - n_ctx, o_ref.shape[1]), o_ref.dtype)
        for col in range(SWA_GROUP // 2):
            q2, sink = pair_query(0, col), pair_sink(col)
            sc = _dot_nt(q2, kc)
            m = jnp.maximum(jnp.max(sc, axis=-1, keepdims=True), sink)
            finish(0, col, _dot(jnp.exp(sc - m).astype(BF16), vc), m, sink)


def _swa_attn(sink, q, k, v, n_lat, n_ctx):
    b, t, nq = q.shape
    nt = t // ROW_TILE
    gw = SWA_GROUP * SWA_HEAD_DIM
    kern = functools.partial(_swa_attn_kernel, n_lat_tiles=n_lat // ROW_TILE, n_lat=n_lat, n_ctx=n_ctx)
    kv = lambda n: pl.BlockSpec((None, None, t, n), lambda bi, gi, i: (bi, gi, 0, 0))
    return pl.pallas_call(
        kern,
        out_shape=jax.ShapeDtypeStruct((b, t, nq), BF16),
        grid=(b, SWA_KV_HEADS, nt),
        in_specs=[pl.BlockSpec(memory_space=pltpu.SMEM),
                  pl.BlockSpec((None, ROW_TILE, gw), lambda bi, gi, i: (bi, i, gi)), kv(LANES), kv(2 * LANES)],
        out_specs=pl.BlockSpec((None, ROW_TILE, gw), lambda bi, gi, i: (bi, i, gi)),
        compiler_params=_cparams(3), name="swa_attn",
    )(sink, q, k, v)


def _cf_proj_kernel(x_ref, g_ref, mod_ref, w_ref, u_ref, sg_ref):
    h = _norm_mod(x_ref[...], g_ref[...], mod_ref[...]).astype(BF16)
    c = u_ref.shape[-1]
    tn = c // 2
    for c0 in range(0, c, tn):
        a = _dot(h, w_ref[:, c0:c0 + tn])
        gl = _dot(h, w_ref[:, c + c0:c + c0 + tn])
        u_ref[:, c0:c0 + tn] = a * jax.nn.sigmoid(gl)
        sg_ref[:, c0:c0 + tn] = _silu(_dot(h, w_ref[:, 2 * c + c0:2 * c + c0 + tn])).astype(sg_ref.dtype)


def _cf_proj(x, g, mod, w, n_lat):
    b, _, d = x.shape
    c = w.shape[1] // 3
    full = lambda a: pl.BlockSpec(a.shape, lambda bi, i: (0,) * a.ndim)
    row = lambda bi, i: (bi, i, 0)
    return pl.pallas_call(
        _cf_proj_kernel,
        out_shape=(jax.ShapeDtypeStruct((b, n_lat, c), F32), jax.ShapeDtypeStruct((b, n_lat, c), BF16)),
        grid=(b, n_lat // ROW_TILE),
        in_specs=[pl.BlockSpec((None, ROW_TILE, d), row), full(g),
                  pl.BlockSpec((None, None, 3, d), lambda bi, i: (bi, 0, 0, 0)), full(w)],
        out_specs=(pl.BlockSpec((None, ROW_TILE, c), row),) * 2,
        compiler_params=_cparams(2), name="cf_proj",
    )(x, g, mod, w)


def _cf_tail_kernel(up_ref, u_ref, un_ref, sg_ref, x_ref, mod_ref, dw_ref, db_ref, lg_ref, lb_ref,
                    w_ref, fg_ref, o_ref, ext_ref, win_ref):
    i = pl.program_id(1)
    nt = pl.num_programs(1)
    tm = u_ref.shape[0]
    zero = jnp.zeros(up_ref.shape, F32)
    ext_ref[0:CF_HALO, :] = jnp.where(i > 0, up_ref[...], zero)
    ext_ref[CF_HALO:CF_HALO + tm, :] = u_ref[...]
    ext_ref[CF_HALO + tm:, :] = jnp.where(i < nt - 1, un_ref[...], zero)
    pad = (CF_KERNEL - 1) // 2
    acc = jnp.zeros(u_ref.shape, F32) + db_ref[...]
    top = CF_HALO - pad + CF_KERNEL - 1
    span = tm + (top // SUBLANES) * SUBLANES
    for r in range(SUBLANES):
        taps = [k for k in range(CF_KERNEL) if (CF_HALO - pad + k) % SUBLANES == r]
        if not taps:
            continue
        if r:
            win_ref[...] = ext_ref[pl.ds(r, span), :]
        src = win_ref if r else ext_ref
        for k in taps:
            acc = acc + dw_ref[k:k + 1, :] * src[pl.ds(CF_HALO - pad + k - r, tm), :]
    mu = jnp.mean(acc, axis=-1, keepdims=True)
    xc = acc - mu
    var = jnp.mean(xc * xc, axis=-1, keepdims=True)
    ln = xc * lax.rsqrt(var + NORM_EPS) * lg_ref[...] + lb_ref[...]
    og = (_silu(ln) * sg_ref[...].astype(F32)).astype(BF16)
    xo = x_ref[...] + mod_ref[2:3, :] * _dot(og, w_ref[...])
    o_ref[...] = _rms(xo, fg_ref[...])


def _cf_tail(u, sg, x, mod, dw, db, lg, lb, w_out, fg):
    b, n_lat, c = u.shape
    d = x.shape[-1]
    nt = n_lat // ROW_TILE
    per = ROW_TILE // CF_HALO
    last_halo = n_lat // CF_HALO - 1
    full = lambda a: pl.BlockSpec(a.shape, lambda bi, i: (0,) * a.ndim)
    row = lambda bi, i: (bi, i, 0)
    return pl.pallas_call(
        _cf_tail_kernel,
        out_shape=jax.ShapeDtypeStruct((b, n_lat, d), F32),
        grid=(b, nt),
        in_specs=[pl.BlockSpec((None, CF_HALO, c), lambda bi, i: (bi, jnp.maximum(i * per - 1, 0), 0)),
                  pl.BlockSpec((None, ROW_TILE, c), row),
                  pl.BlockSpec((None, CF_HALO, c), lambda bi, i: (bi, jnp.minimum((i + 1) * per, last_halo), 0)),
                  pl.BlockSpec((None, ROW_TILE, c), row),
                  pl.BlockSpec((None, ROW_TILE, d), row),
                  pl.BlockSpec((None, None, 3, d), lambda bi, i: (bi, 0, 0, 0)),
                  full(dw), full(db), full(lg), full(lb), full(w_out), full(fg)],
        out_specs=pl.BlockSpec((None, ROW_TILE, d), row),
        scratch_shapes=[pltpu.VMEM((ROW_TILE + 2 * CF_HALO, c), F32),
                        pltpu.VMEM((ROW_TILE + 2 * CF_HALO - SUBLANES, c), F32)],
        compiler_params=_cparams(2), name="cf_tail",
    )(u, u, u, sg, x, mod, dw, db, lg, lb, w_out, fg)


def _rope_tables(n_lat, n_ctx):
    rows = n_lat // GRID_W
    row = jnp.repeat(jnp.arange(rows, dtype=F32), GRID_W)
    col = jnp.tile(jnp.arange(GRID_W, dtype=F32), rows)
    n_freq = ROPE_DIM // 4
    inv = ROPE_BASE ** (-jnp.arange(n_freq, dtype=F32) / n_freq)
    ang = jnp.concatenate([row[:, None] * inv, col[:, None] * inv], axis=-1)
    cos = jnp.concatenate([jnp.cos(ang), jnp.ones((n_ctx, ROPE_DIM // 2), F32)], axis=0)
    sin = jnp.concatenate([jnp.sin(ang), jnp.zeros((n_ctx, ROPE_DIM // 2), F32)], axis=0)
    return jnp.tile(cos, (1, 2)), jnp.tile(sin, (1, 2))


def kernel(x, c, ctx, c_ctx, norm_g, ada_w, ada_b, final_g, mla_w_in, mla_q_norm_g, mla_kv_norm_g, mla_w_uq, mla_w_ukv, mla_w_out, hy_w_in, hy_conv_w, hy_conv_b, hy_filt_w_in, hy_filt_w_hid, hy_filt_b, hy_filt_freq, hy_filt_w_out, hy_bias, hy_w_out, swa_w_in, swa_sink, swa_w_out, cf_w_in, cf_dw_w, cf_dw_b, cf_ln_g, cf_ln_b, cf_w_out):
    b, n_lat, d = x.shape
    n_ctx = ctx.shape[1]
    depth = norm_g.shape[0]
    assert depth == 4 and n_lat % ROW_TILE == 0 and n_ctx == ATTN_TILE and ROW_TILE % n_ctx == 0
    assert n_lat % (FFT_N1H * SUBLANES) == 0 and n_lat % GRID_W == 0
    n_lat_tiles = n_lat // ROW_TILE
    bf = lambda a: a.astype(BF16)

    n_pad = ROW_TILE - n_ctx
    xs = jnp.concatenate([x, ctx, jnp.zeros((b, n_pad, d), F32)], axis=1)
    cs = jnp.concatenate([c, c_ctx[None, :], jnp.zeros((SUBLANES - b - 1, d), F32)], axis=0)
    mods = _adaln(cs, ada_w, ada_b)[:, :b + 1].reshape(depth, b + 1, 3, d)
    mods = jnp.stack([mods[:, :b], jnp.broadcast_to(mods[:, b:], (depth, b, 3, d))], axis=2)
    cos64, sin64 = _rope_tables(n_lat, ROW_TILE)

    c2 = MLA_Q_RANK + MLA_KV_RANK + MLA_ROPE
    w_lat, w_gate = bf(mla_w_in[0][:, :c2]), bf(mla_w_in[0][:, c2:])
    wuq = bf(mla_w_uq[0].reshape(MLA_Q_RANK, MLA_HEADS, MLA_QK).transpose(1, 0, 2))
    wukv = bf(mla_w_ukv[0].reshape(MLA_KV_RANK, MLA_HEADS, MLA_NOPE + MLA_V).transpose(1, 0, 2))
    q, k, v, sg = _mla_proj(xs, norm_g[0:1], mods[0], w_lat, w_gate, mla_q_norm_g[0:1], mla_kv_norm_g[0:1],
                            wuq, wukv, cos64, sin64, n_lat_tiles)
    o = _mla_attn(q, k, v, n_lat, n_ctx)
    xs = _outproj(o, sg, xs, mods[0], bf(mla_w_out[0]), n_lat_tiles)

    z0, g0, g1, sg = _hy_proj(xs, norm_g[1:2], mods[1], bf(hy_w_in[0]), hy_conv_w[0], hy_conv_b[0][None, :],
                              n_lat_tiles, n_ctx)
    cch = z0.shape[-1]
    filt = (hy_filt_w_in[0], hy_filt_w_hid[0], hy_filt_b[0], hy_filt_freq[0], hy_filt_w_out[0])
    assert cch % HY_SLAB == 0
    mats = _fft_matrices(n_lat)
    spec = _hy_spec(_hy_filter_taps(n_lat, *filt), mats[1], mats[2])
    cfwd, cinv, nfp = _dense_dft_matrices(n_ctx)
    cspec = _hy_ctx_spec(_hy_filter_taps(n_ctx, *filt), cfwd)
    gates = (g0, g1)
    z = z0
    for order in range(HY_ORDER):
        bias = hy_bias[0][order][None, :]
        zn = _hy_conv(z, gates[order], spec, order, mats, bias, n_lat)
        z = _hy_ctx_conv(z, gates[order], cspec, order, cfwd[:, :n_ctx], cinv, nfp, bias, zn, n_lat, n_ctx)
    xs = _outproj(z, sg, xs, mods[1], bf(hy_w_out[0]), n_lat_tiles)

    nq = SWA_Q_HEADS * SWA_HEAD_DIM
    nkv = SWA_KV_HEADS * SWA_HEAD_DIM
    w_in = swa_w_in[0]
    dup = lambda w: jnp.tile(w.reshape(d, SWA_KV_HEADS, 1, SWA_HEAD_DIM), (1, 1, 2, 1)).reshape(d, 2 * nkv)
    w2 = bf(jnp.concatenate([w_in[:, :nq], dup(w_in[:, nq:nq + nkv]), dup(w_in[:, nq + nkv:nq + 2 * nkv]),
                             w_in[:, nq + 2 * nkv:]], axis=1))
    cos128, sin128 = jnp.tile(cos64, (1, 2)), jnp.tile(sin64, (1, 2))
    q, k, v, sg = _swa_proj(xs, norm_g[2:3], mods[2], w2, cos128, sin128, n_lat_tiles)
    o = _swa_attn(swa_sink[0], q, k, v, n_lat, n_ctx)
    xs = _outproj(o, sg, xs, mods[2], bf(swa_w_out[0]), n_lat_tiles)

    u, sg = _cf_proj(xs, norm_g[3:4], mods[3], bf(cf_w_in[0]), n_lat)
    return _cf_tail(u, sg, xs, mods[3], cf_dw_w[0], cf_dw_b[0][None, :], cf_ln_g[0][None, :], cf_ln_b[0][None, :],
                    bf(cf_w_out[0]), final_g[None, :])
```

```python
import functools
import math

import numpy as np
import jax
import jax.numpy as jnp
from jax import lax
from jax.experimental import pallas as pl
from jax.experimental.pallas import tpu as pltpu

F32 = jnp.float32
BF16 = jnp.bfloat16

GRID_W = 64
NORM_EPS = 1e-6
NEG_INF = -1e30
LOG2E = math.log2(math.e)
ROPE_BASE = 10000.0
ROPE_DIM = 64

MLA_HEADS = 8
MLA_Q_RANK = 384
MLA_KV_RANK = 256
MLA_NOPE = 128
MLA_ROPE = ROPE_DIM
MLA_V = 128
MLA_QK = MLA_NOPE + MLA_ROPE

HY_ORDER = 2
HY_EMB = 33
HY_EMB_PAD = 64
HY_BANDS = (HY_EMB - 1) // 2
HY_DECAY_TARGET = 1e-2
HY_FAST_DECAY = 0.3
HY_SLOW_DECAY = 1.5

SWA_Q_HEADS = 16
SWA_KV_HEADS = 4
SWA_GROUP = SWA_Q_HEADS // SWA_KV_HEADS
SWA_HEAD_DIM = ROPE_DIM
SWA_WINDOW = 128

CF_KERNEL = 31
CF_HALO = 16

ROW_TILE = 512
ATTN_TILE = 256
SUBLANES = 8
LANES = 128
FFT_N1 = 64
FFT_N1H = FFT_N1 // 2
FFT_K1 = FFT_N1 // 2 + 1
HY_SLAB = 256
VMEM_LIMIT = 48 * 1024 * 1024
VMEM_LIMIT_BIG = 56 * 1024 * 1024


def _cparams(n_axes):
    return pltpu.CompilerParams(dimension_semantics=("arbitrary",) * n_axes,
                                vmem_limit_bytes=VMEM_LIMIT)


def _cparams_big(n_axes):
    return pltpu.CompilerParams(dimension_semantics=("arbitrary",) * n_axes,
                                vmem_limit_bytes=VMEM_LIMIT_BIG)


def _once(shape, index_map):
    return pl.BlockSpec(shape, index_map, pipeline_mode=pl.Buffered(1))


def _dot(a, b):
    return jnp.dot(a, b, preferred_element_type=F32)


def _dot_nt(a, b):
    return lax.dot_general(a, b, (((1,), (1,)), ((), ())), preferred_element_type=F32)


def _rms(x, g):
    return x * lax.rsqrt(jnp.mean(x * x, axis=-1, keepdims=True) + NORM_EPS) * g


def _norm_mod(x, g, mod):
    return _rms(x, g) * (1.0 + mod[1:2, :]) + mod[0:1, :]


def _silu(x):
    return x * jax.nn.sigmoid(x)


def _adaln_kernel(c_ref, w_ref, b_ref, o_ref):
    s = _silu(c_ref[...]).astype(BF16)
    o_ref[...] = _dot(s, w_ref[...].astype(BF16)) + b_ref[...]


def _adaln(cs, ada_w, ada_b):
    depth, d, d3 = ada_w.shape
    tn = 512
    return pl.pallas_call(
        _adaln_kernel,
        out_shape=jax.ShapeDtypeStruct((depth, cs.shape[0], d3), F32),
        grid=(depth, d3 // tn),
        in_specs=[pl.BlockSpec(cs.shape, lambda i, j: (0, 0)),
                  pl.BlockSpec((None, d, tn), lambda i, j: (i, 0, j)),
                  pl.BlockSpec((None, 1, tn), lambda i, j: (i, 0, j))],
        out_specs=pl.BlockSpec((None, cs.shape[0], tn), lambda i, j: (i, 0, j)),
        compiler_params=_cparams(2), name="adaln",
    )(cs, ada_w, ada_b.reshape(depth, 1, d3))


def _input_tile(x_ref, ctx_ref, is_ctx):
    pad = jnp.zeros((x_ref.shape[0] - ctx_ref.shape[0], x_ref.shape[1]), F32)
    return jnp.where(is_ctx, jnp.concatenate([ctx_ref[...], pad], axis=0), x_ref[...])


def _input_specs(d, n_ctx, n_lat_tiles):
    return [pl.BlockSpec((None, ROW_TILE, d), lambda bi, i: (bi, jnp.minimum(i, n_lat_tiles - 1), 0)),
            pl.BlockSpec((None, n_ctx, d), lambda bi, i: (bi, 0, 0))]


def _outproj_first_kernel(o_ref, sg_ref, x_ref, ctx_ref, mod_ref, w_ref, xo_ref, *, n_lat_tiles):
    og = (o_ref[...].astype(F32) * sg_ref[...].astype(F32)).astype(BF16)
    x = _input_tile(x_ref, ctx_ref, pl.program_id(1) >= n_lat_tiles)
    xo_ref[...] = x + mod_ref[2:3, :] * _dot(og, w_ref[...])


def _outproj_first(o, sg, x, ctx, mod, w_out, n_lat_tiles):
    b, t, width = o.shape
    d = w_out.shape[1]
    row = lambda bi, i: (bi, i, 0)
    return pl.pallas_call(
        functools.partial(_outproj_first_kernel, n_lat_tiles=n_lat_tiles),
        out_shape=jax.ShapeDtypeStruct((b, t, d), F32),
        grid=(b, t // ROW_TILE),
        in_specs=[pl.BlockSpec((None, ROW_TILE, width), row),
                  pl.BlockSpec((None, ROW_TILE, width), row)] + _input_specs(d, ctx.shape[1], n_lat_tiles) +
                 [pl.BlockSpec((None, None, 3, d), lambda bi, i: (bi, i // n_lat_tiles, 0, 0)),
                  pl.BlockSpec(w_out.shape, lambda bi, i: (0, 0))],
        out_specs=pl.BlockSpec((None, ROW_TILE, d), row),
        compiler_params=_cparams(2), name="outproj",
    )(o, sg, x, ctx, mod, w_out)


def _residual_in(o_ref, sgp_ref, x_ref, modp_ref, wout_ref, xo_ref):
    og = (o_ref[...].astype(F32) * sgp_ref[...].astype(F32)).astype(BF16)
    x = x_ref[...] + modp_ref[2:3, :] * _dot(og, wout_ref[...])
    xo_ref[...] = x
    return x


def _residual_specs(o, d, n_lat_tiles):
    width = o.shape[-1]
    row = lambda bi, i: (bi, i, 0)
    return [pl.BlockSpec((None, ROW_TILE, width), row), pl.BlockSpec((None, ROW_TILE, width), row),
            pl.BlockSpec((None, ROW_TILE, d), row),
            pl.BlockSpec((None, None, 3, d), lambda bi, i: (bi, i // n_lat_tiles, 0, 0)),
            _once((width, d), lambda bi, i: (0, 0))]


def _rope_swap_matrix():
    half = ROPE_DIM // 2
    p = np.zeros((ROPE_DIM, ROPE_DIM), np.float32)
    for j in range(half):
        p[j + half, j] = -1.0
        p[j, j + half] = 1.0
    return p


def _mla_proj_kernel(x_ref, ctx_ref, g_ref, mod_ref, wl_ref, wg_ref, qg_ref, kvg_ref, wuq_ref, wukv_ref,
                     cos_ref, sin_ref, p_ref, q_ref, k_ref, v_ref, sg_ref, *, n_lat_tiles):
    x = _input_tile(x_ref, ctx_ref, pl.program_id(1) >= n_lat_tiles)
    h = _norm_mod(x, g_ref[...], mod_ref[...]).astype(BF16)
    sg_ref[...] = _silu(_dot(h, wg_ref[...])).astype(sg_ref.dtype)
    res = _dot(h, wl_ref[...])
    c0, c1 = MLA_Q_RANK, MLA_Q_RANK + MLA_KV_RANK
    cqn = _rms(res[:, :c0], qg_ref[...]).astype(BF16)
    ckvn = _rms(res[:, c0:c1], kvg_ref[...]).astype(BF16)
    cos, sin, swap = cos_ref[...], sin_ref[...], p_ref[...]

    def rope(t):
        return t * cos + _dot(t.astype(BF16), swap) * sin

    k_rope_t = rope(res[:, c1:c1 + MLA_ROPE]).T.astype(k_ref.dtype)
    scale = MLA_QK ** -0.5 * LOG2E
    for hd in range(MLA_HEADS):
        q = _dot(cqn, wuq_ref[hd])
        q_ref[hd, :, :MLA_NOPE] = (q[:, :MLA_NOPE] * scale).astype(q_ref.dtype)
        q_ref[hd, :, MLA_NOPE:] = (rope(q[:, MLA_NOPE:]) * scale).astype(q_ref.dtype)
        kv = _dot(ckvn, wukv_ref[hd])
        k_ref[hd, :MLA_NOPE, :] = kv[:, :MLA_NOPE].T.astype(k_ref.dtype)
        k_ref[hd, MLA_NOPE:, :] = k_rope_t
        v_ref[hd, :, :MLA_V] = kv[:, MLA_NOPE:].astype(v_ref.dtype)
        v_ref[hd, :, MLA_V:] = jnp.ones((v_ref.shape[1], MLA_V), v_ref.dtype)


def _mla_proj(x, ctx, g, mod, wl, wg, qg, kvg, wuq, wukv, cos, sin, n_lat_tiles):
    b, n_lat, d = x.shape
    nt = n_lat_tiles + 1
    t = nt * ROW_TILE
    width = MLA_HEADS * MLA_V
    swap = jnp.asarray(_rope_swap_matrix()).astype(BF16)
    full = lambda a: _once(a.shape, lambda bi, i: (0,) * a.ndim)
    head_out = lambda n: pl.BlockSpec((None, MLA_HEADS, ROW_TILE, n), lambda bi, i: (bi, 0, i, 0))
    return pl.pallas_call(
        functools.partial(_mla_proj_kernel, n_lat_tiles=n_lat_tiles),
        out_shape=(jax.ShapeDtypeStruct((b, MLA_HEADS, t, MLA_QK), BF16),
                   jax.ShapeDtypeStruct((b, MLA_HEADS, MLA_QK, t), BF16),
                   jax.ShapeDtypeStruct((b, MLA_HEADS, t, 2 * MLA_V), BF16),
                   jax.ShapeDtypeStruct((b, t, width), BF16)),
        grid=(b, nt),
        in_specs=_input_specs(d, ctx.shape[1], n_lat_tiles) +
                 [full(g),
                  pl.BlockSpec((None, None, 3, d), lambda bi, i: (bi, i // n_lat_tiles, 0, 0)),
                  full(wl), full(wg), full(qg), full(kvg), full(wuq), full(wukv),
                  pl.BlockSpec((ROW_TILE, ROPE_DIM), lambda bi, i: (i, 0)),
                  pl.BlockSpec((ROW_TILE, ROPE_DIM), lambda bi, i: (i, 0)),
                  full(swap)],
        out_specs=(head_out(MLA_QK),
                   pl.BlockSpec((None, MLA_HEADS, MLA_QK, ROW_TILE), lambda bi, i: (bi, 0, 0, i)),
                   head_out(2 * MLA_V),
                   pl.BlockSpec((None, ROW_TILE, width), lambda bi, i: (bi, i, 0))),
        compiler_params=_cparams(2), name="mla_proj",
    )(x, ctx, g, mod, wl, wg, qg, kvg, wuq, wukv, cos, sin, swap)


def _attend(q, k_parts, v_parts):
    s = [_dot(q, k) for k in k_parts]
    m = functools.reduce(jnp.maximum, [jnp.max(x, axis=-1, keepdims=True) for x in s])
    ov = functools.reduce(jnp.add, [_dot(jnp.exp2(x - m).astype(BF16), v) for x, v in zip(s, v_parts)])
    n = ov.shape[-1] // 2
    return ov[:, :n] / ov[:, n:]


MLA_HEADS_PER_STEP = 2


def _mla_attn_kernel(q_ref, k_ref, v_ref, o_ref, *, n_lat_tiles, n_lat, n_ctx):
    qi = pl.program_id(2)

    n_keys = n_lat + n_ctx

    @pl.when(qi < n_lat_tiles)
    def _():
        chains = [(hd, r0) for hd in range(MLA_HEADS_PER_STEP) for r0 in range(0, q_ref.shape[1], ATTN_TILE)]
        s = [_dot(q_ref[hd, r0:r0 + ATTN_TILE, :], k_ref[hd, :, 0:n_keys]) for hd, r0 in chains]
        p = [jnp.exp2(x - jnp.max(x, axis=-1, keepdims=True)).astype(BF16) for x in s]
        for (hd, r0), pc in zip(chains, p):
            ov = _dot(pc, v_ref[hd, 0:n_keys, :])
            o = ov[:, :MLA_V] / ov[:, MLA_V:]
            o_ref[r0:r0 + ATTN_TILE, hd * MLA_V:(hd + 1) * MLA_V] = o.astype(o_ref.dtype)

    @pl.when(qi >= n_lat_tiles)
    def _():
        o_ref[n_ctx:, :] = jnp.zeros((o_ref.shape[0] - n_ctx, o_ref.shape[1]), o_ref.dtype)
        for hd in range(MLA_HEADS_PER_STEP):
            k = k_ref[hd, :, n_lat:n_keys]
            v = v_ref[hd, n_lat:n_keys, :]
            o = _attend(q_ref[hd, 0:n_ctx, :], [k], [v])
            o_ref[0:n_ctx, hd * MLA_V:(hd + 1) * MLA_V] = o.astype(o_ref.dtype)


def _mla_attn(q, k, v, n_lat, n_ctx):
    b, hds, t, _ = q.shape
    nt = t // ROW_TILE
    hps = MLA_HEADS_PER_STEP
    kern = functools.partial(_mla_attn_kernel, n_lat_tiles=n_lat // ROW_TILE, n_lat=n_lat, n_ctx=n_ctx)
    return pl.pallas_call(
        kern,
        out_shape=jax.ShapeDtypeStruct((b, t, hds * MLA_V), BF16),
        grid=(b, hds // hps, nt),
        in_specs=[pl.BlockSpec((None, hps, ROW_TILE, MLA_QK), lambda bi, h, i: (bi, h, i, 0)),
                  pl.BlockSpec((None, hps, MLA_QK, t), lambda bi, h, i: (bi, h, 0, 0)),
                  pl.BlockSpec((None, hps, t, 2 * MLA_V), lambda bi, h, i: (bi, h, 0, 0))],
        out_specs=pl.BlockSpec((None, ROW_TILE, hps * MLA_V), lambda bi, h, i: (bi, i, h)),
        compiler_params=_cparams_big(3), name="mla_attn",
    )(q, k, v)


def _hy_proj_kernel(xp_ref, x_ref, xn_ref, g_ref, mod_ref, w_ref, cw_ref, cb_ref,
                    z_ref, g0_ref, g1_ref, sg_ref, *, n_lat_tiles, n_ctx):
    i = pl.program_id(1)
    tm = x_ref.shape[0]
    c = z_ref.shape[-1]
    n = tm + 2 * SUBLANES
    xa = jnp.concatenate([xp_ref[...], x_ref[...], xn_ref[...]], axis=0)
    hf = _norm_mod(xa, g_ref[...], mod_ref[...])
    h = hf.astype(BF16)
    ctx_tile = i == n_lat_tiles
    first = jnp.logical_or(i == 0, ctx_tile)
    last = jnp.logical_or(i == n_lat_tiles - 1, ctx_tile)
    rows = lax.broadcasted_iota(jnp.int32, (n, 1), 0)
    end = jnp.where(ctx_tile, n_ctx + SUBLANES, jnp.where(last, tm + SUBLANES, n))
    valid = jnp.logical_and(jnp.logical_or(rows >= SUBLANES, jnp.logical_not(first)), rows < end)
    outs = (z_ref, g0_ref, g1_ref)
    for j in range(len(outs)):
        u = jnp.where(valid, _dot(h, w_ref[:, j * c:(j + 1) * c]), 0.0)
        cw = cw_ref[:, j * c:(j + 1) * c]
        cv = cw[0:1, :] * pltpu.roll(u, 1, 0) + cw[1:2, :] * u + cw[2:3, :] * pltpu.roll(u, n - 1, 0)
        outs[j][...] = cv[SUBLANES:SUBLANES + tm] + cb_ref[:, j * c:(j + 1) * c]
    h_mid = hf[SUBLANES:SUBLANES + tm].astype(BF16)
    sg_ref[...] = _silu(_dot(h_mid, w_ref[:, len(outs) * c:])).astype(sg_ref.dtype)


def _hy_proj(x, g, mod, w, cw, cb, n_lat_tiles, n_ctx):
    b, t, d = x.shape
    nt = t // ROW_TILE
    c = w.shape[1] // (HY_ORDER + 2)
    per = ROW_TILE // SUBLANES
    last_halo = t // SUBLANES - 1
    full = lambda a: _once(a.shape, lambda bi, i: (0,) * a.ndim)
    row = lambda bi, i: (bi, i, 0)
    kern = functools.partial(_hy_proj_kernel, n_lat_tiles=n_lat_tiles, n_ctx=n_ctx)
    return pl.pallas_call(
        kern,
        out_shape=(jax.ShapeDtypeStruct((b, t, c), F32),) * 3 + (jax.ShapeDtypeStruct((b, t, c), BF16),),
        grid=(b, nt),
        in_specs=[pl.BlockSpec((None, SUBLANES, d), lambda bi, i: (bi, jnp.maximum(i * per - 1, 0), 0)),
                  pl.BlockSpec((None, ROW_TILE, d), row),
                  pl.BlockSpec((None, SUBLANES, d), lambda bi, i: (bi, jnp.minimum((i + 1) * per, last_halo), 0)),
                  full(g),
                  pl.BlockSpec((None, None, 3, d), lambda bi, i: (bi, i // n_lat_tiles, 0, 0)),
                  full(w), full(cw), full(cb)],
        out_specs=(pl.BlockSpec((None, ROW_TILE, c), row),) * 4,
        compiler_params=_cparams(2), name="hy_proj",
    )(x, x, x, g, mod, w, cw, cb)


def _hy_filter_kernel(ef_ref, eb_ref, w0_ref, wh_ref, b_ref, fr_ref, wf_ref, wb_ref, dl_ref, o_ref):
    i = pl.program_id(0)
    c = dl_ref.shape[-1]

    def taps(e, wo):
        hdn = e
        ws = (w0_ref[...], wh_ref[0], wh_ref[1])
        for k in range(3):
            hdn = jnp.sin(fr_ref[k:k + 1, :] * (_dot(hdn.astype(BF16), ws[k].astype(BF16)) + b_ref[k:k + 1, :]))
        decay = jnp.exp(-e[:, 0:1] * dl_ref[...])
        hh = _dot(hdn.astype(BF16), wo.astype(BF16))
        return [hh[:, o * c:(o + 1) * c] * decay for o in range(HY_ORDER)]

    hf = taps(ef_ref[...], wf_ref[...])
    hb = taps(eb_ref[...], wb_ref[...])
    rows = lax.broadcasted_iota(jnp.int32, (ef_ref.shape[0], 1), 0)
    lag0 = jnp.logical_and(rows == 0, i == 0)
    for o in range(HY_ORDER):
        o_ref[o, 0] = hf[o] + jnp.where(lag0, hb[o], 0.0)
        o_ref[o, 1] = jnp.where(lag0, 0.0, hb[o])


def _hy_position_features(n):
    t = jnp.linspace(0.0, 1.0, n, dtype=F32)[:, None]
    wpos = (2.0 * math.pi / n) * jnp.arange(n, dtype=F32)[:, None]
    bands = jnp.linspace(1e-4, HY_BANDS - 1, HY_BANDS, dtype=F32)[None, :]
    hdn = jnp.concatenate([t, jnp.cos(bands * wpos), -jnp.sin(bands * wpos)], axis=-1)
    return jnp.pad(hdn, ((0, 0), (0, HY_EMB_PAD - HY_EMB)))


def _hy_filter_taps(n, w_in, w_hid, bias, freq, w_out):
    c = w_out.shape[1] // (2 * HY_ORDER)
    ef = _hy_position_features(n)
    rev = (n - jnp.arange(n)) % n
    eb = ef[rev]
    w0 = jnp.pad(w_in, ((0, HY_EMB_PAD - HY_EMB), (0, 0)))
    wo = w_out.reshape(w_out.shape[0], HY_ORDER, 2, c)
    wf = wo[:, :, 0, :].reshape(w_out.shape[0], HY_ORDER * c)
    wb = wo[:, :, 1, :].reshape(w_out.shape[0], HY_ORDER * c)
    deltas = jnp.abs(jnp.linspace(math.log(HY_DECAY_TARGET) / HY_FAST_DECAY,
                                  math.log(HY_DECAY_TARGET) / HY_SLOW_DECAY, c, dtype=F32))[None, :]
    tl = min(n, 512)
    full = lambda a: pl.BlockSpec(a.shape, lambda i: (0,) * a.ndim)
    return pl.pallas_call(
        _hy_filter_kernel,
        out_shape=jax.ShapeDtypeStruct((HY_ORDER, 2, n, c), F32),
        grid=(n // tl,),
        in_specs=[pl.BlockSpec((tl, HY_EMB_PAD), lambda i: (i, 0)),
                  pl.BlockSpec((tl, HY_EMB_PAD), lambda i: (i, 0)),
                  full(w0), full(w_hid), full(bias), full(freq), full(wf), full(wb), full(deltas)],
        out_specs=pl.BlockSpec((HY_ORDER, 2, tl, c), lambda i: (0, 0, i, 0)),
        compiler_params=_cparams(1), name="hy_filter",
    )(ef, eb, w0, w_hid, bias, freq, wf, wb, deltas)


def _fft_matrices(n_lat):
    n = 2 * n_lat
    n2 = n_lat // FFT_N1H
    k1 = np.arange(FFT_K1)
    n1 = np.arange(FFT_N1H)
    ang = 2.0 * np.pi * np.outer(k1, n1) / FFT_N1
    eye = np.eye(SUBLANES)
    f1 = np.kron(np.concatenate([np.cos(ang), -np.sin(ang)], axis=0), eye)
    wgt = np.full(FFT_K1, 2.0)
    wgt[0] = wgt[-1] = 1.0
    g1 = np.kron(np.concatenate([wgt * np.cos(ang.T), -wgt * np.sin(ang.T)], axis=1) / n, eye)
    idx = np.arange(n2)
    f2 = np.zeros((FFT_K1, 2 * n2, 2 * n2))
    f2i = np.zeros((FFT_K1, 2 * n2, 2 * n2))
    for a in range(FFT_K1):
        ph = 2.0 * np.pi * (np.outer(idx, idx) / n2 + (a * idx)[None, :] / n)
        f2[a] = np.block([[np.cos(ph), np.sin(ph)], [-np.sin(ph), np.cos(ph)]])
        f2i[a] = np.block([[np.cos(ph.T), -np.sin(ph.T)], [np.sin(ph.T), np.cos(ph.T)]])
    sign = np.kron(np.tile((-1.0) ** k1, 2), np.ones(SUBLANES))[:, None]
    f1s = np.concatenate([f1, sign * f1], axis=1)
    as_bf16 = lambda m: jnp.asarray(m.astype(np.float32)).astype(BF16)
    return as_bf16(f1), as_bf16(f1s), as_bf16(f2), as_bf16(f2i), as_bf16(g1)


def _dense_dft_matrices(n_ctx):
    n = 2 * n_ctx
    nf = n_ctx + 1
    nfp = -(-nf // SUBLANES) * SUBLANES
    k = np.arange(nf)
    ang = 2.0 * np.pi * np.outer(k, np.arange(n)) / n
    fwd = np.zeros((2 * nfp, n))
    fwd[:nf] = np.cos(ang)
    fwd[nfp:nfp + nf] = -np.sin(ang)
    wgt = np.full(nf, 2.0)
    wgt[0] = wgt[-1] = 1.0
    angi = 2.0 * np.pi * np.outer(np.arange(n_ctx), k) / n
    inv = np.zeros((n_ctx, 2 * nfp))
    inv[:, :nf] = wgt * np.cos(angi) / n
    inv[:, nfp:nfp + nf] = -wgt * np.sin(angi) / n
    as_bf16 = lambda m: jnp.asarray(m.astype(np.float32)).astype(BF16)
    return as_bf16(fwd), as_bf16(inv), nfp


def _hy_spec_kernel(t_ref, f1s_ref, f2_ref, h_ref, a_ref):
    _, n1h, nj, s, tc = t_ref.shape
    n2 = nj * s

    def stage1(j, carry):
        sig = jnp.concatenate([t_ref[0, :, j].reshape(n1h * s, tc), t_ref[1, :, j].reshape(n1h * s, tc)], axis=0)
        a_ref[:, :, j] = _dot(f1s_ref[...], sig.astype(BF16)).reshape(2, FFT_K1, s, tc)
        return carry

    lax.fori_loop(0, nj, stage1, 0, unroll=min(nj, 4))

    def stage2(k, carry):
        a = a_ref[:, k].reshape(2 * n2, tc).astype(BF16)
        h_ref[:, k] = _dot(f2_ref[k], a).reshape(2, n2, tc).astype(h_ref.dtype)
        return carry

    lax.fori_loop(0, FFT_K1, stage2, 0, unroll=11)


def _hy_spec(taps, f1s, f2):
    _, _, n_lat, c = taps.shape
    n2 = n_lat // FFT_N1H
    nj = n2 // SUBLANES
    tc = HY_SLAB
    return pl.pallas_call(
        _hy_spec_kernel,
        out_shape=jax.ShapeDtypeStruct((HY_ORDER, 2, FFT_K1, n2, c), BF16),
        grid=(c // tc, HY_ORDER),
        in_specs=[pl.BlockSpec((None, 2, FFT_N1H, nj, SUBLANES, tc), lambda ci, o: (o, 0, 0, 0, 0, ci)),
                  _once(f1s.shape, lambda ci, o: (0, 0)),
                  _once(f2.shape, lambda ci, o: (0, 0, 0))],
        out_specs=pl.BlockSpec((None, 2, FFT_K1, n2, tc), lambda ci, o: (o, 0, 0, 0, ci)),
        scratch_shapes=[pltpu.VMEM((2, FFT_K1, nj, SUBLANES, tc), F32)],
        compiler_params=_cparams_big(2), name="hy_spec",
    )(taps.reshape(HY_ORDER, 2, FFT_N1H, nj, SUBLANES, c), f1s, f2)


def _hy_conv_kernel(z_ref, gt_ref, h_ref, f1_ref, f2_ref, f2i_ref, g1_ref, bias_ref, o_ref, a_ref):
    n1h, nj, s, tc = z_ref.shape
    n2 = nj * s

    def stage1(j, carry):
        zz = z_ref[:, j].reshape(n1h * s, tc).astype(BF16)
        a_ref[:, :, j] = _dot(f1_ref[...], zz).reshape(2, FFT_K1, s, tc)
        return carry

    lax.fori_loop(0, nj, stage1, 0, unroll=min(nj, 4))

    def stage2(k, carry):
        x = _dot(f2_ref[k], a_ref[:, k].reshape(2 * n2, tc).astype(BF16))
        xr, xi = x[:n2], x[n2:]
        hr, hi = h_ref[0, k].astype(F32), h_ref[1, k].astype(F32)
        y = jnp.concatenate([xr * hr - xi * hi, xr * hi + xi * hr], axis=0).astype(BF16)
        a_ref[:, k] = _dot(f2i_ref[k], y).reshape(2, nj, s, tc)
        return carry

    lax.fori_loop(0, FFT_K1, stage2, 0, unroll=11)

    def stage3(j, carry):
        bp = a_ref[:, :, j].reshape(2 * FFT_K1 * s, tc).astype(BF16)
        y = _dot(g1_ref[...], bp).reshape(n1h, s, tc)
        o_ref[:, j] = gt_ref[:, j] * (y + z_ref[:, j] * bias_ref[...])
        return carry

    lax.fori_loop(0, nj, stage3, 0, unroll=min(nj, 4))


def _hy_conv(z, gate, h, order, mats, bias, n_lat):
    f1, _, f2, f2i, g1 = mats
    bz, t, c = z.shape
    n2 = n_lat // FFT_N1H
    nj = n2 // SUBLANES
    tc = HY_SLAB
    view = (bz, t // n2, nj, SUBLANES, c)
    slab = pl.BlockSpec((None, FFT_N1H, nj, SUBLANES, tc), lambda ci, bi: (bi, 0, 0, 0, ci))
    out = pl.pallas_call(
        _hy_conv_kernel,
        out_shape=jax.ShapeDtypeStruct(view, F32),
        grid=(c // tc, bz),
        in_specs=[slab, slab,
                  _once((None, 2, FFT_K1, n2, tc), lambda ci, bi: (order, 0, 0, 0, ci)),
                  _once(f1.shape, lambda ci, bi: (0, 0)),
                  _once(f2.shape, lambda ci, bi: (0, 0, 0)),
                  _once(f2i.shape, lambda ci, bi: (0, 0, 0)),
                  _once(g1.shape, lambda ci, bi: (0, 0)),
                  pl.BlockSpec((1, tc), lambda ci, bi: (0, ci))],
        out_specs=slab,
        scratch_shapes=[pltpu.VMEM((2, FFT_K1, nj, SUBLANES, tc), F32)],
        compiler_params=_cparams_big(2), name="hy_conv",
    )(z.reshape(view), gate.reshape(view), h, f1, f2, f2i, g1, bias)
    return out.reshape(bz, t, c)


def _hy_ctx_spec_kernel(t_ref, f_ref, h_ref):
    sig = jnp.concatenate([t_ref[0], t_ref[1]], axis=0).astype(BF16)
    h_ref[...] = _dot(f_ref[...], sig)


def _hy_ctx_spec(taps, fwd):
    _, _, n_ctx, c = taps.shape
    return pl.pallas_call(
        _hy_ctx_spec_kernel,
        out_shape=jax.ShapeDtypeStruct((HY_ORDER, fwd.shape[0], c), F32),
        grid=(HY_ORDER,),
        in_specs=[pl.BlockSpec((None, 2, n_ctx, c), lambda o: (o, 0, 0, 0)),
                  pl.BlockSpec(fwd.shape, lambda o: (0, 0))],
        out_specs=pl.BlockSpec((None, fwd.shape[0], c), lambda o: (o, 0, 0)),
        compiler_params=_cparams(1), name="hy_ctx_spec",
    )(taps, fwd)


def _hy_ctx_conv_kernel(zin_ref, gt_ref, h_ref, f_ref, fi_ref, bias_ref, dst_ref, o_ref, *, nfp):
    del dst_ref
    z = zin_ref[...]
    x = _dot(f_ref[...], z.astype(BF16))
    xr, xi = x[:nfp], x[nfp:]
    hr, hi = h_ref[:nfp, :], h_ref[nfp:, :]
    y = jnp.concatenate([xr * hr - xi * hi, xr * hi + xi * hr], axis=0).astype(BF16)
    n_ctx = z.shape[0]
    o_ref[0:n_ctx, :] = gt_ref[...] * (_dot(fi_ref[...], y) + z * bias_ref[...])
    o_ref[n_ctx:, :] = jnp.zeros((o_ref.shape[0] - n_ctx, o_ref.shape[1]), o_ref.dtype)


def _hy_ctx_conv(z, gate, h, order, fwd_half, inv, nfp, bias, dst, n_lat, n_ctx):
    bz, t, c = z.shape
    blk = pl.BlockSpec((None, n_ctx, c), lambda bi: (bi, n_lat // n_ctx, 0))
    kern = functools.partial(_hy_ctx_conv_kernel, nfp=nfp)
    return pl.pallas_call(
        kern,
        out_shape=jax.ShapeDtypeStruct(dst.shape, F32),
        grid=(bz,),
        in_specs=[blk, blk,
                  pl.BlockSpec((None, 2 * nfp, c), lambda bi: (order, 0, 0)),
                  pl.BlockSpec(fwd_half.shape, lambda bi: (0, 0)),
                  pl.BlockSpec(inv.shape, lambda bi: (0, 0)),
                  pl.BlockSpec((1, c), lambda bi: (0, 0)),
                  pl.BlockSpec(memory_space=pl.ANY)],
        out_specs=pl.BlockSpec((None, t - n_lat, c), lambda bi: (bi, n_lat // (t - n_lat), 0)),
        input_output_aliases={6: 0},
        compiler_params=_cparams(1), name="hy_ctx_conv",
    )(z, gate, h, fwd_half, inv, bias, dst)


def _rope_lanes(x, cos, sin):
    half = ROPE_DIM // 2
    lane = lax.broadcasted_iota(jnp.int32, x.shape, 1) % ROPE_DIM
    rot = jnp.where(lane < half, -pltpu.roll(x, LANES - half, 1), pltpu.roll(x, half, 1))
    return x * cos + rot * sin


def _swa_proj_kernel(o_ref, sgp_ref, x_ref, modp_ref, wout_ref, g_ref, mod_ref, w_ref, cos_ref, sin_ref,
                     xo_ref, q_ref, k_ref, v_ref, sg_ref):
    x = _residual_in(o_ref, sgp_ref, x_ref, modp_ref, wout_ref, xo_ref)
    h = _norm_mod(x, g_ref[...], mod_ref[...]).astype(BF16)
    cos, sin = cos_ref[...], sin_ref[...]
    nq = q_ref.shape[-1]
    nk = SWA_KV_HEADS * LANES
    scale = SWA_HEAD_DIM ** -0.5 * LOG2E
    for c0 in range(0, nq, nk):
        res = _dot(h, w_ref[:, c0:c0 + nk])
        for cb in range(nk // LANES):
            sl = slice(cb * LANES, (cb + 1) * LANES)
            q_ref[:, c0 + cb * LANES:c0 + (cb + 1) * LANES] = (_rope_lanes(res[:, sl], cos, sin) * scale).astype(q_ref.dtype)
    res = _dot(h, w_ref[:, nq:nq + nk])
    for hd in range(SWA_KV_HEADS):
        k_ref[hd] = _rope_lanes(res[:, hd * LANES:(hd + 1) * LANES], cos, sin).astype(k_ref.dtype)
    res = _dot(h, w_ref[:, nq + nk:nq + 2 * nk])
    for hd in range(SWA_KV_HEADS):
        v_ref[hd, :, :LANES] = res[:, hd * LANES:(hd + 1) * LANES].astype(v_ref.dtype)
        v_ref[hd, :, LANES:] = jnp.ones((v_ref.shape[1], LANES), v_ref.dtype)
    for c0 in range(0, nq, nk):
        sg_ref[:, c0:c0 + nk] = _silu(_dot(h, w_ref[:, nq + 2 * nk + c0:nq + 2 * nk + c0 + nk])).astype(sg_ref.dtype)


def _swa_proj(o, sgp, x, modp, w_out, g, mod, w, cos, sin, n_lat_tiles):
    b, t, d = x.shape
    nt = t // ROW_TILE
    nq = SWA_Q_HEADS * SWA_HEAD_DIM
    full = lambda a: _once(a.shape, lambda bi, i: (0,) * a.ndim)
    row = lambda bi, i: (bi, i, 0)
    kv = lambda n: pl.BlockSpec((None, SWA_KV_HEADS, ROW_TILE, n), lambda bi, i: (bi, 0, i, 0))
    return pl.pallas_call(
        _swa_proj_kernel,
        out_shape=(jax.ShapeDtypeStruct((b, t, d), F32),
                   jax.ShapeDtypeStruct((b, t, nq), BF16),
                   jax.ShapeDtypeStruct((b, SWA_KV_HEADS, t, LANES), BF16),
                   jax.ShapeDtypeStruct((b, SWA_KV_HEADS, t, 2 * LANES), BF16),
                   jax.ShapeDtypeStruct((b, t, nq), BF16)),
        grid=(b, nt),
        in_specs=_residual_specs(o, d, n_lat_tiles) +
                 [full(g),
                  pl.BlockSpec((None, None, 3, d), lambda bi, i: (bi, i // n_lat_tiles, 0, 0)),
                  full(w),
                  pl.BlockSpec((ROW_TILE, LANES), lambda bi, i: (i, 0)),
                  pl.BlockSpec((ROW_TILE, LANES), lambda bi, i: (i, 0))],
        out_specs=(pl.BlockSpec((None, ROW_TILE, d), row), pl.BlockSpec((None, ROW_TILE, nq), row),
                   kv(LANES), kv(2 * LANES), pl.BlockSpec((None, ROW_TILE, nq), row)),
        compiler_params=_cparams(2), name="swa_proj",
    )(o, sgp, x, modp, w_out, g, mod, w, cos, sin)


def _swa_attn_kernel(sink_ref, q_ref, k_ref, v_ref, o_ref, *, n_lat_tiles, n_lat, n_ctx):
    g = pl.program_id(1)
    qi = pl.program_id(2)
    tq = ATTN_TILE
    span = tq + 2 * SWA_WINDOW
    lane_half = lax.broadcasted_iota(jnp.int32, (tq, LANES), 1) // SWA_HEAD_DIM
    first = lax.broadcasted_iota(jnp.int32, (2 * tq, 1), 0) < tq
    kc = k_ref[pl.ds(n_lat, n_ctx), :]
    vc = v_ref[pl.ds(n_lat, n_ctx), :]

    def pair_query(r0, col):
        qcol = q_ref[r0:r0 + tq, col * LANES:(col + 1) * LANES]
        zero = jnp.zeros_like(qcol)
        return jnp.concatenate([jnp.where(lane_half == 0, qcol, zero), jnp.where(lane_half == 1, qcol, zero)], axis=0)

    def pair_sink(col):
        base = g * SWA_GROUP + 2 * col
        return jnp.where(first, sink_ref[base], sink_ref[base + 1]) * LOG2E

    def finish(r0, col, ov, m, sink):
        o = ov[:, :LANES] / (ov[:, LANES:] + jnp.exp2(sink - m))
        o_ref[r0:r0 + tq, col * LANES:(col + 1) * LANES] = jnp.where(lane_half == 0, o[:tq], o[tq:]).astype(o_ref.dtype)

    @pl.when(qi < n_lat_tiles)
    def _():
        chains = []
        for r0 in range(0, q_ref.shape[0], tq):
            q0 = qi * q_ref.shape[0] + r0
            start = pl.multiple_of(jnp.clip(q0 - SWA_WINDOW, 0, n_lat - span), SWA_WINDOW)
            kw = k_ref[pl.ds(start, span), :]
            vw = v_ref[pl.ds(start, span), :]
            qpos = q0 + lax.broadcasted_iota(jnp.int32, (tq, 1), 0)
            kpos = start + lax.broadcasted_iota(jnp.int32, (1, span), 1)
            band = jnp.abs(kpos - qpos) <= SWA_WINDOW
            for col in range(SWA_GROUP // 2):
                q2, sink = pair_query(r0, col), pair_sink(col)
                sw = _dot_nt(q2, kw)
                sw = jnp.concatenate([jnp.where(band, sw[:tq], NEG_INF), jnp.where(band, sw[tq:], NEG_INF)], axis=0)
                sc = _dot_nt(q2, kc)
                m = jnp.maximum(jnp.maximum(jnp.max(sw, axis=-1, keepdims=True),
                                            jnp.max(sc, axis=-1, keepdims=True)), sink)
                chains.append((r0, col, jnp.exp2(sw - m).astype(BF16), jnp.exp2(sc - m).astype(BF16), vw, m, sink))
        for r0, col, pw, pc, vw, m, sink in chains:
            finish(r0, col, _dot(pw, vw) + _dot(pc, vc), m, sink)

    @pl.when(qi >= n_lat_tiles)
    def _():
        o_ref[n_ctx:, :] = jnp.zeros((o_ref.shape[0] - n_ctx, o_ref.shape[1]), o_ref.dtype)
        for col in range(SWA_GROUP // 2):
            q2, sink = pair_query(0, col), pair_sink(col)
            sc = _dot_nt(q2, kc)
            m = jnp.maximum(jnp.max(sc, axis=-1, keepdims=True), sink)
            finish(0, col, _dot(jnp.exp2(sc - m).astype(BF16), vc), m, sink)


def _swa_attn(sink, q, k, v, n_lat, n_ctx):
    b, t, nq = q.shape
    nt = t // ROW_TILE
    gw = SWA_GROUP * SWA_HEAD_DIM
    kern = functools.partial(_swa_attn_kernel, n_lat_tiles=n_lat // ROW_TILE, n_lat=n_lat, n_ctx=n_ctx)
    kv = lambda n: pl.BlockSpec((None, None, t, n), lambda bi, gi, i: (bi, gi, 0, 0))
    return pl.pallas_call(
        kern,
        out_shape=jax.ShapeDtypeStruct((b, t, nq), BF16),
        grid=(b, SWA_KV_HEADS, nt),
        in_specs=[pl.BlockSpec(memory_space=pltpu.SMEM),
                  pl.BlockSpec((None, ROW_TILE, gw), lambda bi, gi, i: (bi, i, gi)), kv(LANES), kv(2 * LANES)],
        out_specs=pl.BlockSpec((None, ROW_TILE, gw), lambda bi, gi, i: (bi, i, gi)),
        compiler_params=_cparams(3), name="swa_attn",
    )(sink, q, k, v)


def _cf_proj_kernel(o_ref, sgp_ref, x_ref, modp_ref, wout_ref, g_ref, mod_ref, w_ref, xo_ref, u_ref, sg_ref,
                    *, n_lat_tiles):
    x = _residual_in(o_ref, sgp_ref, x_ref, modp_ref, wout_ref, xo_ref)

    @pl.when(pl.program_id(1) < n_lat_tiles)
    def _():
        h = _norm_mod(x, g_ref[...], mod_ref[...]).astype(BF16)
        c = u_ref.shape[-1]
        tn = c // 2
        for c0 in range(0, c, tn):
            a = _dot(h, w_ref[:, c0:c0 + tn])
            gl = _dot(h, w_ref[:, c + c0:c + c0 + tn])
            u_ref[:, c0:c0 + tn] = a * jax.nn.sigmoid(gl)
            sg_ref[:, c0:c0 + tn] = _silu(_dot(h, w_ref[:, 2 * c + c0:2 * c + c0 + tn])).astype(sg_ref.dtype)


def _cf_proj(o, sgp, x, modp, w_out, g, mod, w, n_lat_tiles):
    b, t, d = x.shape
    c = w.shape[1] // 3
    n_lat = n_lat_tiles * ROW_TILE
    full = lambda a: _once(a.shape, lambda bi, i: (0,) * a.ndim)
    row = lambda bi, i: (bi, i, 0)
    lat_row = lambda bi, i: (bi, jnp.minimum(i, n_lat_tiles - 1), 0)
    return pl.pallas_call(
        functools.partial(_cf_proj_kernel, n_lat_tiles=n_lat_tiles),
        out_shape=(jax.ShapeDtypeStruct((b, t, d), F32),
                   jax.ShapeDtypeStruct((b, n_lat, c), F32), jax.ShapeDtypeStruct((b, n_lat, c), BF16)),
        grid=(b, t // ROW_TILE),
        in_specs=_residual_specs(o, d, n_lat_tiles) +
                 [full(g), pl.BlockSpec((None, None, 3, d), lambda bi, i: (bi, 0, 0, 0)), full(w)],
        out_specs=(pl.BlockSpec((None, ROW_TILE, d), row),
                   pl.BlockSpec((None, ROW_TILE, c), lat_row), pl.BlockSpec((None, ROW_TILE, c), lat_row)),
        compiler_params=_cparams(2), name="cf_proj",
    )(o, sgp, x, modp, w_out, g, mod, w)


def _cf_tail_kernel(up_ref, u_ref, un_ref, sg_ref, x_ref, mod_ref, dw_ref, db_ref, lg_ref, lb_ref,
                    w_ref, fg_ref, o_ref, ext_ref, win_ref):
    i = pl.program_id(1)
    nt = pl.num_programs(1)
    tm = u_ref.shape[0]
    zero = jnp.zeros(up_ref.shape, F32)
    ext_ref[0:CF_HALO, :] = jnp.where(i > 0, up_ref[...], zero)
    ext_ref[CF_HALO:CF_HALO + tm, :] = u_ref[...]
    ext_ref[CF_HALO + tm:, :] = jnp.where(i < nt - 1, un_ref[...], zero)
    pad = (CF_KERNEL - 1) // 2
    acc = jnp.zeros(u_ref.shape, F32) + db_ref[...]
    top = CF_HALO - pad + CF_KERNEL - 1
    span = tm + (top // SUBLANES) * SUBLANES
    for r in range(SUBLANES):
        taps = [k for k in range(CF_KERNEL) if (CF_HALO - pad + k) % SUBLANES == r]
        if not taps:
            continue
        if r:
            win_ref[...] = ext_ref[pl.ds(r, span), :]
        src = win_ref if r else ext_ref
        for k in taps:
            acc = acc + dw_ref[k:k + 1, :] * src[pl.ds(CF_HALO - pad + k - r, tm), :]
    mu = jnp.mean(acc, axis=-1, keepdims=True)
    xc = acc - mu
    var = jnp.mean(xc * xc, axis=-1, keepdims=True)
    ln = xc * lax.rsqrt(var + NORM_EPS) * lg_ref[...] + lb_ref[...]
    og = (_silu(ln) * sg_ref[...].astype(F32)).astype(BF16)
    xo = x_ref[...] + mod_ref[2:3, :] * _dot(og, w_ref[...])
    o_ref[...] = _rms(xo, fg_ref[...])


def _cf_tail(u, sg, x, mod, dw, db, lg, lb, w_out, fg):
    b, n_lat, c = u.shape
    d = x.shape[-1]
    nt = n_lat // ROW_TILE
    per = ROW_TILE // CF_HALO
    last_halo = n_lat // CF_HALO - 1
    full = lambda a: pl.BlockSpec(a.shape, lambda bi, i: (0,) * a.ndim)
    row = lambda bi, i: (bi, i, 0)
    return pl.pallas_call(
        _cf_tail_kernel,
        out_shape=jax.ShapeDtypeStruct((b, n_lat, d), F32),
        grid=(b, nt),
        in_specs=[pl.BlockSpec((None, CF_HALO, c), lambda bi, i: (bi, jnp.maximum(i * per - 1, 0), 0)),
                  pl.BlockSpec((None, ROW_TILE, c), row),
                  pl.BlockSpec((None, CF_HALO, c), lambda bi, i: (bi, jnp.minimum((i + 1) * per, last_halo), 0)),
                  pl.BlockSpec((None, ROW_TILE, c), row),
                  pl.BlockSpec((None, ROW_TILE, d), row),
                  pl.BlockSpec((None, None, 3, d), lambda bi, i: (bi, 0, 0, 0)),
                  full(dw), full(db), full(lg), full(lb), full(w_out), full(fg)],
        out_specs=pl.BlockSpec((None, ROW_TILE, d), row),
        scratch_shapes=[pltpu.VMEM((ROW_TILE + 2 * CF_HALO, c), F32),
                        pltpu.VMEM((ROW_TILE + 2 * CF_HALO - SUBLANES, c), F32)],
        compiler_params=_cparams(2), name="cf_tail",
    )(u, u, u, sg, x, mod, dw, db, lg, lb, w_out, fg)


def _rope_tables(n_lat, n_ctx):
    rows = n_lat // GRID_W
    row = jnp.repeat(jnp.arange(rows, dtype=F32), GRID_W)
    col = jnp.tile(jnp.arange(GRID_W, dtype=F32), rows)
    n_freq = ROPE_DIM // 4
    inv = ROPE_BASE ** (-jnp.arange(n_freq, dtype=F32) / n_freq)
    ang = jnp.concatenate([row[:, None] * inv, col[:, None] * inv], axis=-1)
    cos = jnp.concatenate([jnp.cos(ang), jnp.ones((n_ctx, ROPE_DIM // 2), F32)], axis=0)
    sin = jnp.concatenate([jnp.sin(ang), jnp.zeros((n_ctx, ROPE_DIM // 2), F32)], axis=0)
    return jnp.tile(cos, (1, 2)), jnp.tile(sin, (1, 2))


def kernel(x, c, ctx, c_ctx, norm_g, ada_w, ada_b, final_g, mla_w_in, mla_q_norm_g, mla_kv_norm_g, mla_w_uq, mla_w_ukv, mla_w_out, hy_w_in, hy_conv_w, hy_conv_b, hy_filt_w_in, hy_filt_w_hid, hy_filt_b, hy_filt_freq, hy_filt_w_out, hy_bias, hy_w_out, swa_w_in, swa_sink, swa_w_out, cf_w_in, cf_dw_w, cf_dw_b, cf_ln_g, cf_ln_b, cf_w_out):
    b, n_lat, d = x.shape
    n_ctx = ctx.shape[1]
    depth = norm_g.shape[0]
    assert depth == 4 and n_lat % ROW_TILE == 0 and n_ctx == ATTN_TILE and ROW_TILE % n_ctx == 0
    assert n_lat % (FFT_N1H * SUBLANES) == 0 and n_lat % GRID_W == 0
    n_lat_tiles = n_lat // ROW_TILE
    bf = lambda a: a.astype(BF16)

    cs = jnp.concatenate([c, c_ctx[None, :], jnp.zeros((SUBLANES - b - 1, d), F32)], axis=0)
    mods = _adaln(cs, ada_w, ada_b)[:, :b + 1].reshape(depth, b + 1, 3, d)
    mods = jnp.stack([mods[:, :b], jnp.broadcast_to(mods[:, b:], (depth, b, 3, d))], axis=2)
    cos64, sin64 = _rope_tables(n_lat, ROW_TILE)

    c2 = MLA_Q_RANK + MLA_KV_RANK + MLA_ROPE
    w_lat, w_gate = bf(mla_w_in[0][:, :c2]), bf(mla_w_in[0][:, c2:])
    wuq = bf(mla_w_uq[0].reshape(MLA_Q_RANK, MLA_HEADS, MLA_QK).transpose(1, 0, 2))
    wukv = bf(mla_w_ukv[0].reshape(MLA_KV_RANK, MLA_HEADS, MLA_NOPE + MLA_V).transpose(1, 0, 2))
    q, k, v, sg = _mla_proj(x, ctx, norm_g[0:1], mods[0], w_lat, w_gate, mla_q_norm_g[0:1], mla_kv_norm_g[0:1],
                            wuq, wukv, cos64, sin64, n_lat_tiles)
    o = _mla_attn(q, k, v, n_lat, n_ctx)
    xs = _outproj_first(o, sg, x, ctx, mods[0], bf(mla_w_out[0]), n_lat_tiles)

    z0, g0, g1, sg = _hy_proj(xs, norm_g[1:2], mods[1], bf(hy_w_in[0]), hy_conv_w[0], hy_conv_b[0][None, :],
                              n_lat_tiles, n_ctx)
    cch = z0.shape[-1]
    filt = (hy_filt_w_in[0], hy_filt_w_hid[0], hy_filt_b[0], hy_filt_freq[0], hy_filt_w_out[0])
    assert cch % HY_SLAB == 0
    mats = _fft_matrices(n_lat)
    spec = _hy_spec(_hy_filter_taps(n_lat, *filt), mats[1], mats[2])
    cfwd, cinv, nfp = _dense_dft_matrices(n_ctx)
    cspec = _hy_ctx_spec(_hy_filter_taps(n_ctx, *filt), cfwd)
    gates = (g0, g1)
    z = z0
    for order in range(HY_ORDER):
        bias = hy_bias[0][order][None, :]
        zn = _hy_conv(z, gates[order], spec, order, mats, bias, n_lat)
        z = _hy_ctx_conv(z, gates[order], cspec, order, cfwd[:, :n_ctx], cinv, nfp, bias, zn, n_lat, n_ctx)

    nq = SWA_Q_HEADS * SWA_HEAD_DIM
    nkv = SWA_KV_HEADS * SWA_HEAD_DIM
    w_in = swa_w_in[0]
    dup = lambda w: jnp.tile(w.reshape(d, SWA_KV_HEADS, 1, SWA_HEAD_DIM), (1, 1, 2, 1)).reshape(d, 2 * nkv)
    w2 = bf(jnp.concatenate([w_in[:, :nq], dup(w_in[:, nq:nq + nkv]), dup(w_in[:, nq + nkv:nq + 2 * nkv]),
                             w_in[:, nq + 2 * nkv:]], axis=1))
    cos128, sin128 = jnp.tile(cos64, (1, 2)), jnp.tile(sin64, (1, 2))
    xs, q, k, v, sg = _swa_proj(z, sg, xs, mods[1], bf(hy_w_out[0]), norm_g[2:3], mods[2], w2, cos128, sin128,
                                n_lat_tiles)
    o = _swa_attn(swa_sink[0], q, k, v, n_lat, n_ctx)

    xs, u, sg = _cf_proj(o, sg, xs, mods[2], bf(swa_w_out[0]), norm_g[3:4], mods[3], bf(cf_w_in[0]), n_lat_tiles)
    return _cf_tail(u, sg, xs, mods[3], cf_dw_w[0], cf_dw_b[0][None, :], cf_ln_g[0][None, :], cf_ln_b[0][None, :],
                    bf(cf_w_out[0]), final_g[None, :])
```

```python
import functools
import math

import numpy as np
import jax
import jax.numpy as jnp
from jax import lax
from jax.experimental import pallas as pl
from jax.experimental.pallas import tpu as pltpu

F32 = jnp.float32
BF16 = jnp.bfloat16

GRID_W = 64
NORM_EPS = 1e-6
NEG_INF = -1e30
LOG2E = math.log2(math.e)
ROPE_BASE = 10000.0
ROPE_DIM = 64

MLA_HEADS = 8
MLA_Q_RANK = 384
MLA_KV_RANK = 256
MLA_NOPE = 128
MLA_ROPE = ROPE_DIM
MLA_V = 128
MLA_QK = MLA_NOPE + MLA_ROPE

HY_ORDER = 2
HY_EMB = 33
HY_EMB_PAD = 64
HY_BANDS = (HY_EMB - 1) // 2
HY_DECAY_TARGET = 1e-2
HY_FAST_DECAY = 0.3
HY_SLOW_DECAY = 1.5

SWA_Q_HEADS = 16
SWA_KV_HEADS = 4
SWA_GROUP = SWA_Q_HEADS // SWA_KV_HEADS
SWA_HEAD_DIM = ROPE_DIM
SWA_WINDOW = 128

CF_KERNEL = 31
CF_HALO = 16

ROW_TILE = 512
ATTN_TILE = 256
SUBLANES = 8
LANES = 128
FFT_N1 = 64
FFT_N1H = FFT_N1 // 2
FFT_K1 = FFT_N1 // 2 + 1
HY_SLAB = 256
HY_HALO = 16
VMEM_LIMIT = 48 * 1024 * 1024
VMEM_LIMIT_BIG = 56 * 1024 * 1024


def _cparams(n_axes):
    return pltpu.CompilerParams(dimension_semantics=("arbitrary",) * n_axes,
                                vmem_limit_bytes=VMEM_LIMIT)


def _cparams_big(n_axes):
    return pltpu.CompilerParams(dimension_semantics=("arbitrary",) * n_axes,
                                vmem_limit_bytes=VMEM_LIMIT_BIG)


def _once(shape, index_map):
    return pl.BlockSpec(shape, index_map, pipeline_mode=pl.Buffered(1))


def _dot(a, b):
    return jnp.dot(a, b, preferred_element_type=F32)


def _dot_nt(a, b):
    return lax.dot_general(a, b, (((1,), (1,)), ((), ())), preferred_element_type=F32)


def _rms(x, g):
    return x * lax.rsqrt(jnp.mean(x * x, axis=-1, keepdims=True) + NORM_EPS) * g


def _norm_mod(x, g, mod):
    return _rms(x, g) * (1.0 + mod[1:2, :]) + mod[0:1, :]


def _silu(x):
    return x * jax.nn.sigmoid(x)


def _adaln_kernel(c_ref, w_ref, b_ref, o_ref):
    s = _silu(c_ref[...]).astype(BF16)
    o_ref[...] = _dot(s, w_ref[...].astype(BF16)) + b_ref[...]


def _adaln(cs, ada_w, ada_b):
    depth, d, d3 = ada_w.shape
    tn = 512
    return pl.pallas_call(
        _adaln_kernel,
        out_shape=jax.ShapeDtypeStruct((depth, cs.shape[0], d3), F32),
        grid=(depth, d3 // tn),
        in_specs=[pl.BlockSpec(cs.shape, lambda i, j: (0, 0)),
                  pl.BlockSpec((None, d, tn), lambda i, j: (i, 0, j)),
                  pl.BlockSpec((None, 1, tn), lambda i, j: (i, 0, j))],
        out_specs=pl.BlockSpec((None, cs.shape[0], tn), lambda i, j: (i, 0, j)),
        compiler_params=_cparams(2), name="adaln",
    )(cs, ada_w, ada_b.reshape(depth, 1, d3))


def _input_tile(x_ref, ctx_ref, is_ctx):
    pad = jnp.zeros((x_ref.shape[0] - ctx_ref.shape[0], x_ref.shape[1]), F32)
    return jnp.where(is_ctx, jnp.concatenate([ctx_ref[...], pad], axis=0), x_ref[...])


def _input_specs(d, n_ctx, n_lat_tiles):
    return [pl.BlockSpec((None, ROW_TILE, d), lambda bi, i: (bi, jnp.minimum(i, n_lat_tiles - 1), 0)),
            pl.BlockSpec((None, n_ctx, d), lambda bi, i: (bi, 0, 0))]


def _residual_in(o_ref, sgp_ref, x_ref, modp_ref, wout_ref, xo_ref):
    og = (o_ref[...].astype(F32) * sgp_ref[...].astype(F32)).astype(BF16)
    x = x_ref[...] + modp_ref[2:3, :] * _dot(og, wout_ref[...])
    xo_ref[...] = x
    return x


def _residual_specs(o, d, n_lat_tiles):
    width = o.shape[-1]
    row = lambda bi, i: (bi, i, 0)
    return [pl.BlockSpec((None, ROW_TILE, width), row), pl.BlockSpec((None, ROW_TILE, width), row),
            pl.BlockSpec((None, ROW_TILE, d), row),
            pl.BlockSpec((None, None, 3, d), lambda bi, i: (bi, i // n_lat_tiles, 0, 0)),
            _once((width, d), lambda bi, i: (0, 0))]


def _rope_lanes(x, cos, sin):
    half = ROPE_DIM // 2
    lane = lax.broadcasted_iota(jnp.int32, x.shape, 1) % ROPE_DIM
    rot = jnp.where(lane < half, -pltpu.roll(x, LANES - half, 1), pltpu.roll(x, half, 1))
    return x * cos + rot * sin


def _mla_proj_kernel(x_ref, ctx_ref, g_ref, mod_ref, wl_ref, wg_ref, qg_ref, kvg_ref, wuq_ref, wukv_ref,
                     cos_ref, sin_ref, q_ref, k_ref, v_ref, sg_ref, *, n_lat_tiles):
    x = _input_tile(x_ref, ctx_ref, pl.program_id(1) >= n_lat_tiles)
    h = _norm_mod(x, g_ref[...], mod_ref[...]).astype(BF16)
    sg_ref[...] = _silu(_dot(h, wg_ref[...])).astype(sg_ref.dtype)
    res = _dot(h, wl_ref[...])
    c0, c1 = MLA_Q_RANK, MLA_Q_RANK + MLA_KV_RANK
    cqn = _rms(res[:, :c0], qg_ref[...]).astype(BF16)
    ckvn = _rms(res[:, c0:c1], kvg_ref[...]).astype(BF16)
    cos, sin = cos_ref[...], sin_ref[...]

    def rope(slab):
        return _rope_lanes(slab, cos, sin)[:, :MLA_ROPE]

    k_rope_t = rope(res[:, c1:c1 + LANES]).T.astype(k_ref.dtype)
    scale = MLA_QK ** -0.5 * LOG2E
    per = 2 * LANES
    for h0 in range(0, MLA_HEADS, 2):
        q2 = _dot(cqn, wuq_ref[:, h0 * per:(h0 + 2) * per])
        kv2 = _dot(ckvn, wukv_ref[:, h0 * per:(h0 + 2) * per])
        for j in range(2):
            hd = h0 + j
            q, kv = q2[:, j * per:(j + 1) * per], kv2[:, j * per:(j + 1) * per]
            q_ref[hd, :, :MLA_NOPE] = (q[:, :MLA_NOPE] * scale).astype(q_ref.dtype)
            q_ref[hd, :, MLA_NOPE:] = (rope(q[:, MLA_NOPE:]) * scale).astype(q_ref.dtype)
            k_ref[hd, :MLA_NOPE, :] = kv[:, :MLA_NOPE].T.astype(k_ref.dtype)
            k_ref[hd, MLA_NOPE:, :] = k_rope_t
            v_ref[hd, :, :MLA_V] = kv[:, MLA_NOPE:].astype(v_ref.dtype)
            v_ref[hd, :, MLA_V:] = jnp.ones((v_ref.shape[1], MLA_V), v_ref.dtype)


def _mla_proj(x, ctx, g, mod, wl, wg, qg, kvg, wuq, wukv, cos, sin, n_lat_tiles):
    b, n_lat, d = x.shape
    nt = n_lat_tiles + 1
    t = nt * ROW_TILE
    width = MLA_HEADS * MLA_V
    full = lambda a: _once(a.shape, lambda bi, i: (0,) * a.ndim)
    head_out = lambda n: pl.BlockSpec((None, MLA_HEADS, ROW_TILE, n), lambda bi, i: (bi, 0, i, 0))
    return pl.pallas_call(
        functools.partial(_mla_proj_kernel, n_lat_tiles=n_lat_tiles),
        out_shape=(jax.ShapeDtypeStruct((b, MLA_HEADS, t, MLA_QK), BF16),
                   jax.ShapeDtypeStruct((b, MLA_HEADS, MLA_QK, t), BF16),
                   jax.ShapeDtypeStruct((b, MLA_HEADS, t, 2 * MLA_V), BF16),
                   jax.ShapeDtypeStruct((b, t, width), BF16)),
        grid=(b, nt),
        in_specs=_input_specs(d, ctx.shape[1], n_lat_tiles) +
                 [full(g),
                  pl.BlockSpec((None, None, 3, d), lambda bi, i: (bi, i // n_lat_tiles, 0, 0)),
                  full(wl), full(wg), full(qg), full(kvg), full(wuq), full(wukv),
                  pl.BlockSpec((ROW_TILE, LANES), lambda bi, i: (i, 0)),
                  pl.BlockSpec((ROW_TILE, LANES), lambda bi, i: (i, 0))],
        out_specs=(head_out(MLA_QK),
                   pl.BlockSpec((None, MLA_HEADS, MLA_QK, ROW_TILE), lambda bi, i: (bi, 0, 0, i)),
                   head_out(2 * MLA_V),
                   pl.BlockSpec((None, ROW_TILE, width), lambda bi, i: (bi, i, 0))),
        compiler_params=_cparams(2), name="mla_proj",
    )(x, ctx, g, mod, wl, wg, qg, kvg, wuq, wukv, cos, sin)


def _attend(q, k_parts, v_parts):
    s = [_dot(q, k) for k in k_parts]
    m = functools.reduce(jnp.maximum, [jnp.max(x, axis=-1, keepdims=True) for x in s])
    ov = functools.reduce(jnp.add, [_dot(jnp.exp2(x - m).astype(BF16), v) for x, v in zip(s, v_parts)])
    n = ov.shape[-1] // 2
    return ov[:, :n] / ov[:, n:]


MLA_HEADS_PER_STEP = 2


def _mla_attn_kernel(q_ref, k_ref, v_ref, o_ref, *, n_lat_tiles, n_lat, n_ctx):
    qi = pl.program_id(2)

    n_keys = n_lat + n_ctx

    @pl.when(qi < n_lat_tiles)
    def _():
        chains = [(hd, r0) for hd in range(MLA_HEADS_PER_STEP) for r0 in range(0, q_ref.shape[1], ATTN_TILE)]
        s = [_dot(q_ref[hd, r0:r0 + ATTN_TILE, :], k_ref[hd, :, 0:n_keys]) for hd, r0 in chains]
        p = [jnp.exp2(x - jnp.max(x, axis=-1, keepdims=True)).astype(BF16) for x in s]
        for (hd, r0), pc in zip(chains, p):
            ov = _dot(pc, v_ref[hd, 0:n_keys, :])
            o = ov[:, :MLA_V] / ov[:, MLA_V:]
            o_ref[r0:r0 + ATTN_TILE, hd * MLA_V:(hd + 1) * MLA_V] = o.astype(o_ref.dtype)

    @pl.when(qi >= n_lat_tiles)
    def _():
        o_ref[n_ctx:, :] = jnp.zeros((o_ref.shape[0] - n_ctx, o_ref.shape[1]), o_ref.dtype)
        for hd in range(MLA_HEADS_PER_STEP):
            k = k_ref[hd, :, n_lat:n_keys]
            v = v_ref[hd, n_lat:n_keys, :]
            o = _attend(q_ref[hd, 0:n_ctx, :], [k], [v])
            o_ref[0:n_ctx, hd * MLA_V:(hd + 1) * MLA_V] = o.astype(o_ref.dtype)


def _mla_attn(q, k, v, n_lat, n_ctx):
    b, hds, t, _ = q.shape
    nt = t // ROW_TILE
    hps = MLA_HEADS_PER_STEP
    kern = functools.partial(_mla_attn_kernel, n_lat_tiles=n_lat // ROW_TILE, n_lat=n_lat, n_ctx=n_ctx)
    return pl.pallas_call(
        kern,
        out_shape=jax.ShapeDtypeStruct((b, t, hds * MLA_V), BF16),
        grid=(b, hds // hps, nt),
        in_specs=[pl.BlockSpec((None, hps, ROW_TILE, MLA_QK), lambda bi, h, i: (bi, h, i, 0)),
                  pl.BlockSpec((None, hps, MLA_QK, t), lambda bi, h, i: (bi, h, 0, 0)),
                  pl.BlockSpec((None, hps, t, 2 * MLA_V), lambda bi, h, i: (bi, h, 0, 0))],
        out_specs=pl.BlockSpec((None, ROW_TILE, hps * MLA_V), lambda bi, h, i: (bi, i, h)),
        compiler_params=_cparams_big(3), name="mla_attn",
    )(q, k, v)


def _hy_proj_kernel(op_ref, o_ref, on_ref, sp_ref, s_ref, sn_ref, xp_ref, x_ref, ctx_ref, xn_ref, modp_ref, wout_ref,
                    g_ref, mod_ref, w_ref, cw_ref, cb_ref, xo_ref, z_ref, g0_ref, g1_ref, sg_ref,
                    *, n_lat_tiles, n_ctx):
    i = pl.program_id(1)
    tm = x_ref.shape[0]
    c = z_ref.shape[-1]
    n = tm + 2 * SUBLANES
    halo = xp_ref.shape[0]
    cat = lambda a, m, z: jnp.concatenate([a[...], m, z[...]], axis=0)
    og = (cat(op_ref, o_ref[...], on_ref).astype(F32) * cat(sp_ref, s_ref[...], sn_ref).astype(F32)).astype(BF16)
    xe = cat(xp_ref, _input_tile(x_ref, ctx_ref, i >= n_lat_tiles), xn_ref)
    xe = xe + modp_ref[2:3, :] * _dot(og, wout_ref[...])
    xo_ref[...] = xe[halo:halo + tm]
    xa = xe[halo - SUBLANES:halo + tm + SUBLANES]
    hf = _norm_mod(xa, g_ref[...], mod_ref[...])
    h = hf.astype(BF16)
    ctx_tile = i == n_lat_tiles
    first = jnp.logical_or(i == 0, ctx_tile)
    last = jnp.logical_or(i == n_lat_tiles - 1, ctx_tile)
    rows = lax.broadcasted_iota(jnp.int32, (n, 1), 0)
    end = jnp.where(ctx_tile, n_ctx + SUBLANES, jnp.where(last, tm + SUBLANES, n))
    valid = jnp.logical_and(jnp.logical_or(rows >= SUBLANES, jnp.logical_not(first)), rows < end)
    outs = (z_ref, g0_ref, g1_ref)
    for j in range(len(outs)):
        u = jnp.where(valid, _dot(h, w_ref[:, j * c:(j + 1) * c]), 0.0)
        cw = cw_ref[:, j * c:(j + 1) * c]
        cv = cw[0:1, :] * pltpu.roll(u, 1, 0) + cw[1:2, :] * u + cw[2:3, :] * pltpu.roll(u, n - 1, 0)
        outs[j][...] = cv[SUBLANES:SUBLANES + tm] + cb_ref[:, j * c:(j + 1) * c]
    h_mid = hf[SUBLANES:SUBLANES + tm].astype(BF16)
    sg_ref[...] = _silu(_dot(h_mid, w_ref[:, len(outs) * c:])).astype(sg_ref.dtype)


def _hy_proj(o, sgp, x, ctx, modp, w_out, g, mod, w, cw, cb, n_lat_tiles):
    b, t, width = o.shape
    n_lat, d = x.shape[1:]
    n_ctx = ctx.shape[1]
    c = w.shape[1] // (HY_ORDER + 2)
    per = ROW_TILE // HY_HALO
    last = n_lat // HY_HALO - 1
    full = lambda a: _once(a.shape, lambda bi, i: (0,) * a.ndim)
    row = lambda bi, i: (bi, i, 0)
    prev = lambda bi, i: (bi, jnp.maximum(i * per - 1, 0), 0)
    nxt = lambda bi, i: (bi, jnp.minimum((i + 1) * per, last), 0)
    x_main, x_ctx = _input_specs(d, n_ctx, n_lat_tiles)
    kern = functools.partial(_hy_proj_kernel, n_lat_tiles=n_lat_tiles, n_ctx=n_ctx)
    halo = lambda n, m: pl.BlockSpec((None, HY_HALO, n), m)
    return pl.pallas_call(
        kern,
        out_shape=(jax.ShapeDtypeStruct((b, t, d), F32),) + (jax.ShapeDtypeStruct((b, t, c), F32),) * 3 +
                  (jax.ShapeDtypeStruct((b, t, c), BF16),),
        grid=(b, t // ROW_TILE),
        in_specs=[halo(width, prev), pl.BlockSpec((None, ROW_TILE, width), row), halo(width, nxt),
                  halo(width, prev), pl.BlockSpec((None, ROW_TILE, width), row), halo(width, nxt),
                  halo(d, prev), x_main, x_ctx, halo(d, nxt),
                  pl.BlockSpec((None, None, 3, d), lambda bi, i: (bi, i // n_lat_tiles, 0, 0)),
                  full(w_out), full(g),
                  pl.BlockSpec((None, None, 3, d), lambda bi, i: (bi, i // n_lat_tiles, 0, 0)),
                  full(w), full(cw), full(cb)],
        out_specs=(pl.BlockSpec((None, ROW_TILE, d), row),) + (pl.BlockSpec((None, ROW_TILE, c), row),) * 4,
        compiler_params=_cparams(2), name="hy_proj",
    )(o, o, o, sgp, sgp, sgp, x, x, ctx, x, modp, w_out, g, mod, w, cw, cb)


def _hy_filter_kernel(ef_ref, eb_ref, w0_ref, wh_ref, b_ref, fr_ref, wf_ref, wb_ref, dl_ref, o_ref):
    i = pl.program_id(0)
    c = dl_ref.shape[-1]

    def taps(e, wo):
        hdn = e
        ws = (w0_ref[...], wh_ref[0], wh_ref[1])
        for k in range(3):
            hdn = jnp.sin(fr_ref[k:k + 1, :] * (_dot(hdn.astype(BF16), ws[k].astype(BF16)) + b_ref[k:k + 1, :]))
        decay = jnp.exp(-e[:, 0:1] * dl_ref[...])
        hh = _dot(hdn.astype(BF16), wo.astype(BF16))
        return [hh[:, o * c:(o + 1) * c] * decay for o in range(HY_ORDER)]

    hf = taps(ef_ref[...], wf_ref[...])
    hb = taps(eb_ref[...], wb_ref[...])
    rows = lax.broadcasted_iota(jnp.int32, (ef_ref.shape[0], 1), 0)
    lag0 = jnp.logical_and(rows == 0, i == 0)
    for o in range(HY_ORDER):
        o_ref[o, 0] = hf[o] + jnp.where(lag0, hb[o], 0.0)
        o_ref[o, 1] = jnp.where(lag0, 0.0, hb[o])


def _hy_position_features(n):
    t = jnp.linspace(0.0, 1.0, n, dtype=F32)[:, None]
    wpos = (2.0 * math.pi / n) * jnp.arange(n, dtype=F32)[:, None]
    bands = jnp.linspace(1e-4, HY_BANDS - 1, HY_BANDS, dtype=F32)[None, :]
    hdn = jnp.concatenate([t, jnp.cos(bands * wpos), -jnp.sin(bands * wpos)], axis=-1)
    return jnp.pad(hdn, ((0, 0), (0, HY_EMB_PAD - HY_EMB)))


def _hy_filter_taps(n, w_in, w_hid, bias, freq, w_out):
    c = w_out.shape[1] // (2 * HY_ORDER)
    ef = _hy_position_features(n)
    rev = (n - jnp.arange(n)) % n
    eb = ef[rev]
    w0 = jnp.pad(w_in, ((0, HY_EMB_PAD - HY_EMB), (0, 0)))
    wo = w_out.reshape(w_out.shape[0], HY_ORDER, 2, c)
    wf = wo[:, :, 0, :].reshape(w_out.shape[0], HY_ORDER * c)
    wb = wo[:, :, 1, :].reshape(w_out.shape[0], HY_ORDER * c)
    deltas = jnp.abs(jnp.linspace(math.log(HY_DECAY_TARGET) / HY_FAST_DECAY,
                                  math.log(HY_DECAY_TARGET) / HY_SLOW_DECAY, c, dtype=F32))[None, :]
    tl = min(n, 512)
    full = lambda a: pl.BlockSpec(a.shape, lambda i: (0,) * a.ndim)
    return pl.pallas_call(
        _hy_filter_kernel,
        out_shape=jax.ShapeDtypeStruct((HY_ORDER, 2, n, c), F32),
        grid=(n // tl,),
        in_specs=[pl.BlockSpec((tl, HY_EMB_PAD), lambda i: (i, 0)),
                  pl.BlockSpec((tl, HY_EMB_PAD), lambda i: (i, 0)),
                  full(w0), full(w_hid), full(bias), full(freq), full(wf), full(wb), full(deltas)],
        out_specs=pl.BlockSpec((HY_ORDER, 2, tl, c), lambda i: (0, 0, i, 0)),
        compiler_params=_cparams(1), name="hy_filter",
    )(ef, eb, w0, w_hid, bias, freq, wf, wb, deltas)


def _fft_matrices(n_lat):
    n = 2 * n_lat
    n2 = n_lat // FFT_N1H
    k1 = np.arange(FFT_K1)
    n1 = np.arange(FFT_N1H)
    ang = 2.0 * np.pi * np.outer(k1, n1) / FFT_N1
    eye = np.eye(SUBLANES)
    f1 = np.kron(np.concatenate([np.cos(ang), -np.sin(ang)], axis=0), eye)
    wgt = np.full(FFT_K1, 2.0)
    wgt[0] = wgt[-1] = 1.0
    g1 = np.kron(np.concatenate([wgt * np.cos(ang.T), -wgt * np.sin(ang.T)], axis=1) / n, eye)
    idx = np.arange(n2)
    f2 = np.zeros((FFT_K1, 2 * n2, 2 * n2))
    f2i = np.zeros((FFT_K1, 2 * n2, 2 * n2))
    for a in range(FFT_K1):
        ph = 2.0 * np.pi * (np.outer(idx, idx) / n2 + (a * idx)[None, :] / n)
        f2[a] = np.block([[np.cos(ph), np.sin(ph)], [-np.sin(ph), np.cos(ph)]])
        f2i[a] = np.block([[np.cos(ph.T), -np.sin(ph.T)], [np.sin(ph.T), np.cos(ph.T)]])
    sign = np.kron(np.tile((-1.0) ** k1, 2), np.ones(SUBLANES))[:, None]
    f1s = np.concatenate([f1, sign * f1], axis=1)
    as_bf16 = lambda m: jnp.asarray(m.astype(np.float32)).astype(BF16)
    return as_bf16(f1), as_bf16(f1s), as_bf16(f2), as_bf16(f2i), as_bf16(g1)


def _dense_dft_matrices(n_ctx):
    n = 2 * n_ctx
    nf = n_ctx + 1
    nfp = -(-nf // SUBLANES) * SUBLANES
    k = np.arange(nf)
    ang = 2.0 * np.pi * np.outer(k, np.arange(n)) / n
    fwd = np.zeros((2 * nfp, n))
    fwd[:nf] = np.cos(ang)
    fwd[nfp:nfp + nf] = -np.sin(ang)
    wgt = np.full(nf, 2.0)
    wgt[0] = wgt[-1] = 1.0
    angi = 2.0 * np.pi * np.outer(np.arange(n_ctx), k) / n
    inv = np.zeros((n_ctx, 2 * nfp))
    inv[:, :nf] = wgt * np.cos(angi) / n
    inv[:, nfp:nfp + nf] = -wgt * np.sin(angi) / n
    as_bf16 = lambda m: jnp.asarray(m.astype(np.float32)).astype(BF16)
    return as_bf16(fwd), as_bf16(inv), nfp


def _hy_spec_kernel(t_ref, f1s_ref, f2_ref, h_ref, a_ref):
    _, n1h, nj, s, tc = t_ref.shape
    n2 = nj * s

    def stage1(j, carry):
        sig = jnp.concatenate([t_ref[0, :, j].reshape(n1h * s, tc), t_ref[1, :, j].reshape(n1h * s, tc)], axis=0)
        a_ref[:, :, j] = _dot(f1s_ref[...], sig.astype(BF16)).reshape(2, FFT_K1, s, tc)
        return carry

    lax.fori_loop(0, nj, stage1, 0, unroll=min(nj, 4))

    def stage2(k, carry):
        a = a_ref[:, k].reshape(2 * n2, tc).astype(BF16)
        h_ref[:, k] = _dot(f2_ref[k], a).reshape(2, n2, tc).astype(h_ref.dtype)
        return carry

    lax.fori_loop(0, FFT_K1, stage2, 0, unroll=11)


def _hy_spec(taps, f1s, f2):
    _, _, n_lat, c = taps.shape
    n2 = n_lat // FFT_N1H
    nj = n2 // SUBLANES
    tc = HY_SLAB
    return pl.pallas_call(
        _hy_spec_kernel,
        out_shape=jax.ShapeDtypeStruct((HY_ORDER, 2, FFT_K1, n2, c), BF16),
        grid=(c // tc, HY_ORDER),
        in_specs=[pl.BlockSpec((None, 2, FFT_N1H, nj, SUBLANES, tc), lambda ci, o: (o, 0, 0, 0, 0, ci)),
                  _once(f1s.shape, lambda ci, o: (0, 0)),
                  _once(f2.shape, lambda ci, o: (0, 0, 0))],
        out_specs=pl.BlockSpec((None, 2, FFT_K1, n2, tc), lambda ci, o: (o, 0, 0, 0, ci)),
        scratch_shapes=[pltpu.VMEM((2, FFT_K1, nj, SUBLANES, tc), F32)],
        compiler_params=_cparams_big(2), name="hy_spec",
    )(taps.reshape(HY_ORDER, 2, FFT_N1H, nj, SUBLANES, c), f1s, f2)


def _hy_conv_kernel(z_ref, gt_ref, h_ref, f1_ref, f2_ref, f2i_ref, g1_ref, bias_ref, o_ref, a_ref):
    n1h, nj, s, tc = z_ref.shape
    n2 = nj * s

    def stage1(j, carry):
        zz = z_ref[:, j].reshape(n1h * s, tc).astype(BF16)
        a_ref[:, :, j] = _dot(f1_ref[...], zz).reshape(2, FFT_K1, s, tc)
        return carry

    lax.fori_loop(0, nj, stage1, 0, unroll=min(nj, 4))

    def stage2(k, carry):
        x = _dot(f2_ref[k], a_ref[:, k].reshape(2 * n2, tc).astype(BF16))
        xr, xi = x[:n2], x[n2:]
        hr, hi = h_ref[0, k].astype(F32), h_ref[1, k].astype(F32)
        y = jnp.concatenate([xr * hr - xi * hi, xr * hi + xi * hr], axis=0).astype(BF16)
        a_ref[:, k] = _dot(f2i_ref[k], y).reshape(2, nj, s, tc)
        return carry

    lax.fori_loop(0, FFT_K1, stage2, 0, unroll=11)

    def stage3(j, carry):
        bp = a_ref[:, :, j].reshape(2 * FFT_K1 * s, tc).astype(BF16)
        y = _dot(g1_ref[...], bp).reshape(n1h, s, tc)
        o_ref[:, j] = gt_ref[:, j] * (y + z_ref[:, j] * bias_ref[...])
        return carry

    lax.fori_loop(0, nj, stage3, 0, unroll=min(nj, 4))


def _hy_conv(z, gate, h, order, mats, bias, n_lat):
    f1, _, f2, f2i, g1 = mats
    bz, t, c = z.shape
    n2 = n_lat // FFT_N1H
    nj = n2 // SUBLANES
    tc = HY_SLAB
    view = (bz, t // n2, nj, SUBLANES, c)
    slab = pl.BlockSpec((None, FFT_N1H, nj, SUBLANES, tc), lambda ci, bi: (bi, 0, 0, 0, ci))
    out = pl.pallas_call(
        _hy_conv_kernel,
        out_shape=jax.ShapeDtypeStruct(view, F32),
        grid=(c // tc, bz),
        in_specs=[slab, slab,
                  _once((None, 2, FFT_K1, n2, tc), lambda ci, bi: (order, 0, 0, 0, ci)),
                  _once(f1.shape, lambda ci, bi: (0, 0)),
                  _once(f2.shape, lambda ci, bi: (0, 0, 0)),
                  _once(f2i.shape, lambda ci, bi: (0, 0, 0)),
                  _once(g1.shape, lambda ci, bi: (0, 0)),
                  pl.BlockSpec((1, tc), lambda ci, bi: (0, ci))],
        out_specs=slab,
        input_output_aliases={0: 0},
        scratch_shapes=[pltpu.VMEM((2, FFT_K1, nj, SUBLANES, tc), F32)],
        compiler_params=_cparams_big(2), name="hy_conv",
    )(z.reshape(view), gate.reshape(view), h, f1, f2, f2i, g1, bias)
    return out.reshape(bz, t, c)


def _hy_ctx_spec_kernel(t_ref, f_ref, h_ref):
    sig = jnp.concatenate([t_ref[0], t_ref[1]], axis=0).astype(BF16)
    h_ref[...] = _dot(f_ref[...], sig)


def _hy_ctx_spec(taps, fwd):
    _, _, n_ctx, c = taps.shape
    return pl.pallas_call(
        _hy_ctx_spec_kernel,
        out_shape=jax.ShapeDtypeStruct((HY_ORDER, fwd.shape[0], c), F32),
        grid=(HY_ORDER,),
        in_specs=[pl.BlockSpec((None, 2, n_ctx, c), lambda o: (o, 0, 0, 0)),
                  pl.BlockSpec(fwd.shape, lambda o: (0, 0))],
        out_specs=pl.BlockSpec((None, fwd.shape[0], c), lambda o: (o, 0, 0)),
        compiler_params=_cparams(1), name="hy_ctx_spec",
    )(taps, fwd)


def _hy_ctx_conv_kernel(zin_ref, gt_ref, h_ref, f_ref, fi_ref, bias_ref, o_ref, *, nfp):
    z = zin_ref[...]
    x = _dot(f_ref[...], z.astype(BF16))
    xr, xi = x[:nfp], x[nfp:]
    hr, hi = h_ref[:nfp, :], h_ref[nfp:, :]
    y = jnp.concatenate([xr * hr - xi * hi, xr * hi + xi * hr], axis=0).astype(BF16)
    n_ctx = z.shape[0]
    o_ref[0:n_ctx, :] = gt_ref[...] * (_dot(fi_ref[...], y) + z * bias_ref[...])
    o_ref[n_ctx:, :] = jnp.zeros((o_ref.shape[0] - n_ctx, o_ref.shape[1]), o_ref.dtype)


def _hy_ctx_conv(z, gate, h, order, fwd_half, inv, nfp, bias, n_lat, n_ctx):
    bz, t, c = z.shape
    blk = pl.BlockSpec((None, n_ctx, c), lambda bi: (bi, n_lat // n_ctx, 0))
    kern = functools.partial(_hy_ctx_conv_kernel, nfp=nfp)
    return pl.pallas_call(
        kern,
        out_shape=jax.ShapeDtypeStruct(z.shape, F32),
        grid=(bz,),
        in_specs=[blk, blk,
                  pl.BlockSpec((None, 2 * nfp, c), lambda bi: (order, 0, 0)),
                  pl.BlockSpec(fwd_half.shape, lambda bi: (0, 0)),
                  pl.BlockSpec(inv.shape, lambda bi: (0, 0)),
                  pl.BlockSpec((1, c), lambda bi: (0, 0))],
        out_specs=pl.BlockSpec((None, t - n_lat, c), lambda bi: (bi, n_lat // (t - n_lat), 0)),
        input_output_aliases={0: 0},
        compiler_params=_cparams(1), name="hy_ctx_conv",
    )(z, gate, h, fwd_half, inv, bias)


def _swa_proj_kernel(o_ref, sgp_ref, x_ref, modp_ref, wout_ref, g_ref, mod_ref, w_ref, cos_ref, sin_ref,
                     xo_ref, q_ref, k_ref, v_ref, sg_ref):
    x = _residual_in(o_ref, sgp_ref, x_ref, modp_ref, wout_ref, xo_ref)
    h = _norm_mod(x, g_ref[...], mod_ref[...]).astype(BF16)
    cos, sin = cos_ref[...], sin_ref[...]
    nq = q_ref.shape[-1]
    nk = SWA_KV_HEADS * LANES
    scale = SWA_HEAD_DIM ** -0.5 * LOG2E
    for c0 in range(0, nq, nk):
        res = _dot(h, w_ref[:, c0:c0 + nk])
        for cb in range(nk // LANES):
            sl = slice(cb * LANES, (cb + 1) * LANES)
            q_ref[:, c0 + cb * LANES:c0 + (cb + 1) * LANES] = (_rope_lanes(res[:, sl], cos, sin) * scale).astype(q_ref.dtype)
    res = _dot(h, w_ref[:, nq:nq + nk])
    for hd in range(SWA_KV_HEADS):
        k_ref[hd] = _rope_lanes(res[:, hd * LANES:(hd + 1) * LANES], cos, sin).astype(k_ref.dtype)
    res = _dot(h, w_ref[:, nq + nk:nq + 2 * nk])
    for hd in range(SWA_KV_HEADS):
        v_ref[hd, :, :LANES] = res[:, hd * LANES:(hd + 1) * LANES].astype(v_ref.dtype)
        v_ref[hd, :, LANES:] = jnp.ones((v_ref.shape[1], LANES), v_ref.dtype)
    for c0 in range(0, nq, nk):
        sg_ref[:, c0:c0 + nk] = _silu(_dot(h, w_ref[:, nq + 2 * nk + c0:nq + 2 * nk + c0 + nk])).astype(sg_ref.dtype)


def _swa_proj(o, sgp, x, modp, w_out, g, mod, w, cos, sin, n_lat_tiles):
    b, t, d = x.shape
    nt = t // ROW_TILE
    nq = SWA_Q_HEADS * SWA_HEAD_DIM
    full = lambda a: _once(a.shape, lambda bi, i: (0,) * a.ndim)
    row = lambda bi, i: (bi, i, 0)
    kv = lambda n: pl.BlockSpec((None, SWA_KV_HEADS, ROW_TILE, n), lambda bi, i: (bi, 0, i, 0))
    return pl.pallas_call(
        _swa_proj_kernel,
        out_shape=(jax.ShapeDtypeStruct((b, t, d), F32),
                   jax.ShapeDtypeStruct((b, t, nq), BF16),
                   jax.ShapeDtypeStruct((b, SWA_KV_HEADS, t, LANES), BF16),
                   jax.ShapeDtypeStruct((b, SWA_KV_HEADS, t, 2 * LANES), BF16),
                   jax.ShapeDtypeStruct((b, t, nq), BF16)),
        grid=(b, nt),
        in_specs=_residual_specs(o, d, n_lat_tiles) +
                 [full(g),
                  pl.BlockSpec((None, None, 3, d), lambda bi, i: (bi, i // n_lat_tiles, 0, 0)),
                  full(w),
                  pl.BlockSpec((ROW_TILE, LANES), lambda bi, i: (i, 0)),
                  pl.BlockSpec((ROW_TILE, LANES), lambda bi, i: (i, 0))],
        out_specs=(pl.BlockSpec((None, ROW_TILE, d), row), pl.BlockSpec((None, ROW_TILE, nq), row),
                   kv(LANES), kv(2 * LANES), pl.BlockSpec((None, ROW_TILE, nq), row)),
        compiler_params=_cparams(2), name="swa_proj",
    )(o, sgp, x, modp, w_out, g, mod, w, cos, sin)


def _swa_attn_kernel(sink_ref, q_ref, k_ref, v_ref, o_ref, *, n_lat_tiles, n_lat, n_ctx):
    g = pl.program_id(1)
    qi = pl.program_id(2)
    tq = ATTN_TILE
    span = tq + 2 * SWA_WINDOW
    lane_half = lax.broadcasted_iota(jnp.int32, (tq, LANES), 1) // SWA_HEAD_DIM
    first = lax.broadcasted_iota(jnp.int32, (2 * tq, 1), 0) < tq
    kc = k_ref[pl.ds(n_lat, n_ctx), :]
    vc = v_ref[pl.ds(n_lat, n_ctx), :]

    def pair_query(r0, col):
        qcol = q_ref[r0:r0 + tq, col * LANES:(col + 1) * LANES]
        zero = jnp.zeros_like(qcol)
        return jnp.concatenate([jnp.where(lane_half == 0, qcol, zero), jnp.where(lane_half == 1, qcol, zero)], axis=0)

    def pair_sink(col):
        base = g * SWA_GROUP + 2 * col
        return jnp.where(first, sink_ref[base], sink_ref[base + 1]) * LOG2E

    def finish(r0, col, ov, m, sink):
        o = ov[:, :LANES] / (ov[:, LANES:] + jnp.exp2(sink - m))
        o_ref[r0:r0 + tq, col * LANES:(col + 1) * LANES] = jnp.where(lane_half == 0, o[:tq], o[tq:]).astype(o_ref.dtype)

    @pl.when(qi < n_lat_tiles)
    def _():
        chains = []
        for r0 in range(0, q_ref.shape[0], tq):
            q0 = qi * q_ref.shape[0] + r0
            start = pl.multiple_of(jnp.clip(q0 - SWA_WINDOW, 0, n_lat - span), SWA_WINDOW)
            kw = k_ref[pl.ds(start, span), :]
            vw = v_ref[pl.ds(start, span), :]
            qpos = q0 + lax.broadcasted_iota(jnp.int32, (tq, 1), 0)
            kpos = start + lax.broadcasted_iota(jnp.int32, (1, span), 1)
            band = jnp.abs(kpos - qpos) <= SWA_WINDOW
            for col in range(SWA_GROUP // 2):
                q2, sink = pair_query(r0, col), pair_sink(col)
                sw = _dot_nt(q2, kw)
                sw = jnp.concatenate([jnp.where(band, sw[:tq], NEG_INF), jnp.where(band, sw[tq:], NEG_INF)], axis=0)
                sc = _dot_nt(q2, kc)
                m = jnp.maximum(jnp.maximum(jnp.max(sw, axis=-1, keepdims=True),
                                            jnp.max(sc, axis=-1, keepdims=True)), sink)
                chains.append((r0, col, jnp.exp2(sw - m).astype(BF16), jnp.exp2(sc - m).astype(BF16), vw, m, sink))
        for r0, col, pw, pc, vw, m, sink in chains:
            finish(r0, col, _dot(pw, vw) + _dot(pc, vc), m, sink)

    @pl.when(qi >= n_lat_tiles)
    def _():
        o_ref[n_ctx:, :] = jnp.zeros((o_ref.shape[0] - n_ctx, o_ref.shape[1]), o_ref.dtype)
        for col in range(SWA_GROUP // 2):
            q2, sink = pair_query(0, col), pair_sink(col)
            sc = _dot_nt(q2, kc)
            m = jnp.maximum(jnp.max(sc, axis=-1, keepdims=True), sink)
            finish(0, col, _dot(jnp.exp2(sc - m).astype(BF16), vc), m, sink)


def _swa_attn(sink, q, k, v, n_lat, n_ctx):
    b, t, nq = q.shape
    nt = t // ROW_TILE
    gw = SWA_GROUP * SWA_HEAD_DIM
    kern = functools.partial(_swa_attn_kernel, n_lat_tiles=n_lat // ROW_TILE, n_lat=n_lat, n_ctx=n_ctx)
    kv = lambda n: pl.BlockSpec((None, None, t, n), lambda bi, gi, i: (bi, gi, 0, 0))
    return pl.pallas_call(
        kern,
        out_shape=jax.ShapeDtypeStruct((b, t, nq), BF16),
        grid=(b, SWA_KV_HEADS, nt),
        in_specs=[pl.BlockSpec(memory_space=pltpu.SMEM),
                  pl.BlockSpec((None, ROW_TILE, gw), lambda bi, gi, i: (bi, i, gi)), kv(LANES), kv(2 * LANES)],
        out_specs=pl.BlockSpec((None, ROW_TILE, gw), lambda bi, gi, i: (bi, i, gi)),
        compiler_params=_cparams(3), name="swa_attn",
    )(sink, q, k, v)


def _cf_proj_kernel(o_ref, sgp_ref, x_ref, modp_ref, wout_ref, g_ref, mod_ref, w_ref, xo_ref, u_ref, sg_ref,
                    *, n_lat_tiles):
    x = _residual_in(o_ref, sgp_ref, x_ref, modp_ref, wout_ref, xo_ref)

    @pl.when(pl.program_id(1) < n_lat_tiles)
    def _():
        h = _norm_mod(x, g_ref[...], mod_ref[...]).astype(BF16)
        c = u_ref.shape[-1]
        tn = c // 2
        for c0 in range(0, c, tn):
            a = _dot(h, w_ref[:, c0:c0 + tn])
            gl = _dot(h, w_ref[:, c + c0:c + c0 + tn])
            u_ref[:, c0:c0 + tn] = a * jax.nn.sigmoid(gl)
            sg_ref[:, c0:c0 + tn] = _silu(_dot(h, w_ref[:, 2 * c + c0:2 * c + c0 + tn])).astype(sg_ref.dtype)


def _cf_proj(o, sgp, x, modp, w_out, g, mod, w, n_lat_tiles):
    b, t, d = x.shape
    c = w.shape[1] // 3
    n_lat = n_lat_tiles * ROW_TILE
    full = lambda a: _once(a.shape, lambda bi, i: (0,) * a.ndim)
    row = lambda bi, i: (bi, i, 0)
    lat_row = lambda bi, i: (bi, jnp.minimum(i, n_lat_tiles - 1), 0)
    return pl.pallas_call(
        functools.partial(_cf_proj_kernel, n_lat_tiles=n_lat_tiles),
        out_shape=(jax.ShapeDtypeStruct((b, t, d), F32),
                   jax.ShapeDtypeStruct((b, n_lat, c), F32), jax.ShapeDtypeStruct((b, n_lat, c), BF16)),
        grid=(b, t // ROW_TILE),
        in_specs=_residual_specs(o, d, n_lat_tiles) +
                 [full(g), pl.BlockSpec((None, None, 3, d), lambda bi, i: (bi, 0, 0, 0)), full(w)],
        out_specs=(pl.BlockSpec((None, ROW_TILE, d), row),
                   pl.BlockSpec((None, ROW_TILE, c), lat_row), pl.BlockSpec((None, ROW_TILE, c), lat_row)),
        compiler_params=_cparams(2), name="cf_proj",
    )(o, sgp, x, modp, w_out, g, mod, w)


def _cf_tail_kernel(up_ref, u_ref, un_ref, sg_ref, x_ref, mod_ref, dw_ref, db_ref, lg_ref, lb_ref,
                    w_ref, fg_ref, o_ref, ext_ref, win_ref):
    i = pl.program_id(1)
    nt = pl.num_programs(1)
    tm = u_ref.shape[0]
    zero = jnp.zeros(up_ref.shape, F32)
    ext_ref[0:CF_HALO, :] = jnp.where(i > 0, up_ref[...], zero)
    ext_ref[CF_HALO:CF_HALO + tm, :] = u_ref[...]
    ext_ref[CF_HALO + tm:, :] = jnp.where(i < nt - 1, un_ref[...], zero)
    pad = (CF_KERNEL - 1) // 2
    acc = jnp.zeros(u_ref.shape, F32) + db_ref[...]
    top = CF_HALO - pad + CF_KERNEL - 1
    span = tm + (top // SUBLANES) * SUBLANES
    for r in range(SUBLANES):
        taps = [k for k in range(CF_KERNEL) if (CF_HALO - pad + k) % SUBLANES == r]
        if not taps:
            continue
        if r:
            win_ref[...] = ext_ref[pl.ds(r, span), :]
        src = win_ref if r else ext_ref
        for k in taps:
            acc = acc + dw_ref[k:k + 1, :] * src[pl.ds(CF_HALO - pad + k - r, tm), :]
    mu = jnp.mean(acc, axis=-1, keepdims=True)
    xc = acc - mu
    var = jnp.mean(xc * xc, axis=-1, keepdims=True)
    ln = xc * lax.rsqrt(var + NORM_EPS) * lg_ref[...] + lb_ref[...]
    og = (_silu(ln) * sg_ref[...].astype(F32)).astype(BF16)
    xo = x_ref[...] + mod_ref[2:3, :] * _dot(og, w_ref[...])
    o_ref[...] = _rms(xo, fg_ref[...])


def _cf_tail(u, sg, x, mod, dw, db, lg, lb, w_out, fg):
    b, n_lat, c = u.shape
    d = x.shape[-1]
    nt = n_lat // ROW_TILE
    per = ROW_TILE // CF_HALO
    last_halo = n_lat // CF_HALO - 1
    full = lambda a: pl.BlockSpec(a.shape, lambda bi, i: (0,) * a.ndim)
    row = lambda bi, i: (bi, i, 0)
    return pl.pallas_call(
        _cf_tail_kernel,
        out_shape=jax.ShapeDtypeStruct((b, n_lat, d), F32),
        grid=(b, nt),
        in_specs=[pl.BlockSpec((None, CF_HALO, c), lambda bi, i: (bi, jnp.maximum(i * per - 1, 0), 0)),
                  pl.BlockSpec((None, ROW_TILE, c), row),
                  pl.BlockSpec((None, CF_HALO, c), lambda bi, i: (bi, jnp.minimum((i + 1) * per, last_halo), 0)),
                  pl.BlockSpec((None, ROW_TILE, c), row),
                  pl.BlockSpec((None, ROW_TILE, d), row),
                  pl.BlockSpec((None, None, 3, d), lambda bi, i: (bi, 0, 0, 0)),
                  full(dw), full(db), full(lg), full(lb), full(w_out), full(fg)],
        out_specs=pl.BlockSpec((None, ROW_TILE, d), row),
        scratch_shapes=[pltpu.VMEM((ROW_TILE + 2 * CF_HALO, c), F32),
                        pltpu.VMEM((ROW_TILE + 2 * CF_HALO - SUBLANES, c), F32)],
        compiler_params=_cparams(2), name="cf_tail",
    )(u, u, u, sg, x, mod, dw, db, lg, lb, w_out, fg)


def _rope_tables(n_lat, n_ctx):
    rows = n_lat // GRID_W
    row = jnp.repeat(jnp.arange(rows, dtype=F32), GRID_W)
    col = jnp.tile(jnp.arange(GRID_W, dtype=F32), rows)
    n_freq = ROPE_DIM // 4
    inv = ROPE_BASE ** (-jnp.arange(n_freq, dtype=F32) / n_freq)
    ang = jnp.concatenate([row[:, None] * inv, col[:, None] * inv], axis=-1)
    cos = jnp.concatenate([jnp.cos(ang), jnp.ones((n_ctx, ROPE_DIM // 2), F32)], axis=0)
    sin = jnp.concatenate([jnp.sin(ang), jnp.zeros((n_ctx, ROPE_DIM // 2), F32)], axis=0)
    return jnp.tile(cos, (1, 2)), jnp.tile(sin, (1, 2))


def kernel(x, c, ctx, c_ctx, norm_g, ada_w, ada_b, final_g, mla_w_in, mla_q_norm_g, mla_kv_norm_g, mla_w_uq, mla_w_ukv, mla_w_out, hy_w_in, hy_conv_w, hy_conv_b, hy_filt_w_in, hy_filt_w_hid, hy_filt_b, hy_filt_freq, hy_filt_w_out, hy_bias, hy_w_out, swa_w_in, swa_sink, swa_w_out, cf_w_in, cf_dw_w, cf_dw_b, cf_ln_g, cf_ln_b, cf_w_out):
    b, n_lat, d = x.shape
    n_ctx = ctx.shape[1]
    depth = norm_g.shape[0]
    assert depth == 4 and n_lat % ROW_TILE == 0 and n_ctx == ATTN_TILE and ROW_TILE % n_ctx == 0
    assert n_lat % (FFT_N1H * SUBLANES) == 0 and n_lat % GRID_W == 0
    n_lat_tiles = n_lat // ROW_TILE
    bf = lambda a: a.astype(BF16)

    cs = jnp.concatenate([c, c_ctx[None, :], jnp.zeros((SUBLANES - b - 1, d), F32)], axis=0)
    mods = _adaln(cs, ada_w, ada_b)[:, :b + 1].reshape(depth, b + 1, 3, d)
    mods = jnp.stack([mods[:, :b], jnp.broadcast_to(mods[:, b:], (depth, b, 3, d))], axis=2)
    cos64, sin64 = _rope_tables(n_lat, ROW_TILE)
    cos128, sin128 = jnp.tile(cos64, (1, 2)), jnp.tile(sin64, (1, 2))

    c2 = MLA_Q_RANK + MLA_KV_RANK + MLA_ROPE
    w_lat = bf(jnp.pad(mla_w_in[0][:, :c2], ((0, 0), (0, LANES - MLA_ROPE))))
    w_gate = bf(mla_w_in[0][:, c2:])
    wuq = bf(jnp.pad(mla_w_uq[0].reshape(MLA_Q_RANK, MLA_HEADS, MLA_QK), ((0, 0), (0, 0), (0, LANES - MLA_ROPE)))
             .reshape(MLA_Q_RANK, MLA_HEADS * 2 * LANES))
    q, k, v, sg = _mla_proj(x, ctx, norm_g[0:1], mods[0], w_lat, w_gate, mla_q_norm_g[0:1], mla_kv_norm_g[0:1],
                            wuq, bf(mla_w_ukv[0]), cos128, sin128, n_lat_tiles)
    o = _mla_attn(q, k, v, n_lat, n_ctx)

    xs, z0, g0, g1, sg = _hy_proj(o, sg, x, ctx, mods[0], bf(mla_w_out[0]), norm_g[1:2], mods[1], bf(hy_w_in[0]),
                                  hy_conv_w[0], hy_conv_b[0][None, :], n_lat_tiles)
    cch = z0.shape[-1]
    filt = (hy_filt_w_in[0], hy_filt_w_hid[0], hy_filt_b[0], hy_filt_freq[0], hy_filt_w_out[0])
    assert cch % HY_SLAB == 0
    mats = _fft_matrices(n_lat)
    spec = _hy_spec(_hy_filter_taps(n_lat, *filt), mats[1], mats[2])
    cfwd, cinv, nfp = _dense_dft_matrices(n_ctx)
    cspec = _hy_ctx_spec(_hy_filter_taps(n_ctx, *filt), cfwd)
    gates = (g0, g1)
    z = z0
    for order in range(HY_ORDER):
        bias = hy_bias[0][order][None, :]
        z = _hy_conv(z, gates[order], spec, order, mats, bias, n_lat)
        z = _hy_ctx_conv(z, gates[order], cspec, order, cfwd[:, :n_ctx], cinv, nfp, bias, n_lat, n_ctx)

    nq = SWA_Q_HEADS * SWA_HEAD_DIM
    nkv = SWA_KV_HEADS * SWA_HEAD_DIM
    w_in = swa_w_in[0]
    dup = lambda w: jnp.tile(w.reshape(d, SWA_KV_HEADS, 1, SWA_HEAD_DIM), (1, 1, 2, 1)).reshape(d, 2 * nkv)
    w2 = bf(jnp.concatenate([w_in[:, :nq], dup(w_in[:, nq:nq + nkv]), dup(w_in[:, nq + nkv:nq + 2 * nkv]),
                             w_in[:, nq + 2 * nkv:]], axis=1))
    xs, q, k, v, sg = _swa_proj(z, sg, xs, mods[1], bf(hy_w_out[0]), norm_g[2:3], mods[2], w2, cos128, sin128,
                                n_lat_tiles)
    o = _swa_attn(swa_sink[0], q, k, v, n_lat, n_ctx)

    xs, u, sg = _cf_proj(o, sg, xs, mods[2], bf(swa_w_out[0]), norm_g[3:4], mods[3], bf(cf_w_in[0]), n_lat_tiles)
    return _cf_tail(u, sg, xs, mods[3], cf_dw_w[0], cf_dw_b[0][None, :], cf_ln_g[0][None, :], cf_ln_b[0][None, :],
                    bf(cf_w_out[0]), final_g[None, :])
```

```python
import functools
import math

import numpy as np
import jax
import jax.numpy as jnp
from jax import lax
from jax.experimental import pallas as pl
from jax.experimental.pallas import tpu as pltpu

F32 = jnp.float32
BF16 = jnp.bfloat16

GRID_W = 64
NORM_EPS = 1e-6
NEG_INF = -1e30
LOG2E = math.log2(math.e)
ROPE_BASE = 10000.0
ROPE_DIM = 64

MLA_HEADS = 8
MLA_Q_RANK = 384
MLA_KV_RANK = 256
MLA_NOPE = 128
MLA_ROPE = ROPE_DIM
MLA_V = 128
MLA_QK = MLA_NOPE + MLA_ROPE

HY_ORDER = 2
HY_EMB = 33
HY_EMB_PAD = 64
HY_BANDS = (HY_EMB - 1) // 2
HY_DECAY_TARGET = 1e-2
HY_FAST_DECAY = 0.3
HY_SLOW_DECAY = 1.5

SWA_Q_HEADS = 16
SWA_KV_HEADS = 4
SWA_GROUP = SWA_Q_HEADS // SWA_KV_HEADS
SWA_HEAD_DIM = ROPE_DIM
SWA_WINDOW = 128

CF_KERNEL = 31
CF_HALO = 16

ROW_TILE = 512
ATTN_TILE = 256
SUBLANES = 8
LANES = 128
FFT_N1 = 64
FFT_N1H = FFT_N1 // 2
FFT_K1 = FFT_N1 // 2 + 1
HY_SLAB = 256
HY_HALO = 16
VMEM_LIMIT = 48 * 1024 * 1024
VMEM_LIMIT_BIG = 56 * 1024 * 1024


def _cparams(n_axes):
    return pltpu.CompilerParams(dimension_semantics=("arbitrary",) * n_axes,
                                vmem_limit_bytes=VMEM_LIMIT)


def _cparams_big(n_axes):
    return pltpu.CompilerParams(dimension_semantics=("arbitrary",) * n_axes,
                                vmem_limit_bytes=VMEM_LIMIT_BIG)


def _once(shape, index_map):
    return pl.BlockSpec(shape, index_map, pipeline_mode=pl.Buffered(1))


def _dot(a, b):
    return jnp.dot(a, b, preferred_element_type=F32)


def _dot_nt(a, b):
    return lax.dot_general(a, b, (((1,), (1,)), ((), ())), preferred_element_type=F32)


def _rms(x, g):
    return x * lax.rsqrt(jnp.mean(x * x, axis=-1, keepdims=True) + NORM_EPS) * g


def _norm_mod(x, g, mod):
    return _rms(x, g) * (1.0 + mod[1:2, :]) + mod[0:1, :]


def _silu(x):
    return x * jax.nn.sigmoid(x)


def _adaln_kernel(c_ref, w_ref, b_ref, o_ref):
    s = _silu(c_ref[...]).astype(BF16)
    o_ref[...] = _dot(s, w_ref[...].astype(BF16)) + b_ref[...]


def _adaln(cs, ada_w, ada_b):
    depth, d, d3 = ada_w.shape
    tn = 512
    return pl.pallas_call(
        _adaln_kernel,
        out_shape=jax.ShapeDtypeStruct((depth, cs.shape[0], d3), F32),
        grid=(depth, d3 // tn),
        in_specs=[pl.BlockSpec(cs.shape, lambda i, j: (0, 0)),
                  pl.BlockSpec((None, d, tn), lambda i, j: (i, 0, j)),
                  pl.BlockSpec((None, 1, tn), lambda i, j: (i, 0, j))],
        out_specs=pl.BlockSpec((None, cs.shape[0], tn), lambda i, j: (i, 0, j)),
        compiler_params=_cparams(2), name="adaln",
    )(cs, ada_w, ada_b.reshape(depth, 1, d3))


def _input_tile(x_ref, ctx_ref, is_ctx):
    pad = jnp.zeros((x_ref.shape[0] - ctx_ref.shape[0], x_ref.shape[1]), F32)
    return jnp.where(is_ctx, jnp.concatenate([ctx_ref[...], pad], axis=0), x_ref[...])


def _input_specs(d, n_ctx, n_lat_tiles):
    return [pl.BlockSpec((None, ROW_TILE, d), lambda bi, i: (bi, jnp.minimum(i, n_lat_tiles - 1), 0)),
            pl.BlockSpec((None, n_ctx, d), lambda bi, i: (bi, 0, 0))]


def _residual_in(o_ref, sgp_ref, x_ref, modp_ref, wout_ref, xo_ref):
    og = (o_ref[...].astype(F32) * sgp_ref[...].astype(F32)).astype(BF16)
    x = x_ref[...] + modp_ref[2:3, :] * _dot(og, wout_ref[...])
    xo_ref[...] = x
    return x


def _residual_specs(o, d, n_lat_tiles):
    width = o.shape[-1]
    row = lambda bi, i: (bi, i, 0)
    return [pl.BlockSpec((None, ROW_TILE, width), row), pl.BlockSpec((None, ROW_TILE, width), row),
            pl.BlockSpec((None, ROW_TILE, d), row),
            pl.BlockSpec((None, None, 3, d), lambda bi, i: (bi, i // n_lat_tiles, 0, 0)),
            _once((width, d), lambda bi, i: (0, 0))]


def _rope_lanes(x, cos, sin):
    half = ROPE_DIM // 2
    lane = lax.broadcasted_iota(jnp.int32, x.shape, 1) % ROPE_DIM
    rot = jnp.where(lane < half, -pltpu.roll(x, LANES - half, 1), pltpu.roll(x, half, 1))
    return x * cos + rot * sin


def _mla_proj_kernel(x_ref, ctx_ref, g_ref, mod_ref, wl_ref, wg_ref, qg_ref, kvg_ref, wuq_ref, wukv_ref,
                     cos_ref, sin_ref, q_ref, k_ref, v_ref, sg_ref, *, n_lat_tiles):
    x = _input_tile(x_ref, ctx_ref, pl.program_id(1) >= n_lat_tiles)
    h = _norm_mod(x, g_ref[...], mod_ref[...]).astype(BF16)
    sg_ref[...] = _silu(_dot(h, wg_ref[...])).astype(sg_ref.dtype)
    res = _dot(h, wl_ref[...])
    c0, c1 = MLA_Q_RANK, MLA_Q_RANK + MLA_KV_RANK
    cqn = _rms(res[:, :c0], qg_ref[...]).astype(BF16)
    ckvn = _rms(res[:, c0:c1], kvg_ref[...]).astype(BF16)
    cos, sin = cos_ref[...], sin_ref[...]

    def rope(slab):
        return _rope_lanes(slab, cos, sin)[:, :MLA_ROPE]

    k_rope_t = rope(res[:, c1:c1 + LANES]).T.astype(k_ref.dtype)
    scale = MLA_QK ** -0.5 * LOG2E
    per = 2 * LANES
    for h0 in range(0, MLA_HEADS, 2):
        q2 = _dot(cqn, wuq_ref[:, h0 * per:(h0 + 2) * per])
        kv2 = _dot(ckvn, wukv_ref[:, h0 * per:(h0 + 2) * per])
        for j in range(2):
            hd = h0 + j
            q, kv = q2[:, j * per:(j + 1) * per], kv2[:, j * per:(j + 1) * per]
            q_ref[hd, :, :MLA_NOPE] = (q[:, :MLA_NOPE] * scale).astype(q_ref.dtype)
            q_ref[hd, :, MLA_NOPE:] = (rope(q[:, MLA_NOPE:]) * scale).astype(q_ref.dtype)
            k_ref[hd, :MLA_NOPE, :] = kv[:, :MLA_NOPE].T.astype(k_ref.dtype)
            k_ref[hd, MLA_NOPE:, :] = k_rope_t
            v_ref[hd, :, :MLA_V] = kv[:, MLA_NOPE:].astype(v_ref.dtype)
            v_ref[hd, :, MLA_V:] = jnp.ones((v_ref.shape[1], MLA_V), v_ref.dtype)


def _mla_proj(x, ctx, g, mod, wl, wg, qg, kvg, wuq, wukv, cos, sin, n_lat_tiles):
    b, n_lat, d = x.shape
    nt = n_lat_tiles + 1
    t = nt * ROW_TILE
    width = MLA_HEADS * MLA_V
    full = lambda a: _once(a.shape, lambda bi, i: (0,) * a.ndim)
    head_out = lambda n: pl.BlockSpec((None, MLA_HEADS, ROW_TILE, n), lambda bi, i: (bi, 0, i, 0))
    return pl.pallas_call(
        functools.partial(_mla_proj_kernel, n_lat_tiles=n_lat_tiles),
        out_shape=(jax.ShapeDtypeStruct((b, MLA_HEADS, t, MLA_QK), BF16),
                   jax.ShapeDtypeStruct((b, MLA_HEADS, MLA_QK, t), BF16),
                   jax.ShapeDtypeStruct((b, MLA_HEADS, t, 2 * MLA_V), BF16),
                   jax.ShapeDtypeStruct((b, t, width), BF16)),
        grid=(b, nt),
        in_specs=_input_specs(d, ctx.shape[1], n_lat_tiles) +
                 [full(g),
                  pl.BlockSpec((None, None, 3, d), lambda bi, i: (bi, i // n_lat_tiles, 0, 0)),
                  full(wl), full(wg), full(qg), full(kvg), full(wuq), full(wukv),
                  pl.BlockSpec((ROW_TILE, LANES), lambda bi, i: (i, 0)),
                  pl.BlockSpec((ROW_TILE, LANES), lambda bi, i: (i, 0))],
        out_specs=(head_out(MLA_QK),
                   pl.BlockSpec((None, MLA_HEADS, MLA_QK, ROW_TILE), lambda bi, i: (bi, 0, 0, i)),
                   head_out(2 * MLA_V),
                   pl.BlockSpec((None, ROW_TILE, width), lambda bi, i: (bi, i, 0))),
        compiler_params=_cparams(2), name="mla_proj",
    )(x, ctx, g, mod, wl, wg, qg, kvg, wuq, wukv, cos, sin)


def _attend(q, k_parts, v_parts):
    s = [_dot(q, k) for k in k_parts]
    m = functools.reduce(jnp.maximum, [jnp.max(x, axis=-1, keepdims=True) for x in s])
    ov = functools.reduce(jnp.add, [_dot(jnp.exp2(x - m).astype(BF16), v) for x, v in zip(s, v_parts)])
    n = ov.shape[-1] // 2
    return ov[:, :n] / ov[:, n:]


MLA_HEADS_PER_STEP = 2


def _mla_attn_kernel(q_ref, k_ref, v_ref, o_ref, *, n_lat_tiles, n_lat, n_ctx):
    qi = pl.program_id(2)

    n_keys = n_lat + n_ctx

    @pl.when(qi < n_lat_tiles)
    def _():
        chains = [(hd, r0) for hd in range(MLA_HEADS_PER_STEP) for r0 in range(0, q_ref.shape[1], ATTN_TILE)]
        s = [_dot(q_ref[hd, r0:r0 + ATTN_TILE, :], k_ref[hd, :, 0:n_keys]) for hd, r0 in chains]
        p = [jnp.exp2(x - jnp.max(x, axis=-1, keepdims=True)).astype(BF16) for x in s]
        for (hd, r0), pc in zip(chains, p):
            ov = _dot(pc, v_ref[hd, 0:n_keys, :])
            o = ov[:, :MLA_V] / ov[:, MLA_V:]
            o_ref[r0:r0 + ATTN_TILE, hd * MLA_V:(hd + 1) * MLA_V] = o.astype(o_ref.dtype)

    @pl.when(qi >= n_lat_tiles)
    def _():
        o_ref[n_ctx:, :] = jnp.zeros((o_ref.shape[0] - n_ctx, o_ref.shape[1]), o_ref.dtype)
        for hd in range(MLA_HEADS_PER_STEP):
            k = k_ref[hd, :, n_lat:n_keys]
            v = v_ref[hd, n_lat:n_keys, :]
            o = _attend(q_ref[hd, 0:n_ctx, :], [k], [v])
            o_ref[0:n_ctx, hd * MLA_V:(hd + 1) * MLA_V] = o.astype(o_ref.dtype)


def _mla_attn(q, k, v, n_lat, n_ctx):
    b, hds, t, _ = q.shape
    nt = t // ROW_TILE
    hps = MLA_HEADS_PER_STEP
    kern = functools.partial(_mla_attn_kernel, n_lat_tiles=n_lat // ROW_TILE, n_lat=n_lat, n_ctx=n_ctx)
    return pl.pallas_call(
        kern,
        out_shape=jax.ShapeDtypeStruct((b, t, hds * MLA_V), BF16),
        grid=(b, hds // hps, nt),
        in_specs=[pl.BlockSpec((None, hps, ROW_TILE, MLA_QK), lambda bi, h, i: (bi, h, i, 0)),
                  pl.BlockSpec((None, hps, MLA_QK, t), lambda bi, h, i: (bi, h, 0, 0)),
                  pl.BlockSpec((None, hps, t, 2 * MLA_V), lambda bi, h, i: (bi, h, 0, 0))],
        out_specs=pl.BlockSpec((None, ROW_TILE, hps * MLA_V), lambda bi, h, i: (bi, i, h)),
        compiler_params=_cparams_big(3), name="mla_attn",
    )(q, k, v)


def _hy_proj_kernel(op_ref, o_ref, on_ref, sp_ref, s_ref, sn_ref, xp_ref, x_ref, ctx_ref, xn_ref, modp_ref, wout_ref,
                    g_ref, mod_ref, w_ref, cw_ref, cb_ref, xo_ref, z_ref, g0_ref, g1_ref, sg_ref,
                    *, n_lat_tiles, n_ctx):
    i = pl.program_id(1)
    tm = x_ref.shape[0]
    c = z_ref.shape[-1]
    n = tm + 2 * SUBLANES
    halo = xp_ref.shape[0]
    cat = lambda a, m, z: jnp.concatenate([a[...], m, z[...]], axis=0)
    og = (cat(op_ref, o_ref[...], on_ref).astype(F32) * cat(sp_ref, s_ref[...], sn_ref).astype(F32)).astype(BF16)
    xe = cat(xp_ref, _input_tile(x_ref, ctx_ref, i >= n_lat_tiles), xn_ref)
    xe = xe + modp_ref[2:3, :] * _dot(og, wout_ref[...])
    xo_ref[...] = xe[halo:halo + tm]
    xa = xe[halo - SUBLANES:halo + tm + SUBLANES]
    hf = _norm_mod(xa, g_ref[...], mod_ref[...])
    h = hf.astype(BF16)
    ctx_tile = i == n_lat_tiles
    first = jnp.logical_or(i == 0, ctx_tile)
    last = jnp.logical_or(i == n_lat_tiles - 1, ctx_tile)
    rows = lax.broadcasted_iota(jnp.int32, (n, 1), 0)
    end = jnp.where(ctx_tile, n_ctx + SUBLANES, jnp.where(last, tm + SUBLANES, n))
    valid = jnp.logical_and(jnp.logical_or(rows >= SUBLANES, jnp.logical_not(first)), rows < end)
    outs = (z_ref, g0_ref, g1_ref)
    for j in range(len(outs)):
        u = jnp.where(valid, _dot(h, w_ref[:, j * c:(j + 1) * c]), 0.0)
        cw = cw_ref[:, j * c:(j + 1) * c]
        cv = cw[0:1, :] * pltpu.roll(u, 1, 0) + cw[1:2, :] * u + cw[2:3, :] * pltpu.roll(u, n - 1, 0)
        outs[j][...] = cv[SUBLANES:SUBLANES + tm] + cb_ref[:, j * c:(j + 1) * c]
    h_mid = hf[SUBLANES:SUBLANES + tm].astype(BF16)
    sg_ref[...] = _silu(_dot(h_mid, w_ref[:, len(outs) * c:])).astype(sg_ref.dtype)


def _hy_proj(o, sgp, x, ctx, modp, w_out, g, mod, w, cw, cb, n_lat_tiles):
    b, t, width = o.shape
    n_lat, d = x.shape[1:]
    n_ctx = ctx.shape[1]
    c = w.shape[1] // (HY_ORDER + 2)
    per = ROW_TILE // HY_HALO
    last = n_lat // HY_HALO - 1
    full = lambda a: _once(a.shape, lambda bi, i: (0,) * a.ndim)
    row = lambda bi, i: (bi, i, 0)
    prev = lambda bi, i: (bi, jnp.maximum(i * per - 1, 0), 0)
    nxt = lambda bi, i: (bi, jnp.minimum((i + 1) * per, last), 0)
    x_main, x_ctx = _input_specs(d, n_ctx, n_lat_tiles)
    kern = functools.partial(_hy_proj_kernel, n_lat_tiles=n_lat_tiles, n_ctx=n_ctx)
    halo = lambda n, m: pl.BlockSpec((None, HY_HALO, n), m)
    return pl.pallas_call(
        kern,
        out_shape=(jax.ShapeDtypeStruct((b, t, d), F32),) + (jax.ShapeDtypeStruct((b, t, c), F32),) * 3 +
                  (jax.ShapeDtypeStruct((b, t, c), BF16),),
        grid=(b, t // ROW_TILE),
        in_specs=[halo(width, prev), pl.BlockSpec((None, ROW_TILE, width), row), halo(width, nxt),
                  halo(width, prev), pl.BlockSpec((None, ROW_TILE, width), row), halo(width, nxt),
                  halo(d, prev), x_main, x_ctx, halo(d, nxt),
                  pl.BlockSpec((None, None, 3, d), lambda bi, i: (bi, i // n_lat_tiles, 0, 0)),
                  full(w_out), full(g),
                  pl.BlockSpec((None, None, 3, d), lambda bi, i: (bi, i // n_lat_tiles, 0, 0)),
                  full(w), full(cw), full(cb)],
        out_specs=(pl.BlockSpec((None, ROW_TILE, d), row),) + (pl.BlockSpec((None, ROW_TILE, c), row),) * 4,
        compiler_params=_cparams(2), name="hy_proj",
    )(o, o, o, sgp, sgp, sgp, x, x, ctx, x, modp, w_out, g, mod, w, cw, cb)


def _hy_filter_kernel(ef_ref, eb_ref, w0_ref, wh_ref, b_ref, fr_ref, wf_ref, wb_ref, dl_ref, o_ref):
    i = pl.program_id(0)
    c = dl_ref.shape[-1]
    half = ef_ref.shape[0]

    def taps(e, wo):
        hdn = e
        ws = (w0_ref[...], wh_ref[0], wh_ref[1])
        for k in range(3):
            hdn = jnp.sin(fr_ref[k:k + 1, :] * (_dot(hdn.astype(BF16), ws[k].astype(BF16)) + b_ref[k:k + 1, :]))
        hh = _dot(hdn.astype(BF16), wo.astype(BF16))
        out = []
        for part in range(2):
            decay = jnp.exp(-e[:, part * HY_EMB_PAD:part * HY_EMB_PAD + 1] * dl_ref[...])
            out.append([hh[:, (part * HY_ORDER + o) * c:(part * HY_ORDER + o + 1) * c] * decay
                        for o in range(HY_ORDER)])
        return out

    hf = taps(ef_ref[...], wf_ref[...])
    hb = taps(eb_ref[...], wb_ref[...])
    rows = lax.broadcasted_iota(jnp.int32, (half, 1), 0)
    lag0 = jnp.logical_and(rows == 0, i == 0)
    for o in range(HY_ORDER):
        o_ref[o, 0, 0:half] = hf[0][o] + jnp.where(lag0, hb[0][o], 0.0)
        o_ref[o, 1, 0:half] = jnp.where(lag0, 0.0, hb[0][o])
        o_ref[o, 0, half:] = hf[1][o]
        o_ref[o, 1, half:] = hb[1][o]


def _hy_position_features(n):
    t = jnp.linspace(0.0, 1.0, n, dtype=F32)[:, None]
    wpos = (2.0 * math.pi / n) * jnp.arange(n, dtype=F32)[:, None]
    bands = jnp.linspace(1e-4, HY_BANDS - 1, HY_BANDS, dtype=F32)[None, :]
    hdn = jnp.concatenate([t, jnp.cos(bands * wpos), -jnp.sin(bands * wpos)], axis=-1)
    return jnp.pad(hdn, ((0, 0), (0, HY_EMB_PAD - HY_EMB)))


def _hy_filter_taps(n, w_in, w_hid, bias, freq, w_out):
    c = w_out.shape[1] // (2 * HY_ORDER)
    ef = _hy_position_features(n)
    rev = (n - jnp.arange(n)) % n
    eb = ef[rev]
    w0 = jnp.pad(w_in, ((0, HY_EMB_PAD - HY_EMB), (0, 0)))
    wo = w_out.reshape(w_out.shape[0], HY_ORDER, 2, c)
    wf = wo[:, :, 0, :].reshape(w_out.shape[0], HY_ORDER * c)
    wb = wo[:, :, 1, :].reshape(w_out.shape[0], HY_ORDER * c)
    deltas = jnp.abs(jnp.linspace(math.log(HY_DECAY_TARGET) / HY_FAST_DECAY,
                                  math.log(HY_DECAY_TARGET) / HY_SLOW_DECAY, c, dtype=F32))[None, :]
    tl = min(n, 512)
    half = tl // 2
    pair = lambda e: e.reshape(n // tl, 2, half, HY_EMB_PAD).transpose(0, 2, 1, 3).reshape(n // 2, 2 * HY_EMB_PAD)
    both = lambda w: jnp.kron(jnp.eye(2, dtype=F32), w)
    twice = lambda v: jnp.tile(v, (1, 2))
    w_hid2 = jnp.stack([both(w_hid[0]), both(w_hid[1])])
    full = lambda a: pl.BlockSpec(a.shape, lambda i: (0,) * a.ndim)
    args = (pair(ef), pair(eb), both(w0), w_hid2, twice(bias), twice(freq), both(wf), both(wb), deltas)
    return pl.pallas_call(
        _hy_filter_kernel,
        out_shape=jax.ShapeDtypeStruct((HY_ORDER, 2, n, c), F32),
        grid=(n // tl,),
        in_specs=[pl.BlockSpec((half, 2 * HY_EMB_PAD), lambda i: (i, 0)),
                  pl.BlockSpec((half, 2 * HY_EMB_PAD), lambda i: (i, 0))] + [full(a) for a in args[2:]],
        out_specs=pl.BlockSpec((HY_ORDER, 2, tl, c), lambda i: (0, 0, i, 0)),
        compiler_params=_cparams(1), name="hy_filter",
    )(*args)


def _fft_matrices(n_lat):
    n = 2 * n_lat
    n2 = n_lat // FFT_N1H
    k1 = np.arange(FFT_K1)
    n1 = np.arange(FFT_N1H)
    ang = 2.0 * np.pi * np.outer(k1, n1) / FFT_N1
    eye = np.eye(SUBLANES)
    f1 = np.kron(np.concatenate([np.cos(ang), -np.sin(ang)], axis=0), eye)
    wgt = np.full(FFT_K1, 2.0)
    wgt[0] = wgt[-1] = 1.0
    g1 = np.kron(np.concatenate([wgt * np.cos(ang.T), -wgt * np.sin(ang.T)], axis=1) / n, eye)
    idx = np.arange(n2)
    f2 = np.zeros((FFT_K1, 2 * n2, 2 * n2))
    f2i = np.zeros((FFT_K1, 2 * n2, 2 * n2))
    for a in range(FFT_K1):
        ph = 2.0 * np.pi * (np.outer(idx, idx) / n2 + (a * idx)[None, :] / n)
        f2[a] = np.block([[np.cos(ph), np.sin(ph)], [-np.sin(ph), np.cos(ph)]])
        f2i[a] = np.block([[np.cos(ph.T), -np.sin(ph.T)], [np.sin(ph.T), np.cos(ph.T)]])
    sign = np.kron(np.tile((-1.0) ** k1, 2), np.ones(SUBLANES))[:, None]
    f1s = np.concatenate([f1, sign * f1], axis=1)
    as_bf16 = lambda m: jnp.asarray(m.astype(np.float32)).astype(BF16)
    return as_bf16(f1), as_bf16(f1s), as_bf16(f2), as_bf16(f2i), as_bf16(g1)


def _dense_dft_matrices(n_ctx):
    n = 2 * n_ctx
    nf = n_ctx + 1
    nfp = -(-nf // SUBLANES) * SUBLANES
    k = np.arange(nf)
    ang = 2.0 * np.pi * np.outer(k, np.arange(n)) / n
    fwd = np.zeros((2 * nfp, n))
    fwd[:nf] = np.cos(ang)
    fwd[nfp:nfp + nf] = -np.sin(ang)
    wgt = np.full(nf, 2.0)
    wgt[0] = wgt[-1] = 1.0
    angi = 2.0 * np.pi * np.outer(np.arange(n_ctx), k) / n
    inv = np.zeros((n_ctx, 2 * nfp))
    inv[:, :nf] = wgt * np.cos(angi) / n
    inv[:, nfp:nfp + nf] = -wgt * np.sin(angi) / n
    as_bf16 = lambda m: jnp.asarray(m.astype(np.float32)).astype(BF16)
    return as_bf16(fwd), as_bf16(inv), nfp


def _hy_spec_kernel(t_ref, f1s_ref, f2_ref, h_ref, a_ref):
    _, n1h, nj, s, tc = t_ref.shape
    n2 = nj * s

    def stage1(j, carry):
        sig = jnp.concatenate([t_ref[0, :, j].reshape(n1h * s, tc), t_ref[1, :, j].reshape(n1h * s, tc)], axis=0)
        a_ref[:, :, j] = _dot(f1s_ref[...], sig.astype(BF16)).reshape(2, FFT_K1, s, tc)
        return carry

    lax.fori_loop(0, nj, stage1, 0, unroll=min(nj, 4))

    def stage2(k, carry):
        a = a_ref[:, k].reshape(2 * n2, tc).astype(BF16)
        h_ref[:, k] = _dot(f2_ref[k], a).reshape(2, n2, tc).astype(h_ref.dtype)
        return carry

    lax.fori_loop(0, FFT_K1, stage2, 0, unroll=11)


def _hy_spec(taps, f1s, f2):
    _, _, n_lat, c = taps.shape
    n2 = n_lat // FFT_N1H
    nj = n2 // SUBLANES
    tc = HY_SLAB
    return pl.pallas_call(
        _hy_spec_kernel,
        out_shape=jax.ShapeDtypeStruct((HY_ORDER, 2, FFT_K1, n2, c), BF16),
        grid=(c // tc, HY_ORDER),
        in_specs=[pl.BlockSpec((None, 2, FFT_N1H, nj, SUBLANES, tc), lambda ci, o: (o, 0, 0, 0, 0, ci)),
                  _once(f1s.shape, lambda ci, o: (0, 0)),
                  _once(f2.shape, lambda ci, o: (0, 0, 0))],
        out_specs=pl.BlockSpec((None, 2, FFT_K1, n2, tc), lambda ci, o: (o, 0, 0, 0, ci)),
        scratch_shapes=[pltpu.VMEM((2, FFT_K1, nj, SUBLANES, tc), F32)],
        compiler_params=_cparams_big(2), name="hy_spec",
    )(taps.reshape(HY_ORDER, 2, FFT_N1H, nj, SUBLANES, c), f1s, f2)


def _hy_conv_kernel(z_ref, gt_ref, h_ref, f1_ref, f2_ref, f2i_ref, g1_ref, bias_ref, o_ref, a_ref):
    n1h, nj, s, tc = z_ref.shape
    n2 = nj * s

    def stage1(j, carry):
        zz = z_ref[:, j].reshape(n1h * s, tc).astype(BF16)
        a_ref[:, :, j] = _dot(f1_ref[...], zz).reshape(2, FFT_K1, s, tc)
        return carry

    lax.fori_loop(0, nj, stage1, 0, unroll=min(nj, 4))

    def stage2(k, carry):
        x = _dot(f2_ref[k], a_ref[:, k].reshape(2 * n2, tc).astype(BF16))
        xr, xi = x[:n2], x[n2:]
        hr, hi = h_ref[0, k].astype(F32), h_ref[1, k].astype(F32)
        y = jnp.concatenate([xr * hr - xi * hi, xr * hi + xi * hr], axis=0).astype(BF16)
        a_ref[:, k] = _dot(f2i_ref[k], y).reshape(2, nj, s, tc)
        return carry

    lax.fori_loop(0, FFT_K1, stage2, 0, unroll=11)

    def stage3(j, carry):
        bp = a_ref[:, :, j].reshape(2 * FFT_K1 * s, tc).astype(BF16)
        y = _dot(g1_ref[...], bp).reshape(n1h, s, tc)
        o_ref[:, j] = gt_ref[:, j] * (y + z_ref[:, j] * bias_ref[...])
        return carry

    lax.fori_loop(0, nj, stage3, 0, unroll=min(nj, 4))


def _hy_conv(z, gate, h, order, mats, bias, n_lat):
    f1, _, f2, f2i, g1 = mats
    bz, t, c = z.shape
    n2 = n_lat // FFT_N1H
    nj = n2 // SUBLANES
    tc = HY_SLAB
    view = (bz, t // n2, nj, SUBLANES, c)
    slab = pl.BlockSpec((None, FFT_N1H, nj, SUBLANES, tc), lambda ci, bi: (bi, 0, 0, 0, ci))
    out = pl.pallas_call(
        _hy_conv_kernel,
        out_shape=jax.ShapeDtypeStruct(view, F32),
        grid=(c // tc, bz),
        in_specs=[slab, slab,
                  _once((None, 2, FFT_K1, n2, tc), lambda ci, bi: (order, 0, 0, 0, ci)),
                  _once(f1.shape, lambda ci, bi: (0, 0)),
                  _once(f2.shape, lambda ci, bi: (0, 0, 0)),
                  _once(f2i.shape, lambda ci, bi: (0, 0, 0)),
                  _once(g1.shape, lambda ci, bi: (0, 0)),
                  pl.BlockSpec((1, tc), lambda ci, bi: (0, ci))],
        out_specs=slab,
        input_output_aliases={0: 0},
        scratch_shapes=[pltpu.VMEM((2, FFT_K1, nj, SUBLANES, tc), F32)],
        compiler_params=_cparams_big(2), name="hy_conv",
    )(z.reshape(view), gate.reshape(view), h, f1, f2, f2i, g1, bias)
    return out.reshape(bz, t, c)


def _hy_ctx_spec_kernel(t_ref, f_ref, h_ref):
    sig = jnp.concatenate([t_ref[0], t_ref[1]], axis=0).astype(BF16)
    h_ref[...] = _dot(f_ref[...], sig)


def _hy_ctx_spec(taps, fwd):
    _, _, n_ctx, c = taps.shape
    return pl.pallas_call(
        _hy_ctx_spec_kernel,
        out_shape=jax.ShapeDtypeStruct((HY_ORDER, fwd.shape[0], c), F32),
        grid=(HY_ORDER,),
        in_specs=[pl.BlockSpec((None, 2, n_ctx, c), lambda o: (o, 0, 0, 0)),
                  pl.BlockSpec(fwd.shape, lambda o: (0, 0))],
        out_specs=pl.BlockSpec((None, fwd.shape[0], c), lambda o: (o, 0, 0)),
        compiler_params=_cparams(1), name="hy_ctx_spec",
    )(taps, fwd)


def _hy_ctx_conv_kernel(zin_ref, gt_ref, h_ref, f_ref, fi_ref, bias_ref, o_ref, *, nfp):
    z = zin_ref[...]
    x = _dot(f_ref[...], z.astype(BF16))
    xr, xi = x[:nfp], x[nfp:]
    hr, hi = h_ref[:nfp, :], h_ref[nfp:, :]
    y = jnp.concatenate([xr * hr - xi * hi, xr * hi + xi * hr], axis=0).astype(BF16)
    n_ctx = z.shape[0]
    o_ref[0:n_ctx, :] = gt_ref[...] * (_dot(fi_ref[...], y) + z * bias_ref[...])
    o_ref[n_ctx:, :] = jnp.zeros((o_ref.shape[0] - n_ctx, o_ref.shape[1]), o_ref.dtype)


def _hy_ctx_conv(z, gate, h, order, fwd_half, inv, nfp, bias, n_lat, n_ctx):
    bz, t, c = z.shape
    blk = pl.BlockSpec((None, n_ctx, c), lambda bi: (bi, n_lat // n_ctx, 0))
    kern = functools.partial(_hy_ctx_conv_kernel, nfp=nfp)
    return pl.pallas_call(
        kern,
        out_shape=jax.ShapeDtypeStruct(z.shape, F32),
        grid=(bz,),
        in_specs=[blk, blk,
                  pl.BlockSpec((None, 2 * nfp, c), lambda bi: (order, 0, 0)),
                  pl.BlockSpec(fwd_half.shape, lambda bi: (0, 0)),
                  pl.BlockSpec(inv.shape, lambda bi: (0, 0)),
                  pl.BlockSpec((1, c), lambda bi: (0, 0))],
        out_specs=pl.BlockSpec((None, t - n_lat, c), lambda bi: (bi, n_lat // (t - n_lat), 0)),
        input_output_aliases={0: 0},
        compiler_params=_cparams(1), name="hy_ctx_conv",
    )(z, gate, h, fwd_half, inv, bias)


def _swa_proj_kernel(o_ref, sgp_ref, x_ref, modp_ref, wout_ref, g_ref, mod_ref, w_ref, cos_ref, sin_ref,
                     xo_ref, q_ref, k_ref, v_ref, sg_ref):
    x = _residual_in(o_ref, sgp_ref, x_ref, modp_ref, wout_ref, xo_ref)
    h = _norm_mod(x, g_ref[...], mod_ref[...]).astype(BF16)
    cos, sin = cos_ref[...], sin_ref[...]
    nq = q_ref.shape[-1]
    nk = SWA_KV_HEADS * LANES
    scale = SWA_HEAD_DIM ** -0.5 * LOG2E
    for c0 in range(0, nq, nk):
        res = _dot(h, w_ref[:, c0:c0 + nk])
        for cb in range(nk // LANES):
            sl = slice(cb * LANES, (cb + 1) * LANES)
            q_ref[:, c0 + cb * LANES:c0 + (cb + 1) * LANES] = (_rope_lanes(res[:, sl], cos, sin) * scale).astype(q_ref.dtype)
    res = _dot(h, w_ref[:, nq:nq + nk])
    for hd in range(SWA_KV_HEADS):
        k_ref[hd] = _rope_lanes(res[:, hd * LANES:(hd + 1) * LANES], cos, sin).astype(k_ref.dtype)
    res = _dot(h, w_ref[:, nq + nk:nq + 2 * nk])
    for hd in range(SWA_KV_HEADS):
        v_ref[hd, :, :LANES] = res[:, hd * LANES:(hd + 1) * LANES].astype(v_ref.dtype)
        v_ref[hd, :, LANES:] = jnp.ones((v_ref.shape[1], LANES), v_ref.dtype)
    for c0 in range(0, nq, nk):
        sg_ref[:, c0:c0 + nk] = _silu(_dot(h, w_ref[:, nq + 2 * nk + c0:nq + 2 * nk + c0 + nk])).astype(sg_ref.dtype)


def _swa_proj(o, sgp, x, modp, w_out, g, mod, w, cos, sin, n_lat_tiles):
    b, t, d = x.shape
    nt = t // ROW_TILE
    nq = SWA_Q_HEADS * SWA_HEAD_DIM
    full = lambda a: _once(a.shape, lambda bi, i: (0,) * a.ndim)
    row = lambda bi, i: (bi, i, 0)
    kv = lambda n: pl.BlockSpec((None, SWA_KV_HEADS, ROW_TILE, n), lambda bi, i: (bi, 0, i, 0))
    return pl.pallas_call(
        _swa_proj_kernel,
        out_shape=(jax.ShapeDtypeStruct((b, t, d), F32),
                   jax.ShapeDtypeStruct((b, t, nq), BF16),
                   jax.ShapeDtypeStruct((b, SWA_KV_HEADS, t, LANES), BF16),
                   jax.ShapeDtypeStruct((b, SWA_KV_HEADS, t, 2 * LANES), BF16),
                   jax.ShapeDtypeStruct((b, t, nq), BF16)),
        grid=(b, nt),
        in_specs=_residual_specs(o, d, n_lat_tiles) +
                 [full(g),
                  pl.BlockSpec((None, None, 3, d), lambda bi, i: (bi, i // n_lat_tiles, 0, 0)),
                  full(w),
                  pl.BlockSpec((ROW_TILE, LANES), lambda bi, i: (i, 0)),
                  pl.BlockSpec((ROW_TILE, LANES), lambda bi, i: (i, 0))],
        out_specs=(pl.BlockSpec((None, ROW_TILE, d), row), pl.BlockSpec((None, ROW_TILE, nq), row),
                   kv(LANES), kv(2 * LANES), pl.BlockSpec((None, ROW_TILE, nq), row)),
        compiler_params=_cparams(2), name="swa_proj",
    )(o, sgp, x, modp, w_out, g, mod, w, cos, sin)


def _swa_attn_kernel(sink_ref, q_ref, k_ref, v_ref, o_ref, *, n_lat_tiles, n_lat, n_ctx):
    g = pl.program_id(1)
    qi = pl.program_id(2)
    tq = ATTN_TILE
    span = tq + 2 * SWA_WINDOW
    lane_half = lax.broadcasted_iota(jnp.int32, (tq, LANES), 1) // SWA_HEAD_DIM
    first = lax.broadcasted_iota(jnp.int32, (2 * tq, 1), 0) < tq
    kc = k_ref[pl.ds(n_lat, n_ctx), :]
    vc = v_ref[pl.ds(n_lat, n_ctx), :]

    def pair_query(r0, col):
        qcol = q_ref[r0:r0 + tq, col * LANES:(col + 1) * LANES]
        zero = jnp.zeros_like(qcol)
        return jnp.concatenate([jnp.where(lane_half == 0, qcol, zero), jnp.where(lane_half == 1, qcol, zero)], axis=0)

    def pair_sink(col):
        base = g * SWA_GROUP + 2 * col
        return jnp.where(first, sink_ref[base], sink_ref[base + 1]) * LOG2E

    def finish(r0, col, ov, m, sink):
        o = ov[:, :LANES] / (ov[:, LANES:] + jnp.exp2(sink - m))
        o_ref[r0:r0 + tq, col * LANES:(col + 1) * LANES] = jnp.where(lane_half == 0, o[:tq], o[tq:]).astype(o_ref.dtype)

    @pl.when(qi < n_lat_tiles)
    def _():
        chains = []
        for r0 in range(0, q_ref.shape[0], tq):
            q0 = qi * q_ref.shape[0] + r0
            start = pl.multiple_of(jnp.clip(q0 - SWA_WINDOW, 0, n_lat - span), SWA_WINDOW)
            kw = k_ref[pl.ds(start, span), :]
            vw = v_ref[pl.ds(start, span), :]
            qpos = q0 + lax.broadcasted_iota(jnp.int32, (tq, 1), 0)
            kpos = start + lax.broadcasted_iota(jnp.int32, (1, span), 1)
            band = jnp.abs(kpos - qpos) <= SWA_WINDOW
            for col in range(SWA_GROUP // 2):
                q2, sink = pair_query(r0, col), pair_sink(col)
                sw = _dot_nt(q2, kw)
                sw = jnp.concatenate([jnp.where(band, sw[:tq], NEG_INF), jnp.where(band, sw[tq:], NEG_INF)], axis=0)
                sc = _dot_nt(q2, kc)
                m = jnp.maximum(jnp.maximum(jnp.max(sw, axis=-1, keepdims=True),
                                            jnp.max(sc, axis=-1, keepdims=True)), sink)
                chains.append((r0, col, jnp.exp2(sw - m).astype(BF16), jnp.exp2(sc - m).astype(BF16), vw, m, sink))
        for r0, col, pw, pc, vw, m, sink in chains:
            finish(r0, col, _dot(pw, vw) + _dot(pc, vc), m, sink)

    @pl.when(qi >= n_lat_tiles)
    def _():
        o_ref[n_ctx:, :] = jnp.zeros((o_ref.shape[0] - n_ctx, o_ref.shape[1]), o_ref.dtype)
        for col in range(SWA_GROUP // 2):
            q2, sink = pair_query(0, col), pair_sink(col)
            sc = _dot_nt(q2, kc)
            m = jnp.maximum(jnp.max(sc, axis=-1, keepdims=True), sink)
            finish(0, col, _dot(jnp.exp2(sc - m).astype(BF16), vc), m, sink)


def _swa_attn(sink, q, k, v, n_lat, n_ctx):
    b, t, nq = q.shape
    nt = t // ROW_TILE
    gw = SWA_GROUP * SWA_HEAD_DIM
    kern = functools.partial(_swa_attn_kernel, n_lat_tiles=n_lat // ROW_TILE, n_lat=n_lat, n_ctx=n_ctx)
    kv = lambda n: pl.BlockSpec((None, None, t, n), lambda bi, gi, i: (bi, gi, 0, 0))
    return pl.pallas_call(
        kern,
        out_shape=jax.ShapeDtypeStruct((b, t, nq), BF16),
        grid=(b, SWA_KV_HEADS, nt),
        in_specs=[pl.BlockSpec(memory_space=pltpu.SMEM),
                  pl.BlockSpec((None, ROW_TILE, gw), lambda bi, gi, i: (bi, i, gi)), kv(LANES), kv(2 * LANES)],
        out_specs=pl.BlockSpec((None, ROW_TILE, gw), lambda bi, gi, i: (bi, i, gi)),
        compiler_params=_cparams(3), name="swa_attn",
    )(sink, q, k, v)


def _cf_proj_kernel(o_ref, sgp_ref, x_ref, modp_ref, wout_ref, g_ref, mod_ref, w_ref, xo_ref, u_ref, sg_ref,
                    *, n_lat_tiles):
    x = _residual_in(o_ref, sgp_ref, x_ref, modp_ref, wout_ref, xo_ref)

    @pl.when(pl.program_id(1) < n_lat_tiles)
    def _():
        h = _norm_mod(x, g_ref[...], mod_ref[...]).astype(BF16)
        c = u_ref.shape[-1]
        tn = c // 2
        for c0 in range(0, c, tn):
            a = _dot(h, w_ref[:, c0:c0 + tn])
            gl = _dot(h, w_ref[:, c + c0:c + c0 + tn])
            u_ref[:, c0:c0 + tn] = a * jax.nn.sigmoid(gl)
            sg_ref[:, c0:c0 + tn] = _silu(_dot(h, w_ref[:, 2 * c + c0:2 * c + c0 + tn])).astype(sg_ref.dtype)


def _cf_proj(o, sgp, x, modp, w_out, g, mod, w, n_lat_tiles):
    b, t, d = x.shape
    c = w.shape[1] // 3
    n_lat = n_lat_tiles * ROW_TILE
    full = lambda a: _once(a.shape, lambda bi, i: (0,) * a.ndim)
    row = lambda bi, i: (bi, i, 0)
    lat_row = lambda bi, i: (bi, jnp.minimum(i, n_lat_tiles - 1), 0)
    return pl.pallas_call(
        functools.partial(_cf_proj_kernel, n_lat_tiles=n_lat_tiles),
        out_shape=(jax.ShapeDtypeStruct((b, t, d), F32),
                   jax.ShapeDtypeStruct((b, n_lat, c), F32), jax.ShapeDtypeStruct((b, n_lat, c), BF16)),
        grid=(b, t // ROW_TILE),
        in_specs=_residual_specs(o, d, n_lat_tiles) +
                 [full(g), pl.BlockSpec((None, None, 3, d), lambda bi, i: (bi, 0, 0, 0)), full(w)],
        out_specs=(pl.BlockSpec((None, ROW_TILE, d), row),
                   pl.BlockSpec((None, ROW_TILE, c), lat_row), pl.BlockSpec((None, ROW_TILE, c), lat_row)),
        compiler_params=_cparams(2), name="cf_proj",
    )(o, sgp, x, modp, w_out, g, mod, w)


def _cf_tail_kernel(up_ref, u_ref, un_ref, sg_ref, x_ref, mod_ref, dw_ref, db_ref, lg_ref, lb_ref,
                    w_ref, fg_ref, o_ref, ext_ref, win_ref):
    i = pl.program_id(1)
    nt = pl.num_programs(1)
    tm = u_ref.shape[0]
    zero = jnp.zeros(up_ref.shape, F32)
    ext_ref[0:CF_HALO, :] = jnp.where(i > 0, up_ref[...], zero)
    ext_ref[CF_HALO:CF_HALO + tm, :] = u_ref[...]
    ext_ref[CF_HALO + tm:, :] = jnp.where(i < nt - 1, un_ref[...], zero)
    pad = (CF_KERNEL - 1) // 2
    acc = jnp.zeros(u_ref.shape, F32) + db_ref[...]
    top = CF_HALO - pad + CF_KERNEL - 1
    span = tm + (top // SUBLANES) * SUBLANES
    for r in range(SUBLANES):
        taps = [k for k in range(CF_KERNEL) if (CF_HALO - pad + k) % SUBLANES == r]
        if not taps:
            continue
        if r:
            win_ref[...] = ext_ref[pl.ds(r, span), :]
        src = win_ref if r else ext_ref
        for k in taps:
            acc = acc + dw_ref[k:k + 1, :] * src[pl.ds(CF_HALO - pad + k - r, tm), :]
    mu = jnp.mean(acc, axis=-1, keepdims=True)
    xc = acc - mu
    var = jnp.mean(xc * xc, axis=-1, keepdims=True)
    ln = xc * lax.rsqrt(var + NORM_EPS) * lg_ref[...] + lb_ref[...]
    og = (_silu(ln) * sg_ref[...].astype(F32)).astype(BF16)
    xo = x_ref[...] + mod_ref[2:3, :] * _dot(og, w_ref[...])
    o_ref[...] = _rms(xo, fg_ref[...])


def _cf_tail(u, sg, x, mod, dw, db, lg, lb, w_out, fg):
    b, n_lat, c = u.shape
    d = x.shape[-1]
    nt = n_lat // ROW_TILE
    per = ROW_TILE // CF_HALO
    last_halo = n_lat // CF_HALO - 1
    full = lambda a: pl.BlockSpec(a.shape, lambda bi, i: (0,) * a.ndim)
    row = lambda bi, i: (bi, i, 0)
    return pl.pallas_call(
        _cf_tail_kernel,
        out_shape=jax.ShapeDtypeStruct((b, n_lat, d), F32),
        grid=(b, nt),
        in_specs=[pl.BlockSpec((None, CF_HALO, c), lambda bi, i: (bi, jnp.maximum(i * per - 1, 0), 0)),
                  pl.BlockSpec((None, ROW_TILE, c), row),
                  pl.BlockSpec((None, CF_HALO, c), lambda bi, i: (bi, jnp.minimum((i + 1) * per, last_halo), 0)),
                  pl.BlockSpec((None, ROW_TILE, c), row),
                  pl.BlockSpec((None, ROW_TILE, d), row),
                  pl.BlockSpec((None, None, 3, d), lambda bi, i: (bi, 0, 0, 0)),
                  full(dw), full(db), full(lg), full(lb), full(w_out), full(fg)],
        out_specs=pl.BlockSpec((None, ROW_TILE, d), row),
        scratch_shapes=[pltpu.VMEM((ROW_TILE + 2 * CF_HALO, c), F32),
                        pltpu.VMEM((ROW_TILE + 2 * CF_HALO - SUBLANES, c), F32)],
        compiler_params=_cparams(2), name="cf_tail",
    )(u, u, u, sg, x, mod, dw, db, lg, lb, w_out, fg)


def _rope_tables(n_lat, n_ctx):
    rows = n_lat // GRID_W
    row = jnp.repeat(jnp.arange(rows, dtype=F32), GRID_W)
    col = jnp.tile(jnp.arange(GRID_W, dtype=F32), rows)
    n_freq = ROPE_DIM // 4
    inv = ROPE_BASE ** (-jnp.arange(n_freq, dtype=F32) / n_freq)
    ang = jnp.concatenate([row[:, None] * inv, col[:, None] * inv], axis=-1)
    cos = jnp.concatenate([jnp.cos(ang), jnp.ones((n_ctx, ROPE_DIM // 2), F32)], axis=0)
    sin = jnp.concatenate([jnp.sin(ang), jnp.zeros((n_ctx, ROPE_DIM // 2), F32)], axis=0)
    return jnp.tile(cos, (1, 2)), jnp.tile(sin, (1, 2))


def kernel(x, c, ctx, c_ctx, norm_g, ada_w, ada_b, final_g, mla_w_in, mla_q_norm_g, mla_kv_norm_g, mla_w_uq, mla_w_ukv, mla_w_out, hy_w_in, hy_conv_w, hy_conv_b, hy_filt_w_in, hy_filt_w_hid, hy_filt_b, hy_filt_freq, hy_filt_w_out, hy_bias, hy_w_out, swa_w_in, swa_sink, swa_w_out, cf_w_in, cf_dw_w, cf_dw_b, cf_ln_g, cf_ln_b, cf_w_out):
    b, n_lat, d = x.shape
    n_ctx = ctx.shape[1]
    depth = norm_g.shape[0]
    assert depth == 4 and n_lat % ROW_TILE == 0 and n_ctx == ATTN_TILE and ROW_TILE % n_ctx == 0
    assert n_lat % (FFT_N1H * SUBLANES) == 0 and n_lat % GRID_W == 0
    n_lat_tiles = n_lat // ROW_TILE
    bf = lambda a: a.astype(BF16)

    cs = jnp.concatenate([c, c_ctx[None, :], jnp.zeros((SUBLANES - b - 1, d), F32)], axis=0)
    mods = _adaln(cs, ada_w, ada_b)[:, :b + 1].reshape(depth, b + 1, 3, d)
    mods = jnp.stack([mods[:, :b], jnp.broadcast_to(mods[:, b:], (depth, b, 3, d))], axis=2)
    cos64, sin64 = _rope_tables(n_lat, ROW_TILE)
    cos128, sin128 = jnp.tile(cos64, (1, 2)), jnp.tile(sin64, (1, 2))

    c2 = MLA_Q_RANK + MLA_KV_RANK + MLA_ROPE
    w_lat = bf(jnp.pad(mla_w_in[0][:, :c2], ((0, 0), (0, LANES - MLA_ROPE))))
    w_gate = bf(mla_w_in[0][:, c2:])
    wuq = bf(jnp.pad(mla_w_uq[0].reshape(MLA_Q_RANK, MLA_HEADS, MLA_QK), ((0, 0), (0, 0), (0, LANES - MLA_ROPE)))
             .reshape(MLA_Q_RANK, MLA_HEADS * 2 * LANES))
    q, k, v, sg = _mla_proj(x, ctx, norm_g[0:1], mods[0], w_lat, w_gate, mla_q_norm_g[0:1], mla_kv_norm_g[0:1],
                            wuq, bf(mla_w_ukv[0]), cos128, sin128, n_lat_tiles)
    o = _mla_attn(q, k, v, n_lat, n_ctx)

    xs, z0, g0, g1, sg = _hy_proj(o, sg, x, ctx, mods[0], bf(mla_w_out[0]), norm_g[1:2], mods[1], bf(hy_w_in[0]),
                                  hy_conv_w[0], hy_conv_b[0][None, :], n_lat_tiles)
    cch = z0.shape[-1]
    filt = (hy_filt_w_in[0], hy_filt_w_hid[0], hy_filt_b[0], hy_filt_freq[0], hy_filt_w_out[0])
    assert cch % HY_SLAB == 0
    mats = _fft_matrices(n_lat)
    spec = _hy_spec(_hy_filter_taps(n_lat, *filt), mats[1], mats[2])
    cfwd, cinv, nfp = _dense_dft_matrices(n_ctx)
    cspec = _hy_ctx_spec(_hy_filter_taps(n_ctx, *filt), cfwd)
    gates = (g0, g1)
    z = z0
    for order in range(HY_ORDER):
        bias = hy_bias[0][order][None, :]
        z = _hy_conv(z, gates[order], spec, order, mats, bias, n_lat)
        z = _hy_ctx_conv(z, gates[order], cspec, order, cfwd[:, :n_ctx], cinv, nfp, bias, n_lat, n_ctx)

    nq = SWA_Q_HEADS * SWA_HEAD_DIM
    nkv = SWA_KV_HEADS * SWA_HEAD_DIM
    w_in = swa_w_in[0]
    dup = lambda w: jnp.tile(w.reshape(d, SWA_KV_HEADS, 1, SWA_HEAD_DIM), (1, 1, 2, 1)).reshape(d, 2 * nkv)
    w2 = bf(jnp.concatenate([w_in[:, :nq], dup(w_in[:, nq:nq + nkv]), dup(w_in[:, nq + nkv:nq + 2 * nkv]),
                             w_in[:, nq + 2 * nkv:]], axis=1))
    xs, q, k, v, sg = _swa_proj(z, sg, xs, mods[1], bf(hy_w_out[0]), norm_g[2:3], mods[2], w2, cos128, sin128,
                                n_lat_tiles)
    o = _swa_attn(swa_sink[0], q, k, v, n_lat, n_ctx)

    xs, u, sg = _cf_proj(o, sg, xs, mods[2], bf(swa_w_out[0]), norm_g[3:4], mods[3], bf(cf_w_in[0]), n_lat_tiles)
    return _cf_tail(u, sg, xs, mods[3], cf_dw_w[0], cf_dw_b[0][None, :], cf_ln_g[0][None, :], cf_ln_b[0][None, :],
                    bf(cf_w_out[0]), final_g[None, :])
```

```python
import functools
import math

import numpy as np
import jax
import jax.numpy as jnp
from jax import lax
from jax.experimental import pallas as pl
from jax.experimental.pallas import tpu as pltpu

F32 = jnp.float32
BF16 = jnp.bfloat16

GRID_W = 64
NORM_EPS = 1e-6
NEG_INF = -1e30
LOG2E = math.log2(math.e)
ROPE_BASE = 10000.0
ROPE_DIM = 64

MLA_HEADS = 8
MLA_Q_RANK = 384
MLA_KV_RANK = 256
MLA_NOPE = 128
MLA_ROPE = ROPE_DIM
MLA_V = 128
MLA_QK = MLA_NOPE + MLA_ROPE

HY_ORDER = 2
HY_EMB = 33
HY_EMB_PAD = 64
HY_BANDS = (HY_EMB - 1) // 2
HY_DECAY_TARGET = 1e-2
HY_FAST_DECAY = 0.3
HY_SLOW_DECAY = 1.5

SWA_Q_HEADS = 16
SWA_KV_HEADS = 4
SWA_GROUP = SWA_Q_HEADS // SWA_KV_HEADS
SWA_HEAD_DIM = ROPE_DIM
SWA_WINDOW = 128

CF_KERNEL = 31
CF_HALO = 16

ROW_TILE = 512
ATTN_TILE = 256
SUBLANES = 8
LANES = 128
FFT_N1 = 64
FFT_N1H = FFT_N1 // 2
FFT_K1 = FFT_N1 // 2 + 1
HY_SLAB = 256
HY_HALO = 16
VMEM_LIMIT = 48 * 1024 * 1024
VMEM_LIMIT_BIG = 56 * 1024 * 1024


def _cparams(n_axes):
    return pltpu.CompilerParams(dimension_semantics=("arbitrary",) * n_axes,
                                vmem_limit_bytes=VMEM_LIMIT)


def _cparams_big(n_axes):
    return pltpu.CompilerParams(dimension_semantics=("arbitrary",) * n_axes,
                                vmem_limit_bytes=VMEM_LIMIT_BIG)


def _once(shape, index_map):
    return pl.BlockSpec(shape, index_map, pipeline_mode=pl.Buffered(1))


def _dot(a, b):
    return jnp.dot(a, b, preferred_element_type=F32)


def _dot_nt(a, b):
    return lax.dot_general(a, b, (((1,), (1,)), ((), ())), preferred_element_type=F32)


def _rms(x, g):
    return x * lax.rsqrt(jnp.mean(x * x, axis=-1, keepdims=True) + NORM_EPS) * g


def _norm_mod(x, g, mod):
    return _rms(x, g) * (1.0 + mod[1:2, :]) + mod[0:1, :]


def _silu(x):
    return x * jax.nn.sigmoid(x)


def _adaln_kernel(c_ref, w_ref, b_ref, o_ref):
    s = _silu(c_ref[...]).astype(BF16)
    o_ref[...] = _dot(s, w_ref[...].astype(BF16)) + b_ref[...]


def _adaln(cs, ada_w, ada_b):
    depth, d, d3 = ada_w.shape
    tn = 512
    return pl.pallas_call(
        _adaln_kernel,
        out_shape=jax.ShapeDtypeStruct((depth, cs.shape[0], d3), F32),
        grid=(depth, d3 // tn),
        in_specs=[pl.BlockSpec(cs.shape, lambda i, j: (0, 0)),
                  pl.BlockSpec((None, d, tn), lambda i, j: (i, 0, j)),
                  pl.BlockSpec((None, 1, tn), lambda i, j: (i, 0, j))],
        out_specs=pl.BlockSpec((None, cs.shape[0], tn), lambda i, j: (i, 0, j)),
        compiler_params=_cparams(2), name="adaln",
    )(cs, ada_w, ada_b.reshape(depth, 1, d3))


def _input_tile(x_ref, ctx_ref, is_ctx):
    pad = jnp.zeros((x_ref.shape[0] - ctx_ref.shape[0], x_ref.shape[1]), F32)
    return jnp.where(is_ctx, jnp.concatenate([ctx_ref[...], pad], axis=0), x_ref[...])


def _input_specs(d, n_ctx, n_lat_tiles):
    return [pl.BlockSpec((None, ROW_TILE, d), lambda bi, i: (bi, jnp.minimum(i, n_lat_tiles - 1), 0)),
            pl.BlockSpec((None, n_ctx, d), lambda bi, i: (bi, 0, 0))]


def _residual_in(o_ref, sgp_ref, x_ref, modp_ref, wout_ref, xo_ref):
    og = (o_ref[...].astype(F32) * sgp_ref[...].astype(F32)).astype(BF16)
    x = x_ref[...] + modp_ref[2:3, :] * _dot(og, wout_ref[...])
    xo_ref[...] = x
    return x


def _residual_specs(o, d, n_lat_tiles):
    width = o.shape[-1]
    row = lambda bi, i: (bi, i, 0)
    return [pl.BlockSpec((None, ROW_TILE, width), row), pl.BlockSpec((None, ROW_TILE, width), row),
            pl.BlockSpec((None, ROW_TILE, d), row),
            pl.BlockSpec((None, None, 3, d), lambda bi, i: (bi, i // n_lat_tiles, 0, 0)),
            _once((width, d), lambda bi, i: (0, 0))]


def _rope_lanes(x, cos, sin):
    half = ROPE_DIM // 2
    lane = lax.broadcasted_iota(jnp.int32, x.shape, 1) % ROPE_DIM
    rot = jnp.where(lane < half, -pltpu.roll(x, LANES - half, 1), pltpu.roll(x, half, 1))
    return x * cos + rot * sin


def _mla_proj_kernel(x_ref, ctx_ref, g_ref, mod_ref, wl_ref, wg_ref, qg_ref, kvg_ref, wuq_ref, wukv_ref,
                     cos_ref, sin_ref, q_ref, k_ref, v_ref, sg_ref, *, n_lat_tiles):
    x = _input_tile(x_ref, ctx_ref, pl.program_id(1) >= n_lat_tiles)
    h = _norm_mod(x, g_ref[...], mod_ref[...]).astype(BF16)
    sg_ref[...] = _silu(_dot(h, wg_ref[...])).astype(sg_ref.dtype)
    res = _dot(h, wl_ref[...])
    c0, c1 = MLA_Q_RANK, MLA_Q_RANK + MLA_KV_RANK
    cqn = _rms(res[:, :c0], qg_ref[...]).astype(BF16)
    ckvn = _rms(res[:, c0:c1], kvg_ref[...]).astype(BF16)
    cos, sin = cos_ref[...], sin_ref[...]

    def rope(slab):
        return _rope_lanes(slab, cos, sin)[:, :MLA_ROPE]

    k_rope_t = rope(res[:, c1:c1 + LANES]).T.astype(k_ref.dtype)
    scale = MLA_QK ** -0.5 * LOG2E
    per = 2 * LANES
    for h0 in range(0, MLA_HEADS, 2):
        q2 = _dot(cqn, wuq_ref[:, h0 * per:(h0 + 2) * per])
        kv2 = _dot(ckvn, wukv_ref[:, h0 * per:(h0 + 2) * per])
        for j in range(2):
            hd = h0 + j
            q, kv = q2[:, j * per:(j + 1) * per], kv2[:, j * per:(j + 1) * per]
            q_ref[hd, :, :MLA_NOPE] = (q[:, :MLA_NOPE] * scale).astype(q_ref.dtype)
            q_ref[hd, :, MLA_NOPE:] = (rope(q[:, MLA_NOPE:]) * scale).astype(q_ref.dtype)
            k_ref[hd, :MLA_NOPE, :] = kv[:, :MLA_NOPE].T.astype(k_ref.dtype)
            k_ref[hd, MLA_NOPE:, :] = k_rope_t
            v_ref[hd, :, :MLA_V] = kv[:, MLA_NOPE:].astype(v_ref.dtype)
            v_ref[hd, :, MLA_V:] = jnp.ones((v_ref.shape[1], MLA_V), v_ref.dtype)


def _mla_proj(x, ctx, g, mod, wl, wg, qg, kvg, wuq, wukv, cos, sin, n_lat_tiles):
    b, n_lat, d = x.shape
    nt = n_lat_tiles + 1
    t = nt * ROW_TILE
    width = MLA_HEADS * MLA_V
    full = lambda a: _once(a.shape, lambda bi, i: (0,) * a.ndim)
    head_out = lambda n: pl.BlockSpec((None, MLA_HEADS, ROW_TILE, n), lambda bi, i: (bi, 0, i, 0))
    return pl.pallas_call(
        functools.partial(_mla_proj_kernel, n_lat_tiles=n_lat_tiles),
        out_shape=(jax.ShapeDtypeStruct((b, MLA_HEADS, t, MLA_QK), BF16),
                   jax.ShapeDtypeStruct((b, MLA_HEADS, MLA_QK, t), BF16),
                   jax.ShapeDtypeStruct((b, MLA_HEADS, t, 2 * MLA_V), BF16),
                   jax.ShapeDtypeStruct((b, t, width), BF16)),
        grid=(b, nt),
        in_specs=_input_specs(d, ctx.shape[1], n_lat_tiles) +
                 [full(g),
                  pl.BlockSpec((None, None, 3, d), lambda bi, i: (bi, i // n_lat_tiles, 0, 0)),
                  full(wl), full(wg), full(qg), full(kvg), full(wuq), full(wukv),
                  pl.BlockSpec((ROW_TILE, LANES), lambda bi, i: (i, 0)),
                  pl.BlockSpec((ROW_TILE, LANES), lambda bi, i: (i, 0))],
        out_specs=(head_out(MLA_QK),
                   pl.BlockSpec((None, MLA_HEADS, MLA_QK, ROW_TILE), lambda bi, i: (bi, 0, 0, i)),
                   head_out(2 * MLA_V),
                   pl.BlockSpec((None, ROW_TILE, width), lambda bi, i: (bi, i, 0))),
        compiler_params=_cparams(2), name="mla_proj",
    )(x, ctx, g, mod, wl, wg, qg, kvg, wuq, wukv, cos, sin)


def _attend(q, k_parts, v_parts):
    s = [_dot(q, k) for k in k_parts]
    m = functools.reduce(jnp.maximum, [jnp.max(x, axis=-1, keepdims=True) for x in s])
    ov = functools.reduce(jnp.add, [_dot(jnp.exp2(x - m).astype(BF16), v) for x, v in zip(s, v_parts)])
    n = ov.shape[-1] // 2
    return ov[:, :n] / ov[:, n:]


MLA_HEADS_PER_STEP = 2


def _mla_attn_kernel(q_ref, k_ref, v_ref, o_ref, *, n_lat_tiles, n_lat, n_ctx):
    qi = pl.program_id(2)

    n_keys = n_lat + n_ctx

    @pl.when(qi < n_lat_tiles)
    def _():
        chains = [(hd, r0) for hd in range(MLA_HEADS_PER_STEP) for r0 in range(0, q_ref.shape[1], ATTN_TILE)]
        s = [_dot(q_ref[hd, r0:r0 + ATTN_TILE, :], k_ref[hd, :, 0:n_keys]) for hd, r0 in chains]
        p = [jnp.exp2(x - jnp.max(x, axis=-1, keepdims=True)).astype(BF16) for x in s]
        for (hd, r0), pc in zip(chains, p):
            ov = _dot(pc, v_ref[hd, 0:n_keys, :])
            o = ov[:, :MLA_V] / ov[:, MLA_V:]
            o_ref[r0:r0 + ATTN_TILE, hd * MLA_V:(hd + 1) * MLA_V] = o.astype(o_ref.dtype)

    @pl.when(qi >= n_lat_tiles)
    def _():
        o_ref[n_ctx:, :] = jnp.zeros((o_ref.shape[0] - n_ctx, o_ref.shape[1]), o_ref.dtype)
        for hd in range(MLA_HEADS_PER_STEP):
            k = k_ref[hd, :, n_lat:n_keys]
            v = v_ref[hd, n_lat:n_keys, :]
            o = _attend(q_ref[hd, 0:n_ctx, :], [k], [v])
            o_ref[0:n_ctx, hd * MLA_V:(hd + 1) * MLA_V] = o.astype(o_ref.dtype)


def _mla_attn(q, k, v, n_lat, n_ctx):
    b, hds, t, _ = q.shape
    nt = t // ROW_TILE
    hps = MLA_HEADS_PER_STEP
    kern = functools.partial(_mla_attn_kernel, n_lat_tiles=n_lat // ROW_TILE, n_lat=n_lat, n_ctx=n_ctx)
    return pl.pallas_call(
        kern,
        out_shape=jax.ShapeDtypeStruct((b, t, hds * MLA_V), BF16),
        grid=(b, hds // hps, nt),
        in_specs=[pl.BlockSpec((None, hps, ROW_TILE, MLA_QK), lambda bi, h, i: (bi, h, i, 0)),
                  pl.BlockSpec((None, hps, MLA_QK, t), lambda bi, h, i: (bi, h, 0, 0)),
                  pl.BlockSpec((None, hps, t, 2 * MLA_V), lambda bi, h, i: (bi, h, 0, 0))],
        out_specs=pl.BlockSpec((None, ROW_TILE, hps * MLA_V), lambda bi, h, i: (bi, i, h)),
        compiler_params=_cparams_big(3), name="mla_attn",
    )(q, k, v)


def _hy_proj_kernel(op_ref, o_ref, on_ref, sp_ref, s_ref, sn_ref, xp_ref, x_ref, ctx_ref, xn_ref, modp_ref, wout_ref,
                    g_ref, mod_ref, w_ref, cw_ref, cb_ref, xo_ref, z_ref, g0_ref, g1_ref, sg_ref,
                    *, n_lat_tiles, n_ctx):
    i = pl.program_id(1)
    tm = x_ref.shape[0]
    c = z_ref.shape[-1]
    n = tm + 2 * SUBLANES
    halo = xp_ref.shape[0]
    cat = lambda a, m, z: jnp.concatenate([a[...], m, z[...]], axis=0)
    og = (cat(op_ref, o_ref[...], on_ref).astype(F32) * cat(sp_ref, s_ref[...], sn_ref).astype(F32)).astype(BF16)
    xe = cat(xp_ref, _input_tile(x_ref, ctx_ref, i >= n_lat_tiles), xn_ref)
    xe = xe + modp_ref[2:3, :] * _dot(og, wout_ref[...])
    xo_ref[...] = xe[halo:halo + tm]
    xa = xe[halo - SUBLANES:halo + tm + SUBLANES]
    hf = _norm_mod(xa, g_ref[...], mod_ref[...])
    h = hf.astype(BF16)
    ctx_tile = i == n_lat_tiles
    first = jnp.logical_or(i == 0, ctx_tile)
    last = jnp.logical_or(i == n_lat_tiles - 1, ctx_tile)
    rows = lax.broadcasted_iota(jnp.int32, (n, 1), 0)
    end = jnp.where(ctx_tile, n_ctx + SUBLANES, jnp.where(last, tm + SUBLANES, n))
    valid = jnp.logical_and(jnp.logical_or(rows >= SUBLANES, jnp.logical_not(first)), rows < end)
    outs = (z_ref, g0_ref, g1_ref)
    for j in range(len(outs)):
        u = jnp.where(valid, _dot(h, w_ref[:, j * c:(j + 1) * c]), 0.0)
        cw = cw_ref[:, j * c:(j + 1) * c]
        cv = cw[0:1, :] * pltpu.roll(u, 1, 0) + cw[1:2, :] * u + cw[2:3, :] * pltpu.roll(u, n - 1, 0)
        outs[j][...] = cv[SUBLANES:SUBLANES + tm] + cb_ref[:, j * c:(j + 1) * c]
    h_mid = hf[SUBLANES:SUBLANES + tm].astype(BF16)
    sg_ref[...] = _silu(_dot(h_mid, w_ref[:, len(outs) * c:])).astype(sg_ref.dtype)


def _hy_proj(o, sgp, x, ctx, modp, w_out, g, mod, w, cw, cb, n_lat_tiles):
    b, t, width = o.shape
    n_lat, d = x.shape[1:]
    n_ctx = ctx.shape[1]
    c = w.shape[1] // (HY_ORDER + 2)
    per = ROW_TILE // HY_HALO
    last = n_lat // HY_HALO - 1
    full = lambda a: _once(a.shape, lambda bi, i: (0,) * a.ndim)
    row = lambda bi, i: (bi, i, 0)
    prev = lambda bi, i: (bi, jnp.maximum(i * per - 1, 0), 0)
    nxt = lambda bi, i: (bi, jnp.minimum((i + 1) * per, last), 0)
    x_main, x_ctx = _input_specs(d, n_ctx, n_lat_tiles)
    kern = functools.partial(_hy_proj_kernel, n_lat_tiles=n_lat_tiles, n_ctx=n_ctx)
    halo = lambda n, m: pl.BlockSpec((None, HY_HALO, n), m)
    return pl.pallas_call(
        kern,
        out_shape=(jax.ShapeDtypeStruct((b, t, d), F32),) + (jax.ShapeDtypeStruct((b, t, c), F32),) * 3 +
                  (jax.ShapeDtypeStruct((b, t, c), BF16),),
        grid=(b, t // ROW_TILE),
        in_specs=[halo(width, prev), pl.BlockSpec((None, ROW_TILE, width), row), halo(width, nxt),
                  halo(width, prev), pl.BlockSpec((None, ROW_TILE, width), row), halo(width, nxt),
                  halo(d, prev), x_main, x_ctx, halo(d, nxt),
                  pl.BlockSpec((None, None, 3, d), lambda bi, i: (bi, i // n_lat_tiles, 0, 0)),
                  full(w_out), full(g),
                  pl.BlockSpec((None, None, 3, d), lambda bi, i: (bi, i // n_lat_tiles, 0, 0)),
                  full(w), full(cw), full(cb)],
        out_specs=(pl.BlockSpec((None, ROW_TILE, d), row),) + (pl.BlockSpec((None, ROW_TILE, c), row),) * 4,
        compiler_params=_cparams(2), name="hy_proj",
    )(o, o, o, sgp, sgp, sgp, x, x, ctx, x, modp, w_out, g, mod, w, cw, cb)


def _hy_filter_kernel(ef_ref, eb_ref, w0_ref, wh_ref, b_ref, fr_ref, wf_ref, wb_ref, dl_ref, o_ref):
    i = pl.program_id(0)
    c = dl_ref.shape[-1]
    half = ef_ref.shape[0]

    def taps(e, wo):
        hdn = e
        ws = (w0_ref[...], wh_ref[0], wh_ref[1])
        for k in range(3):
            hdn = jnp.sin(fr_ref[k:k + 1, :] * (_dot(hdn.astype(BF16), ws[k].astype(BF16)) + b_ref[k:k + 1, :]))
        hh = _dot(hdn.astype(BF16), wo.astype(BF16))
        out = []
        for part in range(2):
            decay = jnp.exp(-e[:, part * HY_EMB_PAD:part * HY_EMB_PAD + 1] * dl_ref[...])
            out.append([hh[:, (part * HY_ORDER + o) * c:(part * HY_ORDER + o + 1) * c] * decay
                        for o in range(HY_ORDER)])
        return out

    hf = taps(ef_ref[...], wf_ref[...])
    hb = taps(eb_ref[...], wb_ref[...])
    rows = lax.broadcasted_iota(jnp.int32, (half, 1), 0)
    lag0 = jnp.logical_and(rows == 0, i == 0)
    for o in range(HY_ORDER):
        o_ref[o, 0, 0:half] = hf[0][o] + jnp.where(lag0, hb[0][o], 0.0)
        o_ref[o, 1, 0:half] = jnp.where(lag0, 0.0, hb[0][o])
        o_ref[o, 0, half:] = hf[1][o]
        o_ref[o, 1, half:] = hb[1][o]


def _hy_position_features(n):
    t = jnp.linspace(0.0, 1.0, n, dtype=F32)[:, None]
    wpos = (2.0 * math.pi / n) * jnp.arange(n, dtype=F32)[:, None]
    bands = jnp.linspace(1e-4, HY_BANDS - 1, HY_BANDS, dtype=F32)[None, :]
    hdn = jnp.concatenate([t, jnp.cos(bands * wpos), -jnp.sin(bands * wpos)], axis=-1)
    return jnp.pad(hdn, ((0, 0), (0, HY_EMB_PAD - HY_EMB)))


def _hy_filter_taps(n, w_in, w_hid, bias, freq, w_out):
    c = w_out.shape[1] // (2 * HY_ORDER)
    ef = _hy_position_features(n)
    rev = (n - jnp.arange(n)) % n
    eb = ef[rev]
    w0 = jnp.pad(w_in, ((0, HY_EMB_PAD - HY_EMB), (0, 0)))
    wo = w_out.reshape(w_out.shape[0], HY_ORDER, 2, c)
    wf = wo[:, :, 0, :].reshape(w_out.shape[0], HY_ORDER * c)
    wb = wo[:, :, 1, :].reshape(w_out.shape[0], HY_ORDER * c)
    deltas = jnp.abs(jnp.linspace(math.log(HY_DECAY_TARGET) / HY_FAST_DECAY,
                                  math.log(HY_DECAY_TARGET) / HY_SLOW_DECAY, c, dtype=F32))[None, :]
    tl = min(n, 512)
    half = tl // 2
    pair = lambda e: e.reshape(n // tl, 2, half, HY_EMB_PAD).transpose(0, 2, 1, 3).reshape(n // 2, 2 * HY_EMB_PAD)
    both = lambda w: jnp.kron(jnp.eye(2, dtype=F32), w)
    twice = lambda v: jnp.tile(v, (1, 2))
    w_hid2 = jnp.stack([both(w_hid[0]), both(w_hid[1])])
    full = lambda a: pl.BlockSpec(a.shape, lambda i: (0,) * a.ndim)
    args = (pair(ef), pair(eb), both(w0), w_hid2, twice(bias), twice(freq), both(wf), both(wb), deltas)
    return pl.pallas_call(
        _hy_filter_kernel,
        out_shape=jax.ShapeDtypeStruct((HY_ORDER, 2, n, c), F32),
        grid=(n // tl,),
        in_specs=[pl.BlockSpec((half, 2 * HY_EMB_PAD), lambda i: (i, 0)),
                  pl.BlockSpec((half, 2 * HY_EMB_PAD), lambda i: (i, 0))] + [full(a) for a in args[2:]],
        out_specs=pl.BlockSpec((HY_ORDER, 2, tl, c), lambda i: (0, 0, i, 0)),
        compiler_params=_cparams(1), name="hy_filter",
    )(*args)


def _fft_matrices(n_lat):
    n = 2 * n_lat
    n2 = n_lat // FFT_N1H
    k1 = np.arange(FFT_K1)
    n1 = np.arange(FFT_N1H)
    ang = 2.0 * np.pi * np.outer(k1, n1) / FFT_N1
    eye = np.eye(SUBLANES)
    f1 = np.kron(np.concatenate([np.cos(ang), -np.sin(ang)], axis=0), eye)
    wgt = np.full(FFT_K1, 2.0)
    wgt[0] = wgt[-1] = 1.0
    g1 = np.kron(np.concatenate([wgt * np.cos(ang.T), (-wgt * np.sin(ang.T))[:, 1:-1]], axis=1) / n, eye)
    idx = np.arange(n2)
    f2 = np.zeros((FFT_K1, 2 * n2, 2 * n2))
    f2i = np.zeros((FFT_K1, 2 * n2, 2 * n2))
    for a in range(FFT_K1):
        ph = 2.0 * np.pi * (np.outer(idx, idx) / n2 + (a * idx)[None, :] / n)
        f2[a] = np.block([[np.cos(ph), np.sin(ph)], [-np.sin(ph), np.cos(ph)]])
        f2i[a] = np.block([[np.cos(ph.T), -np.sin(ph.T)], [np.sin(ph.T), np.cos(ph.T)]])
    sign = np.kron(np.tile((-1.0) ** k1, 2), np.ones(SUBLANES))[:, None]
    f1s = np.concatenate([f1, sign * f1], axis=1)
    as_bf16 = lambda m: jnp.asarray(m.astype(np.float32)).astype(BF16)
    return as_bf16(f1), as_bf16(f1s), as_bf16(f2), as_bf16(f2i), as_bf16(g1)


def _dense_dft_matrices(n_ctx):
    n = 2 * n_ctx
    nf = n_ctx + 1
    nfp = -(-nf // SUBLANES) * SUBLANES
    k = np.arange(nf)
    ang = 2.0 * np.pi * np.outer(k, np.arange(n)) / n
    fwd = np.zeros((2 * nfp, n))
    fwd[:nf] = np.cos(ang)
    fwd[nfp:nfp + nf] = -np.sin(ang)
    wgt = np.full(nf, 2.0)
    wgt[0] = wgt[-1] = 1.0
    angi = 2.0 * np.pi * np.outer(np.arange(n_ctx), k) / n
    inv = np.zeros((n_ctx, 2 * nfp))
    inv[:, :nf] = wgt * np.cos(angi) / n
    inv[:, nfp:nfp + nf] = -wgt * np.sin(angi) / n
    as_bf16 = lambda m: jnp.asarray(m.astype(np.float32)).astype(BF16)
    return as_bf16(fwd), as_bf16(inv), nfp


def _hy_spec_kernel(t_ref, f1s_ref, f2_ref, h_ref, a_ref):
    _, n1h, nj, s, tc = t_ref.shape
    n2 = nj * s
    for j in range(nj):
        sig = jnp.concatenate([t_ref[0, :, j].reshape(n1h * s, tc), t_ref[1, :, j].reshape(n1h * s, tc)], axis=0)
        a_ref[:, :, j] = _dot(f1s_ref[...], sig.astype(BF16)).reshape(2, FFT_K1, s, tc)
    for k in range(FFT_K1):
        a = a_ref[:, k].reshape(2 * n2, tc).astype(BF16)
        h_ref[:, k] = _dot(f2_ref[k], a).reshape(2, n2, tc).astype(h_ref.dtype)


def _hy_spec(taps, f1s, f2):
    _, _, n_lat, c = taps.shape
    n2 = n_lat // FFT_N1H
    nj = n2 // SUBLANES
    tc = HY_SLAB
    return pl.pallas_call(
        _hy_spec_kernel,
        out_shape=jax.ShapeDtypeStruct((HY_ORDER, 2, FFT_K1, n2, c), BF16),
        grid=(c // tc, HY_ORDER),
        in_specs=[pl.BlockSpec((None, 2, FFT_N1H, nj, SUBLANES, tc), lambda ci, o: (o, 0, 0, 0, 0, ci)),
                  _once(f1s.shape, lambda ci, o: (0, 0)),
                  _once(f2.shape, lambda ci, o: (0, 0, 0))],
        out_specs=pl.BlockSpec((None, 2, FFT_K1, n2, tc), lambda ci, o: (o, 0, 0, 0, ci)),
        scratch_shapes=[pltpu.VMEM((2, FFT_K1, nj, SUBLANES, tc), F32)],
        compiler_params=_cparams_big(2), name="hy_spec",
    )(taps.reshape(HY_ORDER, 2, FFT_N1H, nj, SUBLANES, c), f1s, f2)


def _hy_conv_kernel(z_ref, gt_ref, h_ref, f1_ref, f2_ref, f2i_ref, g1_ref, bias_ref, o_ref, a_ref):
    n1h, nj, s, tc = z_ref.shape
    n2 = nj * s
    for j in range(nj):
        zz = z_ref[:, j].reshape(n1h * s, tc).astype(BF16)
        a_ref[:, :, j] = _dot(f1_ref[...], zz).reshape(2, FFT_K1, s, tc)
    for k in range(FFT_K1):
        x = _dot(f2_ref[k], a_ref[:, k].reshape(2 * n2, tc).astype(BF16))
        xr, xi = x[:n2], x[n2:]
        hr, hi = h_ref[0, k].astype(F32), h_ref[1, k].astype(F32)
        y = jnp.concatenate([xr * hr - xi * hi, xr * hi + xi * hr], axis=0).astype(BF16)
        a_ref[:, k] = _dot(f2i_ref[k], y).reshape(2, nj, s, tc)
    for j in range(nj):
        bp = a_ref[:, :, j].reshape(2 * FFT_K1 * s, tc)
        bp = jnp.concatenate([bp[:FFT_K1 * s], bp[(FFT_K1 + 1) * s:(2 * FFT_K1 - 1) * s]], axis=0)
        y = _dot(g1_ref[...], bp.astype(BF16)).reshape(n1h, s, tc)
        o_ref[:, j] = gt_ref[:, j] * (y + z_ref[:, j] * bias_ref[...])


def _hy_conv(z, gate, h, order, mats, bias, n_lat):
    f1, _, f2, f2i, g1 = mats
    bz, t, c = z.shape
    n2 = n_lat // FFT_N1H
    nj = n2 // SUBLANES
    tc = HY_SLAB
    view = (bz, t // n2, nj, SUBLANES, c)
    slab = pl.BlockSpec((None, FFT_N1H, nj, SUBLANES, tc), lambda ci, bi: (bi, 0, 0, 0, ci))
    out = pl.pallas_call(
        _hy_conv_kernel,
        out_shape=jax.ShapeDtypeStruct(view, F32),
        grid=(c // tc, bz),
        in_specs=[slab, slab,
                  _once((None, 2, FFT_K1, n2, tc), lambda ci, bi: (order, 0, 0, 0, ci)),
                  _once(f1.shape, lambda ci, bi: (0, 0)),
                  _once(f2.shape, lambda ci, bi: (0, 0, 0)),
                  _once(f2i.shape, lambda ci, bi: (0, 0, 0)),
                  _once(g1.shape, lambda ci, bi: (0, 0)),
                  pl.BlockSpec((1, tc), lambda ci, bi: (0, ci))],
        out_specs=slab,
        input_output_aliases={0: 0},
        scratch_shapes=[pltpu.VMEM((2, FFT_K1, nj, SUBLANES, tc), F32)],
        compiler_params=_cparams_big(2), name="hy_conv",
    )(z.reshape(view), gate.reshape(view), h, f1, f2, f2i, g1, bias)
    return out.reshape(bz, t, c)


def _hy_ctx_spec_kernel(t_ref, f_ref, h_ref):
    sig = jnp.concatenate([t_ref[0], t_ref[1]], axis=0).astype(BF16)
    h_ref[...] = _dot(f_ref[...], sig)


def _hy_ctx_spec(taps, fwd):
    _, _, n_ctx, c = taps.shape
    return pl.pallas_call(
        _hy_ctx_spec_kernel,
        out_shape=jax.ShapeDtypeStruct((HY_ORDER, fwd.shape[0], c), F32),
        grid=(HY_ORDER,),
        in_specs=[pl.BlockSpec((None, 2, n_ctx, c), lambda o: (o, 0, 0, 0)),
                  pl.BlockSpec(fwd.shape, lambda o: (0, 0))],
        out_specs=pl.BlockSpec((None, fwd.shape[0], c), lambda o: (o, 0, 0)),
        compiler_params=_cparams(1), name="hy_ctx_spec",
    )(taps, fwd)


def _hy_ctx_conv_kernel(zin_ref, gt_ref, h_ref, f_ref, fi_ref, bias_ref, o_ref, *, nfp):
    z = zin_ref[...]
    x = _dot(f_ref[...], z.astype(BF16))
    xr, xi = x[:nfp], x[nfp:]
    hr, hi = h_ref[:nfp, :], h_ref[nfp:, :]
    y = jnp.concatenate([xr * hr - xi * hi, xr * hi + xi * hr], axis=0).astype(BF16)
    n_ctx = z.shape[0]
    o_ref[0:n_ctx, :] = gt_ref[...] * (_dot(fi_ref[...], y) + z * bias_ref[...])
    o_ref[n_ctx:, :] = jnp.zeros((o_ref.shape[0] - n_ctx, o_ref.shape[1]), o_ref.dtype)


def _hy_ctx_conv(z, gate, h, order, fwd_half, inv, nfp, bias, n_lat, n_ctx):
    bz, t, c = z.shape
    blk = pl.BlockSpec((None, n_ctx, c), lambda bi: (bi, n_lat // n_ctx, 0))
    kern = functools.partial(_hy_ctx_conv_kernel, nfp=nfp)
    return pl.pallas_call(
        kern,
        out_shape=jax.ShapeDtypeStruct(z.shape, F32),
        grid=(bz,),
        in_specs=[blk, blk,
                  pl.BlockSpec((None, 2 * nfp, c), lambda bi: (order, 0, 0)),
                  pl.BlockSpec(fwd_half.shape, lambda bi: (0, 0)),
                  pl.BlockSpec(inv.shape, lambda bi: (0, 0)),
                  pl.BlockSpec((1, c), lambda bi: (0, 0))],
        out_specs=pl.BlockSpec((None, t - n_lat, c), lambda bi: (bi, n_lat // (t - n_lat), 0)),
        input_output_aliases={0: 0},
        compiler_params=_cparams(1), name="hy_ctx_conv",
    )(z, gate, h, fwd_half, inv, bias)


def _swa_proj_kernel(o_ref, sgp_ref, x_ref, modp_ref, wout_ref, g_ref, mod_ref, w_ref, cos_ref, sin_ref,
                     xo_ref, q_ref, k_ref, v_ref, sg_ref):
    x = _residual_in(o_ref, sgp_ref, x_ref, modp_ref, wout_ref, xo_ref)
    h = _norm_mod(x, g_ref[...], mod_ref[...]).astype(BF16)
    cos, sin = cos_ref[...], sin_ref[...]
    nq = q_ref.shape[-1]
    nk = SWA_KV_HEADS * LANES
    scale = SWA_HEAD_DIM ** -0.5 * LOG2E
    for c0 in range(0, nq, nk):
        res = _dot(h, w_ref[:, c0:c0 + nk])
        for cb in range(nk // LANES):
            sl = slice(cb * LANES, (cb + 1) * LANES)
            q_ref[:, c0 + cb * LANES:c0 + (cb + 1) * LANES] = (_rope_lanes(res[:, sl], cos, sin) * scale).astype(q_ref.dtype)
    res = _dot(h, w_ref[:, nq:nq + nk])
    for hd in range(SWA_KV_HEADS):
        k_ref[hd] = _rope_lanes(res[:, hd * LANES:(hd + 1) * LANES], cos, sin).astype(k_ref.dtype)
    res = _dot(h, w_ref[:, nq + nk:nq + 2 * nk])
    for hd in range(SWA_KV_HEADS):
        v_ref[hd, :, :LANES] = res[:, hd * LANES:(hd + 1) * LANES].astype(v_ref.dtype)
        v_ref[hd, :, LANES:] = jnp.ones((v_ref.shape[1], LANES), v_ref.dtype)
    for c0 in range(0, nq, nk):
        sg_ref[:, c0:c0 + nk] = _silu(_dot(h, w_ref[:, nq + 2 * nk + c0:nq + 2 * nk + c0 + nk])).astype(sg_ref.dtype)


def _swa_proj(o, sgp, x, modp, w_out, g, mod, w, cos, sin, n_lat_tiles):
    b, t, d = x.shape
    nt = t // ROW_TILE
    nq = SWA_Q_HEADS * SWA_HEAD_DIM
    full = lambda a: _once(a.shape, lambda bi, i: (0,) * a.ndim)
    row = lambda bi, i: (bi, i, 0)
    kv = lambda n: pl.BlockSpec((None, SWA_KV_HEADS, ROW_TILE, n), lambda bi, i: (bi, 0, i, 0))
    return pl.pallas_call(
        _swa_proj_kernel,
        out_shape=(jax.ShapeDtypeStruct((b, t, d), F32),
                   jax.ShapeDtypeStruct((b, t, nq), BF16),
                   jax.ShapeDtypeStruct((b, SWA_KV_HEADS, t, LANES), BF16),
                   jax.ShapeDtypeStruct((b, SWA_KV_HEADS, t, 2 * LANES), BF16),
                   jax.ShapeDtypeStruct((b, t, nq), BF16)),
        grid=(b, nt),
        in_specs=_residual_specs(o, d, n_lat_tiles) +
                 [full(g),
                  pl.BlockSpec((None, None, 3, d), lambda bi, i: (bi, i // n_lat_tiles, 0, 0)),
                  full(w),
                  pl.BlockSpec((ROW_TILE, LANES), lambda bi, i: (i, 0)),
                  pl.BlockSpec((ROW_TILE, LANES), lambda bi, i: (i, 0))],
        out_specs=(pl.BlockSpec((None, ROW_TILE, d), row), pl.BlockSpec((None, ROW_TILE, nq), row),
                   kv(LANES), kv(2 * LANES), pl.BlockSpec((None, ROW_TILE, nq), row)),
        compiler_params=_cparams(2), name="swa_proj",
    )(o, sgp, x, modp, w_out, g, mod, w, cos, sin)


def _swa_attn_kernel(sink_ref, q_ref, k_ref, v_ref, o_ref, *, n_lat_tiles, n_lat, n_ctx):
    g = pl.program_id(1)
    qi = pl.program_id(2)
    tq = ATTN_TILE
    span = tq + 2 * SWA_WINDOW
    lane_half = lax.broadcasted_iota(jnp.int32, (tq, LANES), 1) // SWA_HEAD_DIM
    first = lax.broadcasted_iota(jnp.int32, (2 * tq, 1), 0) < tq
    kc = k_ref[pl.ds(n_lat, n_ctx), :]
    vc = v_ref[pl.ds(n_lat, n_ctx), :]

    def pair_query(r0, col):
        qcol = q_ref[r0:r0 + tq, col * LANES:(col + 1) * LANES]
        zero = jnp.zeros_like(qcol)
        return jnp.concatenate([jnp.where(lane_half == 0, qcol, zero), jnp.where(lane_half == 1, qcol, zero)], axis=0)

    def pair_sink(col):
        base = g * SWA_GROUP + 2 * col
        return jnp.where(first, sink_ref[base], sink_ref[base + 1]) * LOG2E

    def finish(r0, col, ov, m, sink):
        o = ov[:, :LANES] / (ov[:, LANES:] + jnp.exp2(sink - m))
        o_ref[r0:r0 + tq, col * LANES:(col + 1) * LANES] = jnp.where(lane_half == 0, o[:tq], o[tq:]).astype(o_ref.dtype)

    @pl.when(qi < n_lat_tiles)
    def _():
        chains = []
        for r0 in range(0, q_ref.shape[0], tq):
            q0 = qi * q_ref.shape[0] + r0
            start = pl.multiple_of(jnp.clip(q0 - SWA_WINDOW, 0, n_lat - span), SWA_WINDOW)
            kw = k_ref[pl.ds(start, span), :]
            vw = v_ref[pl.ds(start, span), :]
            qpos = q0 + lax.broadcasted_iota(jnp.int32, (tq, 1), 0)
            kpos = start + lax.broadcasted_iota(jnp.int32, (1, span), 1)
            band = jnp.abs(kpos - qpos) <= SWA_WINDOW
            for col in range(SWA_GROUP // 2):
                q2, sink = pair_query(r0, col), pair_sink(col)
                sw = _dot_nt(q2, kw)
                sw = jnp.concatenate([jnp.where(band, sw[:tq], NEG_INF), jnp.where(band, sw[tq:], NEG_INF)], axis=0)
                sc = _dot_nt(q2, kc)
                m = jnp.maximum(jnp.maximum(jnp.max(sw, axis=-1, keepdims=True),
                                            jnp.max(sc, axis=-1, keepdims=True)), sink)
                chains.append((r0, col, jnp.exp2(sw - m).astype(BF16), jnp.exp2(sc - m).astype(BF16), vw, m, sink))
        for r0, col, pw, pc, vw, m, sink in chains:
            finish(r0, col, _dot(pw, vw) + _dot(pc, vc), m, sink)

    @pl.when(qi >= n_lat_tiles)
    def _():
        o_ref[n_ctx:, :] = jnp.zeros((o_ref.shape[0] - n_ctx, o_ref.shape[1]), o_ref.dtype)
        for col in range(SWA_GROUP // 2):
            q2, sink = pair_query(0, col), pair_sink(col)
            sc = _dot_nt(q2, kc)
            m = jnp.maximum(jnp.max(sc, axis=-1, keepdims=True), sink)
            finish(0, col, _dot(jnp.exp2(sc - m).astype(BF16), vc), m, sink)


def _swa_attn(sink, q, k, v, n_lat, n_ctx):
    b, t, nq = q.shape
    nt = t // ROW_TILE
    gw = SWA_GROUP * SWA_HEAD_DIM
    kern = functools.partial(_swa_attn_kernel, n_lat_tiles=n_lat // ROW_TILE, n_lat=n_lat, n_ctx=n_ctx)
    kv = lambda n: pl.BlockSpec((None, None, t, n), lambda bi, gi, i: (bi, gi, 0, 0))
    return pl.pallas_call(
        kern,
        out_shape=jax.ShapeDtypeStruct((b, t, nq), BF16),
        grid=(b, SWA_KV_HEADS, nt),
        in_specs=[pl.BlockSpec(memory_space=pltpu.SMEM),
                  pl.BlockSpec((None, ROW_TILE, gw), lambda bi, gi, i: (bi, i, gi)), kv(LANES), kv(2 * LANES)],
        out_specs=pl.BlockSpec((None, ROW_TILE, gw), lambda bi, gi, i: (bi, i, gi)),
        compiler_params=_cparams(3), name="swa_attn",
    )(sink, q, k, v)


def _cf_proj_kernel(o_ref, sgp_ref, x_ref, modp_ref, wout_ref, g_ref, mod_ref, w_ref, xo_ref, u_ref, sg_ref,
                    *, n_lat_tiles):
    x = _residual_in(o_ref, sgp_ref, x_ref, modp_ref, wout_ref, xo_ref)

    @pl.when(pl.program_id(1) < n_lat_tiles)
    def _():
        h = _norm_mod(x, g_ref[...], mod_ref[...]).astype(BF16)
        c = u_ref.shape[-1]
        tn = c // 2
        for c0 in range(0, c, tn):
            a = _dot(h, w_ref[:, c0:c0 + tn])
            gl = _dot(h, w_ref[:, c + c0:c + c0 + tn])
            u_ref[:, c0:c0 + tn] = a * jax.nn.sigmoid(gl)
            sg_ref[:, c0:c0 + tn] = _silu(_dot(h, w_ref[:, 2 * c + c0:2 * c + c0 + tn])).astype(sg_ref.dtype)


def _cf_proj(o, sgp, x, modp, w_out, g, mod, w, n_lat_tiles):
    b, t, d = x.shape
    c = w.shape[1] // 3
    n_lat = n_lat_tiles * ROW_TILE
    full = lambda a: _once(a.shape, lambda bi, i: (0,) * a.ndim)
    row = lambda bi, i: (bi, i, 0)
    lat_row = lambda bi, i: (bi, jnp.minimum(i, n_lat_tiles - 1), 0)
    return pl.pallas_call(
        functools.partial(_cf_proj_kernel, n_lat_tiles=n_lat_tiles),
        out_shape=(jax.ShapeDtypeStruct((b, t, d), F32),
                   jax.ShapeDtypeStruct((b, n_lat, c), F32), jax.ShapeDtypeStruct((b, n_lat, c), BF16)),
        grid=(b, t // ROW_TILE),
        in_specs=_residual_specs(o, d, n_lat_tiles) +
                 [full(g), pl.BlockSpec((None, None, 3, d), lambda bi, i: (bi, 0, 0, 0)), full(w)],
        out_specs=(pl.BlockSpec((None, ROW_TILE, d), row),
                   pl.BlockSpec((None, ROW_TILE, c), lat_row), pl.BlockSpec((None, ROW_TILE, c), lat_row)),
        compiler_params=_cparams(2), name="cf_proj",
    )(o, sgp, x, modp, w_out, g, mod, w)


def _cf_tail_kernel(up_ref, u_ref, un_ref, sg_ref, x_ref, mod_ref, dw_ref, db_ref, lg_ref, lb_ref,
                    w_ref, fg_ref, o_ref, ext_ref, win_ref):
    i = pl.program_id(1)
    nt = pl.num_programs(1)
    tm = u_ref.shape[0]
    zero = jnp.zeros(up_ref.shape, F32)
    ext_ref[0:CF_HALO, :] = jnp.where(i > 0, up_ref[...], zero)
    ext_ref[CF_HALO:CF_HALO + tm, :] = u_ref[...]
    ext_ref[CF_HALO + tm:, :] = jnp.where(i < nt - 1, un_ref[...], zero)
    pad = (CF_KERNEL - 1) // 2
    acc = jnp.zeros(u_ref.shape, F32) + db_ref[...]
    top = CF_HALO - pad + CF_KERNEL - 1
    span = tm + (top // SUBLANES) * SUBLANES
    for r in range(SUBLANES):
        taps = [k for k in range(CF_KERNEL) if (CF_HALO - pad + k) % SUBLANES == r]
        if not taps:
            continue
        if r:
            win_ref[...] = ext_ref[pl.ds(r, span), :]
        src = win_ref if r else ext_ref
        for k in taps:
            acc = acc + dw_ref[k:k + 1, :] * src[pl.ds(CF_HALO - pad + k - r, tm), :]
    mu = jnp.mean(acc, axis=-1, keepdims=True)
    xc = acc - mu
    var = jnp.mean(xc * xc, axis=-1, keepdims=True)
    ln = xc * lax.rsqrt(var + NORM_EPS) * lg_ref[...] + lb_ref[...]
    og = (_silu(ln) * sg_ref[...].astype(F32)).astype(BF16)
    xo = x_ref[...] + mod_ref[2:3, :] * _dot(og, w_ref[...])
    o_ref[...] = _rms(xo, fg_ref[...])


def _cf_tail(u, sg, x, mod, dw, db, lg, lb, w_out, fg):
    b, n_lat, c = u.shape
    d = x.shape[-1]
    nt = n_lat // ROW_TILE
    per = ROW_TILE // CF_HALO
    last_halo = n_lat // CF_HALO - 1
    full = lambda a: pl.BlockSpec(a.shape, lambda bi, i: (0,) * a.ndim)
    row = lambda bi, i: (bi, i, 0)
    return pl.pallas_call(
        _cf_tail_kernel,
        out_shape=jax.ShapeDtypeStruct((b, n_lat, d), F32),
        grid=(b, nt),
        in_specs=[pl.BlockSpec((None, CF_HALO, c), lambda bi, i: (bi, jnp.maximum(i * per - 1, 0), 0)),
                  pl.BlockSpec((None, ROW_TILE, c), row),
                  pl.BlockSpec((None, CF_HALO, c), lambda bi, i: (bi, jnp.minimum((i + 1) * per, last_halo), 0)),
                  pl.BlockSpec((None, ROW_TILE, c), row),
                  pl.BlockSpec((None, ROW_TILE, d), row),
                  pl.BlockSpec((None, None, 3, d), lambda bi, i: (bi, 0, 0, 0)),
                  full(dw), full(db), full(lg), full(lb), full(w_out), full(fg)],
        out_specs=pl.BlockSpec((None, ROW_TILE, d), row),
        scratch_shapes=[pltpu.VMEM((ROW_TILE + 2 * CF_HALO, c), F32),
                        pltpu.VMEM((ROW_TILE + 2 * CF_HALO - SUBLANES, c), F32)],
        compiler_params=_cparams(2), name="cf_tail",
    )(u, u, u, sg, x, mod, dw, db, lg, lb, w_out, fg)


def _rope_tables(n_lat, n_ctx):
    rows = n_lat // GRID_W
    row = jnp.repeat(jnp.arange(rows, dtype=F32), GRID_W)
    col = jnp.tile(jnp.arange(GRID_W, dtype=F32), rows)
    n_freq = ROPE_DIM // 4
    inv = ROPE_BASE ** (-jnp.arange(n_freq, dtype=F32) / n_freq)
    ang = jnp.concatenate([row[:, None] * inv, col[:, None] * inv], axis=-1)
    cos = jnp.concatenate([jnp.cos(ang), jnp.ones((n_ctx, ROPE_DIM // 2), F32)], axis=0)
    sin = jnp.concatenate([jnp.sin(ang), jnp.zeros((n_ctx, ROPE_DIM // 2), F32)], axis=0)
    return jnp.tile(cos, (1, 2)), jnp.tile(sin, (1, 2))


def kernel(x, c, ctx, c_ctx, norm_g, ada_w, ada_b, final_g, mla_w_in, mla_q_norm_g, mla_kv_norm_g, mla_w_uq, mla_w_ukv, mla_w_out, hy_w_in, hy_conv_w, hy_conv_b, hy_filt_w_in, hy_filt_w_hid, hy_filt_b, hy_filt_freq, hy_filt_w_out, hy_bias, hy_w_out, swa_w_in, swa_sink, swa_w_out, cf_w_in, cf_dw_w, cf_dw_b, cf_ln_g, cf_ln_b, cf_w_out):
    b, n_lat, d = x.shape
    n_ctx = ctx.shape[1]
    depth = norm_g.shape[0]
    assert depth == 4 and n_lat % ROW_TILE == 0 and n_ctx == ATTN_TILE and ROW_TILE % n_ctx == 0
    assert n_lat % (FFT_N1H * SUBLANES) == 0 and n_lat % GRID_W == 0
    n_lat_tiles = n_lat // ROW_TILE
    bf = lambda a: a.astype(BF16)

    cs = jnp.concatenate([c, c_ctx[None, :], jnp.zeros((SUBLANES - b - 1, d), F32)], axis=0)
    mods = _adaln(cs, ada_w, ada_b)[:, :b + 1].reshape(depth, b + 1, 3, d)
    mods = jnp.stack([mods[:, :b], jnp.broadcast_to(mods[:, b:], (depth, b, 3, d))], axis=2)
    cos64, sin64 = _rope_tables(n_lat, ROW_TILE)
    cos128, sin128 = jnp.tile(cos64, (1, 2)), jnp.tile(sin64, (1, 2))

    c2 = MLA_Q_RANK + MLA_KV_RANK + MLA_ROPE
    w_lat = bf(jnp.pad(mla_w_in[0][:, :c2], ((0, 0), (0, LANES - MLA_ROPE))))
    w_gate = bf(mla_w_in[0][:, c2:])
    wuq = bf(jnp.pad(mla_w_uq[0].reshape(MLA_Q_RANK, MLA_HEADS, MLA_QK), ((0, 0), (0, 0), (0, LANES - MLA_ROPE)))
             .reshape(MLA_Q_RANK, MLA_HEADS * 2 * LANES))
    q, k, v, sg = _mla_proj(x, ctx, norm_g[0:1], mods[0], w_lat, w_gate, mla_q_norm_g[0:1], mla_kv_norm_g[0:1],
                            wuq, bf(mla_w_ukv[0]), cos128, sin128, n_lat_tiles)
    o = _mla_attn(q, k, v, n_lat, n_ctx)

    xs, z0, g0, g1, sg = _hy_proj(o, sg, x, ctx, mods[0], bf(mla_w_out[0]), norm_g[1:2], mods[1], bf(hy_w_in[0]),
                                  hy_conv_w[0], hy_conv_b[0][None, :], n_lat_tiles)
    cch = z0.shape[-1]
    filt = (hy_filt_w_in[0], hy_filt_w_hid[0], hy_filt_b[0], hy_filt_freq[0], hy_filt_w_out[0])
    assert cch % HY_SLAB == 0
    mats = _fft_matrices(n_lat)
    spec = _hy_spec(_hy_filter_taps(n_lat, *filt), mats[1], mats[2])
    cfwd, cinv, nfp = _dense_dft_matrices(n_ctx)
    cspec = _hy_ctx_spec(_hy_filter_taps(n_ctx, *filt), cfwd)
    gates = (g0, g1)
    z = z0
    for order in range(HY_ORDER):
        bias = hy_bias[0][order][None, :]
        z = _hy_conv(z, gates[order], spec, order, mats, bias, n_lat)
        z = _hy_ctx_conv(z, gates[order], cspec, order, cfwd[:, :n_ctx], cinv, nfp, bias, n_lat, n_ctx)

    nq = SWA_Q_HEADS * SWA_HEAD_DIM
    nkv = SWA_KV_HEADS * SWA_HEAD_DIM
    w_in = swa_w_in[0]
    dup = lambda w: jnp.tile(w.reshape(d, SWA_KV_HEADS, 1, SWA_HEAD_DIM), (1, 1, 2, 1)).reshape(d, 2 * nkv)
    w2 = bf(jnp.concatenate([w_in[:, :nq], dup(w_in[:, nq:nq + nkv]), dup(w_in[:, nq + nkv:nq + 2 * nkv]),
                             w_in[:, nq + 2 * nkv:]], axis=1))
    xs, q, k, v, sg = _swa_proj(z, sg, xs, mods[1], bf(hy_w_out[0]), norm_g[2:3], mods[2], w2, cos128, sin128,
                                n_lat_tiles)
    o = _swa_attn(swa_sink[0], q, k, v, n_lat, n_ctx)

    xs, u, sg = _cf_proj(o, sg, xs, mods[2], bf(swa_w_out[0]), norm_g[3:4], mods[3], bf(cf_w_in[0]), n_lat_tiles)
    return _cf_tail(u, sg, xs, mods[3], cf_dw_w[0], cf_dw_b[0][None, :], cf_ln_g[0][None, :], cf_ln_b[0][None, :],
                    bf(cf_w_out[0]), final_g[None, :])
```

```python
import functools
import math

import numpy as np
import jax
import jax.numpy as jnp
from jax import lax
from jax.experimental import pallas as pl
from jax.experimental.pallas import tpu as pltpu

F32 = jnp.float32
BF16 = jnp.bfloat16

GRID_W = 64
NORM_EPS = 1e-6
NEG_INF = -1e30
LOG2E = math.log2(math.e)
ROPE_BASE = 10000.0
ROPE_DIM = 64

MLA_HEADS = 8
MLA_Q_RANK = 384
MLA_KV_RANK = 256
MLA_NOPE = 128
MLA_ROPE = ROPE_DIM
MLA_V = 128
MLA_QK = MLA_NOPE + MLA_ROPE

HY_ORDER = 2
HY_EMB = 33
HY_EMB_PAD = 64
HY_BANDS = (HY_EMB - 1) // 2
HY_DECAY_TARGET = 1e-2
HY_FAST_DECAY = 0.3
HY_SLOW_DECAY = 1.5

SWA_Q_HEADS = 16
SWA_KV_HEADS = 4
SWA_GROUP = SWA_Q_HEADS // SWA_KV_HEADS
SWA_HEAD_DIM = ROPE_DIM
SWA_WINDOW = 128

CF_KERNEL = 31
CF_HALO = 16

ROW_TILE = 512
ATTN_TILE = 256
SUBLANES = 8
LANES = 128
FFT_N1 = 64
FFT_N1H = FFT_N1 // 2
FFT_K1 = FFT_N1 // 2 + 1
HY_SLAB = 256
HY_HALO = 16
VMEM_LIMIT = 48 * 1024 * 1024
VMEM_LIMIT_BIG = 56 * 1024 * 1024


def _cparams(n_axes):
    return pltpu.CompilerParams(dimension_semantics=("arbitrary",) * n_axes,
                                vmem_limit_bytes=VMEM_LIMIT)


def _cparams_big(n_axes):
    return pltpu.CompilerParams(dimension_semantics=("arbitrary",) * n_axes,
                                vmem_limit_bytes=VMEM_LIMIT_BIG)


def _once(shape, index_map):
    return pl.BlockSpec(shape, index_map, pipeline_mode=pl.Buffered(1))


def _dot(a, b):
    return jnp.dot(a, b, preferred_element_type=F32)


def _dot_nt(a, b):
    return lax.dot_general(a, b, (((1,), (1,)), ((), ())), preferred_element_type=F32)


def _rms(x, g):
    return x * lax.rsqrt(jnp.mean(x * x, axis=-1, keepdims=True) + NORM_EPS) * g


def _norm_mod(x, g, mod):
    return _rms(x, g) * (1.0 + mod[1:2, :]) + mod[0:1, :]


def _silu(x):
    return x * jax.nn.sigmoid(x)


def _adaln_kernel(c_ref, w_ref, b_ref, o_ref):
    s = _silu(c_ref[...]).astype(BF16)
    o_ref[...] = _dot(s, w_ref[...].astype(BF16)) + b_ref[...]


def _adaln(cs, ada_w, ada_b):
    depth, d, d3 = ada_w.shape
    tn = 2 * ROW_TILE
    return pl.pallas_call(
        _adaln_kernel,
        out_shape=jax.ShapeDtypeStruct((depth, cs.shape[0], d3), F32),
        grid=(depth, d3 // tn),
        in_specs=[pl.BlockSpec(cs.shape, lambda i, j: (0, 0)),
                  pl.BlockSpec((None, d, tn), lambda i, j: (i, 0, j)),
                  pl.BlockSpec((None, 1, tn), lambda i, j: (i, 0, j))],
        out_specs=pl.BlockSpec((None, cs.shape[0], tn), lambda i, j: (i, 0, j)),
        compiler_params=_cparams(2), name="adaln",
    )(cs, ada_w, ada_b.reshape(depth, 1, d3))


def _input_tile(x_ref, ctx_ref, is_ctx):
    pad = jnp.zeros((x_ref.shape[0] - ctx_ref.shape[0], x_ref.shape[1]), F32)
    return jnp.where(is_ctx, jnp.concatenate([ctx_ref[...], pad], axis=0), x_ref[...])


def _input_specs(d, n_ctx, n_lat_tiles):
    return [pl.BlockSpec((None, ROW_TILE, d), lambda bi, i: (bi, jnp.minimum(i, n_lat_tiles - 1), 0)),
            pl.BlockSpec((None, n_ctx, d), lambda bi, i: (bi, 0, 0))]


def _residual_in(o_ref, sgp_ref, x_ref, modp_ref, wout_ref, xo_ref):
    og = (o_ref[...].astype(F32) * sgp_ref[...].astype(F32)).astype(BF16)
    x = x_ref[...] + modp_ref[2:3, :] * _dot(og, wout_ref[...])
    xo_ref[...] = x
    return x


def _residual_specs(o, d, n_lat_tiles):
    width = o.shape[-1]
    row = lambda bi, i: (bi, i, 0)
    return [pl.BlockSpec((None, ROW_TILE, width), row), pl.BlockSpec((None, ROW_TILE, width), row),
            pl.BlockSpec((None, ROW_TILE, d), row),
            pl.BlockSpec((None, None, 3, d), lambda bi, i: (bi, i // n_lat_tiles, 0, 0)),
            _once((width, d), lambda bi, i: (0, 0))]


def _rope_lanes(x, cos, sin):
    half = ROPE_DIM // 2
    lane = lax.broadcasted_iota(jnp.int32, x.shape, 1) % ROPE_DIM
    rot = jnp.where(lane < half, -pltpu.roll(x, LANES - half, 1), pltpu.roll(x, half, 1))
    return x * cos + rot * sin


def _mla_proj_kernel(x_ref, ctx_ref, g_ref, mod_ref, wl_ref, wg_ref, qg_ref, kvg_ref, wuq_ref, wukv_ref,
                     cos_ref, sin_ref, q_ref, k_ref, v_ref, sg_ref, *, n_lat_tiles):
    x = _input_tile(x_ref, ctx_ref, pl.program_id(1) >= n_lat_tiles)
    h = _norm_mod(x, g_ref[...], mod_ref[...]).astype(BF16)
    sg_ref[...] = _silu(_dot(h, wg_ref[...])).astype(sg_ref.dtype)
    res = _dot(h, wl_ref[...])
    c0, c1 = MLA_Q_RANK, MLA_Q_RANK + MLA_KV_RANK
    cqn = _rms(res[:, :c0], qg_ref[...]).astype(BF16)
    ckvn = _rms(res[:, c0:c1], kvg_ref[...]).astype(BF16)
    cos, sin = cos_ref[...], sin_ref[...]

    def rope(slab):
        return _rope_lanes(slab, cos, sin)[:, :MLA_ROPE]

    k_rope_t = rope(res[:, c1:c1 + LANES]).T.astype(k_ref.dtype)
    scale = MLA_QK ** -0.5 * LOG2E
    per = 2 * LANES
    for h0 in range(0, MLA_HEADS, 2):
        q2 = _dot(cqn, wuq_ref[:, h0 * per:(h0 + 2) * per])
        kv2 = _dot(ckvn, wukv_ref[:, h0 * per:(h0 + 2) * per])
        for j in range(2):
            hd = h0 + j
            q, kv = q2[:, j * per:(j + 1) * per], kv2[:, j * per:(j + 1) * per]
            q_ref[hd, :, :MLA_NOPE] = (q[:, :MLA_NOPE] * scale).astype(q_ref.dtype)
            q_ref[hd, :, MLA_NOPE:] = (rope(q[:, MLA_NOPE:]) * scale).astype(q_ref.dtype)
            k_ref[hd, :MLA_NOPE, :] = kv[:, :MLA_NOPE].T.astype(k_ref.dtype)
            k_ref[hd, MLA_NOPE:, :] = k_rope_t
            v_ref[hd, :, :MLA_V] = kv[:, MLA_NOPE:].astype(v_ref.dtype)
            v_ref[hd, :, MLA_V:] = jnp.ones((v_ref.shape[1], MLA_V), v_ref.dtype)


def _mla_proj(x, ctx, g, mod, wl, wg, qg, kvg, wuq, wukv, cos, sin, n_lat_tiles):
    b, n_lat, d = x.shape
    nt = n_lat_tiles + 1
    t = nt * ROW_TILE
    width = MLA_HEADS * MLA_V
    full = lambda a: _once(a.shape, lambda bi, i: (0,) * a.ndim)
    head_out = lambda n: pl.BlockSpec((None, MLA_HEADS, ROW_TILE, n), lambda bi, i: (bi, 0, i, 0))
    return pl.pallas_call(
        functools.partial(_mla_proj_kernel, n_lat_tiles=n_lat_tiles),
        out_shape=(jax.ShapeDtypeStruct((b, MLA_HEADS, t, MLA_QK), BF16),
                   jax.ShapeDtypeStruct((b, MLA_HEADS, MLA_QK, t), BF16),
                   jax.ShapeDtypeStruct((b, MLA_HEADS, t, 2 * MLA_V), BF16),
                   jax.ShapeDtypeStruct((b, t, width), BF16)),
        grid=(b, nt),
        in_specs=_input_specs(d, ctx.shape[1], n_lat_tiles) +
                 [full(g),
                  pl.BlockSpec((None, None, 3, d), lambda bi, i: (bi, i // n_lat_tiles, 0, 0)),
                  full(wl), full(wg), full(qg), full(kvg), full(wuq), full(wukv),
                  pl.BlockSpec((ROW_TILE, LANES), lambda bi, i: (i, 0)),
                  pl.BlockSpec((ROW_TILE, LANES), lambda bi, i: (i, 0))],
        out_specs=(head_out(MLA_QK),
                   pl.BlockSpec((None, MLA_HEADS, MLA_QK, ROW_TILE), lambda bi, i: (bi, 0, 0, i)),
                   head_out(2 * MLA_V),
                   pl.BlockSpec((None, ROW_TILE, width), lambda bi, i: (bi, i, 0))),
        compiler_params=_cparams(2), name="mla_proj",
    )(x, ctx, g, mod, wl, wg, qg, kvg, wuq, wukv, cos, sin)


def _attend(q, k_parts, v_parts):
    s = [_dot(q, k) for k in k_parts]
    m = functools.reduce(jnp.maximum, [jnp.max(x, axis=-1, keepdims=True) for x in s])
    ov = functools.reduce(jnp.add, [_dot(jnp.exp2(x - m).astype(BF16), v) for x, v in zip(s, v_parts)])
    n = ov.shape[-1] // 2
    return ov[:, :n] / ov[:, n:]


MLA_HEADS_PER_STEP = 2


def _mla_attn_kernel(q_ref, k_ref, v_ref, o_ref, *, n_lat_tiles, n_lat, n_ctx):
    qi = pl.program_id(2)

    n_keys = n_lat + n_ctx

    @pl.when(qi < n_lat_tiles)
    def _():
        chains = [(hd, r0) for hd in range(MLA_HEADS_PER_STEP) for r0 in range(0, q_ref.shape[1], ATTN_TILE)]
        s = [_dot(q_ref[hd, r0:r0 + ATTN_TILE, :], k_ref[hd, :, 0:n_keys]) for hd, r0 in chains]
        p = [jnp.exp2(x - jnp.max(x, axis=-1, keepdims=True)).astype(BF16) for x in s]
        for (hd, r0), pc in zip(chains, p):
            ov = _dot(pc, v_ref[hd, 0:n_keys, :])
            o = ov[:, :MLA_V] / ov[:, MLA_V:]
            o_ref[r0:r0 + ATTN_TILE, hd * MLA_V:(hd + 1) * MLA_V] = o.astype(o_ref.dtype)

    @pl.when(qi >= n_lat_tiles)
    def _():
        o_ref[n_ctx:, :] = jnp.zeros((o_ref.shape[0] - n_ctx, o_ref.shape[1]), o_ref.dtype)
        for hd in range(MLA_HEADS_PER_STEP):
            k = k_ref[hd, :, n_lat:n_keys]
            v = v_ref[hd, n_lat:n_keys, :]
            o = _attend(q_ref[hd, 0:n_ctx, :], [k], [v])
            o_ref[0:n_ctx, hd * MLA_V:(hd + 1) * MLA_V] = o.astype(o_ref.dtype)


def _mla_attn(q, k, v, n_lat, n_ctx):
    b, hds, t, _ = q.shape
    nt = t // ROW_TILE
    hps = MLA_HEADS_PER_STEP
    kern = functools.partial(_mla_attn_kernel, n_lat_tiles=n_lat // ROW_TILE, n_lat=n_lat, n_ctx=n_ctx)
    return pl.pallas_call(
        kern,
        out_shape=jax.ShapeDtypeStruct((b, t, hds * MLA_V), BF16),
        grid=(b, hds // hps, nt),
        in_specs=[pl.BlockSpec((None, hps, ROW_TILE, MLA_QK), lambda bi, h, i: (bi, h, i, 0)),
                  pl.BlockSpec((None, hps, MLA_QK, t), lambda bi, h, i: (bi, h, 0, 0)),
                  pl.BlockSpec((None, hps, t, 2 * MLA_V), lambda bi, h, i: (bi, h, 0, 0))],
        out_specs=pl.BlockSpec((None, ROW_TILE, hps * MLA_V), lambda bi, h, i: (bi, i, h)),
        compiler_params=_cparams_big(3), name="mla_attn",
    )(q, k, v)


def _hy_proj_kernel(op_ref, o_ref, on_ref, sp_ref, s_ref, sn_ref, xp_ref, x_ref, ctx_ref, xn_ref, modp_ref, wout_ref,
                    g_ref, mod_ref, w_ref, cw_ref, cb_ref, xo_ref, z_ref, g0_ref, g1_ref, sg_ref,
                    *, n_lat_tiles, n_ctx):
    i = pl.program_id(1)
    tm = x_ref.shape[0]
    c = z_ref.shape[-1]
    n = tm + 2 * SUBLANES
    halo = xp_ref.shape[0]
    cat = lambda a, m, z: jnp.concatenate([a[...], m, z[...]], axis=0)
    og = (cat(op_ref, o_ref[...], on_ref).astype(F32) * cat(sp_ref, s_ref[...], sn_ref).astype(F32)).astype(BF16)
    xe = cat(xp_ref, _input_tile(x_ref, ctx_ref, i >= n_lat_tiles), xn_ref)
    xe = xe + modp_ref[2:3, :] * _dot(og, wout_ref[...])
    xo_ref[...] = xe[halo:halo + tm]
    xa = xe[halo - SUBLANES:halo + tm + SUBLANES]
    ctx_tile = i == n_lat_tiles
    first = jnp.logical_or(i == 0, ctx_tile)
    last = jnp.logical_or(i == n_lat_tiles - 1, ctx_tile)
    rows = lax.broadcasted_iota(jnp.int32, (n, 1), 0)
    end = jnp.where(ctx_tile, n_ctx + SUBLANES, jnp.where(last, tm + SUBLANES, n))
    valid = jnp.logical_and(jnp.logical_or(rows >= SUBLANES, jnp.logical_not(first)), rows < end)
    hf = jnp.where(valid, _norm_mod(xa, g_ref[...], mod_ref[...]), 0.0)
    h = hf.astype(BF16)
    outs = (z_ref, g0_ref, g1_ref)
    for j in range(len(outs)):
        u = _dot(h, w_ref[:, j * c:(j + 1) * c])
        cw = cw_ref[:, j * c:(j + 1) * c]
        cv = cw[0:1, :] * pltpu.roll(u, 1, 0) + cw[1:2, :] * u + cw[2:3, :] * pltpu.roll(u, n - 1, 0)
        outs[j][...] = cv[SUBLANES:SUBLANES + tm] + cb_ref[:, j * c:(j + 1) * c]
    h_mid = hf[SUBLANES:SUBLANES + tm].astype(BF16)
    sg_ref[...] = _silu(_dot(h_mid, w_ref[:, len(outs) * c:])).astype(sg_ref.dtype)


def _hy_proj(o, sgp, x, ctx, modp, w_out, g, mod, w, cw, cb, n_lat_tiles):
    b, t, width = o.shape
    n_lat, d = x.shape[1:]
    n_ctx = ctx.shape[1]
    c = w.shape[1] // (HY_ORDER + 2)
    per = ROW_TILE // HY_HALO
    last = n_lat // HY_HALO - 1
    full = lambda a: _once(a.shape, lambda bi, i: (0,) * a.ndim)
    row = lambda bi, i: (bi, i, 0)
    prev = lambda bi, i: (bi, jnp.maximum(i * per - 1, 0), 0)
    nxt = lambda bi, i: (bi, jnp.minimum((i + 1) * per, last), 0)
    x_main, x_ctx = _input_specs(d, n_ctx, n_lat_tiles)
    kern = functools.partial(_hy_proj_kernel, n_lat_tiles=n_lat_tiles, n_ctx=n_ctx)
    halo = lambda n, m: pl.BlockSpec((None, HY_HALO, n), m)
    return pl.pallas_call(
        kern,
        out_shape=(jax.ShapeDtypeStruct((b, t, d), F32),) + (jax.ShapeDtypeStruct((b, t, c), F32),) * 3 +
                  (jax.ShapeDtypeStruct((b, t, c), BF16),),
        grid=(b, t // ROW_TILE),
        in_specs=[halo(width, prev), pl.BlockSpec((None, ROW_TILE, width), row), halo(width, nxt),
                  halo(width, prev), pl.BlockSpec((None, ROW_TILE, width), row), halo(width, nxt),
                  halo(d, prev), x_main, x_ctx, halo(d, nxt),
                  pl.BlockSpec((None, None, 3, d), lambda bi, i: (bi, i // n_lat_tiles, 0, 0)),
                  full(w_out), full(g),
                  pl.BlockSpec((None, None, 3, d), lambda bi, i: (bi, i // n_lat_tiles, 0, 0)),
                  full(w), full(cw), full(cb)],
        out_specs=(pl.BlockSpec((None, ROW_TILE, d), row),) + (pl.BlockSpec((None, ROW_TILE, c), row),) * 4,
        compiler_params=_cparams(2), name="hy_proj",
    )(o, o, o, sgp, sgp, sgp, x, x, ctx, x, modp, w_out, g, mod, w, cw, cb)


def _hy_filter_kernel(ef_ref, eb_ref, w0_ref, wh_ref, b_ref, fr_ref, wf_ref, wb_ref, dl_ref, o_ref):
    i = pl.program_id(0)
    c = dl_ref.shape[-1]
    half = ef_ref.shape[0]

    def taps(e, wo):
        hdn = e
        ws = (w0_ref[...], wh_ref[0], wh_ref[1])
        for k in range(3):
            hdn = jnp.sin(fr_ref[k:k + 1, :] * (_dot(hdn.astype(BF16), ws[k].astype(BF16)) + b_ref[k:k + 1, :]))
        hh = _dot(hdn.astype(BF16), wo.astype(BF16))
        out = []
        for part in range(2):
            decay = jnp.exp(-e[:, part * HY_EMB_PAD:part * HY_EMB_PAD + 1] * dl_ref[...])
            out.append([hh[:, (part * HY_ORDER + o) * c:(part * HY_ORDER + o + 1) * c] * decay
                        for o in range(HY_ORDER)])
        return out

    hf = taps(ef_ref[...], wf_ref[...])
    hb = taps(eb_ref[...], wb_ref[...])
    rows = lax.broadcasted_iota(jnp.int32, (half, 1), 0)
    lag0 = jnp.logical_and(rows == 0, i == 0)
    for o in range(HY_ORDER):
        o_ref[o, 0, 0:half] = hf[0][o] + jnp.where(lag0, hb[0][o], 0.0)
        o_ref[o, 1, 0:half] = jnp.where(lag0, 0.0, hb[0][o])
        o_ref[o, 0, half:] = hf[1][o]
        o_ref[o, 1, half:] = hb[1][o]


def _hy_position_features(n):
    t = jnp.linspace(0.0, 1.0, n, dtype=F32)[:, None]
    wpos = (2.0 * math.pi / n) * jnp.arange(n, dtype=F32)[:, None]
    bands = jnp.linspace(1e-4, HY_BANDS - 1, HY_BANDS, dtype=F32)[None, :]
    hdn = jnp.concatenate([t, jnp.cos(bands * wpos), -jnp.sin(bands * wpos)], axis=-1)
    return jnp.pad(hdn, ((0, 0), (0, HY_EMB_PAD - HY_EMB)))


def _hy_filter_taps(n, w_in, w_hid, bias, freq, w_out):
    c = w_out.shape[1] // (2 * HY_ORDER)
    ef = _hy_position_features(n)
    rev = (n - jnp.arange(n)) % n
    eb = ef[rev]
    w0 = jnp.pad(w_in, ((0, HY_EMB_PAD - HY_EMB), (0, 0)))
    wo = w_out.reshape(w_out.shape[0], HY_ORDER, 2, c)
    wf = wo[:, :, 0, :].reshape(w_out.shape[0], HY_ORDER * c)
    wb = wo[:, :, 1, :].reshape(w_out.shape[0], HY_ORDER * c)
    deltas = jnp.abs(jnp.linspace(math.log(HY_DECAY_TARGET) / HY_FAST_DECAY,
                                  math.log(HY_DECAY_TARGET) / HY_SLOW_DECAY, c, dtype=F32))[None, :]
    tl = min(n, ROW_TILE)
    half = tl // 2
    pair = lambda e: e.reshape(n // tl, 2, half, HY_EMB_PAD).transpose(0, 2, 1, 3).reshape(n // 2, 2 * HY_EMB_PAD)
    both = lambda w: jnp.kron(jnp.eye(2, dtype=F32), w)
    twice = lambda v: jnp.tile(v, (1, 2))
    w_hid2 = jnp.stack([both(w_hid[0]), both(w_hid[1])])
    full = lambda a: pl.BlockSpec(a.shape, lambda i: (0,) * a.ndim)
    args = (pair(ef), pair(eb), both(w0), w_hid2, twice(bias), twice(freq), both(wf), both(wb), deltas)
    return pl.pallas_call(
        _hy_filter_kernel,
        out_shape=jax.ShapeDtypeStruct((HY_ORDER, 2, n, c), F32),
        grid=(n // tl,),
        in_specs=[pl.BlockSpec((half, 2 * HY_EMB_PAD), lambda i: (i, 0)),
                  pl.BlockSpec((half, 2 * HY_EMB_PAD), lambda i: (i, 0))] + [full(a) for a in args[2:]],
        out_specs=pl.BlockSpec((HY_ORDER, 2, tl, c), lambda i: (0, 0, i, 0)),
        compiler_params=_cparams(1), name="hy_filter",
    )(*args)


def _fft_matrices(n_lat):
    n = 2 * n_lat
    n2 = n_lat // FFT_N1H
    k1 = np.arange(FFT_K1)
    n1 = np.arange(FFT_N1H)
    ang = 2.0 * np.pi * np.outer(k1, n1) / FFT_N1
    eye = np.eye(SUBLANES)
    f1 = np.kron(np.concatenate([np.cos(ang), -np.sin(ang)], axis=0), eye)
    wgt = np.full(FFT_K1, 2.0)
    wgt[0] = wgt[-1] = 1.0
    g1 = np.kron(np.concatenate([wgt * np.cos(ang.T), (-wgt * np.sin(ang.T))[:, 1:-1]], axis=1) / n, eye)
    idx = np.arange(n2)
    f2 = np.zeros((FFT_K1, 2 * n2, 2 * n2))
    f2i = np.zeros((FFT_K1, 2 * n2, 2 * n2))
    for a in range(FFT_K1):
        ph = 2.0 * np.pi * (np.outer(idx, idx) / n2 + (a * idx)[None, :] / n)
        f2[a] = np.block([[np.cos(ph), np.sin(ph)], [-np.sin(ph), np.cos(ph)]])
        f2i[a] = np.block([[np.cos(ph.T), -np.sin(ph.T)], [np.sin(ph.T), np.cos(ph.T)]])
    sign = np.kron(np.tile((-1.0) ** k1, 2), np.ones(SUBLANES))[:, None]
    f1s = np.concatenate([f1, sign * f1], axis=1)
    as_bf16 = lambda m: jnp.asarray(m.astype(np.float32)).astype(BF16)
    return as_bf16(f1), as_bf16(f1s), as_bf16(f2), as_bf16(f2i), as_bf16(g1)


def _dense_dft_matrices(n_ctx):
    n = 2 * n_ctx
    nf = n_ctx + 1
    nfp = -(-nf // SUBLANES) * SUBLANES
    k = np.arange(nf)
    ang = 2.0 * np.pi * np.outer(k, np.arange(n)) / n
    fwd = np.zeros((2 * nfp, n))
    fwd[:nf] = np.cos(ang)
    fwd[nfp:nfp + nf] = -np.sin(ang)
    wgt = np.full(nf, 2.0)
    wgt[0] = wgt[-1] = 1.0
    angi = 2.0 * np.pi * np.outer(np.arange(n_ctx), k) / n
    inv = np.zeros((n_ctx, 2 * nfp))
    inv[:, :nf] = wgt * np.cos(angi) / n
    inv[:, nfp:nfp + nf] = -wgt * np.sin(angi) / n
    as_bf16 = lambda m: jnp.asarray(m.astype(np.float32)).astype(BF16)
    return as_bf16(fwd), as_bf16(inv), nfp


def _hy_spec_kernel(t_ref, f1s_ref, f2_ref, h_ref, a_ref):
    _, n1h, nj, s, tc = t_ref.shape
    n2 = nj * s
    for j in range(nj):
        sig = jnp.concatenate([t_ref[0, :, j].reshape(n1h * s, tc), t_ref[1, :, j].reshape(n1h * s, tc)], axis=0)
        a_ref[:, :, j] = _dot(f1s_ref[...], sig.astype(BF16)).reshape(2, FFT_K1, s, tc)
    for k in range(FFT_K1):
        a = a_ref[:, k].reshape(2 * n2, tc).astype(BF16)
        h_ref[:, k] = _dot(f2_ref[k], a).reshape(2, n2, tc).astype(h_ref.dtype)


def _hy_spec(taps, f1s, f2):
    _, _, n_lat, c = taps.shape
    n2 = n_lat // FFT_N1H
    nj = n2 // SUBLANES
    tc = HY_SLAB
    return pl.pallas_call(
        _hy_spec_kernel,
        out_shape=jax.ShapeDtypeStruct((HY_ORDER, 2, FFT_K1, n2, c), BF16),
        grid=(c // tc, HY_ORDER),
        in_specs=[pl.BlockSpec((None, 2, FFT_N1H, nj, SUBLANES, tc), lambda ci, o: (o, 0, 0, 0, 0, ci)),
                  _once(f1s.shape, lambda ci, o: (0, 0)),
                  _once(f2.shape, lambda ci, o: (0, 0, 0))],
        out_specs=pl.BlockSpec((None, 2, FFT_K1, n2, tc), lambda ci, o: (o, 0, 0, 0, ci)),
        scratch_shapes=[pltpu.VMEM((2, FFT_K1, nj, SUBLANES, tc), F32)],
        compiler_params=_cparams_big(2), name="hy_spec",
    )(taps.reshape(HY_ORDER, 2, FFT_N1H, nj, SUBLANES, c), f1s, f2)


def _hy_conv_kernel(z_ref, gt_ref, h_ref, f1_ref, f2_ref, f2i_ref, g1_ref, bias_ref, o_ref, a_ref):
    n1h, nj, s, tc = z_ref.shape
    n2 = nj * s
    for j in range(nj):
        zz = z_ref[:, j].reshape(n1h * s, tc).astype(BF16)
        a_ref[:, :, j] = _dot(f1_ref[...], zz).reshape(2, FFT_K1, s, tc)
    for k in range(FFT_K1):
        x = _dot(f2_ref[k], a_ref[:, k].reshape(2 * n2, tc).astype(BF16))
        xr, xi = x[:n2], x[n2:]
        hr, hi = h_ref[0, k].astype(F32), h_ref[1, k].astype(F32)
        y = jnp.concatenate([xr * hr - xi * hi, xr * hi + xi * hr], axis=0).astype(BF16)
        a_ref[:, k] = _dot(f2i_ref[k], y).reshape(2, nj, s, tc)
    for j in range(nj):
        bp = a_ref[:, :, j].reshape(2 * FFT_K1 * s, tc)
        bp = jnp.concatenate([bp[:FFT_K1 * s], bp[(FFT_K1 + 1) * s:(2 * FFT_K1 - 1) * s]], axis=0)
        y = _dot(g1_ref[...], bp.astype(BF16)).reshape(n1h, s, tc)
        o_ref[:, j] = gt_ref[:, j] * (y + z_ref[:, j] * bias_ref[...])


def _hy_conv(z, gate, h, order, mats, bias, n_lat):
    f1, _, f2, f2i, g1 = mats
    bz, t, c = z.shape
    n2 = n_lat // FFT_N1H
    nj = n2 // SUBLANES
    tc = HY_SLAB
    view = (bz, t // n2, nj, SUBLANES, c)
    slab = pl.BlockSpec((None, FFT_N1H, nj, SUBLANES, tc), lambda ci, bi: (bi, 0, 0, 0, ci))
    out = pl.pallas_call(
        _hy_conv_kernel,
        out_shape=jax.ShapeDtypeStruct(view, F32),
        grid=(c // tc, bz),
        in_specs=[slab, slab,
                  _once((None, 2, FFT_K1, n2, tc), lambda ci, bi: (order, 0, 0, 0, ci)),
                  _once(f1.shape, lambda ci, bi: (0, 0)),
                  _once(f2.shape, lambda ci, bi: (0, 0, 0)),
                  _once(f2i.shape, lambda ci, bi: (0, 0, 0)),
                  _once(g1.shape, lambda ci, bi: (0, 0)),
                  pl.BlockSpec((1, tc), lambda ci, bi: (0, ci))],
        out_specs=slab,
        input_output_aliases={0: 0},
        scratch_shapes=[pltpu.VMEM((2, FFT_K1, nj, SUBLANES, tc), F32)],
        compiler_params=_cparams_big(2), name="hy_conv",
    )(z.reshape(view), gate.reshape(view), h, f1, f2, f2i, g1, bias)
    return out.reshape(bz, t, c)


def _hy_ctx_spec_kernel(t_ref, f_ref, h_ref):
    sig = jnp.concatenate([t_ref[0], t_ref[1]], axis=0).astype(BF16)
    h_ref[...] = _dot(f_ref[...], sig)


def _hy_ctx_spec(taps, fwd):
    _, _, n_ctx, c = taps.shape
    return pl.pallas_call(
        _hy_ctx_spec_kernel,
        out_shape=jax.ShapeDtypeStruct((HY_ORDER, fwd.shape[0], c), F32),
        grid=(HY_ORDER,),
        in_specs=[pl.BlockSpec((None, 2, n_ctx, c), lambda o: (o, 0, 0, 0)),
                  pl.BlockSpec(fwd.shape, lambda o: (0, 0))],
        out_specs=pl.BlockSpec((None, fwd.shape[0], c), lambda o: (o, 0, 0)),
        compiler_params=_cparams(1), name="hy_ctx_spec",
    )(taps, fwd)


def _hy_ctx_conv_kernel(zin_ref, gt_ref, h_ref, f_ref, fi_ref, bias_ref, o_ref, *, nfp):
    z = zin_ref[...]
    x = _dot(f_ref[...], z.astype(BF16))
    xr, xi = x[:nfp], x[nfp:]
    hr, hi = h_ref[:nfp, :], h_ref[nfp:, :]
    y = jnp.concatenate([xr * hr - xi * hi, xr * hi + xi * hr], axis=0).astype(BF16)
    n_ctx = z.shape[0]
    o_ref[0:n_ctx, :] = gt_ref[...] * (_dot(fi_ref[...], y) + z * bias_ref[...])
    o_ref[n_ctx:, :] = jnp.zeros((o_ref.shape[0] - n_ctx, o_ref.shape[1]), o_ref.dtype)


def _hy_ctx_conv(z, gate, h, order, fwd_half, inv, nfp, bias, n_lat, n_ctx):
    bz, t, c = z.shape
    blk = pl.BlockSpec((None, n_ctx, c), lambda bi: (bi, n_lat // n_ctx, 0))
    kern = functools.partial(_hy_ctx_conv_kernel, nfp=nfp)
    return pl.pallas_call(
        kern,
        out_shape=jax.ShapeDtypeStruct(z.shape, F32),
        grid=(bz,),
        in_specs=[blk, blk,
                  pl.BlockSpec((None, 2 * nfp, c), lambda bi: (order, 0, 0)),
                  pl.BlockSpec(fwd_half.shape, lambda bi: (0, 0)),
                  pl.BlockSpec(inv.shape, lambda bi: (0, 0)),
                  pl.BlockSpec((1, c), lambda bi: (0, 0))],
        out_specs=pl.BlockSpec((None, t - n_lat, c), lambda bi: (bi, n_lat // (t - n_lat), 0)),
        input_output_aliases={0: 0},
        compiler_params=_cparams(1), name="hy_ctx_conv",
    )(z, gate, h, fwd_half, inv, bias)


def _swa_proj_kernel(o_ref, sgp_ref, x_ref, modp_ref, wout_ref, g_ref, mod_ref, w_ref, cos_ref, sin_ref,
                     xo_ref, q_ref, k_ref, v_ref, sg_ref):
    x = _residual_in(o_ref, sgp_ref, x_ref, modp_ref, wout_ref, xo_ref)
    h = _norm_mod(x, g_ref[...], mod_ref[...]).astype(BF16)
    cos, sin = cos_ref[...], sin_ref[...]
    nq = q_ref.shape[-1]
    nk = SWA_KV_HEADS * LANES
    scale = SWA_HEAD_DIM ** -0.5 * LOG2E
    for c0 in range(0, nq, nk):
        res = _dot(h, w_ref[:, c0:c0 + nk])
        for cb in range(nk // LANES):
            sl = slice(cb * LANES, (cb + 1) * LANES)
            q_ref[:, c0 + cb * LANES:c0 + (cb + 1) * LANES] = (_rope_lanes(res[:, sl], cos, sin) * scale).astype(q_ref.dtype)
    res = _dot(h, w_ref[:, nq:nq + nk])
    for hd in range(SWA_KV_HEADS):
        k_ref[hd] = _rope_lanes(res[:, hd * LANES:(hd + 1) * LANES], cos, sin).astype(k_ref.dtype)
    res = _dot(h, w_ref[:, nq + nk:nq + 2 * nk])
    for hd in range(SWA_KV_HEADS):
        v_ref[hd, :, :LANES] = res[:, hd * LANES:(hd + 1) * LANES].astype(v_ref.dtype)
        v_ref[hd, :, LANES:] = jnp.ones((v_ref.shape[1], LANES), v_ref.dtype)
    for c0 in range(0, nq, nk):
        sg_ref[:, c0:c0 + nk] = _silu(_dot(h, w_ref[:, nq + 2 * nk + c0:nq + 2 * nk + c0 + nk])).astype(sg_ref.dtype)


def _swa_proj(o, sgp, x, modp, w_out, g, mod, w, cos, sin, n_lat_tiles):
    b, t, d = x.shape
    nt = t // ROW_TILE
    nq = SWA_Q_HEADS * SWA_HEAD_DIM
    full = lambda a: _once(a.shape, lambda bi, i: (0,) * a.ndim)
    row = lambda bi, i: (bi, i, 0)
    kv = lambda n: pl.BlockSpec((None, SWA_KV_HEADS, ROW_TILE, n), lambda bi, i: (bi, 0, i, 0))
    return pl.pallas_call(
        _swa_proj_kernel,
        out_shape=(jax.ShapeDtypeStruct((b, t, d), F32),
                   jax.ShapeDtypeStruct((b, t, nq), BF16),
                   jax.ShapeDtypeStruct((b, SWA_KV_HEADS, t, LANES), BF16),
                   jax.ShapeDtypeStruct((b, SWA_KV_HEADS, t, 2 * LANES), BF16),
                   jax.ShapeDtypeStruct((b, t, nq), BF16)),
        grid=(b, nt),
        in_specs=_residual_specs(o, d, n_lat_tiles) +
                 [full(g),
                  pl.BlockSpec((None, None, 3, d), lambda bi, i: (bi, i // n_lat_tiles, 0, 0)),
                  full(w),
                  pl.BlockSpec((ROW_TILE, LANES), lambda bi, i: (i, 0)),
                  pl.BlockSpec((ROW_TILE, LANES), lambda bi, i: (i, 0))],
        out_specs=(pl.BlockSpec((None, ROW_TILE, d), row), pl.BlockSpec((None, ROW_TILE, nq), row),
                   kv(LANES), kv(2 * LANES), pl.BlockSpec((None, ROW_TILE, nq), row)),
        compiler_params=_cparams(2), name="swa_proj",
    )(o, sgp, x, modp, w_out, g, mod, w, cos, sin)


def _swa_attn_kernel(sink_ref, q_ref, k_ref, v_ref, o_ref, *, n_lat_tiles, n_lat, n_ctx):
    g = pl.program_id(1)
    qi = pl.program_id(2)
    tq = ATTN_TILE
    span = tq + 2 * SWA_WINDOW
    lane_half = lax.broadcasted_iota(jnp.int32, (tq, LANES), 1) // SWA_HEAD_DIM
    first = lax.broadcasted_iota(jnp.int32, (2 * tq, 1), 0) < tq
    kc = k_ref[pl.ds(n_lat, n_ctx), :]
    vc = v_ref[pl.ds(n_lat, n_ctx), :]

    def pair_query(r0, col):
        qcol = q_ref[r0:r0 + tq, col * LANES:(col + 1) * LANES]
        zero = jnp.zeros_like(qcol)
        return jnp.concatenate([jnp.where(lane_half == 0, qcol, zero), jnp.where(lane_half == 1, qcol, zero)], axis=0)

    def pair_sink(col):
        base = g * SWA_GROUP + 2 * col
        return jnp.where(first, sink_ref[base], sink_ref[base + 1]) * LOG2E

    def finish(r0, col, ov, m, sink):
        o = ov[:, :LANES] / (ov[:, LANES:] + jnp.exp2(sink - m))
        o_ref[r0:r0 + tq, col * LANES:(col + 1) * LANES] = jnp.where(lane_half == 0, o[:tq], o[tq:]).astype(o_ref.dtype)

    @pl.when(qi < n_lat_tiles)
    def _():
        chains = []
        for r0 in range(0, q_ref.shape[0], tq):
            q0 = qi * q_ref.shape[0] + r0
            start = pl.multiple_of(jnp.clip(q0 - SWA_WINDOW, 0, n_lat - span), SWA_WINDOW)
            kw = k_ref[pl.ds(start, span), :]
            vw = v_ref[pl.ds(start, span), :]
            qpos = q0 + lax.broadcasted_iota(jnp.int32, (tq, 1), 0)
            kpos = start + lax.broadcasted_iota(jnp.int32, (1, span), 1)
            band = jnp.abs(kpos - qpos) <= SWA_WINDOW
            for col in range(SWA_GROUP // 2):
                q2, sink = pair_query(r0, col), pair_sink(col)
                sw = _dot_nt(q2, kw)
                sw = jnp.concatenate([jnp.where(band, sw[:tq], NEG_INF), jnp.where(band, sw[tq:], NEG_INF)], axis=0)
                sc = _dot_nt(q2, kc)
                m = jnp.maximum(jnp.maximum(jnp.max(sw, axis=-1, keepdims=True),
                                            jnp.max(sc, axis=-1, keepdims=True)), sink)
                chains.append((r0, col, jnp.exp2(sw - m).astype(BF16), jnp.exp2(sc - m).astype(BF16), vw, m, sink))
        for r0, col, pw, pc, vw, m, sink in chains:
            finish(r0, col, _dot(pw, vw) + _dot(pc, vc), m, sink)

    @pl.when(qi >= n_lat_tiles)
    def _():
        o_ref[n_ctx:, :] = jnp.zeros((o_ref.shape[0] - n_ctx, o_ref.shape[1]), o_ref.dtype)
        for col in range(SWA_GROUP // 2):
            q2, sink = pair_query(0, col), pair_sink(col)
            sc = _dot_nt(q2, kc)
            m = jnp.maximum(jnp.max(sc, axis=-1, keepdims=True), sink)
            finish(0, col, _dot(jnp.exp2(sc - m).astype(BF16), vc), m, sink)


def _swa_attn(sink, q, k, v, n_lat, n_ctx):
    b, t, nq = q.shape
    nt = t // ROW_TILE
    gw = SWA_GROUP * SWA_HEAD_DIM
    kern = functools.partial(_swa_attn_kernel, n_lat_tiles=n_lat // ROW_TILE, n_lat=n_lat, n_ctx=n_ctx)
    kv = lambda n: pl.BlockSpec((None, None, t, n), lambda bi, gi, i: (bi, gi, 0, 0))
    return pl.pallas_call(
        kern,
        out_shape=jax.ShapeDtypeStruct((b, t, nq), BF16),
        grid=(b, SWA_KV_HEADS, nt),
        in_specs=[pl.BlockSpec(memory_space=pltpu.SMEM),
                  pl.BlockSpec((None, ROW_TILE, gw), lambda bi, gi, i: (bi, i, gi)), kv(LANES), kv(2 * LANES)],
        out_specs=pl.BlockSpec((None, ROW_TILE, gw), lambda bi, gi, i: (bi, i, gi)),
        compiler_params=_cparams(3), name="swa_attn",
    )(sink, q, k, v)


def _cf_proj_kernel(o_ref, sgp_ref, x_ref, modp_ref, wout_ref, g_ref, mod_ref, w_ref, xo_ref, u_ref, sg_ref,
                    *, n_lat_tiles):
    x = _residual_in(o_ref, sgp_ref, x_ref, modp_ref, wout_ref, xo_ref)

    @pl.when(pl.program_id(1) < n_lat_tiles)
    def _():
        h = _norm_mod(x, g_ref[...], mod_ref[...]).astype(BF16)
        c = u_ref.shape[-1]
        tn = c // 2
        for c0 in range(0, c, tn):
            a = _dot(h, w_ref[:, c0:c0 + tn])
            gl = _dot(h, w_ref[:, c + c0:c + c0 + tn])
            u_ref[:, c0:c0 + tn] = a * jax.nn.sigmoid(gl)
            sg_ref[:, c0:c0 + tn] = _silu(_dot(h, w_ref[:, 2 * c + c0:2 * c + c0 + tn])).astype(sg_ref.dtype)


def _cf_proj(o, sgp, x, modp, w_out, g, mod, w, n_lat_tiles):
    b, t, d = x.shape
    c = w.shape[1] // 3
    n_lat = n_lat_tiles * ROW_TILE
    full = lambda a: _once(a.shape, lambda bi, i: (0,) * a.ndim)
    row = lambda bi, i: (bi, i, 0)
    lat_row = lambda bi, i: (bi, jnp.minimum(i, n_lat_tiles - 1), 0)
    return pl.pallas_call(
        functools.partial(_cf_proj_kernel, n_lat_tiles=n_lat_tiles),
        out_shape=(jax.ShapeDtypeStruct((b, t, d), F32),
                   jax.ShapeDtypeStruct((b, n_lat, c), F32), jax.ShapeDtypeStruct((b, n_lat, c), BF16)),
        grid=(b, t // ROW_TILE),
        in_specs=_residual_specs(o, d, n_lat_tiles) +
                 [full(g), pl.BlockSpec((None, None, 3, d), lambda bi, i: (bi, 0, 0, 0)), full(w)],
        out_specs=(pl.BlockSpec((None, ROW_TILE, d), row),
                   pl.BlockSpec((None, ROW_TILE, c), lat_row), pl.BlockSpec((None, ROW_TILE, c), lat_row)),
        compiler_params=_cparams(2), name="cf_proj",
    )(o, sgp, x, modp, w_out, g, mod, w)


def _cf_tail_kernel(up_ref, u_ref, un_ref, sg_ref, x_ref, mod_ref, dw_ref, db_ref, lg_ref, lb_ref,
                    w_ref, fg_ref, o_ref, ext_ref, win_ref):
    i = pl.program_id(1)
    nt = pl.num_programs(1)
    tm = u_ref.shape[0]
    zero = jnp.zeros(up_ref.shape, F32)
    ext_ref[0:CF_HALO, :] = jnp.where(i > 0, up_ref[...], zero)
    ext_ref[CF_HALO:CF_HALO + tm, :] = u_ref[...]
    ext_ref[CF_HALO + tm:, :] = jnp.where(i < nt - 1, un_ref[...], zero)
    pad = (CF_KERNEL - 1) // 2
    acc = jnp.zeros(u_ref.shape, F32) + db_ref[...]
    top = CF_HALO - pad + CF_KERNEL - 1
    span = tm + (top // SUBLANES) * SUBLANES
    for r in range(SUBLANES):
        taps = [k for k in range(CF_KERNEL) if (CF_HALO - pad + k) % SUBLANES == r]
        if not taps:
            continue
        if r:
            win_ref[...] = ext_ref[pl.ds(r, span), :]
        src = win_ref if r else ext_ref
        for k in taps:
            acc = acc + dw_ref[k:k + 1, :] * src[pl.ds(CF_HALO - pad + k - r, tm), :]
    mu = jnp.mean(acc, axis=-1, keepdims=True)
    xc = acc - mu
    var = jnp.mean(xc * xc, axis=-1, keepdims=True)
    ln = xc * lax.rsqrt(var + NORM_EPS) * lg_ref[...] + lb_ref[...]
    og = (_silu(ln) * sg_ref[...].astype(F32)).astype(BF16)
    xo = x_ref[...] + mod_ref[2:3, :] * _dot(og, w_ref[...])
    o_ref[...] = _rms(xo, fg_ref[...])


def _cf_tail(u, sg, x, mod, dw, db, lg, lb, w_out, fg):
    b, n_lat, c = u.shape
    d = x.shape[-1]
    nt = n_lat // ROW_TILE
    per = ROW_TILE // CF_HALO
    last_halo = n_lat // CF_HALO - 1
    full = lambda a: pl.BlockSpec(a.shape, lambda bi, i: (0,) * a.ndim)
    row = lambda bi, i: (bi, i, 0)
    return pl.pallas_call(
        _cf_tail_kernel,
        out_shape=jax.ShapeDtypeStruct((b, n_lat, d), F32),
        grid=(b, nt),
        in_specs=[pl.BlockSpec((None, CF_HALO, c), lambda bi, i: (bi, jnp.maximum(i * per - 1, 0), 0)),
                  pl.BlockSpec((None, ROW_TILE, c), row),
                  pl.BlockSpec((None, CF_HALO, c), lambda bi, i: (bi, jnp.minimum((i + 1) * per, last_halo), 0)),
                  pl.BlockSpec((None, ROW_TILE, c), row),
                  pl.BlockSpec((None, ROW_TILE, d), row),
                  pl.BlockSpec((None, None, 3, d), lambda bi, i: (bi, 0, 0, 0)),
                  full(dw), full(db), full(lg), full(lb), full(w_out), full(fg)],
        out_specs=pl.BlockSpec((None, ROW_TILE, d), row),
        scratch_shapes=[pltpu.VMEM((ROW_TILE + 2 * CF_HALO, c), F32),
                        pltpu.VMEM((ROW_TILE + 2 * CF_HALO - SUBLANES, c), F32)],
        compiler_params=_cparams(2), name="cf_tail",
    )(u, u, u, sg, x, mod, dw, db, lg, lb, w_out, fg)


def _rope_tables(n_lat, n_ctx):
    rows = n_lat // GRID_W
    row = jnp.repeat(jnp.arange(rows, dtype=F32), GRID_W)
    col = jnp.tile(jnp.arange(GRID_W, dtype=F32), rows)
    n_freq = ROPE_DIM // 4
    inv = ROPE_BASE ** (-jnp.arange(n_freq, dtype=F32) / n_freq)
    ang = jnp.concatenate([row[:, None] * inv, col[:, None] * inv], axis=-1)
    cos = jnp.concatenate([jnp.cos(ang), jnp.ones((n_ctx, ROPE_DIM // 2), F32)], axis=0)
    sin = jnp.concatenate([jnp.sin(ang), jnp.zeros((n_ctx, ROPE_DIM // 2), F32)], axis=0)
    return jnp.tile(cos, (1, 2)), jnp.tile(sin, (1, 2))


def kernel(x, c, ctx, c_ctx, norm_g, ada_w, ada_b, final_g, mla_w_in, mla_q_norm_g, mla_kv_norm_g, mla_w_uq, mla_w_ukv, mla_w_out, hy_w_in, hy_conv_w, hy_conv_b, hy_filt_w_in, hy_filt_w_hid, hy_filt_b, hy_filt_freq, hy_filt_w_out, hy_bias, hy_w_out, swa_w_in, swa_sink, swa_w_out, cf_w_in, cf_dw_w, cf_dw_b, cf_ln_g, cf_ln_b, cf_w_out):
    b, n_lat, d = x.shape
    n_ctx = ctx.shape[1]
    depth = norm_g.shape[0]
    assert depth == 4 and n_lat % ROW_TILE == 0 and n_ctx == ATTN_TILE and ROW_TILE % n_ctx == 0
    assert n_lat % (FFT_N1H * SUBLANES) == 0 and n_lat % GRID_W == 0
    n_lat_tiles = n_lat // ROW_TILE
    bf = lambda a: a.astype(BF16)

    cs = jnp.concatenate([c, c_ctx[None, :], jnp.zeros((SUBLANES - b - 1, d), F32)], axis=0)
    mods = _adaln(cs, ada_w, ada_b)[:, :b + 1].reshape(depth, b + 1, 3, d)
    mods = jnp.stack([mods[:, :b], jnp.broadcast_to(mods[:, b:], (depth, b, 3, d))], axis=2)
    cos64, sin64 = _rope_tables(n_lat, ROW_TILE)
    cos128, sin128 = jnp.tile(cos64, (1, 2)), jnp.tile(sin64, (1, 2))

    c2 = MLA_Q_RANK + MLA_KV_RANK + MLA_ROPE
    w_lat = bf(jnp.pad(mla_w_in[0][:, :c2], ((0, 0), (0, LANES - MLA_ROPE))))
    w_gate = bf(mla_w_in[0][:, c2:])
    wuq = bf(jnp.pad(mla_w_uq[0].reshape(MLA_Q_RANK, MLA_HEADS, MLA_QK), ((0, 0), (0, 0), (0, LANES - MLA_ROPE)))
             .reshape(MLA_Q_RANK, MLA_HEADS * 2 * LANES))
    q, k, v, sg = _mla_proj(x, ctx, norm_g[0:1], mods[0], w_lat, w_gate, mla_q_norm_g[0:1], mla_kv_norm_g[0:1],
                            wuq, bf(mla_w_ukv[0]), cos128, sin128, n_lat_tiles)
    o = _mla_attn(q, k, v, n_lat, n_ctx)

    xs, z0, g0, g1, sg = _hy_proj(o, sg, x, ctx, mods[0], bf(mla_w_out[0]), norm_g[1:2], mods[1], bf(hy_w_in[0]),
                                  hy_conv_w[0], hy_conv_b[0][None, :], n_lat_tiles)
    cch = z0.shape[-1]
    filt = (hy_filt_w_in[0], hy_filt_w_hid[0], hy_filt_b[0], hy_filt_freq[0], hy_filt_w_out[0])
    assert cch % HY_SLAB == 0
    mats = _fft_matrices(n_lat)
    spec = _hy_spec(_hy_filter_taps(n_lat, *filt), mats[1], mats[2])
    cfwd, cinv, nfp = _dense_dft_matrices(n_ctx)
    cspec = _hy_ctx_spec(_hy_filter_taps(n_ctx, *filt), cfwd)
    gates = (g0, g1)
    z = z0
    for order in range(HY_ORDER):
        bias = hy_bias[0][order][None, :]
        z = _hy_conv(z, gates[order], spec, order, mats, bias, n_lat)
        z = _hy_ctx_conv(z, gates[order], cspec, order, cfwd[:, :n_ctx], cinv, nfp, bias, n_lat, n_ctx)

    nq = SWA_Q_HEADS * SWA_HEAD_DIM
    nkv = SWA_KV_HEADS * SWA_HEAD_DIM
    w_in = swa_w_in[0]
    dup = lambda w: jnp.tile(w.reshape(d, SWA_KV_HEADS, 1, SWA_HEAD_DIM), (1, 1, 2, 1)).reshape(d, 2 * nkv)
    w2 = bf(jnp.concatenate([w_in[:, :nq], dup(w_in[:, nq:nq + nkv]), dup(w_in[:, nq + nkv:nq + 2 * nkv]),
                             w_in[:, nq + 2 * nkv:]], axis=1))
    xs, q, k, v, sg = _swa_proj(z, sg, xs, mods[1], bf(hy_w_out[0]), norm_g[2:3], mods[2], w2, cos128, sin128,
                                n_lat_tiles)
    o = _swa_attn(swa_sink[0], q, k, v, n_lat, n_ctx)

    xs, u, sg = _cf_proj(o, sg, xs, mods[2], bf(swa_w_out[0]), norm_g[3:4], mods[3], bf(cf_w_in[0]), n_lat_tiles)
    return _cf_tail(u, sg, xs, mods[3], cf_dw_w[0], cf_dw_b[0][None, :], cf_ln_g[0][None, :], cf_ln_b[0][None, :],
                    bf(cf_w_out[0]), final_g[None, :])
```

```python
import functools
import math

import numpy as np
import jax
import jax.numpy as jnp
from jax import lax
from jax.experimental import pallas as pl
from jax.experimental.pallas import tpu as pltpu

F32 = jnp.float32
BF16 = jnp.bfloat16

GRID_W = 64
NORM_EPS = 1e-6
NEG_INF = -1e30
LOG2E = math.log2(math.e)
ROPE_BASE = 10000.0
ROPE_DIM = 64

MLA_HEADS = 8
MLA_Q_RANK = 384
MLA_KV_RANK = 256
MLA_NOPE = 128
MLA_ROPE = ROPE_DIM
MLA_V = 128
MLA_QK = MLA_NOPE + MLA_ROPE

HY_ORDER = 2
HY_EMB = 33
HY_EMB_PAD = 64
HY_BANDS = (HY_EMB - 1) // 2
HY_DECAY_TARGET = 1e-2
HY_FAST_DECAY = 0.3
HY_SLOW_DECAY = 1.5

SWA_Q_HEADS = 16
SWA_KV_HEADS = 4
SWA_GROUP = SWA_Q_HEADS // SWA_KV_HEADS
SWA_HEAD_DIM = ROPE_DIM
SWA_WINDOW = 128

CF_KERNEL = 31
CF_HALO = 16

ROW_TILE = 512
ATTN_TILE = 256
SUBLANES = 8
LANES = 128
FFT_N1 = 64
FFT_N1H = FFT_N1 // 2
FFT_K1 = FFT_N1 // 2 + 1
HY_SLAB = 256
HY_HALO = 16
VMEM_LIMIT = 48 * 1024 * 1024
VMEM_LIMIT_BIG = 56 * 1024 * 1024


def _cparams(n_axes):
    return pltpu.CompilerParams(dimension_semantics=("arbitrary",) * n_axes,
                                vmem_limit_bytes=VMEM_LIMIT)


def _cparams_big(n_axes):
    return pltpu.CompilerParams(dimension_semantics=("arbitrary",) * n_axes,
                                vmem_limit_bytes=VMEM_LIMIT_BIG)


def _once(shape, index_map):
    return pl.BlockSpec(shape, index_map, pipeline_mode=pl.Buffered(1))


def _dot(a, b):
    return jnp.dot(a, b, preferred_element_type=F32)


def _dot_nt(a, b):
    return lax.dot_general(a, b, (((1,), (1,)), ((), ())), preferred_element_type=F32)


def _rms(x, g):
    return x * lax.rsqrt(jnp.mean(x * x, axis=-1, keepdims=True) + NORM_EPS) * g


def _norm_mod(x, g, mod):
    return _rms(x, g) * (1.0 + mod[1:2, :]) + mod[0:1, :]


def _silu(x):
    return x * jax.nn.sigmoid(x)


def _adaln_kernel(c_ref, w_ref, b_ref, o_ref):
    s = _silu(c_ref[...]).astype(BF16)
    o_ref[...] = _dot(s, w_ref[...].astype(BF16)) + b_ref[...]


def _adaln(cs, ada_w, ada_b):
    depth, d, d3 = ada_w.shape
    tn = 2 * ROW_TILE
    return pl.pallas_call(
        _adaln_kernel,
        out_shape=jax.ShapeDtypeStruct((depth, cs.shape[0], d3), F32),
        grid=(depth, d3 // tn),
        in_specs=[pl.BlockSpec(cs.shape, lambda i, j: (0, 0)),
                  pl.BlockSpec((None, d, tn), lambda i, j: (i, 0, j)),
                  pl.BlockSpec((None, 1, tn), lambda i, j: (i, 0, j))],
        out_specs=pl.BlockSpec((None, cs.shape[0], tn), lambda i, j: (i, 0, j)),
        compiler_params=_cparams(2), name="adaln",
    )(cs, ada_w, ada_b.reshape(depth, 1, d3))


def _input_tile(x_ref, ctx_ref, is_ctx):
    pad = jnp.zeros((x_ref.shape[0] - ctx_ref.shape[0], x_ref.shape[1]), F32)
    return jnp.where(is_ctx, jnp.concatenate([ctx_ref[...], pad], axis=0), x_ref[...])


def _input_specs(d, n_ctx, n_lat_tiles):
    return [pl.BlockSpec((None, ROW_TILE, d), lambda bi, i: (bi, jnp.minimum(i, n_lat_tiles - 1), 0)),
            pl.BlockSpec((None, n_ctx, d), lambda bi, i: (bi, 0, 0))]


def _residual_in(o_ref, sgp_ref, x_ref, modp_ref, wout_ref, xo_ref):
    og = (o_ref[...].astype(F32) * sgp_ref[...].astype(F32)).astype(BF16)
    x = x_ref[...] + modp_ref[2:3, :] * _dot(og, wout_ref[...])
    xo_ref[...] = x
    return x


def _residual_specs(o, d, n_lat_tiles):
    width = o.shape[-1]
    row = lambda bi, i: (bi, i, 0)
    return [pl.BlockSpec((None, ROW_TILE, width), row), pl.BlockSpec((None, ROW_TILE, width), row),
            pl.BlockSpec((None, ROW_TILE, d), row),
            pl.BlockSpec((None, None, 3, d), lambda bi, i: (bi, i // n_lat_tiles, 0, 0)),
            _once((width, d), lambda bi, i: (0, 0))]


def _rope_lanes(x, cos, sin):
    half = ROPE_DIM // 2
    lane = lax.broadcasted_iota(jnp.int32, x.shape, 1) % ROPE_DIM
    rot = jnp.where(lane < half, -pltpu.roll(x, LANES - half, 1), pltpu.roll(x, half, 1))
    return x * cos + rot * sin


def _mla_proj_kernel(x_ref, ctx_ref, g_ref, mod_ref, wl_ref, wg_ref, qg_ref, kvg_ref, wuq_ref, wukv_ref,
                     cos_ref, sin_ref, q_ref, k_ref, v_ref, sg_ref, *, n_lat_tiles):
    x = _input_tile(x_ref, ctx_ref, pl.program_id(1) >= n_lat_tiles)
    h = _norm_mod(x, g_ref[...], mod_ref[...]).astype(BF16)
    sg_ref[...] = _silu(_dot(h, wg_ref[...])).astype(sg_ref.dtype)
    res = _dot(h, wl_ref[...])
    c0, c1 = MLA_Q_RANK, MLA_Q_RANK + MLA_KV_RANK
    cqn = _rms(res[:, :c0], qg_ref[...]).astype(BF16)
    ckvn = _rms(res[:, c0:c1], kvg_ref[...]).astype(BF16)
    cos, sin = cos_ref[...], sin_ref[...]

    def rope(slab):
        return _rope_lanes(slab, cos, sin)[:, :MLA_ROPE]

    k_rope_t = rope(res[:, c1:c1 + LANES]).T.astype(k_ref.dtype)
    scale = MLA_QK ** -0.5 * LOG2E
    per = 2 * LANES
    for h0 in range(0, MLA_HEADS, 2):
        q2 = _dot(cqn, wuq_ref[:, h0 * per:(h0 + 2) * per])
        kv2 = _dot(ckvn, wukv_ref[:, h0 * per:(h0 + 2) * per])
        for j in range(2):
            hd = h0 + j
            q, kv = q2[:, j * per:(j + 1) * per], kv2[:, j * per:(j + 1) * per]
            q_ref[hd, :, :MLA_NOPE] = (q[:, :MLA_NOPE] * scale).astype(q_ref.dtype)
            q_ref[hd, :, MLA_NOPE:] = (rope(q[:, MLA_NOPE:]) * scale).astype(q_ref.dtype)
            k_ref[hd, :MLA_NOPE, :] = kv[:, :MLA_NOPE].T.astype(k_ref.dtype)
            k_ref[hd, MLA_NOPE:, :] = k_rope_t
            v_ref[hd, :, :MLA_V] = kv[:, MLA_NOPE:].astype(v_ref.dtype)
            v_ref[hd, :, MLA_V:] = jnp.ones((v_ref.shape[1], MLA_V), v_ref.dtype)


def _mla_proj(x, ctx, g, mod, wl, wg, qg, kvg, wuq, wukv, cos, sin, n_lat_tiles):
    b, n_lat, d = x.shape
    nt = n_lat_tiles + 1
    t = nt * ROW_TILE
    width = MLA_HEADS * MLA_V
    full = lambda a: _once(a.shape, lambda bi, i: (0,) * a.ndim)
    head_out = lambda n: pl.BlockSpec((None, MLA_HEADS, ROW_TILE, n), lambda bi, i: (bi, 0, i, 0))
    return pl.pallas_call(
        functools.partial(_mla_proj_kernel, n_lat_tiles=n_lat_tiles),
        out_shape=(jax.ShapeDtypeStruct((b, MLA_HEADS, t, MLA_QK), BF16),
                   jax.ShapeDtypeStruct((b, MLA_HEADS, MLA_QK, t), BF16),
                   jax.ShapeDtypeStruct((b, MLA_HEADS, t, 2 * MLA_V), BF16),
                   jax.ShapeDtypeStruct((b, t, width), BF16)),
        grid=(b, nt),
        in_specs=_input_specs(d, ctx.shape[1], n_lat_tiles) +
                 [full(g),
                  pl.BlockSpec((None, None, 3, d), lambda bi, i: (bi, i // n_lat_tiles, 0, 0)),
                  full(wl), full(wg), full(qg), full(kvg), full(wuq), full(wukv),
                  pl.BlockSpec((ROW_TILE, LANES), lambda bi, i: (i, 0)),
                  pl.BlockSpec((ROW_TILE, LANES), lambda bi, i: (i, 0))],
        out_specs=(head_out(MLA_QK),
                   pl.BlockSpec((None, MLA_HEADS, MLA_QK, ROW_TILE), lambda bi, i: (bi, 0, 0, i)),
                   head_out(2 * MLA_V),
                   pl.BlockSpec((None, ROW_TILE, width), lambda bi, i: (bi, i, 0))),
        compiler_params=_cparams(2), name="mla_proj",
    )(x, ctx, g, mod, wl, wg, qg, kvg, wuq, wukv, cos, sin)


def _attend(q, k_parts, v_parts):
    s = [_dot(q, k) for k in k_parts]
    m = functools.reduce(jnp.maximum, [jnp.max(x, axis=-1, keepdims=True) for x in s])
    ov = functools.reduce(jnp.add, [_dot(jnp.exp2(x - m).astype(BF16), v) for x, v in zip(s, v_parts)])
    n = ov.shape[-1] // 2
    return ov[:, :n] / ov[:, n:]


MLA_HEADS_PER_STEP = 2


def _mla_attn_kernel(q_ref, k_ref, v_ref, o_ref, *, n_lat_tiles, n_lat, n_ctx):
    qi = pl.program_id(2)

    n_keys = n_lat + n_ctx

    @pl.when(qi < n_lat_tiles)
    def _():
        chains = [(hd, r0) for hd in range(MLA_HEADS_PER_STEP) for r0 in range(0, q_ref.shape[1], ATTN_TILE)]
        s = [_dot(q_ref[hd, r0:r0 + ATTN_TILE, :], k_ref[hd, :, 0:n_keys]) for hd, r0 in chains]
        p = [jnp.exp2(x - jnp.max(x, axis=-1, keepdims=True)).astype(BF16) for x in s]
        for (hd, r0), pc in zip(chains, p):
            ov = _dot(pc, v_ref[hd, 0:n_keys, :])
            o = ov[:, :MLA_V] / ov[:, MLA_V:]
            o_ref[r0:r0 + ATTN_TILE, hd * MLA_V:(hd + 1) * MLA_V] = o.astype(o_ref.dtype)

    @pl.when(qi >= n_lat_tiles)
    def _():
        o_ref[n_ctx:, :] = jnp.zeros((o_ref.shape[0] - n_ctx, o_ref.shape[1]), o_ref.dtype)
        for hd in range(MLA_HEADS_PER_STEP):
            k = k_ref[hd, :, n_lat:n_keys]
            v = v_ref[hd, n_lat:n_keys, :]
            o = _attend(q_ref[hd, 0:n_ctx, :], [k], [v])
            o_ref[0:n_ctx, hd * MLA_V:(hd + 1) * MLA_V] = o.astype(o_ref.dtype)


def _mla_attn(q, k, v, n_lat, n_ctx):
    b, hds, t, _ = q.shape
    nt = t // ROW_TILE
    hps = MLA_HEADS_PER_STEP
    kern = functools.partial(_mla_attn_kernel, n_lat_tiles=n_lat // ROW_TILE, n_lat=n_lat, n_ctx=n_ctx)
    return pl.pallas_call(
        kern,
        out_shape=jax.ShapeDtypeStruct((b, t, hds * MLA_V), BF16),
        grid=(b, hds // hps, nt),
        in_specs=[pl.BlockSpec((None, hps, ROW_TILE, MLA_QK), lambda bi, h, i: (bi, h, i, 0)),
                  pl.BlockSpec((None, hps, MLA_QK, t), lambda bi, h, i: (bi, h, 0, 0)),
                  pl.BlockSpec((None, hps, t, 2 * MLA_V), lambda bi, h, i: (bi, h, 0, 0))],
        out_specs=pl.BlockSpec((None, ROW_TILE, hps * MLA_V), lambda bi, h, i: (bi, i, h)),
        compiler_params=_cparams_big(3), name="mla_attn",
    )(q, k, v)


def _hy_proj_kernel(op_ref, o_ref, on_ref, sp_ref, s_ref, sn_ref, xp_ref, x_ref, ctx_ref, xn_ref, modp_ref, wout_ref,
                    g_ref, mod_ref, w_ref, cw_ref, cb_ref, xo_ref, z_ref, g0_ref, g1_ref, sg_ref,
                    *, n_lat_tiles, n_ctx):
    i = pl.program_id(1)
    tm = x_ref.shape[0]
    c = z_ref.shape[-1]
    n = tm + 2 * SUBLANES
    halo = xp_ref.shape[0]
    cat = lambda a, m, z: jnp.concatenate([a[...], m, z[...]], axis=0)
    og = (cat(op_ref, o_ref[...], on_ref).astype(F32) * cat(sp_ref, s_ref[...], sn_ref).astype(F32)).astype(BF16)
    xe = cat(xp_ref, _input_tile(x_ref, ctx_ref, i >= n_lat_tiles), xn_ref)
    xe = xe + modp_ref[2:3, :] * _dot(og, wout_ref[...])
    xo_ref[...] = xe[halo:halo + tm]
    xa = xe[halo - SUBLANES:halo + tm + SUBLANES]
    ctx_tile = i == n_lat_tiles
    first = jnp.logical_or(i == 0, ctx_tile)
    last = jnp.logical_or(i == n_lat_tiles - 1, ctx_tile)
    rows = lax.broadcasted_iota(jnp.int32, (n, 1), 0)
    end = jnp.where(ctx_tile, n_ctx + SUBLANES, jnp.where(last, tm + SUBLANES, n))
    valid = jnp.logical_and(jnp.logical_or(rows >= SUBLANES, jnp.logical_not(first)), rows < end)
    hf = jnp.where(valid, _norm_mod(xa, g_ref[...], mod_ref[...]), 0.0)
    h = hf.astype(BF16)
    outs = (z_ref, g0_ref, g1_ref)
    for j in range(len(outs)):
        u = _dot(h, w_ref[:, j * c:(j + 1) * c])
        cw = cw_ref[:, j * c:(j + 1) * c]
        cv = cw[0:1, :] * pltpu.roll(u, 1, 0) + cw[1:2, :] * u + cw[2:3, :] * pltpu.roll(u, n - 1, 0)
        outs[j][...] = cv[SUBLANES:SUBLANES + tm] + cb_ref[:, j * c:(j + 1) * c]
    h_mid = hf[SUBLANES:SUBLANES + tm].astype(BF16)
    sg_ref[...] = _silu(_dot(h_mid, w_ref[:, len(outs) * c:])).astype(sg_ref.dtype)


def _hy_proj(o, sgp, x, ctx, modp, w_out, g, mod, w, cw, cb, n_lat_tiles):
    b, t, width = o.shape
    n_lat, d = x.shape[1:]
    n_ctx = ctx.shape[1]
    c = w.shape[1] // (HY_ORDER + 2)
    per = ROW_TILE // HY_HALO
    last = n_lat // HY_HALO - 1
    full = lambda a: _once(a.shape, lambda bi, i: (0,) * a.ndim)
    row = lambda bi, i: (bi, i, 0)
    prev = lambda bi, i: (bi, jnp.maximum(i * per - 1, 0), 0)
    nxt = lambda bi, i: (bi, jnp.minimum((i + 1) * per, last), 0)
    x_main, x_ctx = _input_specs(d, n_ctx, n_lat_tiles)
    kern = functools.partial(_hy_proj_kernel, n_lat_tiles=n_lat_tiles, n_ctx=n_ctx)
    halo = lambda n, m: pl.BlockSpec((None, HY_HALO, n), m)
    return pl.pallas_call(
        kern,
        out_shape=(jax.ShapeDtypeStruct((b, t, d), F32),) + (jax.ShapeDtypeStruct((b, t, c), F32),) * 3 +
                  (jax.ShapeDtypeStruct((b, t, c), BF16),),
        grid=(b, t // ROW_TILE),
        in_specs=[halo(width, prev), pl.BlockSpec((None, ROW_TILE, width), row), halo(width, nxt),
                  halo(width, prev), pl.BlockSpec((None, ROW_TILE, width), row), halo(width, nxt),
                  halo(d, prev), x_main, x_ctx, halo(d, nxt),
                  pl.BlockSpec((None, None, 3, d), lambda bi, i: (bi, i // n_lat_tiles, 0, 0)),
                  full(w_out), full(g),
                  pl.BlockSpec((None, None, 3, d), lambda bi, i: (bi, i // n_lat_tiles, 0, 0)),
                  full(w), full(cw), full(cb)],
        out_specs=(pl.BlockSpec((None, ROW_TILE, d), row),) + (pl.BlockSpec((None, ROW_TILE, c), row),) * 4,
        compiler_params=_cparams(2), name="hy_proj",
    )(o, o, o, sgp, sgp, sgp, x, x, ctx, x, modp, w_out, g, mod, w, cw, cb)


def _hy_filter_kernel(ef_ref, eb_ref, w0_ref, wh_ref, b_ref, fr_ref, wf_ref, wb_ref, dl_ref, o_ref):
    i = pl.program_id(0)
    c = dl_ref.shape[-1]
    half = ef_ref.shape[0]

    def taps(e, wo):
        hdn = e
        ws = (w0_ref[...], wh_ref[0], wh_ref[1])
        for k in range(3):
            hdn = jnp.sin(fr_ref[k:k + 1, :] * (_dot(hdn.astype(BF16), ws[k].astype(BF16)) + b_ref[k:k + 1, :]))
        hh = _dot(hdn.astype(BF16), wo.astype(BF16))
        out = []
        for part in range(2):
            decay = jnp.exp(-e[:, part * HY_EMB_PAD:part * HY_EMB_PAD + 1] * dl_ref[...])
            out.append([hh[:, (part * HY_ORDER + o) * c:(part * HY_ORDER + o + 1) * c] * decay
                        for o in range(HY_ORDER)])
        return out

    hf = taps(ef_ref[...], wf_ref[...])
    hb = taps(eb_ref[...], wb_ref[...])
    rows = lax.broadcasted_iota(jnp.int32, (half, 1), 0)
    lag0 = jnp.logical_and(rows == 0, i == 0)
    for o in range(HY_ORDER):
        o_ref[o, 0, 0:half] = hf[0][o] + jnp.where(lag0, hb[0][o], 0.0)
        o_ref[o, 1, 0:half] = jnp.where(lag0, 0.0, hb[0][o])
        o_ref[o, 0, half:] = hf[1][o]
        o_ref[o, 1, half:] = hb[1][o]


def _hy_position_features(n):
    t = jnp.linspace(0.0, 1.0, n, dtype=F32)[:, None]
    wpos = (2.0 * math.pi / n) * jnp.arange(n, dtype=F32)[:, None]
    bands = jnp.linspace(1e-4, HY_BANDS - 1, HY_BANDS, dtype=F32)[None, :]
    hdn = jnp.concatenate([t, jnp.cos(bands * wpos), -jnp.sin(bands * wpos)], axis=-1)
    return jnp.pad(hdn, ((0, 0), (0, HY_EMB_PAD - HY_EMB)))


def _hy_filter_taps(n, w_in, w_hid, bias, freq, w_out):
    c = w_out.shape[1] // (2 * HY_ORDER)
    ef = _hy_position_features(n)
    rev = (n - jnp.arange(n)) % n
    eb = ef[rev]
    w0 = jnp.pad(w_in, ((0, HY_EMB_PAD - HY_EMB), (0, 0)))
    wo = w_out.reshape(w_out.shape[0], HY_ORDER, 2, c)
    wf = wo[:, :, 0, :].reshape(w_out.shape[0], HY_ORDER * c)
    wb = wo[:, :, 1, :].reshape(w_out.shape[0], HY_ORDER * c)
    deltas = jnp.abs(jnp.linspace(math.log(HY_DECAY_TARGET) / HY_FAST_DECAY,
                                  math.log(HY_DECAY_TARGET) / HY_SLOW_DECAY, c, dtype=F32))[None, :]
    tl = min(n, ROW_TILE)
    half = tl // 2
    pair = lambda e: e.reshape(n // tl, 2, half, HY_EMB_PAD).transpose(0, 2, 1, 3).reshape(n // 2, 2 * HY_EMB_PAD)
    both = lambda w: jnp.kron(jnp.eye(2, dtype=F32), w)
    twice = lambda v: jnp.tile(v, (1, 2))
    w_hid2 = jnp.stack([both(w_hid[0]), both(w_hid[1])])
    full = lambda a: pl.BlockSpec(a.shape, lambda i: (0,) * a.ndim)
    args = (pair(ef), pair(eb), both(w0), w_hid2, twice(bias), twice(freq), both(wf), both(wb), deltas)
    return pl.pallas_call(
        _hy_filter_kernel,
        out_shape=jax.ShapeDtypeStruct((HY_ORDER, 2, n, c), F32),
        grid=(n // tl,),
        in_specs=[pl.BlockSpec((half, 2 * HY_EMB_PAD), lambda i: (i, 0)),
                  pl.BlockSpec((half, 2 * HY_EMB_PAD), lambda i: (i, 0))] + [full(a) for a in args[2:]],
        out_specs=pl.BlockSpec((HY_ORDER, 2, tl, c), lambda i: (0, 0, i, 0)),
        compiler_params=_cparams(1), name="hy_filter",
    )(*args)


def _fft_matrices(n_lat):
    n = 2 * n_lat
    n2 = n_lat // FFT_N1H
    k1 = np.arange(FFT_K1)
    n1 = np.arange(FFT_N1H)
    ang = 2.0 * np.pi * np.outer(k1, n1) / FFT_N1
    eye = np.eye(SUBLANES)
    f1 = np.kron(np.concatenate([np.cos(ang), -np.sin(ang)], axis=0), eye)
    wgt = np.full(FFT_K1, 2.0)
    wgt[0] = wgt[-1] = 1.0
    g1 = np.kron(np.concatenate([wgt * np.cos(ang.T), (-wgt * np.sin(ang.T))[:, 1:-1]], axis=1) / n, eye)
    idx = np.arange(n2)
    f2 = np.zeros((FFT_K1, 2 * n2, 2 * n2))
    f2i = np.zeros((FFT_K1, 2 * n2, 2 * n2))
    for a in range(FFT_K1):
        ph = 2.0 * np.pi * (np.outer(idx, idx) / n2 + (a * idx)[None, :] / n)
        f2[a] = np.block([[np.cos(ph), np.sin(ph)], [-np.sin(ph), np.cos(ph)]])
        f2i[a] = np.block([[np.cos(ph.T), -np.sin(ph.T)], [np.sin(ph.T), np.cos(ph.T)]])
    sign = np.kron(np.tile((-1.0) ** k1, 2), np.ones(SUBLANES))[:, None]
    f1s = np.concatenate([f1, sign * f1], axis=1)
    as_bf16 = lambda m: jnp.asarray(m.astype(np.float32)).astype(BF16)
    return as_bf16(f1), as_bf16(f1s), as_bf16(f2), as_bf16(f2i), as_bf16(g1)


def _dense_dft_matrices(n_ctx):
    n = 2 * n_ctx
    nf = n_ctx + 1
    nfp = -(-nf // SUBLANES) * SUBLANES
    k = np.arange(nf)
    ang = 2.0 * np.pi * np.outer(k, np.arange(n)) / n
    fwd = np.zeros((2 * nfp, n))
    fwd[:nf] = np.cos(ang)
    fwd[nfp:nfp + nf] = -np.sin(ang)
    wgt = np.full(nf, 2.0)
    wgt[0] = wgt[-1] = 1.0
    angi = 2.0 * np.pi * np.outer(np.arange(n_ctx), k) / n
    inv = np.zeros((n_ctx, 2 * nfp))
    inv[:, :nf] = wgt * np.cos(angi) / n
    inv[:, nfp:nfp + nf] = -wgt * np.sin(angi) / n
    as_bf16 = lambda m: jnp.asarray(m.astype(np.float32)).astype(BF16)
    return as_bf16(fwd), as_bf16(inv), nfp


def _hy_spec_kernel(t_ref, f1s_ref, f2_ref, h_ref, a_ref):
    _, n1h, nj, s, tc = t_ref.shape
    n2 = nj * s
    for j in range(nj):
        sig = jnp.concatenate([t_ref[0, :, j].reshape(n1h * s, tc), t_ref[1, :, j].reshape(n1h * s, tc)], axis=0)
        a_ref[:, :, j] = _dot(f1s_ref[...], sig.astype(BF16)).reshape(2, FFT_K1, s, tc)
    for k in range(FFT_K1):
        a = a_ref[:, k].reshape(2 * n2, tc).astype(BF16)
        h_ref[:, k] = _dot(f2_ref[k], a).reshape(2, n2, tc).astype(h_ref.dtype)


def _hy_spec(taps, f1s, f2):
    _, _, n_lat, c = taps.shape
    n2 = n_lat // FFT_N1H
    nj = n2 // SUBLANES
    tc = HY_SLAB
    return pl.pallas_call(
        _hy_spec_kernel,
        out_shape=jax.ShapeDtypeStruct((HY_ORDER, 2, FFT_K1, n2, c), BF16),
        grid=(c // tc, HY_ORDER),
        in_specs=[pl.BlockSpec((None, 2, FFT_N1H, nj, SUBLANES, tc), lambda ci, o: (o, 0, 0, 0, 0, ci)),
                  _once(f1s.shape, lambda ci, o: (0, 0)),
                  _once(f2.shape, lambda ci, o: (0, 0, 0))],
        out_specs=pl.BlockSpec((None, 2, FFT_K1, n2, tc), lambda ci, o: (o, 0, 0, 0, ci)),
        scratch_shapes=[pltpu.VMEM((2, FFT_K1, nj, SUBLANES, tc), F32)],
        compiler_params=_cparams_big(2), name="hy_spec",
    )(taps.reshape(HY_ORDER, 2, FFT_N1H, nj, SUBLANES, c), f1s, f2)


def _hy_conv_kernel(z_ref, gt_ref, h_ref, f1_ref, f2_ref, f2i_ref, g1_ref, bias_ref, o_ref, a_ref):
    n1h, nj, s, tc = z_ref.shape
    n2 = nj * s
    for j in range(nj):
        zz = z_ref[:, j].reshape(n1h * s, tc).astype(BF16)
        a_ref[:, :, j] = _dot(f1_ref[...], zz).reshape(2, FFT_K1, s, tc)
    for k in range(FFT_K1):
        x = _dot(f2_ref[k], a_ref[:, k].reshape(2 * n2, tc).astype(BF16))
        xr, xi = x[:n2], x[n2:]
        hr, hi = h_ref[0, k].astype(F32), h_ref[1, k].astype(F32)
        y = jnp.concatenate([xr * hr - xi * hi, xr * hi + xi * hr], axis=0).astype(BF16)
        a_ref[:, k] = _dot(f2i_ref[k], y).reshape(2, nj, s, tc)
    for j in range(nj):
        bp = a_ref[:, :, j].reshape(2 * FFT_K1 * s, tc)
        bp = jnp.concatenate([bp[:FFT_K1 * s], bp[(FFT_K1 + 1) * s:(2 * FFT_K1 - 1) * s]], axis=0)
        y = _dot(g1_ref[...], bp.astype(BF16)).reshape(n1h, s, tc)
        o_ref[:, j] = gt_ref[:, j] * (y + z_ref[:, j] * bias_ref[...])


def _hy_conv(z, gate, h, order, mats, bias, n_lat):
    f1, _, f2, f2i, g1 = mats
    bz, t, c = z.shape
    n2 = n_lat // FFT_N1H
    nj = n2 // SUBLANES
    tc = HY_SLAB
    view = (bz, t // n2, nj, SUBLANES, c)
    slab = pl.BlockSpec((None, FFT_N1H, nj, SUBLANES, tc), lambda ci, bi: (bi, 0, 0, 0, ci))
    out = pl.pallas_call(
        _hy_conv_kernel,
        out_shape=jax.ShapeDtypeStruct(view, F32),
        grid=(c // tc, bz),
        in_specs=[slab, slab,
                  _once((None, 2, FFT_K1, n2, tc), lambda ci, bi: (order, 0, 0, 0, ci)),
                  _once(f1.shape, lambda ci, bi: (0, 0)),
                  _once(f2.shape, lambda ci, bi: (0, 0, 0)),
                  _once(f2i.shape, lambda ci, bi: (0, 0, 0)),
                  _once(g1.shape, lambda ci, bi: (0, 0)),
                  pl.BlockSpec((1, tc), lambda ci, bi: (0, ci))],
        out_specs=slab,
        input_output_aliases={0: 0},
        scratch_shapes=[pltpu.VMEM((2, FFT_K1, nj, SUBLANES, tc), F32)],
        compiler_params=_cparams_big(2), name="hy_conv",
    )(z.reshape(view), gate.reshape(view), h, f1, f2, f2i, g1, bias)
    return out.reshape(bz, t, c)


def _hy_ctx_spec_kernel(t_ref, f_ref, h_ref):
    sig = jnp.concatenate([t_ref[0], t_ref[1]], axis=0).astype(BF16)
    h_ref[...] = _dot(f_ref[...], sig)


def _hy_ctx_spec(taps, fwd):
    _, _, n_ctx, c = taps.shape
    return pl.pallas_call(
        _hy_ctx_spec_kernel,
        out_shape=jax.ShapeDtypeStruct((HY_ORDER, fwd.shape[0], c), F32),
        grid=(HY_ORDER,),
        in_specs=[pl.BlockSpec((None, 2, n_ctx, c), lambda o: (o, 0, 0, 0)),
                  pl.BlockSpec(fwd.shape, lambda o: (0, 0))],
        out_specs=pl.BlockSpec((None, fwd.shape[0], c), lambda o: (o, 0, 0)),
        compiler_params=_cparams(1), name="hy_ctx_spec",
    )(taps, fwd)


def _hy_ctx_conv_kernel(zin_ref, gt_ref, h_ref, f_ref, fi_ref, bias_ref, o_ref, *, nfp):
    z = zin_ref[...]
    x = _dot(f_ref[...], z.astype(BF16))
    xr, xi = x[:nfp], x[nfp:]
    hr, hi = h_ref[:nfp, :], h_ref[nfp:, :]
    y = jnp.concatenate([xr * hr - xi * hi, xr * hi + xi * hr], axis=0).astype(BF16)
    n_ctx = z.shape[0]
    o_ref[0:n_ctx, :] = gt_ref[...] * (_dot(fi_ref[...], y) + z * bias_ref[...])
    o_ref[n_ctx:, :] = jnp.zeros((o_ref.shape[0] - n_ctx, o_ref.shape[1]), o_ref.dtype)


def _hy_ctx_conv(z, gate, h, order, fwd_half, inv, nfp, bias, n_lat, n_ctx):
    bz, t, c = z.shape
    blk = pl.BlockSpec((None, n_ctx, c), lambda bi: (bi, n_lat // n_ctx, 0))
    kern = functools.partial(_hy_ctx_conv_kernel, nfp=nfp)
    return pl.pallas_call(
        kern,
        out_shape=jax.ShapeDtypeStruct(z.shape, F32),
        grid=(bz,),
        in_specs=[blk, blk,
                  pl.BlockSpec((None, 2 * nfp, c), lambda bi: (order, 0, 0)),
                  pl.BlockSpec(fwd_half.shape, lambda bi: (0, 0)),
                  pl.BlockSpec(inv.shape, lambda bi: (0, 0)),
                  pl.BlockSpec((1, c), lambda bi: (0, 0))],
        out_specs=pl.BlockSpec((None, t - n_lat, c), lambda bi: (bi, n_lat // (t - n_lat), 0)),
        input_output_aliases={0: 0},
        compiler_params=_cparams(1), name="hy_ctx_conv",
    )(z, gate, h, fwd_half, inv, bias)


def _swa_proj_kernel(o_ref, sgp_ref, x_ref, modp_ref, wout_ref, g_ref, mod_ref, w_ref, cos_ref, sin_ref,
                     xo_ref, q_ref, k_ref, v_ref, sg_ref):
    x = _residual_in(o_ref, sgp_ref, x_ref, modp_ref, wout_ref, xo_ref)
    h = _norm_mod(x, g_ref[...], mod_ref[...]).astype(BF16)
    cos, sin = cos_ref[...], sin_ref[...]
    nq = q_ref.shape[-1]
    nk = SWA_KV_HEADS * LANES
    scale = SWA_HEAD_DIM ** -0.5 * LOG2E
    for c0 in range(0, nq, nk):
        res = _dot(h, w_ref[:, c0:c0 + nk])
        for cb in range(nk // LANES):
            sl = slice(cb * LANES, (cb + 1) * LANES)
            q_ref[:, c0 + cb * LANES:c0 + (cb + 1) * LANES] = (_rope_lanes(res[:, sl], cos, sin) * scale).astype(q_ref.dtype)
    res = _dot(h, w_ref[:, nq:nq + nk])
    for hd in range(SWA_KV_HEADS):
        k_ref[hd] = _rope_lanes(res[:, hd * LANES:(hd + 1) * LANES], cos, sin).astype(k_ref.dtype)
    res = _dot(h, w_ref[:, nq + nk:nq + 2 * nk])
    for hd in range(SWA_KV_HEADS):
        v_ref[hd, :, :LANES] = res[:, hd * LANES:(hd + 1) * LANES].astype(v_ref.dtype)
        v_ref[hd, :, LANES:] = jnp.ones((v_ref.shape[1], LANES), v_ref.dtype)
    for c0 in range(0, nq, nk):
        sg_ref[:, c0:c0 + nk] = _silu(_dot(h, w_ref[:, nq + 2 * nk + c0:nq + 2 * nk + c0 + nk])).astype(sg_ref.dtype)


def _swa_proj(o, sgp, x, modp, w_out, g, mod, w, cos, sin, n_lat_tiles):
    b, t, d = x.shape
    nt = t // ROW_TILE
    nq = SWA_Q_HEADS * SWA_HEAD_DIM
    full = lambda a: _once(a.shape, lambda bi, i: (0,) * a.ndim)
    row = lambda bi, i: (bi, i, 0)
    kv = lambda n: pl.BlockSpec((None, SWA_KV_HEADS, ROW_TILE, n), lambda bi, i: (bi, 0, i, 0))
    return pl.pallas_call(
        _swa_proj_kernel,
        out_shape=(jax.ShapeDtypeStruct((b, t, d), F32),
                   jax.ShapeDtypeStruct((b, t, nq), BF16),
                   jax.ShapeDtypeStruct((b, SWA_KV_HEADS, t, LANES), BF16),
                   jax.ShapeDtypeStruct((b, SWA_KV_HEADS, t, 2 * LANES), BF16),
                   jax.ShapeDtypeStruct((b, t, nq), BF16)),
        grid=(b, nt),
        in_specs=_residual_specs(o, d, n_lat_tiles) +
                 [full(g),
                  pl.BlockSpec((None, None, 3, d), lambda bi, i: (bi, i // n_lat_tiles, 0, 0)),
                  full(w),
                  pl.BlockSpec((ROW_TILE, LANES), lambda bi, i: (i, 0)),
                  pl.BlockSpec((ROW_TILE, LANES), lambda bi, i: (i, 0))],
        out_specs=(pl.BlockSpec((None, ROW_TILE, d), row), pl.BlockSpec((None, ROW_TILE, nq), row),
                   kv(LANES), kv(2 * LANES), pl.BlockSpec((None, ROW_TILE, nq), row)),
        compiler_params=_cparams(2), name="swa_proj",
    )(o, sgp, x, modp, w_out, g, mod, w, cos, sin)


def _swa_attn_kernel(sink_ref, q_ref, k_ref, v_ref, o_ref, *, n_lat_tiles, n_lat, n_ctx):
    g = pl.program_id(1)
    qi = pl.program_id(2)
    tq = SWA_WINDOW
    span = tq + 2 * SWA_WINDOW
    lane_half = lax.broadcasted_iota(jnp.int32, (tq, LANES), 1) // SWA_HEAD_DIM
    first = lax.broadcasted_iota(jnp.int32, (2 * tq, 1), 0) < tq
    kc = k_ref[pl.ds(n_lat, n_ctx), :]
    vc = v_ref[pl.ds(n_lat, n_ctx), :]

    def pair_query(r0, col):
        qcol = q_ref[r0:r0 + tq, col * LANES:(col + 1) * LANES]
        zero = jnp.zeros_like(qcol)
        return jnp.concatenate([jnp.where(lane_half == 0, qcol, zero), jnp.where(lane_half == 1, qcol, zero)], axis=0)

    def pair_sink(col):
        base = g * SWA_GROUP + 2 * col
        return jnp.where(first, sink_ref[base], sink_ref[base + 1]) * LOG2E

    def finish(r0, col, ov, m, sink):
        o = ov[:, :LANES] / (ov[:, LANES:] + jnp.exp2(sink - m))
        o_ref[r0:r0 + tq, col * LANES:(col + 1) * LANES] = jnp.where(lane_half == 0, o[:tq], o[tq:]).astype(o_ref.dtype)

    @pl.when(qi < n_lat_tiles)
    def _():
        chains = []
        for r0 in range(0, q_ref.shape[0], tq):
            q0 = qi * q_ref.shape[0] + r0
            start = pl.multiple_of(jnp.clip(q0 - SWA_WINDOW, 0, n_lat - span), SWA_WINDOW)
            kw = k_ref[pl.ds(start, span), :]
            vw = v_ref[pl.ds(start, span), :]
            qpos = q0 + lax.broadcasted_iota(jnp.int32, (tq, 1), 0)
            kpos = start + lax.broadcasted_iota(jnp.int32, (1, span), 1)
            band = jnp.abs(kpos - qpos) <= SWA_WINDOW
            for col in range(SWA_GROUP // 2):
                q2, sink = pair_query(r0, col), pair_sink(col)
                sw = _dot_nt(q2, kw)
                sw = jnp.concatenate([jnp.where(band, sw[:tq], NEG_INF), jnp.where(band, sw[tq:], NEG_INF)], axis=0)
                sc = _dot_nt(q2, kc)
                m = jnp.maximum(jnp.maximum(jnp.max(sw, axis=-1, keepdims=True),
                                            jnp.max(sc, axis=-1, keepdims=True)), sink)
                chains.append((r0, col, jnp.exp2(sw - m).astype(BF16), jnp.exp2(sc - m).astype(BF16), vw, m, sink))
        for r0, col, pw, pc, vw, m, sink in chains:
            finish(r0, col, _dot(pw, vw) + _dot(pc, vc), m, sink)

    @pl.when(qi >= n_lat_tiles)
    def _():
        o_ref[n_ctx:, :] = jnp.zeros((o_ref.shape[0] - n_ctx, o_ref.shape[1]), o_ref.dtype)
        for r0 in range(0, n_ctx, tq):
            for col in range(SWA_GROUP // 2):
                q2, sink = pair_query(r0, col), pair_sink(col)
                sc = _dot_nt(q2, kc)
                m = jnp.maximum(jnp.max(sc, axis=-1, keepdims=True), sink)
                finish(r0, col, _dot(jnp.exp2(sc - m).astype(BF16), vc), m, sink)


def _swa_attn(sink, q, k, v, n_lat, n_ctx):
    b, t, nq = q.shape
    nt = t // ROW_TILE
    gw = SWA_GROUP * SWA_HEAD_DIM
    kern = functools.partial(_swa_attn_kernel, n_lat_tiles=n_lat // ROW_TILE, n_lat=n_lat, n_ctx=n_ctx)
    kv = lambda n: pl.BlockSpec((None, None, t, n), lambda bi, gi, i: (bi, gi, 0, 0))
    return pl.pallas_call(
        kern,
        out_shape=jax.ShapeDtypeStruct((b, t, nq), BF16),
        grid=(b, SWA_KV_HEADS, nt),
        in_specs=[pl.BlockSpec(memory_space=pltpu.SMEM),
                  pl.BlockSpec((None, ROW_TILE, gw), lambda bi, gi, i: (bi, i, gi)), kv(LANES), kv(2 * LANES)],
        out_specs=pl.BlockSpec((None, ROW_TILE, gw), lambda bi, gi, i: (bi, i, gi)),
        compiler_params=_cparams(3), name="swa_attn",
    )(sink, q, k, v)


def _cf_proj_kernel(o_ref, sgp_ref, x_ref, modp_ref, wout_ref, g_ref, mod_ref, w_ref, xo_ref, u_ref, sg_ref,
                    *, n_lat_tiles):
    x = _residual_in(o_ref, sgp_ref, x_ref, modp_ref, wout_ref, xo_ref)

    @pl.when(pl.program_id(1) < n_lat_tiles)
    def _():
        h = _norm_mod(x, g_ref[...], mod_ref[...]).astype(BF16)
        c = u_ref.shape[-1]
        tn = c // 2
        for c0 in range(0, c, tn):
            a = _dot(h, w_ref[:, c0:c0 + tn])
            gl = _dot(h, w_ref[:, c + c0:c + c0 + tn])
            u_ref[:, c0:c0 + tn] = a * jax.nn.sigmoid(gl)
            sg_ref[:, c0:c0 + tn] = _silu(_dot(h, w_ref[:, 2 * c + c0:2 * c + c0 + tn])).astype(sg_ref.dtype)


def _cf_proj(o, sgp, x, modp, w_out, g, mod, w, n_lat_tiles):
    b, t, d = x.shape
    c = w.shape[1] // 3
    n_lat = n_lat_tiles * ROW_TILE
    full = lambda a: _once(a.shape, lambda bi, i: (0,) * a.ndim)
    row = lambda bi, i: (bi, i, 0)
    lat_row = lambda bi, i: (bi, jnp.minimum(i, n_lat_tiles - 1), 0)
    return pl.pallas_call(
        functools.partial(_cf_proj_kernel, n_lat_tiles=n_lat_tiles),
        out_shape=(jax.ShapeDtypeStruct((b, t, d), F32),
                   jax.ShapeDtypeStruct((b, n_lat, c), F32), jax.ShapeDtypeStruct((b, n_lat, c), BF16)),
        grid=(b, t // ROW_TILE),
        in_specs=_residual_specs(o, d, n_lat_tiles) +
                 [full(g), pl.BlockSpec((None, None, 3, d), lambda bi, i: (bi, 0, 0, 0)), full(w)],
        out_specs=(pl.BlockSpec((None, ROW_TILE, d), row),
                   pl.BlockSpec((None, ROW_TILE, c), lat_row), pl.BlockSpec((None, ROW_TILE, c), lat_row)),
        compiler_params=_cparams(2), name="cf_proj",
    )(o, sgp, x, modp, w_out, g, mod, w)


def _cf_tail_kernel(up_ref, u_ref, un_ref, sg_ref, x_ref, mod_ref, dw_ref, db_ref, lg_ref, lb_ref,
                    w_ref, fg_ref, o_ref, ext_ref, win_ref):
    i = pl.program_id(1)
    nt = pl.num_programs(1)
    tm = u_ref.shape[0]
    zero = jnp.zeros(up_ref.shape, F32)
    ext_ref[0:CF_HALO, :] = jnp.where(i > 0, up_ref[...], zero)
    ext_ref[CF_HALO:CF_HALO + tm, :] = u_ref[...]
    ext_ref[CF_HALO + tm:, :] = jnp.where(i < nt - 1, un_ref[...], zero)
    pad = (CF_KERNEL - 1) // 2
    acc = jnp.zeros(u_ref.shape, F32) + db_ref[...]
    top = CF_HALO - pad + CF_KERNEL - 1
    span = tm + (top // SUBLANES) * SUBLANES
    for r in range(SUBLANES):
        taps = [k for k in range(CF_KERNEL) if (CF_HALO - pad + k) % SUBLANES == r]
        if not taps:
            continue
        if r:
            win_ref[...] = ext_ref[pl.ds(r, span), :]
        src = win_ref if r else ext_ref
        for k in taps:
            acc = acc + dw_ref[k:k + 1, :] * src[pl.ds(CF_HALO - pad + k - r, tm), :]
    mu = jnp.mean(acc, axis=-1, keepdims=True)
    xc = acc - mu
    var = jnp.mean(xc * xc, axis=-1, keepdims=True)
    ln = xc * lax.rsqrt(var + NORM_EPS) * lg_ref[...] + lb_ref[...]
    og = (_silu(ln) * sg_ref[...].astype(F32)).astype(BF16)
    xo = x_ref[...] + mod_ref[2:3, :] * _dot(og, w_ref[...])
    o_ref[...] = _rms(xo, fg_ref[...])


def _cf_tail(u, sg, x, mod, dw, db, lg, lb, w_out, fg):
    b, n_lat, c = u.shape
    d = x.shape[-1]
    nt = n_lat // ROW_TILE
    per = ROW_TILE // CF_HALO
    last_halo = n_lat // CF_HALO - 1
    full = lambda a: pl.BlockSpec(a.shape, lambda bi, i: (0,) * a.ndim)
    row = lambda bi, i: (bi, i, 0)
    return pl.pallas_call(
        _cf_tail_kernel,
        out_shape=jax.ShapeDtypeStruct((b, n_lat, d), F32),
        grid=(b, nt),
        in_specs=[pl.BlockSpec((None, CF_HALO, c), lambda bi, i: (bi, jnp.maximum(i * per - 1, 0), 0)),
                  pl.BlockSpec((None, ROW_TILE, c), row),
                  pl.BlockSpec((None, CF_HALO, c), lambda bi, i: (bi, jnp.minimum((i + 1) * per, last_halo), 0)),
                  pl.BlockSpec((None, ROW_TILE, c), row),
                  pl.BlockSpec((None, ROW_TILE, d), row),
                  pl.BlockSpec((None, None, 3, d), lambda bi, i: (bi, 0, 0, 0)),
                  full(dw), full(db), full(lg), full(lb), full(w_out), full(fg)],
        out_specs=pl.BlockSpec((None, ROW_TILE, d), row),
        scratch_shapes=[pltpu.VMEM((ROW_TILE + 2 * CF_HALO, c), F32),
                        pltpu.VMEM((ROW_TILE + 2 * CF_HALO - SUBLANES, c), F32)],
        compiler_params=_cparams(2), name="cf_tail",
    )(u, u, u, sg, x, mod, dw, db, lg, lb, w_out, fg)


def _rope_tables(n_lat, n_ctx):
    rows = n_lat // GRID_W
    row = jnp.repeat(jnp.arange(rows, dtype=F32), GRID_W)
    col = jnp.tile(jnp.arange(GRID_W, dtype=F32), rows)
    n_freq = ROPE_DIM // 4
    inv = ROPE_BASE ** (-jnp.arange(n_freq, dtype=F32) / n_freq)
    ang = jnp.concatenate([row[:, None] * inv, col[:, None] * inv], axis=-1)
    cos = jnp.concatenate([jnp.cos(ang), jnp.ones((n_ctx, ROPE_DIM // 2), F32)], axis=0)
    sin = jnp.concatenate([jnp.sin(ang), jnp.zeros((n_ctx, ROPE_DIM // 2), F32)], axis=0)
    return jnp.tile(cos, (1, 2)), jnp.tile(sin, (1, 2))


def kernel(x, c, ctx, c_ctx, norm_g, ada_w, ada_b, final_g, mla_w_in, mla_q_norm_g, mla_kv_norm_g, mla_w_uq, mla_w_ukv, mla_w_out, hy_w_in, hy_conv_w, hy_conv_b, hy_filt_w_in, hy_filt_w_hid, hy_filt_b, hy_filt_freq, hy_filt_w_out, hy_bias, hy_w_out, swa_w_in, swa_sink, swa_w_out, cf_w_in, cf_dw_w, cf_dw_b, cf_ln_g, cf_ln_b, cf_w_out):
    b, n_lat, d = x.shape
    n_ctx = ctx.shape[1]
    depth = norm_g.shape[0]
    assert depth == 4 and n_lat % ROW_TILE == 0 and n_ctx == ATTN_TILE and ROW_TILE % n_ctx == 0
    assert n_lat % (FFT_N1H * SUBLANES) == 0 and n_lat % GRID_W == 0
    n_lat_tiles = n_lat // ROW_TILE
    bf = lambda a: a.astype(BF16)

    cs = jnp.concatenate([c, c_ctx[None, :], jnp.zeros((SUBLANES - b - 1, d), F32)], axis=0)
    mods = _adaln(cs, ada_w, ada_b)[:, :b + 1].reshape(depth, b + 1, 3, d)
    mods = jnp.stack([mods[:, :b], jnp.broadcast_to(mods[:, b:], (depth, b, 3, d))], axis=2)
    cos64, sin64 = _rope_tables(n_lat, ROW_TILE)
    cos128, sin128 = jnp.tile(cos64, (1, 2)), jnp.tile(sin64, (1, 2))

    c2 = MLA_Q_RANK + MLA_KV_RANK + MLA_ROPE
    w_lat = bf(jnp.pad(mla_w_in[0][:, :c2], ((0, 0), (0, LANES - MLA_ROPE))))
    w_gate = bf(mla_w_in[0][:, c2:])
    wuq = bf(jnp.pad(mla_w_uq[0].reshape(MLA_Q_RANK, MLA_HEADS, MLA_QK), ((0, 0), (0, 0), (0, LANES - MLA_ROPE)))
             .reshape(MLA_Q_RANK, MLA_HEADS * 2 * LANES))
    q, k, v, sg = _mla_proj(x, ctx, norm_g[0:1], mods[0], w_lat, w_gate, mla_q_norm_g[0:1], mla_kv_norm_g[0:1],
                            wuq, bf(mla_w_ukv[0]), cos128, sin128, n_lat_tiles)
    o = _mla_attn(q, k, v, n_lat, n_ctx)

    xs, z0, g0, g1, sg = _hy_proj(o, sg, x, ctx, mods[0], bf(mla_w_out[0]), norm_g[1:2], mods[1], bf(hy_w_in[0]),
                                  hy_conv_w[0], hy_conv_b[0][None, :], n_lat_tiles)
    cch = z0.shape[-1]
    filt = (hy_filt_w_in[0], hy_filt_w_hid[0], hy_filt_b[0], hy_filt_freq[0], hy_filt_w_out[0])
    assert cch % HY_SLAB == 0
    mats = _fft_matrices(n_lat)
    spec = _hy_spec(_hy_filter_taps(n_lat, *filt), mats[1], mats[2])
    cfwd, cinv, nfp = _dense_dft_matrices(n_ctx)
    cspec = _hy_ctx_spec(_hy_filter_taps(n_ctx, *filt), cfwd)
    gates = (g0, g1)
    z = z0
    for order in range(HY_ORDER):
        bias = hy_bias[0][order][None, :]
        z = _hy_conv(z, gates[order], spec, order, mats, bias, n_lat)
        z = _hy_ctx_conv(z, gates[order], cspec, order, cfwd[:, :n_ctx], cinv, nfp, bias, n_lat, n_ctx)

    nq = SWA_Q_HEADS * SWA_HEAD_DIM
    nkv = SWA_KV_HEADS * SWA_HEAD_DIM
    w_in = swa_w_in[0]
    dup = lambda w: jnp.tile(w.reshape(d, SWA_KV_HEADS, 1, SWA_HEAD_DIM), (1, 1, 2, 1)).reshape(d, 2 * nkv)
    w2 = bf(jnp.concatenate([w_in[:, :nq], dup(w_in[:, nq:nq + nkv]), dup(w_in[:, nq + nkv:nq + 2 * nkv]),
                             w_in[:, nq + 2 * nkv:]], axis=1))
    xs, q, k, v, sg = _swa_proj(z, sg, xs, mods[1], bf(hy_w_out[0]), norm_g[2:3], mods[2], w2, cos128, sin128,
                                n_lat_tiles)
    o = _swa_attn(swa_sink[0], q, k, v, n_lat, n_ctx)

    xs, u, sg = _cf_proj(o, sg, xs, mods[2], bf(swa_w_out[0]), norm_g[3:4], mods[3], bf(cf_w_in[0]), n_lat_tiles)
    return _cf_tail(u, sg, xs, mods[3], cf_dw_w[0], cf_dw_b[0][None, :], cf_ln_g[0][None, :], cf_ln_b[0][None, :],
                    bf(cf_w_out[0]), final_g[None, :])
```

```python
import functools
import math

import numpy as np
import jax
import jax.numpy as jnp
from jax import lax
from jax.experimental import pallas as pl
from jax.experimental.pallas import tpu as pltpu

F32 = jnp.float32
BF16 = jnp.bfloat16

GRID_W = 64
NORM_EPS = 1e-6
NEG_INF = -1e30
LOG2E = math.log2(math.e)
ROPE_BASE = 10000.0
ROPE_DIM = 64

MLA_HEADS = 8
MLA_Q_RANK = 384
MLA_KV_RANK = 256
MLA_NOPE = 128
MLA_ROPE = ROPE_DIM
MLA_V = 128
MLA_QK = MLA_NOPE + MLA_ROPE

HY_ORDER = 2
HY_EMB = 33
HY_EMB_PAD = 64
HY_BANDS = (HY_EMB - 1) // 2
HY_DECAY_TARGET = 1e-2
HY_FAST_DECAY = 0.3
HY_SLOW_DECAY = 1.5

SWA_Q_HEADS = 16
SWA_KV_HEADS = 4
SWA_GROUP = SWA_Q_HEADS // SWA_KV_HEADS
SWA_HEAD_DIM = ROPE_DIM
SWA_WINDOW = 128

CF_KERNEL = 31
CF_TAPS_PAD = 32

ROW_TILE = 512
ATTN_TILE = 256
SUBLANES = 8
LANES = 128
FFT_N1 = 64
FFT_N1H = FFT_N1 // 2
FFT_K1 = FFT_N1 // 2 + 1
HY_SLAB = 256
HY_HALO = 16
VMEM_LIMIT = 48 * 1024 * 1024
VMEM_LIMIT_BIG = 56 * 1024 * 1024


def _cparams(n_axes):
    return pltpu.CompilerParams(dimension_semantics=("arbitrary",) * n_axes,
                                vmem_limit_bytes=VMEM_LIMIT)


def _cparams_big(n_axes):
    return pltpu.CompilerParams(dimension_semantics=("arbitrary",) * n_axes,
                                vmem_limit_bytes=VMEM_LIMIT_BIG)


def _once(shape, index_map):
    return pl.BlockSpec(shape, index_map, pipeline_mode=pl.Buffered(1))


def _dot(a, b):
    return jnp.dot(a, b, preferred_element_type=F32)


def _dot_nt(a, b):
    return lax.dot_general(a, b, (((1,), (1,)), ((), ())), preferred_element_type=F32)


def _rms(x, g):
    return x * lax.rsqrt(jnp.mean(x * x, axis=-1, keepdims=True) + NORM_EPS) * g


def _norm_mod(x, g, mod):
    return _rms(x, g) * (1.0 + mod[1:2, :]) + mod[0:1, :]


def _silu(x):
    return x * jax.nn.sigmoid(x)


def _adaln_kernel(c_ref, w_ref, b_ref, o_ref):
    s = _silu(c_ref[...]).astype(BF16)
    o_ref[...] = _dot(s, w_ref[...].astype(BF16)) + b_ref[...]


def _adaln(cs, ada_w, ada_b):
    depth, d, d3 = ada_w.shape
    tn = 2 * ROW_TILE
    return pl.pallas_call(
        _adaln_kernel,
        out_shape=jax.ShapeDtypeStruct((depth, cs.shape[0], d3), F32),
        grid=(depth, d3 // tn),
        in_specs=[pl.BlockSpec(cs.shape, lambda i, j: (0, 0)),
                  pl.BlockSpec((None, d, tn), lambda i, j: (i, 0, j)),
                  pl.BlockSpec((None, 1, tn), lambda i, j: (i, 0, j))],
        out_specs=pl.BlockSpec((None, cs.shape[0], tn), lambda i, j: (i, 0, j)),
        compiler_params=_cparams(2), name="adaln",
    )(cs, ada_w, ada_b.reshape(depth, 1, d3))


def _input_tile(x_ref, ctx_ref, is_ctx):
    pad = jnp.zeros((x_ref.shape[0] - ctx_ref.shape[0], x_ref.shape[1]), F32)
    return jnp.where(is_ctx, jnp.concatenate([ctx_ref[...], pad], axis=0), x_ref[...])


def _input_specs(d, n_ctx, n_lat_tiles):
    return [pl.BlockSpec((None, ROW_TILE, d), lambda bi, i: (bi, jnp.minimum(i, n_lat_tiles - 1), 0)),
            pl.BlockSpec((None, n_ctx, d), lambda bi, i: (bi, 0, 0))]


def _residual_in(o_ref, sgp_ref, x_ref, modp_ref, wout_ref, xo_ref):
    og = (o_ref[...].astype(F32) * sgp_ref[...].astype(F32)).astype(BF16)
    x = x_ref[...] + modp_ref[2:3, :] * _dot(og, wout_ref[...])
    xo_ref[...] = x
    return x


def _residual_specs(o, d, n_lat_tiles):
    width = o.shape[-1]
    row = lambda bi, i: (bi, i, 0)
    return [pl.BlockSpec((None, ROW_TILE, width), row), pl.BlockSpec((None, ROW_TILE, width), row),
            pl.BlockSpec((None, ROW_TILE, d), row),
            pl.BlockSpec((None, None, 3, d), lambda bi, i: (bi, i // n_lat_tiles, 0, 0)),
            _once((width, d), lambda bi, i: (0, 0))]


def _rope_lanes(x, cos, sin):
    half = ROPE_DIM // 2
    lane = lax.broadcasted_iota(jnp.int32, x.shape, 1) % ROPE_DIM
    rot = jnp.where(lane < half, -pltpu.roll(x, LANES - half, 1), pltpu.roll(x, half, 1))
    return x * cos + rot * sin


def _mla_proj_kernel(x_ref, ctx_ref, g_ref, mod_ref, wl_ref, wg_ref, qg_ref, kvg_ref, wuq_ref, wukv_ref,
                     cos_ref, sin_ref, q_ref, k_ref, v_ref, sg_ref, *, n_lat_tiles):
    x = _input_tile(x_ref, ctx_ref, pl.program_id(1) >= n_lat_tiles)
    h = _norm_mod(x, g_ref[...], mod_ref[...]).astype(BF16)
    sg_ref[...] = _silu(_dot(h, wg_ref[...])).astype(sg_ref.dtype)
    res = _dot(h, wl_ref[...])
    c0, c1 = MLA_Q_RANK, MLA_Q_RANK + MLA_KV_RANK
    cqn = _rms(res[:, :c0], qg_ref[...]).astype(BF16)
    ckvn = _rms(res[:, c0:c1], kvg_ref[...]).astype(BF16)
    cos, sin = cos_ref[...], sin_ref[...]

    def rope(slab):
        return _rope_lanes(slab, cos, sin)[:, :MLA_ROPE]

    k_rope_t = rope(res[:, c1:c1 + LANES]).T.astype(k_ref.dtype)
    scale = MLA_QK ** -0.5 * LOG2E
    per = 2 * LANES
    for h0 in range(0, MLA_HEADS, 2):
        q2 = _dot(cqn, wuq_ref[:, h0 * per:(h0 + 2) * per])
        kv2 = _dot(ckvn, wukv_ref[:, h0 * per:(h0 + 2) * per])
        for j in range(2):
            hd = h0 + j
            q, kv = q2[:, j * per:(j + 1) * per], kv2[:, j * per:(j + 1) * per]
            q_ref[hd, :, :MLA_NOPE] = (q[:, :MLA_NOPE] * scale).astype(q_ref.dtype)
            q_ref[hd, :, MLA_NOPE:] = (rope(q[:, MLA_NOPE:]) * scale).astype(q_ref.dtype)
            k_ref[hd, :MLA_NOPE, :] = kv[:, :MLA_NOPE].T.astype(k_ref.dtype)
            k_ref[hd, MLA_NOPE:, :] = k_rope_t
            v_ref[hd, :, :MLA_V] = kv[:, MLA_NOPE:].astype(v_ref.dtype)
            v_ref[hd, :, MLA_V:] = jnp.ones((v_ref.shape[1], MLA_V), v_ref.dtype)


def _mla_proj(x, ctx, g, mod, wl, wg, qg, kvg, wuq, wukv, cos, sin, n_lat_tiles):
    b, n_lat, d = x.shape
    nt = n_lat_tiles + 1
    t = nt * ROW_TILE
    width = MLA_HEADS * MLA_V
    full = lambda a: _once(a.shape, lambda bi, i: (0,) * a.ndim)
    head_out = lambda n: pl.BlockSpec((None, MLA_HEADS, ROW_TILE, n), lambda bi, i: (bi, 0, i, 0))
    return pl.pallas_call(
        functools.partial(_mla_proj_kernel, n_lat_tiles=n_lat_tiles),
        out_shape=(jax.ShapeDtypeStruct((b, MLA_HEADS, t, MLA_QK), BF16),
                   jax.ShapeDtypeStruct((b, MLA_HEADS, MLA_QK, t), BF16),
                   jax.ShapeDtypeStruct((b, MLA_HEADS, t, 2 * MLA_V), BF16),
                   jax.ShapeDtypeStruct((b, t, width), BF16)),
        grid=(b, nt),
        in_specs=_input_specs(d, ctx.shape[1], n_lat_tiles) +
                 [full(g),
                  pl.BlockSpec((None, None, 3, d), lambda bi, i: (bi, i // n_lat_tiles, 0, 0)),
                  full(wl), full(wg), full(qg), full(kvg), full(wuq), full(wukv),
                  pl.BlockSpec((ROW_TILE, LANES), lambda bi, i: (i, 0)),
                  pl.BlockSpec((ROW_TILE, LANES), lambda bi, i: (i, 0))],
        out_specs=(head_out(MLA_QK),
                   pl.BlockSpec((None, MLA_HEADS, MLA_QK, ROW_TILE), lambda bi, i: (bi, 0, 0, i)),
                   head_out(2 * MLA_V),
                   pl.BlockSpec((None, ROW_TILE, width), lambda bi, i: (bi, i, 0))),
        compiler_params=_cparams(2), name="mla_proj",
    )(x, ctx, g, mod, wl, wg, qg, kvg, wuq, wukv, cos, sin)


def _attend(q, k_parts, v_parts):
    s = [_dot(q, k) for k in k_parts]
    m = functools.reduce(jnp.maximum, [jnp.max(x, axis=-1, keepdims=True) for x in s])
    ov = functools.reduce(jnp.add, [_dot(jnp.exp2(x - m).astype(BF16), v) for x, v in zip(s, v_parts)])
    n = ov.shape[-1] // 2
    return ov[:, :n] / ov[:, n:]


MLA_HEADS_PER_STEP = 2


def _mla_attn_kernel(q_ref, k_ref, v_ref, o_ref, *, n_lat_tiles, n_lat, n_ctx):
    qi = pl.program_id(2)

    n_keys = n_lat + n_ctx

    @pl.when(qi < n_lat_tiles)
    def _():
        chains = [(hd, r0) for hd in range(MLA_HEADS_PER_STEP) for r0 in range(0, q_ref.shape[1], ATTN_TILE)]
        s = [_dot(q_ref[hd, r0:r0 + ATTN_TILE, :], k_ref[hd, :, 0:n_keys]) for hd, r0 in chains]
        p = [jnp.exp2(x - jnp.max(x, axis=-1, keepdims=True)).astype(BF16) for x in s]
        for (hd, r0), pc in zip(chains, p):
            ov = _dot(pc, v_ref[hd, 0:n_keys, :])
            o = ov[:, :MLA_V] / ov[:, MLA_V:]
            o_ref[r0:r0 + ATTN_TILE, hd * MLA_V:(hd + 1) * MLA_V] = o.astype(o_ref.dtype)

    @pl.when(qi >= n_lat_tiles)
    def _():
        o_ref[n_ctx:, :] = jnp.zeros((o_ref.shape[0] - n_ctx, o_ref.shape[1]), o_ref.dtype)
        for hd in range(MLA_HEADS_PER_STEP):
            k = k_ref[hd, :, n_lat:n_keys]
            v = v_ref[hd, n_lat:n_keys, :]
            o = _attend(q_ref[hd, 0:n_ctx, :], [k], [v])
            o_ref[0:n_ctx, hd * MLA_V:(hd + 1) * MLA_V] = o.astype(o_ref.dtype)


def _mla_attn(q, k, v, n_lat, n_ctx):
    b, hds, t, _ = q.shape
    nt = t // ROW_TILE
    hps = MLA_HEADS_PER_STEP
    kern = functools.partial(_mla_attn_kernel, n_lat_tiles=n_lat // ROW_TILE, n_lat=n_lat, n_ctx=n_ctx)
    return pl.pallas_call(
        kern,
        out_shape=jax.ShapeDtypeStruct((b, t, hds * MLA_V), BF16),
        grid=(b, hds // hps, nt),
        in_specs=[pl.BlockSpec((None, hps, ROW_TILE, MLA_QK), lambda bi, h, i: (bi, h, i, 0)),
                  pl.BlockSpec((None, hps, MLA_QK, t), lambda bi, h, i: (bi, h, 0, 0)),
                  pl.BlockSpec((None, hps, t, 2 * MLA_V), lambda bi, h, i: (bi, h, 0, 0))],
        out_specs=pl.BlockSpec((None, ROW_TILE, hps * MLA_V), lambda bi, h, i: (bi, i, h)),
        compiler_params=_cparams_big(3), name="mla_attn",
    )(q, k, v)


def _hy_proj_kernel(op_ref, o_ref, on_ref, sp_ref, s_ref, sn_ref, xp_ref, x_ref, ctx_ref, xn_ref, modp_ref, wout_ref,
                    g_ref, mod_ref, w_ref, cw_ref, cb_ref, xo_ref, z_ref, g0_ref, g1_ref, sg_ref,
                    *, n_lat_tiles, n_ctx):
    i = pl.program_id(1)
    tm = x_ref.shape[0]
    c = z_ref.shape[-1]
    n = tm + 2 * SUBLANES
    halo = xp_ref.shape[0]
    cat = lambda a, m, z: jnp.concatenate([a[...], m, z[...]], axis=0)
    og = (cat(op_ref, o_ref[...], on_ref).astype(F32) * cat(sp_ref, s_ref[...], sn_ref).astype(F32)).astype(BF16)
    xe = cat(xp_ref, _input_tile(x_ref, ctx_ref, i >= n_lat_tiles), xn_ref)
    xe = xe + modp_ref[2:3, :] * _dot(og, wout_ref[...])
    xo_ref[...] = xe[halo:halo + tm]
    xa = xe[halo - SUBLANES:halo + tm + SUBLANES]
    ctx_tile = i == n_lat_tiles
    first = jnp.logical_or(i == 0, ctx_tile)
    last = jnp.logical_or(i == n_lat_tiles - 1, ctx_tile)
    rows = lax.broadcasted_iota(jnp.int32, (n, 1), 0)
    end = jnp.where(ctx_tile, n_ctx + SUBLANES, jnp.where(last, tm + SUBLANES, n))
    valid = jnp.logical_and(jnp.logical_or(rows >= SUBLANES, jnp.logical_not(first)), rows < end)
    hf = jnp.where(valid, _norm_mod(xa, g_ref[...], mod_ref[...]), 0.0)
    h = hf.astype(BF16)
    outs = (z_ref, g0_ref, g1_ref)
    for j in range(len(outs)):
        u = _dot(h, w_ref[:, j * c:(j + 1) * c])
        cw = cw_ref[:, j * c:(j + 1) * c]
        cv = cw[0:1, :] * pltpu.roll(u, 1, 0) + cw[1:2, :] * u + cw[2:3, :] * pltpu.roll(u, n - 1, 0)
        outs[j][...] = cv[SUBLANES:SUBLANES + tm] + cb_ref[:, j * c:(j + 1) * c]
    h_mid = hf[SUBLANES:SUBLANES + tm].astype(BF16)
    sg_ref[...] = _silu(_dot(h_mid, w_ref[:, len(outs) * c:])).astype(sg_ref.dtype)


def _hy_proj(o, sgp, x, ctx, modp, w_out, g, mod, w, cw, cb, n_lat_tiles):
    b, t, width = o.shape
    n_lat, d = x.shape[1:]
    n_ctx = ctx.shape[1]
    c = w.shape[1] // (HY_ORDER + 2)
    per = ROW_TILE // HY_HALO
    last = n_lat // HY_HALO - 1
    full = lambda a: _once(a.shape, lambda bi, i: (0,) * a.ndim)
    row = lambda bi, i: (bi, i, 0)
    prev = lambda bi, i: (bi, jnp.maximum(i * per - 1, 0), 0)
    nxt = lambda bi, i: (bi, jnp.minimum((i + 1) * per, last), 0)
    x_main, x_ctx = _input_specs(d, n_ctx, n_lat_tiles)
    kern = functools.partial(_hy_proj_kernel, n_lat_tiles=n_lat_tiles, n_ctx=n_ctx)
    halo = lambda n, m: pl.BlockSpec((None, HY_HALO, n), m)
    return pl.pallas_call(
        kern,
        out_shape=(jax.ShapeDtypeStruct((b, t, d), F32),) + (jax.ShapeDtypeStruct((b, t, c), F32),) * 3 +
                  (jax.ShapeDtypeStruct((b, t, c), BF16),),
        grid=(b, t // ROW_TILE),
        in_specs=[halo(width, prev), pl.BlockSpec((None, ROW_TILE, width), row), halo(width, nxt),
                  halo(width, prev), pl.BlockSpec((None, ROW_TILE, width), row), halo(width, nxt),
                  halo(d, prev), x_main, x_ctx, halo(d, nxt),
                  pl.BlockSpec((None, None, 3, d), lambda bi, i: (bi, i // n_lat_tiles, 0, 0)),
                  full(w_out), full(g),
                  pl.BlockSpec((None, None, 3, d), lambda bi, i: (bi, i // n_lat_tiles, 0, 0)),
                  full(w), full(cw), full(cb)],
        out_specs=(pl.BlockSpec((None, ROW_TILE, d), row),) + (pl.BlockSpec((None, ROW_TILE, c), row),) * 4,
        compiler_params=_cparams(2), name="hy_proj",
    )(o, o, o, sgp, sgp, sgp, x, x, ctx, x, modp, w_out, g, mod, w, cw, cb)


def _hy_filter_kernel(ef_ref, eb_ref, w0_ref, wh_ref, b_ref, fr_ref, wf_ref, wb_ref, dl_ref, o_ref):
    i = pl.program_id(0)
    c = dl_ref.shape[-1]
    half = ef_ref.shape[0]

    def taps(e, wo):
        hdn = e
        ws = (w0_ref[...], wh_ref[0], wh_ref[1])
        for k in range(3):
            hdn = jnp.sin(fr_ref[k:k + 1, :] * (_dot(hdn.astype(BF16), ws[k].astype(BF16)) + b_ref[k:k + 1, :]))
        hh = _dot(hdn.astype(BF16), wo.astype(BF16))
        out = []
        for part in range(2):
            decay = jnp.exp(-e[:, part * HY_EMB_PAD:part * HY_EMB_PAD + 1] * dl_ref[...])
            out.append([hh[:, (part * HY_ORDER + o) * c:(part * HY_ORDER + o + 1) * c] * decay
                        for o in range(HY_ORDER)])
        return out

    hf = taps(ef_ref[...], wf_ref[...])
    hb = taps(eb_ref[...], wb_ref[...])
    rows = lax.broadcasted_iota(jnp.int32, (half, 1), 0)
    lag0 = jnp.logical_and(rows == 0, i == 0)
    for o in range(HY_ORDER):
        o_ref[o, 0, 0:half] = hf[0][o] + jnp.where(lag0, hb[0][o], 0.0)
        o_ref[o, 1, 0:half] = jnp.where(lag0, 0.0, hb[0][o])
        o_ref[o, 0, half:] = hf[1][o]
        o_ref[o, 1, half:] = hb[1][o]


def _hy_position_features(n):
    t = jnp.linspace(0.0, 1.0, n, dtype=F32)[:, None]
    wpos = (2.0 * math.pi / n) * jnp.arange(n, dtype=F32)[:, None]
    bands = jnp.linspace(1e-4, HY_BANDS - 1, HY_BANDS, dtype=F32)[None, :]
    hdn = jnp.concatenate([t, jnp.cos(bands * wpos), -jnp.sin(bands * wpos)], axis=-1)
    return jnp.pad(hdn, ((0, 0), (0, HY_EMB_PAD - HY_EMB)))


def _hy_filter_taps(n, w_in, w_hid, bias, freq, w_out):
    c = w_out.shape[1] // (2 * HY_ORDER)
    ef = _hy_position_features(n)
    rev = (n - jnp.arange(n)) % n
    eb = ef[rev]
    w0 = jnp.pad(w_in, ((0, HY_EMB_PAD - HY_EMB), (0, 0)))
    wo = w_out.reshape(w_out.shape[0], HY_ORDER, 2, c)
    wf = wo[:, :, 0, :].reshape(w_out.shape[0], HY_ORDER * c)
    wb = wo[:, :, 1, :].reshape(w_out.shape[0], HY_ORDER * c)
    deltas = jnp.abs(jnp.linspace(math.log(HY_DECAY_TARGET) / HY_FAST_DECAY,
                                  math.log(HY_DECAY_TARGET) / HY_SLOW_DECAY, c, dtype=F32))[None, :]
    tl = min(n, ROW_TILE)
    half = tl // 2
    pair = lambda e: e.reshape(n // tl, 2, half, HY_EMB_PAD).transpose(0, 2, 1, 3).reshape(n // 2, 2 * HY_EMB_PAD)
    both = lambda w: jnp.kron(jnp.eye(2, dtype=F32), w)
    twice = lambda v: jnp.tile(v, (1, 2))
    w_hid2 = jnp.stack([both(w_hid[0]), both(w_hid[1])])
    full = lambda a: pl.BlockSpec(a.shape, lambda i: (0,) * a.ndim)
    args = (pair(ef), pair(eb), both(w0), w_hid2, twice(bias), twice(freq), both(wf), both(wb), deltas)
    return pl.pallas_call(
        _hy_filter_kernel,
        out_shape=jax.ShapeDtypeStruct((HY_ORDER, 2, n, c), F32),
        grid=(n // tl,),
        in_specs=[pl.BlockSpec((half, 2 * HY_EMB_PAD), lambda i: (i, 0)),
                  pl.BlockSpec((half, 2 * HY_EMB_PAD), lambda i: (i, 0))] + [full(a) for a in args[2:]],
        out_specs=pl.BlockSpec((HY_ORDER, 2, tl, c), lambda i: (0, 0, i, 0)),
        compiler_params=_cparams(1), name="hy_filter",
    )(*args)


def _fft_matrices(n_lat):
    n = 2 * n_lat
    n2 = n_lat // FFT_N1H
    k1 = np.arange(FFT_K1)
    n1 = np.arange(FFT_N1H)
    ang = 2.0 * np.pi * np.outer(k1, n1) / FFT_N1
    eye = np.eye(SUBLANES)
    f1 = np.kron(np.concatenate([np.cos(ang), -np.sin(ang)], axis=0), eye)
    wgt = np.full(FFT_K1, 2.0)
    wgt[0] = wgt[-1] = 1.0
    g1 = np.kron(np.concatenate([wgt * np.cos(ang.T), (-wgt * np.sin(ang.T))[:, 1:-1]], axis=1) / n, eye)
    idx = np.arange(n2)
    f2 = np.zeros((FFT_K1, 2 * n2, 2 * n2))
    f2i = np.zeros((FFT_K1, 2 * n2, 2 * n2))
    for a in range(FFT_K1):
        ph = 2.0 * np.pi * (np.outer(idx, idx) / n2 + (a * idx)[None, :] / n)
        f2[a] = np.block([[np.cos(ph), np.sin(ph)], [-np.sin(ph), np.cos(ph)]])
        f2i[a] = np.block([[np.cos(ph.T), -np.sin(ph.T)], [np.sin(ph.T), np.cos(ph.T)]])
    sign = np.kron(np.tile((-1.0) ** k1, 2), np.ones(SUBLANES))[:, None]
    f1s = np.concatenate([f1, sign * f1], axis=1)
    as_bf16 = lambda m: jnp.asarray(m.astype(np.float32)).astype(BF16)
    return as_bf16(f1), as_bf16(f1s), as_bf16(f2), as_bf16(f2i), as_bf16(g1)


def _dense_dft_matrices(n_ctx):
    n = 2 * n_ctx
    nf = n_ctx + 1
    nfp = -(-nf // SUBLANES) * SUBLANES
    k = np.arange(nf)
    ang = 2.0 * np.pi * np.outer(k, np.arange(n)) / n
    fwd = np.zeros((2 * nfp, n))
    fwd[:nf] = np.cos(ang)
    fwd[nfp:nfp + nf] = -np.sin(ang)
    wgt = np.full(nf, 2.0)
    wgt[0] = wgt[-1] = 1.0
    angi = 2.0 * np.pi * np.outer(np.arange(n_ctx), k) / n
    inv = np.zeros((n_ctx, 2 * nfp))
    inv[:, :nf] = wgt * np.cos(angi) / n
    inv[:, nfp:nfp + nf] = -wgt * np.sin(angi) / n
    as_bf16 = lambda m: jnp.asarray(m.astype(np.float32)).astype(BF16)
    return as_bf16(fwd), as_bf16(inv), nfp


def _hy_spec_kernel(t_ref, f1s_ref, f2_ref, h_ref, a_ref):
    _, n1h, nj, s, tc = t_ref.shape
    n2 = nj * s
    for j in range(nj):
        sig = jnp.concatenate([t_ref[0, :, j].reshape(n1h * s, tc), t_ref[1, :, j].reshape(n1h * s, tc)], axis=0)
        a_ref[:, :, j] = _dot(f1s_ref[...], sig.astype(BF16)).reshape(2, FFT_K1, s, tc)
    for k in range(FFT_K1):
        a = a_ref[:, k].reshape(2 * n2, tc).astype(BF16)
        h_ref[:, k] = _dot(f2_ref[k], a).reshape(2, n2, tc).astype(h_ref.dtype)


def _hy_spec(taps, f1s, f2):
    _, _, n_lat, c = taps.shape
    n2 = n_lat // FFT_N1H
    nj = n2 // SUBLANES
    tc = HY_SLAB
    return pl.pallas_call(
        _hy_spec_kernel,
        out_shape=jax.ShapeDtypeStruct((HY_ORDER, 2, FFT_K1, n2, c), BF16),
        grid=(c // tc, HY_ORDER),
        in_specs=[pl.BlockSpec((None, 2, FFT_N1H, nj, SUBLANES, tc), lambda ci, o: (o, 0, 0, 0, 0, ci)),
                  _once(f1s.shape, lambda ci, o: (0, 0)),
                  _once(f2.shape, lambda ci, o: (0, 0, 0))],
        out_specs=pl.BlockSpec((None, 2, FFT_K1, n2, tc), lambda ci, o: (o, 0, 0, 0, ci)),
        scratch_shapes=[pltpu.VMEM((2, FFT_K1, nj, SUBLANES, tc), F32)],
        compiler_params=_cparams_big(2), name="hy_spec",
    )(taps.reshape(HY_ORDER, 2, FFT_N1H, nj, SUBLANES, c), f1s, f2)


def _conv_stages(z_ref, h_ref, f1_ref, f2_ref, f2i_ref, g1_ref, a_ref, emit):
    n1h, nj, s, tc = z_ref.shape
    n2 = nj * s
    for j in range(nj):
        zz = z_ref[:, j].reshape(n1h * s, tc).astype(BF16)
        a_ref[:, :, j] = _dot(f1_ref[...], zz).reshape(2, FFT_K1, s, tc)
    for k in range(FFT_K1):
        x = _dot(f2_ref[k], a_ref[:, k].reshape(2 * n2, tc).astype(BF16))
        xr, xi = x[:n2], x[n2:]
        hr, hi = h_ref[0, k].astype(F32), h_ref[1, k].astype(F32)
        y = jnp.concatenate([xr * hr - xi * hi, xr * hi + xi * hr], axis=0).astype(BF16)
        a_ref[:, k] = _dot(f2i_ref[k], y).reshape(2, nj, s, tc)
    for j in range(nj):
        bp = a_ref[:, :, j].reshape(2 * FFT_K1 * s, tc)
        bp = jnp.concatenate([bp[:FFT_K1 * s], bp[(FFT_K1 + 1) * s:(2 * FFT_K1 - 1) * s]], axis=0)
        emit(j, _dot(g1_ref[...], bp.astype(BF16)).reshape(n1h, s, tc))


def _hy_conv_kernel(z_ref, gt_ref, bias_ref, h_ref, f1_ref, f2_ref, f2i_ref, g1_ref, o_ref, a_ref):
    def emit(j, y):
        o_ref[:, j] = gt_ref[:, j] * (y + z_ref[:, j] * bias_ref[...])
    _conv_stages(z_ref, h_ref, f1_ref, f2_ref, f2i_ref, g1_ref, a_ref, emit)


def _cf_conv_kernel(z_ref, bias_ref, h_ref, f1_ref, f2_ref, f2i_ref, g1_ref, o_ref, a_ref):
    def emit(j, y):
        o_ref[:, j] = y + bias_ref[...]
    _conv_stages(z_ref, h_ref, f1_ref, f2_ref, f2i_ref, g1_ref, a_ref, emit)


def _long_conv(z, h, order, mats, bias, n_lat, gate=None):
    f1, _, f2, f2i, g1 = mats
    bz, t, c = z.shape
    n2 = n_lat // FFT_N1H
    nj = n2 // SUBLANES
    tc = HY_SLAB
    view = (bz, t // n2, nj, SUBLANES, c)
    slab = pl.BlockSpec((None, FFT_N1H, nj, SUBLANES, tc), lambda ci, bi: (bi, 0, 0, 0, ci))
    data, data_specs = [z.reshape(view)], [slab]
    if gate is not None:
        data, data_specs = data + [gate.reshape(view)], data_specs + [slab]
    out = pl.pallas_call(
        _hy_conv_kernel if gate is not None else _cf_conv_kernel,
        out_shape=jax.ShapeDtypeStruct(view, F32),
        grid=(c // tc, bz),
        in_specs=data_specs +
                 [pl.BlockSpec((1, tc), lambda ci, bi: (0, ci)),
                  _once((None, 2, FFT_K1, n2, tc), lambda ci, bi: (order, 0, 0, 0, ci)),
                  _once(f1.shape, lambda ci, bi: (0, 0)),
                  _once(f2.shape, lambda ci, bi: (0, 0, 0)),
                  _once(f2i.shape, lambda ci, bi: (0, 0, 0)),
                  _once(g1.shape, lambda ci, bi: (0, 0))],
        out_specs=slab,
        input_output_aliases={0: 0},
        scratch_shapes=[pltpu.VMEM((2, FFT_K1, nj, SUBLANES, tc), F32)],
        compiler_params=_cparams_big(2), name="long_conv",
    )(*data, bias, h, f1, f2, f2i, g1)
    return out.reshape(bz, t, c)


def _hy_ctx_spec_kernel(t_ref, f_ref, h_ref):
    sig = jnp.concatenate([t_ref[0], t_ref[1]], axis=0).astype(BF16)
    h_ref[...] = _dot(f_ref[...], sig)


def _hy_ctx_spec(taps, fwd):
    _, _, n_ctx, c = taps.shape
    return pl.pallas_call(
        _hy_ctx_spec_kernel,
        out_shape=jax.ShapeDtypeStruct((HY_ORDER, fwd.shape[0], c), F32),
        grid=(HY_ORDER,),
        in_specs=[pl.BlockSpec((None, 2, n_ctx, c), lambda o: (o, 0, 0, 0)),
                  pl.BlockSpec(fwd.shape, lambda o: (0, 0))],
        out_specs=pl.BlockSpec((None, fwd.shape[0], c), lambda o: (o, 0, 0)),
        compiler_params=_cparams(1), name="hy_ctx_spec",
    )(taps, fwd)


def _hy_ctx_conv_kernel(zin_ref, gt_ref, h_ref, f_ref, fi_ref, bias_ref, o_ref, *, nfp):
    z = zin_ref[...]
    x = _dot(f_ref[...], z.astype(BF16))
    xr, xi = x[:nfp], x[nfp:]
    hr, hi = h_ref[:nfp, :], h_ref[nfp:, :]
    y = jnp.concatenate([xr * hr - xi * hi, xr * hi + xi * hr], axis=0).astype(BF16)
    n_ctx = z.shape[0]
    o_ref[0:n_ctx, :] = gt_ref[...] * (_dot(fi_ref[...], y) + z * bias_ref[...])
    o_ref[n_ctx:, :] = jnp.zeros((o_ref.shape[0] - n_ctx, o_ref.shape[1]), o_ref.dtype)


def _hy_ctx_conv(z, gate, h, order, fwd_half, inv, nfp, bias, n_lat, n_ctx):
    bz, t, c = z.shape
    blk = pl.BlockSpec((None, n_ctx, c), lambda bi: (bi, n_lat // n_ctx, 0))
    kern = functools.partial(_hy_ctx_conv_kernel, nfp=nfp)
    return pl.pallas_call(
        kern,
        out_shape=jax.ShapeDtypeStruct(z.shape, F32),
        grid=(bz,),
        in_specs=[blk, blk,
                  pl.BlockSpec((None, 2 * nfp, c), lambda bi: (order, 0, 0)),
                  pl.BlockSpec(fwd_half.shape, lambda bi: (0, 0)),
                  pl.BlockSpec(inv.shape, lambda bi: (0, 0)),
                  pl.BlockSpec((1, c), lambda bi: (0, 0))],
        out_specs=pl.BlockSpec((None, t - n_lat, c), lambda bi: (bi, n_lat // (t - n_lat), 0)),
        input_output_aliases={0: 0},
        compiler_params=_cparams(1), name="hy_ctx_conv",
    )(z, gate, h, fwd_half, inv, bias)


def _swa_proj_kernel(o_ref, sgp_ref, x_ref, modp_ref, wout_ref, g_ref, mod_ref, w_ref, cos_ref, sin_ref,
                     xo_ref, q_ref, k_ref, v_ref, sg_ref):
    x = _residual_in(o_ref, sgp_ref, x_ref, modp_ref, wout_ref, xo_ref)
    h = _norm_mod(x, g_ref[...], mod_ref[...]).astype(BF16)
    cos, sin = cos_ref[...], sin_ref[...]
    nq = q_ref.shape[-1]
    nk = SWA_KV_HEADS * LANES
    scale = SWA_HEAD_DIM ** -0.5 * LOG2E
    for c0 in range(0, nq, nk):
        res = _dot(h, w_ref[:, c0:c0 + nk])
        for cb in range(nk // LANES):
            sl = slice(cb * LANES, (cb + 1) * LANES)
            q_ref[:, c0 + cb * LANES:c0 + (cb + 1) * LANES] = (_rope_lanes(res[:, sl], cos, sin) * scale).astype(q_ref.dtype)
    res = _dot(h, w_ref[:, nq:nq + nk])
    for hd in range(SWA_KV_HEADS):
        k_ref[hd] = _rope_lanes(res[:, hd * LANES:(hd + 1) * LANES], cos, sin).astype(k_ref.dtype)
    res = _dot(h, w_ref[:, nq + nk:nq + 2 * nk])
    for hd in range(SWA_KV_HEADS):
        v_ref[hd, :, :LANES] = res[:, hd * LANES:(hd + 1) * LANES].astype(v_ref.dtype)
        v_ref[hd, :, LANES:] = jnp.ones((v_ref.shape[1], LANES), v_ref.dtype)
    for c0 in range(0, nq, nk):
        sg_ref[:, c0:c0 + nk] = _silu(_dot(h, w_ref[:, nq + 2 * nk + c0:nq + 2 * nk + c0 + nk])).astype(sg_ref.dtype)


def _swa_proj(o, sgp, x, modp, w_out, g, mod, w, cos, sin, n_lat_tiles):
    b, t, d = x.shape
    nt = t // ROW_TILE
    nq = SWA_Q_HEADS * SWA_HEAD_DIM
    full = lambda a: _once(a.shape, lambda bi, i: (0,) * a.ndim)
    row = lambda bi, i: (bi, i, 0)
    kv = lambda n: pl.BlockSpec((None, SWA_KV_HEADS, ROW_TILE, n), lambda bi, i: (bi, 0, i, 0))
    return pl.pallas_call(
        _swa_proj_kernel,
        out_shape=(jax.ShapeDtypeStruct((b, t, d), F32),
                   jax.ShapeDtypeStruct((b, t, nq), BF16),
                   jax.ShapeDtypeStruct((b, SWA_KV_HEADS, t, LANES), BF16),
                   jax.ShapeDtypeStruct((b, SWA_KV_HEADS, t, 2 * LANES), BF16),
                   jax.ShapeDtypeStruct((b, t, nq), BF16)),
        grid=(b, nt),
        in_specs=_residual_specs(o, d, n_lat_tiles) +
                 [full(g),
                  pl.BlockSpec((None, None, 3, d), lambda bi, i: (bi, i // n_lat_tiles, 0, 0)),
                  full(w),
                  pl.BlockSpec((ROW_TILE, LANES), lambda bi, i: (i, 0)),
                  pl.BlockSpec((ROW_TILE, LANES), lambda bi, i: (i, 0))],
        out_specs=(pl.BlockSpec((None, ROW_TILE, d), row), pl.BlockSpec((None, ROW_TILE, nq), row),
                   kv(LANES), kv(2 * LANES), pl.BlockSpec((None, ROW_TILE, nq), row)),
        compiler_params=_cparams(2), name="swa_proj",
    )(o, sgp, x, modp, w_out, g, mod, w, cos, sin)


def _swa_attn_kernel(sink_ref, q_ref, k_ref, v_ref, o_ref, *, n_lat_tiles, n_lat, n_ctx):
    g = pl.program_id(1)
    qi = pl.program_id(2)
    tq = SWA_WINDOW
    span = tq + 2 * SWA_WINDOW
    lane_half = lax.broadcasted_iota(jnp.int32, (tq, LANES), 1) // SWA_HEAD_DIM
    first = lax.broadcasted_iota(jnp.int32, (2 * tq, 1), 0) < tq
    kc = k_ref[pl.ds(n_lat, n_ctx), :]
    vc = v_ref[pl.ds(n_lat, n_ctx), :]

    def pair_query(r0, col):
        qcol = q_ref[r0:r0 + tq, col * LANES:(col + 1) * LANES]
        zero = jnp.zeros_like(qcol)
        return jnp.concatenate([jnp.where(lane_half == 0, qcol, zero), jnp.where(lane_half == 1, qcol, zero)], axis=0)

    def pair_sink(col):
        base = g * SWA_GROUP + 2 * col
        return jnp.where(first, sink_ref[base], sink_ref[base + 1]) * LOG2E

    def finish(r0, col, ov, m, sink):
        o = ov[:, :LANES] / (ov[:, LANES:] + jnp.exp2(sink - m))
        o_ref[r0:r0 + tq, col * LANES:(col + 1) * LANES] = jnp.where(lane_half == 0, o[:tq], o[tq:]).astype(o_ref.dtype)

    @pl.when(qi < n_lat_tiles)
    def _():
        chains = []
        for r0 in range(0, q_ref.shape[0], tq):
            q0 = qi * q_ref.shape[0] + r0
            start = pl.multiple_of(jnp.clip(q0 - SWA_WINDOW, 0, n_lat - span), SWA_WINDOW)
            kw = k_ref[pl.ds(start, span), :]
            vw = v_ref[pl.ds(start, span), :]
            qpos = q0 + lax.broadcasted_iota(jnp.int32, (tq, 1), 0)
            kpos = start + lax.broadcasted_iota(jnp.int32, (1, span), 1)
            band = jnp.abs(kpos - qpos) <= SWA_WINDOW
            for col in range(SWA_GROUP // 2):
                q2, sink = pair_query(r0, col), pair_sink(col)
                sw = _dot_nt(q2, kw)
                sw = jnp.concatenate([jnp.where(band, sw[:tq], NEG_INF), jnp.where(band, sw[tq:], NEG_INF)], axis=0)
                sc = _dot_nt(q2, kc)
                m = jnp.maximum(jnp.maximum(jnp.max(sw, axis=-1, keepdims=True),
                                            jnp.max(sc, axis=-1, keepdims=True)), sink)
                chains.append((r0, col, jnp.exp2(sw - m).astype(BF16), jnp.exp2(sc - m).astype(BF16), vw, m, sink))
        for r0, col, pw, pc, vw, m, sink in chains:
            finish(r0, col, _dot(pw, vw) + _dot(pc, vc), m, sink)

    @pl.when(qi >= n_lat_tiles)
    def _():
        o_ref[n_ctx:, :] = jnp.zeros((o_ref.shape[0] - n_ctx, o_ref.shape[1]), o_ref.dtype)
        for r0 in range(0, n_ctx, tq):
            for col in range(SWA_GROUP // 2):
                q2, sink = pair_query(r0, col), pair_sink(col)
                sc = _dot_nt(q2, kc)
                m = jnp.maximum(jnp.max(sc, axis=-1, keepdims=True), sink)
                finish(r0, col, _dot(jnp.exp2(sc - m).astype(BF16), vc), m, sink)


def _swa_attn(sink, q, k, v, n_lat, n_ctx):
    b, t, nq = q.shape
    nt = t // ROW_TILE
    gw = SWA_GROUP * SWA_HEAD_DIM
    kern = functools.partial(_swa_attn_kernel, n_lat_tiles=n_lat // ROW_TILE, n_lat=n_lat, n_ctx=n_ctx)
    kv = lambda n: pl.BlockSpec((None, None, t, n), lambda bi, gi, i: (bi, gi, 0, 0))
    return pl.pallas_call(
        kern,
        out_shape=jax.ShapeDtypeStruct((b, t, nq), BF16),
        grid=(b, SWA_KV_HEADS, nt),
        in_specs=[pl.BlockSpec(memory_space=pltpu.SMEM),
                  pl.BlockSpec((None, ROW_TILE, gw), lambda bi, gi, i: (bi, i, gi)), kv(LANES), kv(2 * LANES)],
        out_specs=pl.BlockSpec((None, ROW_TILE, gw), lambda bi, gi, i: (bi, i, gi)),
        compiler_params=_cparams(3), name="swa_attn",
    )(sink, q, k, v)


def _cf_proj_kernel(o_ref, sgp_ref, x_ref, modp_ref, wout_ref, g_ref, mod_ref, w_ref, xo_ref, u_ref, sg_ref,
                    *, n_lat_tiles):
    x = _residual_in(o_ref, sgp_ref, x_ref, modp_ref, wout_ref, xo_ref)

    @pl.when(pl.program_id(1) < n_lat_tiles)
    def _():
        h = _norm_mod(x, g_ref[...], mod_ref[...]).astype(BF16)
        c = u_ref.shape[-1]
        tn = c // 2
        for c0 in range(0, c, tn):
            a = _dot(h, w_ref[:, c0:c0 + tn])
            gl = _dot(h, w_ref[:, c + c0:c + c0 + tn])
            u_ref[:, c0:c0 + tn] = a * jax.nn.sigmoid(gl)
            sg_ref[:, c0:c0 + tn] = _silu(_dot(h, w_ref[:, 2 * c + c0:2 * c + c0 + tn])).astype(sg_ref.dtype)


def _cf_proj(o, sgp, x, modp, w_out, g, mod, w, n_lat_tiles):
    b, t, d = x.shape
    c = w.shape[1] // 3
    n_lat = n_lat_tiles * ROW_TILE
    full = lambda a: _once(a.shape, lambda bi, i: (0,) * a.ndim)
    row = lambda bi, i: (bi, i, 0)
    lat_row = lambda bi, i: (bi, jnp.minimum(i, n_lat_tiles - 1), 0)
    return pl.pallas_call(
        functools.partial(_cf_proj_kernel, n_lat_tiles=n_lat_tiles),
        out_shape=(jax.ShapeDtypeStruct((b, t, d), F32),
                   jax.ShapeDtypeStruct((b, n_lat, c), F32), jax.ShapeDtypeStruct((b, n_lat, c), BF16)),
        grid=(b, t // ROW_TILE),
        in_specs=_residual_specs(o, d, n_lat_tiles) +
                 [full(g), pl.BlockSpec((None, None, 3, d), lambda bi, i: (bi, 0, 0, 0)), full(w)],
        out_specs=(pl.BlockSpec((None, ROW_TILE, d), row),
                   pl.BlockSpec((None, ROW_TILE, c), lat_row), pl.BlockSpec((None, ROW_TILE, c), lat_row)),
        compiler_params=_cparams(2), name="cf_proj",
    )(o, sgp, x, modp, w_out, g, mod, w)


def _cf_spec_matrix(n_lat):
    n = 2 * n_lat
    n2 = n_lat // FFT_N1H
    k = (np.arange(FFT_K1)[:, None] + FFT_N1 * np.arange(n2)[None, :]).reshape(-1, 1)
    lag = ((CF_KERNEL - 1) // 2 - np.arange(CF_TAPS_PAD))[None, :]
    ang = 2.0 * np.pi * k * lag / n
    live = (np.arange(CF_TAPS_PAD) < CF_KERNEL)[None, :]
    m = np.concatenate([np.cos(ang) * live, -np.sin(ang) * live], axis=0)
    return jnp.asarray(m.astype(np.float32)).astype(BF16)


def _cf_spec_kernel(e_ref, w_ref, h_ref):
    h_ref[...] = _dot(e_ref[...], w_ref[...].astype(BF16)).reshape(h_ref.shape).astype(h_ref.dtype)


def _cf_spec(dw, n_lat):
    c = dw.shape[1]
    n2 = n_lat // FFT_N1H
    e = _cf_spec_matrix(n_lat)
    wp = jnp.pad(dw, ((0, CF_TAPS_PAD - CF_KERNEL), (0, 0)))
    tc = HY_SLAB
    return pl.pallas_call(
        _cf_spec_kernel,
        out_shape=jax.ShapeDtypeStruct((1, 2, FFT_K1, n2, c), BF16),
        grid=(c // tc,),
        in_specs=[pl.BlockSpec(e.shape, lambda ci: (0, 0)), pl.BlockSpec((CF_TAPS_PAD, tc), lambda ci: (0, ci))],
        out_specs=pl.BlockSpec((None, 2, FFT_K1, n2, tc), lambda ci: (0, 0, 0, 0, ci)),
        compiler_params=_cparams(1), name="cf_spec",
    )(e, wp)


def _cf_tail_kernel(v_ref, sg_ref, x_ref, mod_ref, lg_ref, lb_ref, w_ref, fg_ref, o_ref):
    v = v_ref[...]
    mu = jnp.mean(v, axis=-1, keepdims=True)
    xc = v - mu
    var = jnp.mean(xc * xc, axis=-1, keepdims=True)
    ln = xc * lax.rsqrt(var + NORM_EPS) * lg_ref[...] + lb_ref[...]
    og = (_silu(ln) * sg_ref[...].astype(F32)).astype(BF16)
    xo = x_ref[...] + mod_ref[2:3, :] * _dot(og, w_ref[...])
    o_ref[...] = _rms(xo, fg_ref[...])


def _cf_tail(v, sg, x, mod, lg, lb, w_out, fg):
    b, n_lat, c = v.shape
    d = x.shape[-1]
    full = lambda a: pl.BlockSpec(a.shape, lambda bi, i: (0,) * a.ndim)
    row = lambda bi, i: (bi, i, 0)
    return pl.pallas_call(
        _cf_tail_kernel,
        out_shape=jax.ShapeDtypeStruct((b, n_lat, d), F32),
        grid=(b, n_lat // ROW_TILE),
        in_specs=[pl.BlockSpec((None, ROW_TILE, c), row),
                  pl.BlockSpec((None, ROW_TILE, c), row),
                  pl.BlockSpec((None, ROW_TILE, d), row),
                  pl.BlockSpec((None, None, 3, d), lambda bi, i: (bi, 0, 0, 0)),
                  full(lg), full(lb), full(w_out), full(fg)],
        out_specs=pl.BlockSpec((None, ROW_TILE, d), row),
        compiler_params=_cparams(2), name="cf_tail",
    )(v, sg, x, mod, lg, lb, w_out, fg)


def _rope_tables(n_lat, n_ctx):
    rows = n_lat // GRID_W
    row = jnp.repeat(jnp.arange(rows, dtype=F32), GRID_W)
    col = jnp.tile(jnp.arange(GRID_W, dtype=F32), rows)
    n_freq = ROPE_DIM // 4
    inv = ROPE_BASE ** (-jnp.arange(n_freq, dtype=F32) / n_freq)
    ang = jnp.concatenate([row[:, None] * inv, col[:, None] * inv], axis=-1)
    cos = jnp.concatenate([jnp.cos(ang), jnp.ones((n_ctx, ROPE_DIM // 2), F32)], axis=0)
    sin = jnp.concatenate([jnp.sin(ang), jnp.zeros((n_ctx, ROPE_DIM // 2), F32)], axis=0)
    return jnp.tile(cos, (1, 2)), jnp.tile(sin, (1, 2))


def kernel(x, c, ctx, c_ctx, norm_g, ada_w, ada_b, final_g, mla_w_in, mla_q_norm_g, mla_kv_norm_g, mla_w_uq, mla_w_ukv, mla_w_out, hy_w_in, hy_conv_w, hy_conv_b, hy_filt_w_in, hy_filt_w_hid, hy_filt_b, hy_filt_freq, hy_filt_w_out, hy_bias, hy_w_out, swa_w_in, swa_sink, swa_w_out, cf_w_in, cf_dw_w, cf_dw_b, cf_ln_g, cf_ln_b, cf_w_out):
    b, n_lat, d = x.shape
    n_ctx = ctx.shape[1]
    depth = norm_g.shape[0]
    assert depth == 4 and n_lat % ROW_TILE == 0 and n_ctx == ATTN_TILE and ROW_TILE % n_ctx == 0
    assert n_lat % (FFT_N1H * SUBLANES) == 0 and n_lat % GRID_W == 0
    n_lat_tiles = n_lat // ROW_TILE
    bf = lambda a: a.astype(BF16)

    cs = jnp.concatenate([c, c_ctx[None, :], jnp.zeros((SUBLANES - b - 1, d), F32)], axis=0)
    mods = _adaln(cs, ada_w, ada_b)[:, :b + 1].reshape(depth, b + 1, 3, d)
    mods = jnp.stack([mods[:, :b], jnp.broadcast_to(mods[:, b:], (depth, b, 3, d))], axis=2)
    cos64, sin64 = _rope_tables(n_lat, ROW_TILE)
    cos128, sin128 = jnp.tile(cos64, (1, 2)), jnp.tile(sin64, (1, 2))

    c2 = MLA_Q_RANK + MLA_KV_RANK + MLA_ROPE
    w_lat = bf(jnp.pad(mla_w_in[0][:, :c2], ((0, 0), (0, LANES - MLA_ROPE))))
    w_gate = bf(mla_w_in[0][:, c2:])
    wuq = bf(jnp.pad(mla_w_uq[0].reshape(MLA_Q_RANK, MLA_HEADS, MLA_QK), ((0, 0), (0, 0), (0, LANES - MLA_ROPE)))
             .reshape(MLA_Q_RANK, MLA_HEADS * 2 * LANES))
    q, k, v, sg = _mla_proj(x, ctx, norm_g[0:1], mods[0], w_lat, w_gate, mla_q_norm_g[0:1], mla_kv_norm_g[0:1],
                            wuq, bf(mla_w_ukv[0]), cos128, sin128, n_lat_tiles)
    o = _mla_attn(q, k, v, n_lat, n_ctx)

    xs, z0, g0, g1, sg = _hy_proj(o, sg, x, ctx, mods[0], bf(mla_w_out[0]), norm_g[1:2], mods[1], bf(hy_w_in[0]),
                                  hy_conv_w[0], hy_conv_b[0][None, :], n_lat_tiles)
    cch = z0.shape[-1]
    filt = (hy_filt_w_in[0], hy_filt_w_hid[0], hy_filt_b[0], hy_filt_freq[0], hy_filt_w_out[0])
    assert cch % HY_SLAB == 0
    mats = _fft_matrices(n_lat)
    spec = _hy_spec(_hy_filter_taps(n_lat, *filt), mats[1], mats[2])
    cfwd, cinv, nfp = _dense_dft_matrices(n_ctx)
    cspec = _hy_ctx_spec(_hy_filter_taps(n_ctx, *filt), cfwd)
    gates = (g0, g1)
    z = z0
    for order in range(HY_ORDER):
        bias = hy_bias[0][order][None, :]
        z = _long_conv(z, spec, order, mats, bias, n_lat, gate=gates[order])
        z = _hy_ctx_conv(z, gates[order], cspec, order, cfwd[:, :n_ctx], cinv, nfp, bias, n_lat, n_ctx)

    nq = SWA_Q_HEADS * SWA_HEAD_DIM
    nkv = SWA_KV_HEADS * SWA_HEAD_DIM
    w_in = swa_w_in[0]
    dup = lambda w: jnp.tile(w.reshape(d, SWA_KV_HEADS, 1, SWA_HEAD_DIM), (1, 1, 2, 1)).reshape(d, 2 * nkv)
    w2 = bf(jnp.concatenate([w_in[:, :nq], dup(w_in[:, nq:nq + nkv]), dup(w_in[:, nq + nkv:nq + 2 * nkv]),
                             w_in[:, nq + 2 * nkv:]], axis=1))
    xs, q, k, v, sg = _swa_proj(z, sg, xs, mods[1], bf(hy_w_out[0]), norm_g[2:3], mods[2], w2, cos128, sin128,
                                n_lat_tiles)
    o = _swa_attn(swa_sink[0], q, k, v, n_lat, n_ctx)

    xs, u, sg = _cf_proj(o, sg, xs, mods[2], bf(swa_w_out[0]), norm_g[3:4], mods[3], bf(cf_w_in[0]), n_lat_tiles)
    v = _long_conv(u, _cf_spec(cf_dw_w[0], n_lat), 0, mats, cf_dw_b[0][None, :], n_lat)
    return _cf_tail(v, sg, xs, mods[3], cf_ln_g[0][None, :], cf_ln_b[0][None, :], bf(cf_w_out[0]), final_g[None, :])
```

```python
import functools
import math

import numpy as np
import jax
import jax.numpy as jnp
from jax import lax
from jax.experimental import pallas as pl
from jax.experimental.pallas import tpu as pltpu

F32 = jnp.float32
BF16 = jnp.bfloat16

GRID_W = 64
NORM_EPS = 1e-6
NEG_INF = -1e30
LOG2E = math.log2(math.e)
ROPE_BASE = 10000.0
ROPE_DIM = 64

MLA_HEADS = 8
MLA_Q_RANK = 384
MLA_KV_RANK = 256
MLA_NOPE = 128
MLA_ROPE = ROPE_DIM
MLA_V = 128
MLA_QK = MLA_NOPE + MLA_ROPE

HY_ORDER = 2
HY_EMB = 33
HY_EMB_PAD = 64
HY_BANDS = (HY_EMB - 1) // 2
HY_DECAY_TARGET = 1e-2
HY_FAST_DECAY = 0.3
HY_SLOW_DECAY = 1.5

SWA_Q_HEADS = 16
SWA_KV_HEADS = 4
SWA_GROUP = SWA_Q_HEADS // SWA_KV_HEADS
SWA_HEAD_DIM = ROPE_DIM
SWA_WINDOW = 128

CF_KERNEL = 31
CF_TAPS_PAD = 32

ROW_TILE = 512
ATTN_TILE = 256
SUBLANES = 8
LANES = 128
FFT_N1 = 64
FFT_N1H = FFT_N1 // 2
FFT_K1 = FFT_N1 // 2 + 1
HY_SLAB = 256
HY_HALO = 16
VMEM_LIMIT = 48 * 1024 * 1024
VMEM_LIMIT_BIG = 56 * 1024 * 1024


def _cparams(n_axes):
    return pltpu.CompilerParams(dimension_semantics=("arbitrary",) * n_axes,
                                vmem_limit_bytes=VMEM_LIMIT)


def _cparams_big(n_axes):
    return pltpu.CompilerParams(dimension_semantics=("arbitrary",) * n_axes,
                                vmem_limit_bytes=VMEM_LIMIT_BIG)


def _once(shape, index_map):
    return pl.BlockSpec(shape, index_map, pipeline_mode=pl.Buffered(1))


def _dot(a, b):
    return jnp.dot(a, b, preferred_element_type=F32)


def _dot_nt(a, b):
    return lax.dot_general(a, b, (((1,), (1,)), ((), ())), preferred_element_type=F32)


def _rms(x, g):
    return x * lax.rsqrt(jnp.mean(x * x, axis=-1, keepdims=True) + NORM_EPS) * g


def _norm_mod(x, g, mod):
    return _rms(x, g) * (1.0 + mod[1:2, :]) + mod[0:1, :]


def _silu(x):
    return x * jax.nn.sigmoid(x)


def _adaln_kernel(c_ref, w_ref, b_ref, o_ref):
    s = _silu(c_ref[...]).astype(BF16)
    o_ref[...] = _dot(s, w_ref[...].astype(BF16)) + b_ref[...]


def _adaln(cs, ada_w, ada_b):
    depth, d, d3 = ada_w.shape
    tn = 2 * ROW_TILE
    return pl.pallas_call(
        _adaln_kernel,
        out_shape=jax.ShapeDtypeStruct((depth, cs.shape[0], d3), F32),
        grid=(depth, d3 // tn),
        in_specs=[pl.BlockSpec(cs.shape, lambda i, j: (0, 0)),
                  pl.BlockSpec((None, d, tn), lambda i, j: (i, 0, j)),
                  pl.BlockSpec((None, 1, tn), lambda i, j: (i, 0, j))],
        out_specs=pl.BlockSpec((None, cs.shape[0], tn), lambda i, j: (i, 0, j)),
        compiler_params=_cparams(2), name="adaln",
    )(cs, ada_w, ada_b.reshape(depth, 1, d3))


def _input_tile(x_ref, ctx_ref, is_ctx):
    pad = jnp.zeros((x_ref.shape[0] - ctx_ref.shape[0], x_ref.shape[1]), F32)
    return jnp.where(is_ctx, jnp.concatenate([ctx_ref[...], pad], axis=0), x_ref[...])


def _input_specs(d, n_ctx, n_lat_tiles):
    return [pl.BlockSpec((None, ROW_TILE, d), lambda bi, i: (bi, jnp.minimum(i, n_lat_tiles - 1), 0)),
            pl.BlockSpec((None, n_ctx, d), lambda bi, i: (bi, 0, 0))]


def _residual_in(o_ref, sgp_ref, x_ref, modp_ref, wout_ref, xo_ref):
    og = (o_ref[...].astype(F32) * sgp_ref[...].astype(F32)).astype(BF16)
    x = x_ref[...] + modp_ref[2:3, :] * _dot(og, wout_ref[...])
    xo_ref[...] = x
    return x


def _residual_specs(o, d, n_lat_tiles):
    width = o.shape[-1]
    row = lambda bi, i: (bi, i, 0)
    return [pl.BlockSpec((None, ROW_TILE, width), row), pl.BlockSpec((None, ROW_TILE, width), row),
            pl.BlockSpec((None, ROW_TILE, d), row),
            pl.BlockSpec((None, None, 3, d), lambda bi, i: (bi, i // n_lat_tiles, 0, 0)),
            _once((width, d), lambda bi, i: (0, 0))]


def _rope_lanes(x, cos, sin):
    half = ROPE_DIM // 2
    lane = lax.broadcasted_iota(jnp.int32, x.shape, 1) % ROPE_DIM
    rot = jnp.where(lane < half, -pltpu.roll(x, LANES - half, 1), pltpu.roll(x, half, 1))
    return x * cos + rot * sin


def _mla_proj_kernel(x_ref, ctx_ref, g_ref, mod_ref, wl_ref, wg_ref, qg_ref, kvg_ref, wuq_ref, wukv_ref,
                     cos_ref, sin_ref, q_ref, k_ref, v_ref, sg_ref, *, n_lat_tiles):
    x = _input_tile(x_ref, ctx_ref, pl.program_id(1) >= n_lat_tiles)
    h = _norm_mod(x, g_ref[...], mod_ref[...]).astype(BF16)
    sg_ref[...] = _silu(_dot(h, wg_ref[...])).astype(sg_ref.dtype)
    res = _dot(h, wl_ref[...])
    c0, c1 = MLA_Q_RANK, MLA_Q_RANK + MLA_KV_RANK
    cqn = _rms(res[:, :c0], qg_ref[...]).astype(BF16)
    ckvn = _rms(res[:, c0:c1], kvg_ref[...]).astype(BF16)
    cos, sin = cos_ref[...], sin_ref[...]

    def rope(slab):
        return _rope_lanes(slab, cos, sin)[:, :MLA_ROPE]

    k_rope_t = rope(res[:, c1:c1 + LANES]).T.astype(k_ref.dtype)
    scale = MLA_QK ** -0.5 * LOG2E
    per = 2 * LANES
    for h0 in range(0, MLA_HEADS, 2):
        q2 = _dot(cqn, wuq_ref[:, h0 * per:(h0 + 2) * per])
        kv2 = _dot(ckvn, wukv_ref[:, h0 * per:(h0 + 2) * per])
        for j in range(2):
            hd = h0 + j
            q, kv = q2[:, j * per:(j + 1) * per], kv2[:, j * per:(j + 1) * per]
            q_ref[hd, :, :MLA_NOPE] = (q[:, :MLA_NOPE] * scale).astype(q_ref.dtype)
            q_ref[hd, :, MLA_NOPE:] = (rope(q[:, MLA_NOPE:]) * scale).astype(q_ref.dtype)
            k_ref[hd, :MLA_NOPE, :] = kv[:, :MLA_NOPE].T.astype(k_ref.dtype)
            k_ref[hd, MLA_NOPE:, :] = k_rope_t
            v_ref[hd, :, :MLA_V] = kv[:, MLA_NOPE:].astype(v_ref.dtype)
            v_ref[hd, :, MLA_V:] = jnp.ones((v_ref.shape[1], MLA_V), v_ref.dtype)


def _mla_proj(x, ctx, g, mod, wl, wg, qg, kvg, wuq, wukv, cos, sin, n_lat_tiles):
    b, n_lat, d = x.shape
    nt = n_lat_tiles + 1
    t = nt * ROW_TILE
    width = MLA_HEADS * MLA_V
    full = lambda a: _once(a.shape, lambda bi, i: (0,) * a.ndim)
    head_out = lambda n: pl.BlockSpec((None, MLA_HEADS, ROW_TILE, n), lambda bi, i: (bi, 0, i, 0))
    return pl.pallas_call(
        functools.partial(_mla_proj_kernel, n_lat_tiles=n_lat_tiles),
        out_shape=(jax.ShapeDtypeStruct((b, MLA_HEADS, t, MLA_QK), BF16),
                   jax.ShapeDtypeStruct((b, MLA_HEADS, MLA_QK, t), BF16),
                   jax.ShapeDtypeStruct((b, MLA_HEADS, t, 2 * MLA_V), BF16),
                   jax.ShapeDtypeStruct((b, t, width), BF16)),
        grid=(b, nt),
        in_specs=_input_specs(d, ctx.shape[1], n_lat_tiles) +
                 [full(g),
                  pl.BlockSpec((None, None, 3, d), lambda bi, i: (bi, i // n_lat_tiles, 0, 0)),
                  full(wl), full(wg), full(qg), full(kvg), full(wuq), full(wukv),
                  pl.BlockSpec((ROW_TILE, LANES), lambda bi, i: (i, 0)),
                  pl.BlockSpec((ROW_TILE, LANES), lambda bi, i: (i, 0))],
        out_specs=(head_out(MLA_QK),
                   pl.BlockSpec((None, MLA_HEADS, MLA_QK, ROW_TILE), lambda bi, i: (bi, 0, 0, i)),
                   head_out(2 * MLA_V),
                   pl.BlockSpec((None, ROW_TILE, width), lambda bi, i: (bi, i, 0))),
        compiler_params=_cparams(2), name="mla_proj",
    )(x, ctx, g, mod, wl, wg, qg, kvg, wuq, wukv, cos, sin)


def _attend(q, k_parts, v_parts):
    s = [_dot(q, k) for k in k_parts]
    m = functools.reduce(jnp.maximum, [jnp.max(x, axis=-1, keepdims=True) for x in s])
    ov = functools.reduce(jnp.add, [_dot(jnp.exp2(x - m).astype(BF16), v) for x, v in zip(s, v_parts)])
    n = ov.shape[-1] // 2
    return ov[:, :n] / ov[:, n:]


MLA_HEADS_PER_STEP = 2


def _mla_attn_kernel(q_ref, k_ref, v_ref, o_ref, *, n_lat_tiles, n_lat, n_ctx):
    qi = pl.program_id(2)

    n_keys = n_lat + n_ctx

    @pl.when(qi < n_lat_tiles)
    def _():
        chains = [(hd, r0) for hd in range(MLA_HEADS_PER_STEP) for r0 in range(0, q_ref.shape[1], ATTN_TILE)]
        s = [_dot(q_ref[hd, r0:r0 + ATTN_TILE, :], k_ref[hd, :, 0:n_keys]) for hd, r0 in chains]
        p = [jnp.exp2(x - jnp.max(x, axis=-1, keepdims=True)).astype(BF16) for x in s]
        for (hd, r0), pc in zip(chains, p):
            ov = _dot(pc, v_ref[hd, 0:n_keys, :])
            o = ov[:, :MLA_V] / ov[:, MLA_V:]
            o_ref[r0:r0 + ATTN_TILE, hd * MLA_V:(hd + 1) * MLA_V] = o.astype(o_ref.dtype)

    @pl.when(qi >= n_lat_tiles)
    def _():
        o_ref[n_ctx:, :] = jnp.zeros((o_ref.shape[0] - n_ctx, o_ref.shape[1]), o_ref.dtype)
        for hd in range(MLA_HEADS_PER_STEP):
            k = k_ref[hd, :, n_lat:n_keys]
            v = v_ref[hd, n_lat:n_keys, :]
            o = _attend(q_ref[hd, 0:n_ctx, :], [k], [v])
            o_ref[0:n_ctx, hd * MLA_V:(hd + 1) * MLA_V] = o.astype(o_ref.dtype)


def _mla_attn(q, k, v, n_lat, n_ctx):
    b, hds, t, _ = q.shape
    nt = t // ROW_TILE
    hps = MLA_HEADS_PER_STEP
    kern = functools.partial(_mla_attn_kernel, n_lat_tiles=n_lat // ROW_TILE, n_lat=n_lat, n_ctx=n_ctx)
    return pl.pallas_call(
        kern,
        out_shape=jax.ShapeDtypeStruct((b, t, hds * MLA_V), BF16),
        grid=(b, hds // hps, nt),
        in_specs=[pl.BlockSpec((None, hps, ROW_TILE, MLA_QK), lambda bi, h, i: (bi, h, i, 0)),
                  pl.BlockSpec((None, hps, MLA_QK, t), lambda bi, h, i: (bi, h, 0, 0)),
                  pl.BlockSpec((None, hps, t, 2 * MLA_V), lambda bi, h, i: (bi, h, 0, 0))],
        out_specs=pl.BlockSpec((None, ROW_TILE, hps * MLA_V), lambda bi, h, i: (bi, i, h)),
        compiler_params=_cparams_big(3), name="mla_attn",
    )(q, k, v)


def _hy_proj_kernel(op_ref, o_ref, on_ref, sp_ref, s_ref, sn_ref, xp_ref, x_ref, ctx_ref, xn_ref, modp_ref, wout_ref,
                    g_ref, mod_ref, w_ref, cw_ref, cb_ref, xo_ref, z_ref, g0_ref, g1_ref, sg_ref,
                    *, n_lat_tiles, n_ctx):
    i = pl.program_id(1)
    tm = x_ref.shape[0]
    c = z_ref.shape[-1]
    n = tm + 2 * SUBLANES
    halo = xp_ref.shape[0]
    cat = lambda a, m, z: jnp.concatenate([a[...], m, z[...]], axis=0)
    og = (cat(op_ref, o_ref[...], on_ref).astype(F32) * cat(sp_ref, s_ref[...], sn_ref).astype(F32)).astype(BF16)
    xe = cat(xp_ref, _input_tile(x_ref, ctx_ref, i >= n_lat_tiles), xn_ref)
    xe = xe + modp_ref[2:3, :] * _dot(og, wout_ref[...])
    xo_ref[...] = xe[halo:halo + tm]
    xa = xe[halo - SUBLANES:halo + tm + SUBLANES]
    ctx_tile = i == n_lat_tiles
    first = jnp.logical_or(i == 0, ctx_tile)
    last = jnp.logical_or(i == n_lat_tiles - 1, ctx_tile)
    rows = lax.broadcasted_iota(jnp.int32, (n, 1), 0)
    end = jnp.where(ctx_tile, n_ctx + SUBLANES, jnp.where(last, tm + SUBLANES, n))
    valid = jnp.logical_and(jnp.logical_or(rows >= SUBLANES, jnp.logical_not(first)), rows < end)
    hf = jnp.where(valid, _norm_mod(xa, g_ref[...], mod_ref[...]), 0.0)
    h = hf.astype(BF16)
    outs = (z_ref, g0_ref, g1_ref)
    for j in range(len(outs)):
        u = _dot(h, w_ref[:, j * c:(j + 1) * c])
        cw = cw_ref[:, j * c:(j + 1) * c]
        cv = cw[0:1, :] * pltpu.roll(u, 1, 0) + cw[1:2, :] * u + cw[2:3, :] * pltpu.roll(u, n - 1, 0)
        outs[j][...] = cv[SUBLANES:SUBLANES + tm] + cb_ref[:, j * c:(j + 1) * c]
    h_mid = hf[SUBLANES:SUBLANES + tm].astype(BF16)
    sg_ref[...] = _silu(_dot(h_mid, w_ref[:, len(outs) * c:])).astype(sg_ref.dtype)


def _hy_proj(o, sgp, x, ctx, modp, w_out, g, mod, w, cw, cb, n_lat_tiles):
    b, t, width = o.shape
    n_lat, d = x.shape[1:]
    n_ctx = ctx.shape[1]
    c = w.shape[1] // (HY_ORDER + 2)
    per = ROW_TILE // HY_HALO
    last = n_lat // HY_HALO - 1
    full = lambda a: _once(a.shape, lambda bi, i: (0,) * a.ndim)
    row = lambda bi, i: (bi, i, 0)
    prev = lambda bi, i: (bi, jnp.maximum(i * per - 1, 0), 0)
    nxt = lambda bi, i: (bi, jnp.minimum((i + 1) * per, last), 0)
    x_main, x_ctx = _input_specs(d, n_ctx, n_lat_tiles)
    kern = functools.partial(_hy_proj_kernel, n_lat_tiles=n_lat_tiles, n_ctx=n_ctx)
    halo = lambda n, m: pl.BlockSpec((None, HY_HALO, n), m)
    return pl.pallas_call(
        kern,
        out_shape=(jax.ShapeDtypeStruct((b, t, d), F32),) + (jax.ShapeDtypeStruct((b, t, c), F32),) * 3 +
                  (jax.ShapeDtypeStruct((b, t, c), BF16),),
        grid=(b, t // ROW_TILE),
        in_specs=[halo(width, prev), pl.BlockSpec((None, ROW_TILE, width), row), halo(width, nxt),
                  halo(width, prev), pl.BlockSpec((None, ROW_TILE, width), row), halo(width, nxt),
                  halo(d, prev), x_main, x_ctx, halo(d, nxt),
                  pl.BlockSpec((None, None, 3, d), lambda bi, i: (bi, i // n_lat_tiles, 0, 0)),
                  full(w_out), full(g),
                  pl.BlockSpec((None, None, 3, d), lambda bi, i: (bi, i // n_lat_tiles, 0, 0)),
                  full(w), full(cw), full(cb)],
        out_specs=(pl.BlockSpec((None, ROW_TILE, d), row),) + (pl.BlockSpec((None, ROW_TILE, c), row),) * 4,
        compiler_params=_cparams(2), name="hy_proj",
    )(o, o, o, sgp, sgp, sgp, x, x, ctx, x, modp, w_out, g, mod, w, cw, cb)


def _hy_filter_kernel(ef_ref, eb_ref, w0_ref, wh_ref, b_ref, fr_ref, wf_ref, wb_ref, dl_ref, o_ref):
    i = pl.program_id(0)
    c = dl_ref.shape[-1]
    half = ef_ref.shape[0]

    def taps(e, wo):
        hdn = e
        ws = (w0_ref[...], wh_ref[0], wh_ref[1])
        for k in range(3):
            hdn = jnp.sin(fr_ref[k:k + 1, :] * (_dot(hdn.astype(BF16), ws[k].astype(BF16)) + b_ref[k:k + 1, :]))
        hh = _dot(hdn.astype(BF16), wo.astype(BF16))
        out = []
        for part in range(2):
            decay = jnp.exp(-e[:, part * HY_EMB_PAD:part * HY_EMB_PAD + 1] * dl_ref[...])
            out.append([hh[:, (part * HY_ORDER + o) * c:(part * HY_ORDER + o + 1) * c] * decay
                        for o in range(HY_ORDER)])
        return out

    hf = taps(ef_ref[...], wf_ref[...])
    hb = taps(eb_ref[...], wb_ref[...])
    rows = lax.broadcasted_iota(jnp.int32, (half, 1), 0)
    lag0 = jnp.logical_and(rows == 0, i == 0)
    for o in range(HY_ORDER):
        o_ref[o, 0, 0:half] = hf[0][o] + jnp.where(lag0, hb[0][o], 0.0)
        o_ref[o, 1, 0:half] = jnp.where(lag0, 0.0, hb[0][o])
        o_ref[o, 0, half:] = hf[1][o]
        o_ref[o, 1, half:] = hb[1][o]


def _hy_position_features(n):
    t = jnp.linspace(0.0, 1.0, n, dtype=F32)[:, None]
    wpos = (2.0 * math.pi / n) * jnp.arange(n, dtype=F32)[:, None]
    bands = jnp.linspace(1e-4, HY_BANDS - 1, HY_BANDS, dtype=F32)[None, :]
    hdn = jnp.concatenate([t, jnp.cos(bands * wpos), -jnp.sin(bands * wpos)], axis=-1)
    return jnp.pad(hdn, ((0, 0), (0, HY_EMB_PAD - HY_EMB)))


def _hy_filter_taps(n, w_in, w_hid, bias, freq, w_out):
    c = w_out.shape[1] // (2 * HY_ORDER)
    ef = _hy_position_features(n)
    rev = (n - jnp.arange(n)) % n
    eb = ef[rev]
    w0 = jnp.pad(w_in, ((0, HY_EMB_PAD - HY_EMB), (0, 0)))
    wo = w_out.reshape(w_out.shape[0], HY_ORDER, 2, c)
    wf = wo[:, :, 0, :].reshape(w_out.shape[0], HY_ORDER * c)
    wb = wo[:, :, 1, :].reshape(w_out.shape[0], HY_ORDER * c)
    deltas = jnp.abs(jnp.linspace(math.log(HY_DECAY_TARGET) / HY_FAST_DECAY,
                                  math.log(HY_DECAY_TARGET) / HY_SLOW_DECAY, c, dtype=F32))[None, :]
    tl = min(n, ROW_TILE)
    half = tl // 2
    pair = lambda e: e.reshape(n // tl, 2, half, HY_EMB_PAD).transpose(0, 2, 1, 3).reshape(n // 2, 2 * HY_EMB_PAD)
    both = lambda w: jnp.kron(jnp.eye(2, dtype=F32), w)
    twice = lambda v: jnp.tile(v, (1, 2))
    w_hid2 = jnp.stack([both(w_hid[0]), both(w_hid[1])])
    full = lambda a: pl.BlockSpec(a.shape, lambda i: (0,) * a.ndim)
    args = (pair(ef), pair(eb), both(w0), w_hid2, twice(bias), twice(freq), both(wf), both(wb), deltas)
    return pl.pallas_call(
        _hy_filter_kernel,
        out_shape=jax.ShapeDtypeStruct((HY_ORDER, 2, n, c), F32),
        grid=(n // tl,),
        in_specs=[pl.BlockSpec((half, 2 * HY_EMB_PAD), lambda i: (i, 0)),
                  pl.BlockSpec((half, 2 * HY_EMB_PAD), lambda i: (i, 0))] + [full(a) for a in args[2:]],
        out_specs=pl.BlockSpec((HY_ORDER, 2, tl, c), lambda i: (0, 0, i, 0)),
        compiler_params=_cparams(1), name="hy_filter",
    )(*args)


def _fft_matrices(n_lat):
    n = 2 * n_lat
    n2 = n_lat // FFT_N1H
    k1 = np.arange(FFT_K1)
    n1 = np.arange(FFT_N1H)
    ang = 2.0 * np.pi * np.outer(k1, n1) / FFT_N1
    eye = np.eye(SUBLANES)
    f1 = np.kron(np.concatenate([np.cos(ang), -np.sin(ang)], axis=0), eye)
    wgt = np.full(FFT_K1, 2.0)
    wgt[0] = wgt[-1] = 1.0
    g1 = np.kron(np.concatenate([wgt * np.cos(ang.T), (-wgt * np.sin(ang.T))[:, 1:-1]], axis=1) / n, eye)
    idx = np.arange(n2)
    f2 = np.zeros((FFT_K1, 2 * n2, 2 * n2))
    f2i = np.zeros((FFT_K1, 2 * n2, 2 * n2))
    for a in range(FFT_K1):
        ph = 2.0 * np.pi * (np.outer(idx, idx) / n2 + (a * idx)[None, :] / n)
        f2[a] = np.block([[np.cos(ph), np.sin(ph)], [-np.sin(ph), np.cos(ph)]])
        f2i[a] = np.block([[np.cos(ph.T), -np.sin(ph.T)], [np.sin(ph.T), np.cos(ph.T)]])
    sign = np.kron(np.tile((-1.0) ** k1, 2), np.ones(SUBLANES))[:, None]
    f1s = np.concatenate([f1, sign * f1], axis=1)
    as_bf16 = lambda m: jnp.asarray(m.astype(np.float32)).astype(BF16)
    return as_bf16(f1), as_bf16(f1s), as_bf16(f2), as_bf16(f2i), as_bf16(g1)


def _dense_dft_matrices(n_ctx):
    n = 2 * n_ctx
    nf = n_ctx + 1
    nfp = -(-nf // SUBLANES) * SUBLANES
    k = np.arange(nf)
    ang = 2.0 * np.pi * np.outer(k, np.arange(n)) / n
    fwd = np.zeros((2 * nfp, n))
    fwd[:nf] = np.cos(ang)
    fwd[nfp:nfp + nf] = -np.sin(ang)
    wgt = np.full(nf, 2.0)
    wgt[0] = wgt[-1] = 1.0
    angi = 2.0 * np.pi * np.outer(np.arange(n_ctx), k) / n
    inv = np.zeros((n_ctx, 2 * nfp))
    inv[:, :nf] = wgt * np.cos(angi) / n
    inv[:, nfp:nfp + nf] = -wgt * np.sin(angi) / n
    as_bf16 = lambda m: jnp.asarray(m.astype(np.float32)).astype(BF16)
    return as_bf16(fwd), as_bf16(inv), nfp


def _hy_spec_kernel(t_ref, f1s_ref, f2_ref, h_ref, a_ref):
    _, n1h, nj, s, tc = t_ref.shape
    n2 = nj * s
    for j in range(nj):
        sig = jnp.concatenate([t_ref[0, :, j].reshape(n1h * s, tc), t_ref[1, :, j].reshape(n1h * s, tc)], axis=0)
        a_ref[:, :, j] = _dot(f1s_ref[...], sig.astype(BF16)).reshape(2, FFT_K1, s, tc)
    for k in range(FFT_K1):
        a = a_ref[:, k].reshape(2 * n2, tc).astype(BF16)
        h_ref[:, k] = _dot(f2_ref[k], a).reshape(2, n2, tc).astype(h_ref.dtype)


def _hy_spec(taps, f1s, f2):
    _, _, n_lat, c = taps.shape
    n2 = n_lat // FFT_N1H
    nj = n2 // SUBLANES
    tc = HY_SLAB
    return pl.pallas_call(
        _hy_spec_kernel,
        out_shape=jax.ShapeDtypeStruct((HY_ORDER, 2, FFT_K1, n2, c), BF16),
        grid=(c // tc, HY_ORDER),
        in_specs=[pl.BlockSpec((None, 2, FFT_N1H, nj, SUBLANES, tc), lambda ci, o: (o, 0, 0, 0, 0, ci)),
                  _once(f1s.shape, lambda ci, o: (0, 0)),
                  _once(f2.shape, lambda ci, o: (0, 0, 0))],
        out_specs=pl.BlockSpec((None, 2, FFT_K1, n2, tc), lambda ci, o: (o, 0, 0, 0, ci)),
        scratch_shapes=[pltpu.VMEM((2, FFT_K1, nj, SUBLANES, tc), F32)],
        compiler_params=_cparams_big(2), name="hy_spec",
    )(taps.reshape(HY_ORDER, 2, FFT_N1H, nj, SUBLANES, c), f1s, f2)


def _conv_stages(z_ref, h_ref, f1_ref, f2_ref, f2i_ref, g1_ref, a_ref, emit):
    n1h, nj, s, tc = z_ref.shape
    n2 = nj * s
    for j in range(nj):
        zz = z_ref[:, j].reshape(n1h * s, tc).astype(BF16)
        a_ref[:, :, j] = _dot(f1_ref[...], zz).reshape(2, FFT_K1, s, tc)
    for k in range(FFT_K1):
        x = _dot(f2_ref[k], a_ref[:, k].reshape(2 * n2, tc).astype(BF16))
        xr, xi = x[:n2], x[n2:]
        hr, hi = h_ref[0, k].astype(F32), h_ref[1, k].astype(F32)
        y = jnp.concatenate([xr * hr - xi * hi, xr * hi + xi * hr], axis=0).astype(BF16)
        a_ref[:, k] = _dot(f2i_ref[k], y).reshape(2, nj, s, tc)
    for j in range(nj):
        bp = a_ref[:, :, j].reshape(2 * FFT_K1 * s, tc)
        bp = jnp.concatenate([bp[:FFT_K1 * s], bp[(FFT_K1 + 1) * s:(2 * FFT_K1 - 1) * s]], axis=0)
        emit(j, _dot(g1_ref[...], bp.astype(BF16)).reshape(n1h, s, tc))


def _hy_conv_kernel(z_ref, gt_ref, bias_ref, h_ref, f1_ref, f2_ref, f2i_ref, g1_ref, o_ref, a_ref):
    def emit(j, y):
        o_ref[:, j] = gt_ref[:, j] * (y + z_ref[:, j] * bias_ref[...])
    _conv_stages(z_ref, h_ref, f1_ref, f2_ref, f2i_ref, g1_ref, a_ref, emit)


def _cf_conv_kernel(z_ref, bias_ref, h_ref, f1_ref, f2_ref, f2i_ref, g1_ref, o_ref, a_ref):
    def emit(j, y):
        o_ref[:, j] = y + bias_ref[...]
    _conv_stages(z_ref, h_ref, f1_ref, f2_ref, f2i_ref, g1_ref, a_ref, emit)


def _long_conv(z, h, order, mats, bias, n_lat, gate=None):
    f1, _, f2, f2i, g1 = mats
    bz, t, c = z.shape
    n2 = n_lat // FFT_N1H
    nj = n2 // SUBLANES
    tc = HY_SLAB
    view = (bz, t // n2, nj, SUBLANES, c)
    slab = pl.BlockSpec((None, FFT_N1H, nj, SUBLANES, tc), lambda ci, bi: (bi, 0, 0, 0, ci))
    data, data_specs = [z.reshape(view)], [slab]
    if gate is not None:
        data, data_specs = data + [gate.reshape(view)], data_specs + [slab]
    out = pl.pallas_call(
        _hy_conv_kernel if gate is not None else _cf_conv_kernel,
        out_shape=jax.ShapeDtypeStruct(view, F32),
        grid=(c // tc, bz),
        in_specs=data_specs +
                 [pl.BlockSpec((1, tc), lambda ci, bi: (0, ci)),
                  _once((None, 2, FFT_K1, n2, tc), lambda ci, bi: (order, 0, 0, 0, ci)),
                  _once(f1.shape, lambda ci, bi: (0, 0)),
                  _once(f2.shape, lambda ci, bi: (0, 0, 0)),
                  _once(f2i.shape, lambda ci, bi: (0, 0, 0)),
                  _once(g1.shape, lambda ci, bi: (0, 0))],
        out_specs=slab,
        input_output_aliases={0: 0},
        scratch_shapes=[pltpu.VMEM((2, FFT_K1, nj, SUBLANES, tc), F32)],
        compiler_params=_cparams_big(2), name="long_conv",
    )(*data, bias, h, f1, f2, f2i, g1)
    return out.reshape(bz, t, c)


def _hy_ctx_spec_kernel(t_ref, f_ref, h_ref):
    sig = jnp.concatenate([t_ref[0], t_ref[1]], axis=0).astype(BF16)
    h_ref[...] = _dot(f_ref[...], sig)


def _hy_ctx_spec(taps, fwd):
    _, _, n_ctx, c = taps.shape
    return pl.pallas_call(
        _hy_ctx_spec_kernel,
        out_shape=jax.ShapeDtypeStruct((HY_ORDER, fwd.shape[0], c), F32),
        grid=(HY_ORDER,),
        in_specs=[pl.BlockSpec((None, 2, n_ctx, c), lambda o: (o, 0, 0, 0)),
                  pl.BlockSpec(fwd.shape, lambda o: (0, 0))],
        out_specs=pl.BlockSpec((None, fwd.shape[0], c), lambda o: (o, 0, 0)),
        compiler_params=_cparams(1), name="hy_ctx_spec",
    )(taps, fwd)


def _hy_ctx_conv_kernel(zin_ref, gt_ref, h_ref, f_ref, fi_ref, bias_ref, o_ref, *, nfp):
    z = zin_ref[...]
    x = _dot(f_ref[...], z.astype(BF16))
    xr, xi = x[:nfp], x[nfp:]
    hr, hi = h_ref[:nfp, :], h_ref[nfp:, :]
    y = jnp.concatenate([xr * hr - xi * hi, xr * hi + xi * hr], axis=0).astype(BF16)
    n_ctx = z.shape[0]
    o_ref[0:n_ctx, :] = gt_ref[...] * (_dot(fi_ref[...], y) + z * bias_ref[...])
    o_ref[n_ctx:, :] = jnp.zeros((o_ref.shape[0] - n_ctx, o_ref.shape[1]), o_ref.dtype)


def _hy_ctx_conv(z, gate, h, order, fwd_half, inv, nfp, bias, n_lat, n_ctx):
    bz, t, c = z.shape
    blk = pl.BlockSpec((None, n_ctx, c), lambda bi: (bi, n_lat // n_ctx, 0))
    kern = functools.partial(_hy_ctx_conv_kernel, nfp=nfp)
    return pl.pallas_call(
        kern,
        out_shape=jax.ShapeDtypeStruct(z.shape, F32),
        grid=(bz,),
        in_specs=[blk, blk,
                  pl.BlockSpec((None, 2 * nfp, c), lambda bi: (order, 0, 0)),
                  pl.BlockSpec(fwd_half.shape, lambda bi: (0, 0)),
                  pl.BlockSpec(inv.shape, lambda bi: (0, 0)),
                  pl.BlockSpec((1, c), lambda bi: (0, 0))],
        out_specs=pl.BlockSpec((None, t - n_lat, c), lambda bi: (bi, n_lat // (t - n_lat), 0)),
        input_output_aliases={0: 0},
        compiler_params=_cparams(1), name="hy_ctx_conv",
    )(z, gate, h, fwd_half, inv, bias)


def _swa_proj_kernel(o_ref, sgp_ref, x_ref, modp_ref, wout_ref, g_ref, mod_ref, w_ref, cos_ref, sin_ref,
                     xo_ref, q_ref, k_ref, v_ref, sg_ref):
    x = _residual_in(o_ref, sgp_ref, x_ref, modp_ref, wout_ref, xo_ref)
    h = _norm_mod(x, g_ref[...], mod_ref[...]).astype(BF16)
    cos, sin = cos_ref[...], sin_ref[...]
    nq = q_ref.shape[-1]
    nk = SWA_KV_HEADS * LANES
    scale = SWA_HEAD_DIM ** -0.5 * LOG2E
    for c0 in range(0, nq, nk):
        res = _dot(h, w_ref[:, c0:c0 + nk])
        for cb in range(nk // LANES):
            sl = slice(cb * LANES, (cb + 1) * LANES)
            q_ref[:, c0 + cb * LANES:c0 + (cb + 1) * LANES] = (_rope_lanes(res[:, sl], cos, sin) * scale).astype(q_ref.dtype)
    res = _dot(h, w_ref[:, nq:nq + nk])
    for hd in range(SWA_KV_HEADS):
        k_ref[hd] = _rope_lanes(res[:, hd * LANES:(hd + 1) * LANES], cos, sin).astype(k_ref.dtype)
    res = _dot(h, w_ref[:, nq + nk:nq + 2 * nk])
    for hd in range(SWA_KV_HEADS):
        v_ref[hd, :, :LANES] = res[:, hd * LANES:(hd + 1) * LANES].astype(v_ref.dtype)
        v_ref[hd, :, LANES:] = jnp.ones((v_ref.shape[1], LANES), v_ref.dtype)
    for c0 in range(0, nq, nk):
        sg_ref[:, c0:c0 + nk] = _silu(_dot(h, w_ref[:, nq + 2 * nk + c0:nq + 2 * nk + c0 + nk])).astype(sg_ref.dtype)


def _swa_proj(o, sgp, x, modp, w_out, g, mod, w, cos, sin, n_lat_tiles):
    b, t, d = x.shape
    nt = t // ROW_TILE
    nq = SWA_Q_HEADS * SWA_HEAD_DIM
    full = lambda a: _once(a.shape, lambda bi, i: (0,) * a.ndim)
    row = lambda bi, i: (bi, i, 0)
    kv = lambda n: pl.BlockSpec((None, SWA_KV_HEADS, ROW_TILE, n), lambda bi, i: (bi, 0, i, 0))
    return pl.pallas_call(
        _swa_proj_kernel,
        out_shape=(jax.ShapeDtypeStruct((b, t, d), F32),
                   jax.ShapeDtypeStruct((b, t, nq), BF16),
                   jax.ShapeDtypeStruct((b, SWA_KV_HEADS, t, LANES), BF16),
                   jax.ShapeDtypeStruct((b, SWA_KV_HEADS, t, 2 * LANES), BF16),
                   jax.ShapeDtypeStruct((b, t, nq), BF16)),
        grid=(b, nt),
        in_specs=_residual_specs(o, d, n_lat_tiles) +
                 [full(g),
                  pl.BlockSpec((None, None, 3, d), lambda bi, i: (bi, i // n_lat_tiles, 0, 0)),
                  full(w),
                  pl.BlockSpec((ROW_TILE, LANES), lambda bi, i: (i, 0)),
                  pl.BlockSpec((ROW_TILE, LANES), lambda bi, i: (i, 0))],
        out_specs=(pl.BlockSpec((None, ROW_TILE, d), row), pl.BlockSpec((None, ROW_TILE, nq), row),
                   kv(LANES), kv(2 * LANES), pl.BlockSpec((None, ROW_TILE, nq), row)),
        compiler_params=_cparams(2), name="swa_proj",
    )(o, sgp, x, modp, w_out, g, mod, w, cos, sin)


SWA_GROUPS_PER_STEP = 2


def _swa_attn_kernel(sink_ref, q_ref, k_ref, v_ref, o_ref, *, n_lat_tiles, n_lat, n_ctx):
    g = pl.program_id(1)
    qi = pl.program_id(2)
    tq = SWA_WINDOW
    span = tq + 2 * SWA_WINDOW
    cols = q_ref.shape[1] // LANES
    lane_half = lax.broadcasted_iota(jnp.int32, (tq, LANES), 1) // SWA_HEAD_DIM
    first = lax.broadcasted_iota(jnp.int32, (2 * tq, 1), 0) < tq
    kv_of = lambda col: col // (SWA_GROUP // 2)

    def pair_query(r0, col):
        qcol = q_ref[r0:r0 + tq, col * LANES:(col + 1) * LANES]
        zero = jnp.zeros_like(qcol)
        return jnp.concatenate([jnp.where(lane_half == 0, qcol, zero), jnp.where(lane_half == 1, qcol, zero)], axis=0)

    def pair_sink(col):
        base = g * (2 * cols) + 2 * col
        return jnp.where(first, sink_ref[base], sink_ref[base + 1]) * LOG2E

    def finish(r0, col, ov, m, sink):
        o = ov[:, :LANES] / (ov[:, LANES:] + jnp.exp2(sink - m))
        o_ref[r0:r0 + tq, col * LANES:(col + 1) * LANES] = jnp.where(lane_half == 0, o[:tq], o[tq:]).astype(o_ref.dtype)

    @pl.when(qi < n_lat_tiles)
    def _():
        chains = []
        for r0 in range(0, q_ref.shape[0], tq):
            q0 = qi * q_ref.shape[0] + r0
            start = pl.multiple_of(jnp.clip(q0 - SWA_WINDOW, 0, n_lat - span), SWA_WINDOW)
            qpos = q0 + lax.broadcasted_iota(jnp.int32, (tq, 1), 0)
            kpos = start + lax.broadcasted_iota(jnp.int32, (1, span), 1)
            band = jnp.abs(kpos - qpos) <= SWA_WINDOW
            for col in range(cols):
                kv = kv_of(col)
                q2, sink = pair_query(r0, col), pair_sink(col)
                sw = _dot_nt(q2, k_ref[kv, pl.ds(start, span), :])
                sw = jnp.concatenate([jnp.where(band, sw[:tq], NEG_INF), jnp.where(band, sw[tq:], NEG_INF)], axis=0)
                sc = _dot_nt(q2, k_ref[kv, pl.ds(n_lat, n_ctx), :])
                m = jnp.maximum(jnp.maximum(jnp.max(sw, axis=-1, keepdims=True),
                                            jnp.max(sc, axis=-1, keepdims=True)), sink)
                chains.append((r0, col, start, jnp.exp2(sw - m).astype(BF16), jnp.exp2(sc - m).astype(BF16), m, sink))
        for r0, col, start, pw, pc, m, sink in chains:
            kv = kv_of(col)
            ov = _dot(pw, v_ref[kv, pl.ds(start, span), :]) + _dot(pc, v_ref[kv, pl.ds(n_lat, n_ctx), :])
            finish(r0, col, ov, m, sink)

    @pl.when(qi >= n_lat_tiles)
    def _():
        o_ref[n_ctx:, :] = jnp.zeros((o_ref.shape[0] - n_ctx, o_ref.shape[1]), o_ref.dtype)
        for r0 in range(0, n_ctx, tq):
            for col in range(cols):
                kv = kv_of(col)
                q2, sink = pair_query(r0, col), pair_sink(col)
                sc = _dot_nt(q2, k_ref[kv, pl.ds(n_lat, n_ctx), :])
                m = jnp.maximum(jnp.max(sc, axis=-1, keepdims=True), sink)
                finish(r0, col, _dot(jnp.exp2(sc - m).astype(BF16), v_ref[kv, pl.ds(n_lat, n_ctx), :]), m, sink)


def _swa_attn(sink, q, k, v, n_lat, n_ctx):
    b, t, nq = q.shape
    nt = t // ROW_TILE
    gps = SWA_GROUPS_PER_STEP
    gw = gps * SWA_GROUP * SWA_HEAD_DIM
    kern = functools.partial(_swa_attn_kernel, n_lat_tiles=n_lat // ROW_TILE, n_lat=n_lat, n_ctx=n_ctx)
    kv = lambda n: pl.BlockSpec((None, gps, t, n), lambda bi, gi, i: (bi, gi, 0, 0))
    return pl.pallas_call(
        kern,
        out_shape=jax.ShapeDtypeStruct((b, t, nq), BF16),
        grid=(b, SWA_KV_HEADS // gps, nt),
        in_specs=[pl.BlockSpec(memory_space=pltpu.SMEM),
                  pl.BlockSpec((None, ROW_TILE, gw), lambda bi, gi, i: (bi, i, gi)), kv(LANES), kv(2 * LANES)],
        out_specs=pl.BlockSpec((None, ROW_TILE, gw), lambda bi, gi, i: (bi, i, gi)),
        compiler_params=_cparams(3), name="swa_attn",
    )(sink, q, k, v)


def _cf_proj_kernel(o_ref, sgp_ref, x_ref, modp_ref, wout_ref, g_ref, mod_ref, w_ref, xo_ref, u_ref, sg_ref,
                    *, n_lat_tiles):
    x = _residual_in(o_ref, sgp_ref, x_ref, modp_ref, wout_ref, xo_ref)

    @pl.when(pl.program_id(1) < n_lat_tiles)
    def _():
        h = _norm_mod(x, g_ref[...], mod_ref[...]).astype(BF16)
        c = u_ref.shape[-1]
        tn = c // 2
        for c0 in range(0, c, tn):
            a = _dot(h, w_ref[:, c0:c0 + tn])
            gl = _dot(h, w_ref[:, c + c0:c + c0 + tn])
            u_ref[:, c0:c0 + tn] = a * jax.nn.sigmoid(gl)
            sg_ref[:, c0:c0 + tn] = _silu(_dot(h, w_ref[:, 2 * c + c0:2 * c + c0 + tn])).astype(sg_ref.dtype)


def _cf_proj(o, sgp, x, modp, w_out, g, mod, w, n_lat_tiles):
    b, t, d = x.shape
    c = w.shape[1] // 3
    n_lat = n_lat_tiles * ROW_TILE
    full = lambda a: _once(a.shape, lambda bi, i: (0,) * a.ndim)
    row = lambda bi, i: (bi, i, 0)
    lat_row = lambda bi, i: (bi, jnp.minimum(i, n_lat_tiles - 1), 0)
    return pl.pallas_call(
        functools.partial(_cf_proj_kernel, n_lat_tiles=n_lat_tiles),
        out_shape=(jax.ShapeDtypeStruct((b, t, d), F32),
                   jax.ShapeDtypeStruct((b, n_lat, c), F32), jax.ShapeDtypeStruct((b, n_lat, c), BF16)),
        grid=(b, t // ROW_TILE),
        in_specs=_residual_specs(o, d, n_lat_tiles) +
                 [full(g), pl.BlockSpec((None, None, 3, d), lambda bi, i: (bi, 0, 0, 0)), full(w)],
        out_specs=(pl.BlockSpec((None, ROW_TILE, d), row),
                   pl.BlockSpec((None, ROW_TILE, c), lat_row), pl.BlockSpec((None, ROW_TILE, c), lat_row)),
        compiler_params=_cparams(2), name="cf_proj",
    )(o, sgp, x, modp, w_out, g, mod, w)


def _cf_spec_matrix(n_lat):
    n = 2 * n_lat
    n2 = n_lat // FFT_N1H
    k = (np.arange(FFT_K1)[:, None] + FFT_N1 * np.arange(n2)[None, :]).reshape(-1, 1)
    lag = ((CF_KERNEL - 1) // 2 - np.arange(CF_TAPS_PAD))[None, :]
    ang = 2.0 * np.pi * k * lag / n
    live = (np.arange(CF_TAPS_PAD) < CF_KERNEL)[None, :]
    m = np.concatenate([np.cos(ang) * live, -np.sin(ang) * live], axis=0)
    return jnp.asarray(m.astype(np.float32)).astype(BF16)


def _cf_spec_kernel(e_ref, w_ref, h_ref):
    h_ref[...] = _dot(e_ref[...], w_ref[...].astype(BF16)).reshape(h_ref.shape).astype(h_ref.dtype)


def _cf_spec(dw, n_lat):
    c = dw.shape[1]
    n2 = n_lat // FFT_N1H
    e = _cf_spec_matrix(n_lat)
    wp = jnp.pad(dw, ((0, CF_TAPS_PAD - CF_KERNEL), (0, 0)))
    tc = HY_SLAB
    return pl.pallas_call(
        _cf_spec_kernel,
        out_shape=jax.ShapeDtypeStruct((1, 2, FFT_K1, n2, c), BF16),
        grid=(c // tc,),
        in_specs=[pl.BlockSpec(e.shape, lambda ci: (0, 0)), pl.BlockSpec((CF_TAPS_PAD, tc), lambda ci: (0, ci))],
        out_specs=pl.BlockSpec((None, 2, FFT_K1, n2, tc), lambda ci: (0, 0, 0, 0, ci)),
        compiler_params=_cparams(1), name="cf_spec",
    )(e, wp)


def _cf_tail_kernel(v_ref, sg_ref, x_ref, mod_ref, lg_ref, lb_ref, w_ref, fg_ref, o_ref):
    v = v_ref[...]
    mu = jnp.mean(v, axis=-1, keepdims=True)
    xc = v - mu
    var = jnp.mean(xc * xc, axis=-1, keepdims=True)
    ln = xc * lax.rsqrt(var + NORM_EPS) * lg_ref[...] + lb_ref[...]
    og = (_silu(ln) * sg_ref[...].astype(F32)).astype(BF16)
    xo = x_ref[...] + mod_ref[2:3, :] * _dot(og, w_ref[...])
    o_ref[...] = _rms(xo, fg_ref[...])


def _cf_tail(v, sg, x, mod, lg, lb, w_out, fg):
    b, n_lat, c = v.shape
    d = x.shape[-1]
    full = lambda a: pl.BlockSpec(a.shape, lambda bi, i: (0,) * a.ndim)
    row = lambda bi, i: (bi, i, 0)
    return pl.pallas_call(
        _cf_tail_kernel,
        out_shape=jax.ShapeDtypeStruct((b, n_lat, d), F32),
        grid=(b, n_lat // ROW_TILE),
        in_specs=[pl.BlockSpec((None, ROW_TILE, c), row),
                  pl.BlockSpec((None, ROW_TILE, c), row),
                  pl.BlockSpec((None, ROW_TILE, d), row),
                  pl.BlockSpec((None, None, 3, d), lambda bi, i: (bi, 0, 0, 0)),
                  full(lg), full(lb), full(w_out), full(fg)],
        out_specs=pl.BlockSpec((None, ROW_TILE, d), row),
        compiler_params=_cparams(2), name="cf_tail",
    )(v, sg, x, mod, lg, lb, w_out, fg)


def _rope_tables(n_lat, n_ctx):
    rows = n_lat // GRID_W
    row = jnp.repeat(jnp.arange(rows, dtype=F32), GRID_W)
    col = jnp.tile(jnp.arange(GRID_W, dtype=F32), rows)
    n_freq = ROPE_DIM // 4
    inv = ROPE_BASE ** (-jnp.arange(n_freq, dtype=F32) / n_freq)
    ang = jnp.concatenate([row[:, None] * inv, col[:, None] * inv], axis=-1)
    cos = jnp.concatenate([jnp.cos(ang), jnp.ones((n_ctx, ROPE_DIM // 2), F32)], axis=0)
    sin = jnp.concatenate([jnp.sin(ang), jnp.zeros((n_ctx, ROPE_DIM // 2), F32)], axis=0)
    return jnp.tile(cos, (1, 2)), jnp.tile(sin, (1, 2))


def kernel(x, c, ctx, c_ctx, norm_g, ada_w, ada_b, final_g, mla_w_in, mla_q_norm_g, mla_kv_norm_g, mla_w_uq, mla_w_ukv, mla_w_out, hy_w_in, hy_conv_w, hy_conv_b, hy_filt_w_in, hy_filt_w_hid, hy_filt_b, hy_filt_freq, hy_filt_w_out, hy_bias, hy_w_out, swa_w_in, swa_sink, swa_w_out, cf_w_in, cf_dw_w, cf_dw_b, cf_ln_g, cf_ln_b, cf_w_out):
    b, n_lat, d = x.shape
    n_ctx = ctx.shape[1]
    depth = norm_g.shape[0]
    assert depth == 4 and n_lat % ROW_TILE == 0 and n_ctx == ATTN_TILE and ROW_TILE % n_ctx == 0
    assert n_lat % (FFT_N1H * SUBLANES) == 0 and n_lat % GRID_W == 0
    n_lat_tiles = n_lat // ROW_TILE
    bf = lambda a: a.astype(BF16)

    cs = jnp.concatenate([c, c_ctx[None, :], jnp.zeros((SUBLANES - b - 1, d), F32)], axis=0)
    mods = _adaln(cs, ada_w, ada_b)[:, :b + 1].reshape(depth, b + 1, 3, d)
    mods = jnp.stack([mods[:, :b], jnp.broadcast_to(mods[:, b:], (depth, b, 3, d))], axis=2)
    cos64, sin64 = _rope_tables(n_lat, ROW_TILE)
    cos128, sin128 = jnp.tile(cos64, (1, 2)), jnp.tile(sin64, (1, 2))

    c2 = MLA_Q_RANK + MLA_KV_RANK + MLA_ROPE
    w_lat = bf(jnp.pad(mla_w_in[0][:, :c2], ((0, 0), (0, LANES - MLA_ROPE))))
    w_gate = bf(mla_w_in[0][:, c2:])
    wuq = bf(jnp.pad(mla_w_uq[0].reshape(MLA_Q_RANK, MLA_HEADS, MLA_QK), ((0, 0), (0, 0), (0, LANES - MLA_ROPE)))
             .reshape(MLA_Q_RANK, MLA_HEADS * 2 * LANES))
    q, k, v, sg = _mla_proj(x, ctx, norm_g[0:1], mods[0], w_lat, w_gate, mla_q_norm_g[0:1], mla_kv_norm_g[0:1],
                            wuq, bf(mla_w_ukv[0]), cos128, sin128, n_lat_tiles)
    o = _mla_attn(q, k, v, n_lat, n_ctx)

    xs, z0, g0, g1, sg = _hy_proj(o, sg, x, ctx, mods[0], bf(mla_w_out[0]), norm_g[1:2], mods[1], bf(hy_w_in[0]),
                                  hy_conv_w[0], hy_conv_b[0][None, :], n_lat_tiles)
    cch = z0.shape[-1]
    filt = (hy_filt_w_in[0], hy_filt_w_hid[0], hy_filt_b[0], hy_filt_freq[0], hy_filt_w_out[0])
    assert cch % HY_SLAB == 0
    mats = _fft_matrices(n_lat)
    spec = _hy_spec(_hy_filter_taps(n_lat, *filt), mats[1], mats[2])
    cfwd, cinv, nfp = _dense_dft_matrices(n_ctx)
    cspec = _hy_ctx_spec(_hy_filter_taps(n_ctx, *filt), cfwd)
    gates = (g0, g1)
    z = z0
    for order in range(HY_ORDER):
        bias = hy_bias[0][order][None, :]
        z = _long_conv(z, spec, order, mats, bias, n_lat, gate=gates[order])
        z = _hy_ctx_conv(z, gates[order], cspec, order, cfwd[:, :n_ctx], cinv, nfp, bias, n_lat, n_ctx)

    nq = SWA_Q_HEADS * SWA_HEAD_DIM
    nkv = SWA_KV_HEADS * SWA_HEAD_DIM
    w_in = swa_w_in[0]
    dup = lambda w: jnp.tile(w.reshape(d, SWA_KV_HEADS, 1, SWA_HEAD_DIM), (1, 1, 2, 1)).reshape(d, 2 * nkv)
    w2 = bf(jnp.concatenate([w_in[:, :nq], dup(w_in[:, nq:nq + nkv]), dup(w_in[:, nq + nkv:nq + 2 * nkv]),
                             w_in[:, nq + 2 * nkv:]], axis=1))
    xs, q, k, v, sg = _swa_proj(z, sg, xs, mods[1], bf(hy_w_out[0]), norm_g[2:3], mods[2], w2, cos128, sin128,
                                n_lat_tiles)
    o = _swa_attn(swa_sink[0], q, k, v, n_lat, n_ctx)

    xs, u, sg = _cf_proj(o, sg, xs, mods[2], bf(swa_w_out[0]), norm_g[3:4], mods[3], bf(cf_w_in[0]), n_lat_tiles)
    v = _long_conv(u, _cf_spec(cf_dw_w[0], n_lat), 0, mats, cf_dw_b[0][None, :], n_lat)
    return _cf_tail(v, sg, xs, mods[3], cf_ln_g[0][None, :], cf_ln_b[0][None, :], bf(cf_w_out[0]), final_g[None, :])
```

```python
import functools
import math

import numpy as np
import jax
import jax.numpy as jnp
from jax import lax
from jax.experimental import pallas as pl
from jax.experimental.pallas import tpu as pltpu

F32 = jnp.float32
BF16 = jnp.bfloat16

GRID_W = 64
NORM_EPS = 1e-6
NEG_INF = -1e30
LOG2E = math.log2(math.e)
ROPE_BASE = 10000.0
ROPE_DIM = 64

MLA_HEADS = 8
MLA_Q_RANK = 384
MLA_KV_RANK = 256
MLA_NOPE = 128
MLA_ROPE = ROPE_DIM
MLA_V = 128
MLA_QK = MLA_NOPE + MLA_ROPE

HY_ORDER = 2
HY_EMB = 33
HY_EMB_PAD = 64
HY_BANDS = (HY_EMB - 1) // 2
HY_DECAY_TARGET = 1e-2
HY_FAST_DECAY = 0.3
HY_SLOW_DECAY = 1.5

SWA_Q_HEADS = 16
SWA_KV_HEADS = 4
SWA_GROUP = SWA_Q_HEADS // SWA_KV_HEADS
SWA_HEAD_DIM = ROPE_DIM
SWA_WINDOW = 128

CF_KERNEL = 31
CF_TAPS_PAD = 32

ROW_TILE = 512
ATTN_TILE = 256
SUBLANES = 8
LANES = 128
FFT_N1 = 64
FFT_N1H = FFT_N1 // 2
FFT_K1 = FFT_N1 // 2 + 1
HY_SLAB = 256
HY_HALO = 16
VMEM_LIMIT = 48 * 1024 * 1024
VMEM_LIMIT_BIG = 56 * 1024 * 1024


def _cparams(n_axes):
    return pltpu.CompilerParams(dimension_semantics=("arbitrary",) * n_axes,
                                vmem_limit_bytes=VMEM_LIMIT)


def _cparams_big(n_axes):
    return pltpu.CompilerParams(dimension_semantics=("arbitrary",) * n_axes,
                                vmem_limit_bytes=VMEM_LIMIT_BIG)


def _once(shape, index_map):
    return pl.BlockSpec(shape, index_map, pipeline_mode=pl.Buffered(1))


def _dot(a, b):
    return jnp.dot(a, b, preferred_element_type=F32)


def _dot_nt(a, b):
    return lax.dot_general(a, b, (((1,), (1,)), ((), ())), preferred_element_type=F32)


def _rms(x, g):
    return x * lax.rsqrt(jnp.mean(x * x, axis=-1, keepdims=True) + NORM_EPS) * g


def _norm_mod(x, g, mod):
    return _rms(x, g) * (1.0 + mod[1:2, :]) + mod[0:1, :]


def _silu(x):
    return x * jax.nn.sigmoid(x)


def _adaln_kernel(c_ref, w_ref, b_ref, o_ref):
    s = _silu(c_ref[...]).astype(BF16)
    o_ref[...] = _dot(s, w_ref[...].astype(BF16)) + b_ref[...]


def _adaln(cs, ada_w, ada_b):
    depth, d, d3 = ada_w.shape
    tn = 2 * ROW_TILE
    return pl.pallas_call(
        _adaln_kernel,
        out_shape=jax.ShapeDtypeStruct((depth, cs.shape[0], d3), F32),
        grid=(depth, d3 // tn),
        in_specs=[pl.BlockSpec(cs.shape, lambda i, j: (0, 0)),
                  pl.BlockSpec((None, d, tn), lambda i, j: (i, 0, j)),
                  pl.BlockSpec((None, 1, tn), lambda i, j: (i, 0, j))],
        out_specs=pl.BlockSpec((None, cs.shape[0], tn), lambda i, j: (i, 0, j)),
        compiler_params=_cparams(2), name="adaln",
    )(cs, ada_w, ada_b.reshape(depth, 1, d3))


def _input_tile(x_ref, ctx_ref, is_ctx):
    pad = jnp.zeros((x_ref.shape[0] - ctx_ref.shape[0], x_ref.shape[1]), F32)
    return jnp.where(is_ctx, jnp.concatenate([ctx_ref[...], pad], axis=0), x_ref[...])


def _input_specs(d, n_ctx, n_lat_tiles):
    return [pl.BlockSpec((None, ROW_TILE, d), lambda bi, i: (bi, jnp.minimum(i, n_lat_tiles - 1), 0)),
            pl.BlockSpec((None, n_ctx, d), lambda bi, i: (bi, 0, 0))]


def _residual_in(o_ref, sgp_ref, x_ref, modp_ref, wout_ref, xo_ref):
    og = (o_ref[...].astype(F32) * sgp_ref[...].astype(F32)).astype(BF16)
    x = x_ref[...] + modp_ref[2:3, :] * _dot(og, wout_ref[...])
    xo_ref[...] = x
    return x


def _residual_specs(o, d, n_lat_tiles):
    width = o.shape[-1]
    row = lambda bi, i: (bi, i, 0)
    return [pl.BlockSpec((None, ROW_TILE, width), row), pl.BlockSpec((None, ROW_TILE, width), row),
            pl.BlockSpec((None, ROW_TILE, d), row),
            pl.BlockSpec((None, None, 3, d), lambda bi, i: (bi, i // n_lat_tiles, 0, 0)),
            _once((width, d), lambda bi, i: (0, 0))]


def _rope_lanes(x, cos, sin):
    half = ROPE_DIM // 2
    lane = lax.broadcasted_iota(jnp.int32, x.shape, 1) % ROPE_DIM
    rot = jnp.where(lane < half, -pltpu.roll(x, LANES - half, 1), pltpu.roll(x, half, 1))
    return x * cos + rot * sin


def _mla_proj_kernel(x_ref, ctx_ref, g_ref, mod_ref, wl_ref, wg_ref, qg_ref, kvg_ref, wuq_ref, wukv_ref,
                     cos_ref, sin_ref, q_ref, k_ref, v_ref, sg_ref, *, n_lat_tiles):
    x = _input_tile(x_ref, ctx_ref, pl.program_id(1) >= n_lat_tiles)
    h = _norm_mod(x, g_ref[...], mod_ref[...]).astype(BF16)
    sg_ref[...] = _silu(_dot(h, wg_ref[...])).astype(sg_ref.dtype)
    res = _dot(h, wl_ref[...])
    c0, c1 = MLA_Q_RANK, MLA_Q_RANK + MLA_KV_RANK
    cqn = _rms(res[:, :c0], qg_ref[...]).astype(BF16)
    ckvn = _rms(res[:, c0:c1], kvg_ref[...]).astype(BF16)
    cos, sin = cos_ref[...], sin_ref[...]

    def rope(slab):
        return _rope_lanes(slab, cos, sin)[:, :MLA_ROPE]

    k_rope_t = rope(res[:, c1:c1 + LANES]).T.astype(k_ref.dtype)
    scale = MLA_QK ** -0.5 * LOG2E
    per = 2 * LANES
    for h0 in range(0, MLA_HEADS, 2):
        q2 = _dot(cqn, wuq_ref[:, h0 * per:(h0 + 2) * per])
        kv2 = _dot(ckvn, wukv_ref[:, h0 * per:(h0 + 2) * per])
        for j in range(2):
            hd = h0 + j
            q, kv = q2[:, j * per:(j + 1) * per], kv2[:, j * per:(j + 1) * per]
            q_ref[hd, :, :MLA_NOPE] = (q[:, :MLA_NOPE] * scale).astype(q_ref.dtype)
            q_ref[hd, :, MLA_NOPE:] = (rope(q[:, MLA_NOPE:]) * scale).astype(q_ref.dtype)
            k_ref[hd, :MLA_NOPE, :] = kv[:, :MLA_NOPE].T.astype(k_ref.dtype)
            k_ref[hd, MLA_NOPE:, :] = k_rope_t
            v_ref[hd, :, :MLA_V] = kv[:, MLA_NOPE:].astype(v_ref.dtype)
            v_ref[hd, :, MLA_V:] = jnp.ones((v_ref.shape[1], MLA_V), v_ref.dtype)


def _mla_proj(x, ctx, g, mod, wl, wg, qg, kvg, wuq, wukv, cos, sin, n_lat_tiles):
    b, n_lat, d = x.shape
    nt = n_lat_tiles + 1
    t = nt * ROW_TILE
    width = MLA_HEADS * MLA_V
    full = lambda a: _once(a.shape, lambda bi, i: (0,) * a.ndim)
    head_out = lambda n: pl.BlockSpec((None, MLA_HEADS, ROW_TILE, n), lambda bi, i: (bi, 0, i, 0))
    return pl.pallas_call(
        functools.partial(_mla_proj_kernel, n_lat_tiles=n_lat_tiles),
        out_shape=(jax.ShapeDtypeStruct((b, MLA_HEADS, t, MLA_QK), BF16),
                   jax.ShapeDtypeStruct((b, MLA_HEADS, MLA_QK, t), BF16),
                   jax.ShapeDtypeStruct((b, MLA_HEADS, t, 2 * MLA_V), BF16),
                   jax.ShapeDtypeStruct((b, t, width), BF16)),
        grid=(b, nt),
        in_specs=_input_specs(d, ctx.shape[1], n_lat_tiles) +
                 [full(g),
                  pl.BlockSpec((None, None, 3, d), lambda bi, i: (bi, i // n_lat_tiles, 0, 0)),
                  full(wl), full(wg), full(qg), full(kvg), full(wuq), full(wukv),
                  pl.BlockSpec((ROW_TILE, LANES), lambda bi, i: (i, 0)),
                  pl.BlockSpec((ROW_TILE, LANES), lambda bi, i: (i, 0))],
        out_specs=(head_out(MLA_QK),
                   pl.BlockSpec((None, MLA_HEADS, MLA_QK, ROW_TILE), lambda bi, i: (bi, 0, 0, i)),
                   head_out(2 * MLA_V),
                   pl.BlockSpec((None, ROW_TILE, width), lambda bi, i: (bi, i, 0))),
        compiler_params=_cparams(2), name="mla_proj",
    )(x, ctx, g, mod, wl, wg, qg, kvg, wuq, wukv, cos, sin)


def _attend(q, k_parts, v_parts):
    s = [_dot(q, k) for k in k_parts]
    m = functools.reduce(jnp.maximum, [jnp.max(x, axis=-1, keepdims=True) for x in s])
    ov = functools.reduce(jnp.add, [_dot(jnp.exp2(x - m).astype(BF16), v) for x, v in zip(s, v_parts)])
    n = ov.shape[-1] // 2
    return ov[:, :n] / ov[:, n:]


MLA_HEADS_PER_STEP = 2


def _mla_attn_kernel(q_ref, k_ref, v_ref, o_ref, *, n_lat_tiles, n_lat, n_ctx):
    qi = pl.program_id(2)

    n_keys = n_lat + n_ctx

    @pl.when(qi < n_lat_tiles)
    def _():
        chains = [(hd, r0) for hd in range(MLA_HEADS_PER_STEP) for r0 in range(0, q_ref.shape[1], ATTN_TILE)]
        s = [_dot(q_ref[hd, r0:r0 + ATTN_TILE, :], k_ref[hd, :, 0:n_keys]) for hd, r0 in chains]
        p = [jnp.exp2(x - jnp.max(x, axis=-1, keepdims=True)).astype(BF16) for x in s]
        for (hd, r0), pc in zip(chains, p):
            ov = _dot(pc, v_ref[hd, 0:n_keys, :])
            o = ov[:, :MLA_V] / ov[:, MLA_V:]
            o_ref[r0:r0 + ATTN_TILE, hd * MLA_V:(hd + 1) * MLA_V] = o.astype(o_ref.dtype)

    @pl.when(qi >= n_lat_tiles)
    def _():
        o_ref[n_ctx:, :] = jnp.zeros((o_ref.shape[0] - n_ctx, o_ref.shape[1]), o_ref.dtype)
        for hd in range(MLA_HEADS_PER_STEP):
            k = k_ref[hd, :, n_lat:n_keys]
            v = v_ref[hd, n_lat:n_keys, :]
            o = _attend(q_ref[hd, 0:n_ctx, :], [k], [v])
            o_ref[0:n_ctx, hd * MLA_V:(hd + 1) * MLA_V] = o.astype(o_ref.dtype)


def _mla_attn(q, k, v, n_lat, n_ctx):
    b, hds, t, _ = q.shape
    nt = t // ROW_TILE
    hps = MLA_HEADS_PER_STEP
    kern = functools.partial(_mla_attn_kernel, n_lat_tiles=n_lat // ROW_TILE, n_lat=n_lat, n_ctx=n_ctx)
    return pl.pallas_call(
        kern,
        out_shape=jax.ShapeDtypeStruct((b, t, hds * MLA_V), BF16),
        grid=(b, hds // hps, nt),
        in_specs=[pl.BlockSpec((None, hps, ROW_TILE, MLA_QK), lambda bi, h, i: (bi, h, i, 0)),
                  pl.BlockSpec((None, hps, MLA_QK, t), lambda bi, h, i: (bi, h, 0, 0)),
                  pl.BlockSpec((None, hps, t, 2 * MLA_V), lambda bi, h, i: (bi, h, 0, 0))],
        out_specs=pl.BlockSpec((None, ROW_TILE, hps * MLA_V), lambda bi, h, i: (bi, i, h)),
        compiler_params=_cparams_big(3), name="mla_attn",
    )(q, k, v)


def _hy_proj_kernel(op_ref, o_ref, on_ref, sp_ref, s_ref, sn_ref, xp_ref, x_ref, ctx_ref, xn_ref, modp_ref, wout_ref,
                    g_ref, mod_ref, w_ref, cw_ref, cb_ref, xo_ref, z_ref, g0_ref, g1_ref, sg_ref,
                    *, n_lat_tiles, n_ctx):
    i = pl.program_id(1)
    tm = x_ref.shape[0]
    c = z_ref.shape[-1]
    n = tm + 2 * SUBLANES
    halo = xp_ref.shape[0]
    cat = lambda a, m, z: jnp.concatenate([a[...], m, z[...]], axis=0)
    og = (cat(op_ref, o_ref[...], on_ref).astype(F32) * cat(sp_ref, s_ref[...], sn_ref).astype(F32)).astype(BF16)
    xe = cat(xp_ref, _input_tile(x_ref, ctx_ref, i >= n_lat_tiles), xn_ref)
    xe = xe + modp_ref[2:3, :] * _dot(og, wout_ref[...])
    xo_ref[...] = xe[halo:halo + tm]
    xa = xe[halo - SUBLANES:halo + tm + SUBLANES]
    ctx_tile = i == n_lat_tiles
    first = jnp.logical_or(i == 0, ctx_tile)
    last = jnp.logical_or(i == n_lat_tiles - 1, ctx_tile)
    rows = lax.broadcasted_iota(jnp.int32, (n, 1), 0)
    end = jnp.where(ctx_tile, n_ctx + SUBLANES, jnp.where(last, tm + SUBLANES, n))
    valid = jnp.logical_and(jnp.logical_or(rows >= SUBLANES, jnp.logical_not(first)), rows < end)
    hf = jnp.where(valid, _norm_mod(xa, g_ref[...], mod_ref[...]), 0.0)
    h = hf.astype(BF16)
    outs = (z_ref, g0_ref, g1_ref)
    for j in range(len(outs)):
        u = _dot(h, w_ref[:, j * c:(j + 1) * c])
        cw = cw_ref[:, j * c:(j + 1) * c]
        cv = cw[0:1, :] * pltpu.roll(u, 1, 0) + cw[1:2, :] * u + cw[2:3, :] * pltpu.roll(u, n - 1, 0)
        outs[j][...] = cv[SUBLANES:SUBLANES + tm] + cb_ref[:, j * c:(j + 1) * c]
    h_mid = hf[SUBLANES:SUBLANES + tm].astype(BF16)
    sg_ref[...] = _silu(_dot(h_mid, w_ref[:, len(outs) * c:])).astype(sg_ref.dtype)


def _hy_proj(o, sgp, x, ctx, modp, w_out, g, mod, w, cw, cb, n_lat_tiles):
    b, t, width = o.shape
    n_lat, d = x.shape[1:]
    n_ctx = ctx.shape[1]
    c = w.shape[1] // (HY_ORDER + 2)
    per = ROW_TILE // HY_HALO
    last = n_lat // HY_HALO - 1
    full = lambda a: _once(a.shape, lambda bi, i: (0,) * a.ndim)
    row = lambda bi, i: (bi, i, 0)
    prev = lambda bi, i: (bi, jnp.maximum(i * per - 1, 0), 0)
    nxt = lambda bi, i: (bi, jnp.minimum((i + 1) * per, last), 0)
    x_main, x_ctx = _input_specs(d, n_ctx, n_lat_tiles)
    kern = functools.partial(_hy_proj_kernel, n_lat_tiles=n_lat_tiles, n_ctx=n_ctx)
    halo = lambda n, m: pl.BlockSpec((None, HY_HALO, n), m)
    return pl.pallas_call(
        kern,
        out_shape=(jax.ShapeDtypeStruct((b, t, d), F32),) + (jax.ShapeDtypeStruct((b, t, c), F32),) * 3 +
                  (jax.ShapeDtypeStruct((b, t, c), BF16),),
        grid=(b, t // ROW_TILE),
        in_specs=[halo(width, prev), pl.BlockSpec((None, ROW_TILE, width), row), halo(width, nxt),
                  halo(width, prev), pl.BlockSpec((None, ROW_TILE, width), row), halo(width, nxt),
                  halo(d, prev), x_main, x_ctx, halo(d, nxt),
                  pl.BlockSpec((None, None, 3, d), lambda bi, i: (bi, i // n_lat_tiles, 0, 0)),
                  full(w_out), full(g),
                  pl.BlockSpec((None, None, 3, d), lambda bi, i: (bi, i // n_lat_tiles, 0, 0)),
                  full(w), full(cw), full(cb)],
        out_specs=(pl.BlockSpec((None, ROW_TILE, d), row),) + (pl.BlockSpec((None, ROW_TILE, c), row),) * 4,
        compiler_params=_cparams(2), name="hy_proj",
    )(o, o, o, sgp, sgp, sgp, x, x, ctx, x, modp, w_out, g, mod, w, cw, cb)


def _hy_filter_kernel(ef_ref, eb_ref, w0_ref, wh_ref, b_ref, fr_ref, wf_ref, wb_ref, dl_ref, o_ref):
    i = pl.program_id(0)
    c = dl_ref.shape[-1]
    half = ef_ref.shape[0]

    def taps(e, wo):
        hdn = e
        ws = (w0_ref[...], wh_ref[0], wh_ref[1])
        for k in range(3):
            hdn = jnp.sin(fr_ref[k:k + 1, :] * (_dot(hdn.astype(BF16), ws[k].astype(BF16)) + b_ref[k:k + 1, :]))
        hh = _dot(hdn.astype(BF16), wo.astype(BF16))
        out = []
        for part in range(2):
            decay = jnp.exp(-e[:, part * HY_EMB_PAD:part * HY_EMB_PAD + 1] * dl_ref[...])
            out.append([hh[:, (part * HY_ORDER + o) * c:(part * HY_ORDER + o + 1) * c] * decay
                        for o in range(HY_ORDER)])
        return out

    hf = taps(ef_ref[...], wf_ref[...])
    hb = taps(eb_ref[...], wb_ref[...])
    rows = lax.broadcasted_iota(jnp.int32, (half, 1), 0)
    lag0 = jnp.logical_and(rows == 0, i == 0)
    for o in range(HY_ORDER):
        o_ref[o, 0, 0:half] = hf[0][o] + jnp.where(lag0, hb[0][o], 0.0)
        o_ref[o, 1, 0:half] = jnp.where(lag0, 0.0, hb[0][o])
        o_ref[o, 0, half:] = hf[1][o]
        o_ref[o, 1, half:] = hb[1][o]


def _hy_position_features(n):
    t = jnp.linspace(0.0, 1.0, n, dtype=F32)[:, None]
    wpos = (2.0 * math.pi / n) * jnp.arange(n, dtype=F32)[:, None]
    bands = jnp.linspace(1e-4, HY_BANDS - 1, HY_BANDS, dtype=F32)[None, :]
    hdn = jnp.concatenate([t, jnp.cos(bands * wpos), -jnp.sin(bands * wpos)], axis=-1)
    return jnp.pad(hdn, ((0, 0), (0, HY_EMB_PAD - HY_EMB)))


def _hy_filter_taps(n, w_in, w_hid, bias, freq, w_out):
    c = w_out.shape[1] // (2 * HY_ORDER)
    ef = _hy_position_features(n)
    rev = (n - jnp.arange(n)) % n
    eb = ef[rev]
    w0 = jnp.pad(w_in, ((0, HY_EMB_PAD - HY_EMB), (0, 0)))
    wo = w_out.reshape(w_out.shape[0], HY_ORDER, 2, c)
    wf = wo[:, :, 0, :].reshape(w_out.shape[0], HY_ORDER * c)
    wb = wo[:, :, 1, :].reshape(w_out.shape[0], HY_ORDER * c)
    deltas = jnp.abs(jnp.linspace(math.log(HY_DECAY_TARGET) / HY_FAST_DECAY,
                                  math.log(HY_DECAY_TARGET) / HY_SLOW_DECAY, c, dtype=F32))[None, :]
    tl = min(n, ROW_TILE)
    half = tl // 2
    pair = lambda e: e.reshape(n // tl, 2, half, HY_EMB_PAD).transpose(0, 2, 1, 3).reshape(n // 2, 2 * HY_EMB_PAD)
    both = lambda w: jnp.kron(jnp.eye(2, dtype=F32), w)
    twice = lambda v: jnp.tile(v, (1, 2))
    w_hid2 = jnp.stack([both(w_hid[0]), both(w_hid[1])])
    full = lambda a: pl.BlockSpec(a.shape, lambda i: (0,) * a.ndim)
    args = (pair(ef), pair(eb), both(w0), w_hid2, twice(bias), twice(freq), both(wf), both(wb), deltas)
    return pl.pallas_call(
        _hy_filter_kernel,
        out_shape=jax.ShapeDtypeStruct((HY_ORDER, 2, n, c), F32),
        grid=(n // tl,),
        in_specs=[pl.BlockSpec((half, 2 * HY_EMB_PAD), lambda i: (i, 0)),
                  pl.BlockSpec((half, 2 * HY_EMB_PAD), lambda i: (i, 0))] + [full(a) for a in args[2:]],
        out_specs=pl.BlockSpec((HY_ORDER, 2, tl, c), lambda i: (0, 0, i, 0)),
        compiler_params=_cparams(1), name="hy_filter",
    )(*args)


def _fft_matrices(n_lat):
    n = 2 * n_lat
    n2 = n_lat // FFT_N1H
    k1 = np.arange(FFT_K1)
    n1 = np.arange(FFT_N1H)
    ang = 2.0 * np.pi * np.outer(k1, n1) / FFT_N1
    eye = np.eye(SUBLANES)
    f1 = np.kron(np.concatenate([np.cos(ang), -np.sin(ang)], axis=0), eye)
    wgt = np.full(FFT_K1, 2.0)
    wgt[0] = wgt[-1] = 1.0
    g1 = np.kron(np.concatenate([wgt * np.cos(ang.T), (-wgt * np.sin(ang.T))[:, 1:-1]], axis=1) / n, eye)
    idx = np.arange(n2)
    f2 = np.zeros((FFT_K1, 2 * n2, 2 * n2))
    f2i = np.zeros((FFT_K1, 2 * n2, 2 * n2))
    for a in range(FFT_K1):
        ph = 2.0 * np.pi * (np.outer(idx, idx) / n2 + (a * idx)[None, :] / n)
        f2[a] = np.block([[np.cos(ph), np.sin(ph)], [-np.sin(ph), np.cos(ph)]])
        f2i[a] = np.block([[np.cos(ph.T), -np.sin(ph.T)], [np.sin(ph.T), np.cos(ph.T)]])
    sign = np.kron(np.tile((-1.0) ** k1, 2), np.ones(SUBLANES))[:, None]
    f1s = np.concatenate([f1, sign * f1], axis=1)
    as_bf16 = lambda m: jnp.asarray(m.astype(np.float32)).astype(BF16)
    return as_bf16(f1), as_bf16(f1s), as_bf16(f2), as_bf16(f2i), as_bf16(g1)


def _dense_dft_matrices(n_ctx):
    n = 2 * n_ctx
    nf = n_ctx + 1
    nfp = -(-nf // SUBLANES) * SUBLANES
    k = np.arange(nf)
    ang = 2.0 * np.pi * np.outer(k, np.arange(n)) / n
    fwd = np.zeros((2 * nfp, n))
    fwd[:nf] = np.cos(ang)
    fwd[nfp:nfp + nf] = -np.sin(ang)
    wgt = np.full(nf, 2.0)
    wgt[0] = wgt[-1] = 1.0
    angi = 2.0 * np.pi * np.outer(np.arange(n_ctx), k) / n
    inv = np.zeros((n_ctx, 2 * nfp))
    inv[:, :nf] = wgt * np.cos(angi) / n
    inv[:, nfp:nfp + nf] = -wgt * np.sin(angi) / n
    as_bf16 = lambda m: jnp.asarray(m.astype(np.float32)).astype(BF16)
    return as_bf16(fwd), as_bf16(inv), nfp


def _hy_spec_kernel(t_ref, f1s_ref, f2_ref, h_ref, a_ref):
    _, n1h, nj, s, tc = t_ref.shape
    n2 = nj * s
    for j in range(nj):
        sig = jnp.concatenate([t_ref[0, :, j].reshape(n1h * s, tc), t_ref[1, :, j].reshape(n1h * s, tc)], axis=0)
        a_ref[:, :, j] = _dot(f1s_ref[...], sig.astype(BF16)).reshape(2, FFT_K1, s, tc)
    for k in range(FFT_K1):
        a = a_ref[:, k].reshape(2 * n2, tc).astype(BF16)
        h_ref[:, k] = _dot(f2_ref[k], a).reshape(2, n2, tc).astype(h_ref.dtype)


def _hy_spec(taps, f1s, f2):
    _, _, n_lat, c = taps.shape
    n2 = n_lat // FFT_N1H
    nj = n2 // SUBLANES
    tc = HY_SLAB
    return pl.pallas_call(
        _hy_spec_kernel,
        out_shape=jax.ShapeDtypeStruct((HY_ORDER, 2, FFT_K1, n2, c), BF16),
        grid=(c // tc, HY_ORDER),
        in_specs=[pl.BlockSpec((None, 2, FFT_N1H, nj, SUBLANES, tc), lambda ci, o: (o, 0, 0, 0, 0, ci)),
                  _once(f1s.shape, lambda ci, o: (0, 0)),
                  _once(f2.shape, lambda ci, o: (0, 0, 0))],
        out_specs=pl.BlockSpec((None, 2, FFT_K1, n2, tc), lambda ci, o: (o, 0, 0, 0, ci)),
        scratch_shapes=[pltpu.VMEM((2, FFT_K1, nj, SUBLANES, tc), F32)],
        compiler_params=_cparams_big(2), name="hy_spec",
    )(taps.reshape(HY_ORDER, 2, FFT_N1H, nj, SUBLANES, c), f1s, f2)


def _conv_stages(z_ref, h_ref, f1_ref, f2_ref, f2i_ref, g1_ref, a_ref, emit):
    n1h, nj, s, tc = z_ref.shape
    n2 = nj * s
    for j in range(nj):
        zz = z_ref[:, j].reshape(n1h * s, tc).astype(BF16)
        a_ref[:, :, j] = _dot(f1_ref[...], zz).reshape(2, FFT_K1, s, tc)
    for k in range(FFT_K1):
        x = _dot(f2_ref[k], a_ref[:, k].reshape(2 * n2, tc).astype(BF16))
        xr, xi = x[:n2], x[n2:]
        hr, hi = h_ref[0, k].astype(F32), h_ref[1, k].astype(F32)
        y = jnp.concatenate([xr * hr - xi * hi, xr * hi + xi * hr], axis=0).astype(BF16)
        a_ref[:, k] = _dot(f2i_ref[k], y).reshape(2, nj, s, tc)
    for j in range(nj):
        bp = a_ref[:, :, j].reshape(2 * FFT_K1 * s, tc)
        bp = jnp.concatenate([bp[:FFT_K1 * s], bp[(FFT_K1 + 1) * s:(2 * FFT_K1 - 1) * s]], axis=0)
        emit(j, _dot(g1_ref[...], bp.astype(BF16)).reshape(n1h, s, tc))


def _hy_conv_kernel(z_ref, gt_ref, bias_ref, h_ref, f1_ref, f2_ref, f2i_ref, g1_ref, o_ref, a_ref):
    def emit(j, y):
        o_ref[:, j] = gt_ref[:, j] * (y + z_ref[:, j] * bias_ref[...])
    _conv_stages(z_ref, h_ref, f1_ref, f2_ref, f2i_ref, g1_ref, a_ref, emit)


def _cf_conv_kernel(z_ref, bias_ref, h_ref, f1_ref, f2_ref, f2i_ref, g1_ref, o_ref, a_ref):
    def emit(j, y):
        o_ref[:, j] = y + bias_ref[...]
    _conv_stages(z_ref, h_ref, f1_ref, f2_ref, f2i_ref, g1_ref, a_ref, emit)


def _long_conv(z, h, order, mats, bias, n_lat, gate=None):
    f1, _, f2, f2i, g1 = mats
    bz, t, c = z.shape
    n2 = n_lat // FFT_N1H
    nj = n2 // SUBLANES
    tc = HY_SLAB
    view = (bz, t // n2, nj, SUBLANES, c)
    slab = pl.BlockSpec((None, FFT_N1H, nj, SUBLANES, tc), lambda ci, bi: (bi, 0, 0, 0, ci))
    data, data_specs = [z.reshape(view)], [slab]
    if gate is not None:
        data, data_specs = data + [gate.reshape(view)], data_specs + [slab]
    out = pl.pallas_call(
        _hy_conv_kernel if gate is not None else _cf_conv_kernel,
        out_shape=jax.ShapeDtypeStruct(view, F32),
        grid=(c // tc, bz),
        in_specs=data_specs +
                 [pl.BlockSpec((1, tc), lambda ci, bi: (0, ci)),
                  _once((None, 2, FFT_K1, n2, tc), lambda ci, bi: (order, 0, 0, 0, ci)),
                  _once(f1.shape, lambda ci, bi: (0, 0)),
                  _once(f2.shape, lambda ci, bi: (0, 0, 0)),
                  _once(f2i.shape, lambda ci, bi: (0, 0, 0)),
                  _once(g1.shape, lambda ci, bi: (0, 0))],
        out_specs=slab,
        input_output_aliases={0: 0},
        scratch_shapes=[pltpu.VMEM((2, FFT_K1, nj, SUBLANES, tc), F32)],
        compiler_params=_cparams_big(2), name="long_conv",
    )(*data, bias, h, f1, f2, f2i, g1)
    return out.reshape(bz, t, c)


def _hy_ctx_spec_kernel(t_ref, f_ref, h_ref):
    sig = jnp.concatenate([t_ref[0], t_ref[1]], axis=0).astype(BF16)
    h_ref[...] = _dot(f_ref[...], sig)


def _hy_ctx_spec(taps, fwd):
    _, _, n_ctx, c = taps.shape
    return pl.pallas_call(
        _hy_ctx_spec_kernel,
        out_shape=jax.ShapeDtypeStruct((HY_ORDER, fwd.shape[0], c), F32),
        grid=(HY_ORDER,),
        in_specs=[pl.BlockSpec((None, 2, n_ctx, c), lambda o: (o, 0, 0, 0)),
                  pl.BlockSpec(fwd.shape, lambda o: (0, 0))],
        out_specs=pl.BlockSpec((None, fwd.shape[0], c), lambda o: (o, 0, 0)),
        compiler_params=_cparams(1), name="hy_ctx_spec",
    )(taps, fwd)


def _hy_ctx_conv_kernel(zin_ref, gt_ref, h_ref, f_ref, fi_ref, bias_ref, o_ref, *, nfp):
    z = zin_ref[...]
    x = _dot(f_ref[...], z.astype(BF16))
    xr, xi = x[:nfp], x[nfp:]
    hr, hi = h_ref[:nfp, :], h_ref[nfp:, :]
    y = jnp.concatenate([xr * hr - xi * hi, xr * hi + xi * hr], axis=0).astype(BF16)
    n_ctx = z.shape[0]
    o_ref[0:n_ctx, :] = gt_ref[...] * (_dot(fi_ref[...], y) + z * bias_ref[...])
    o_ref[n_ctx:, :] = jnp.zeros((o_ref.shape[0] - n_ctx, o_ref.shape[1]), o_ref.dtype)


def _hy_ctx_conv(z, gate, h, order, fwd_half, inv, nfp, bias, n_lat, n_ctx):
    bz, t, c = z.shape
    blk = pl.BlockSpec((None, n_ctx, c), lambda bi: (bi, n_lat // n_ctx, 0))
    kern = functools.partial(_hy_ctx_conv_kernel, nfp=nfp)
    return pl.pallas_call(
        kern,
        out_shape=jax.ShapeDtypeStruct(z.shape, F32),
        grid=(bz,),
        in_specs=[blk, blk,
                  pl.BlockSpec((None, 2 * nfp, c), lambda bi: (order, 0, 0)),
                  pl.BlockSpec(fwd_half.shape, lambda bi: (0, 0)),
                  pl.BlockSpec(inv.shape, lambda bi: (0, 0)),
                  pl.BlockSpec((1, c), lambda bi: (0, 0))],
        out_specs=pl.BlockSpec((None, t - n_lat, c), lambda bi: (bi, n_lat // (t - n_lat), 0)),
        input_output_aliases={0: 0},
        compiler_params=_cparams(1), name="hy_ctx_conv",
    )(z, gate, h, fwd_half, inv, bias)


def _swa_proj_kernel(o_ref, sgp_ref, x_ref, modp_ref, wout_ref, g_ref, mod_ref, w_ref, cos_ref, sin_ref,
                     xo_ref, q_ref, k_ref, v_ref, sg_ref):
    x = _residual_in(o_ref, sgp_ref, x_ref, modp_ref, wout_ref, xo_ref)
    h = _norm_mod(x, g_ref[...], mod_ref[...]).astype(BF16)
    cos, sin = cos_ref[...], sin_ref[...]
    nq = q_ref.shape[-1]
    nk = SWA_KV_HEADS * LANES
    scale = SWA_HEAD_DIM ** -0.5 * LOG2E
    for c0 in range(0, nq, nk):
        res = _dot(h, w_ref[:, c0:c0 + nk])
        for cb in range(nk // LANES):
            sl = slice(cb * LANES, (cb + 1) * LANES)
            q_ref[:, c0 + cb * LANES:c0 + (cb + 1) * LANES] = (_rope_lanes(res[:, sl], cos, sin) * scale).astype(q_ref.dtype)
    res = _dot(h, w_ref[:, nq:nq + nk])
    for hd in range(SWA_KV_HEADS):
        k_ref[hd] = _rope_lanes(res[:, hd * LANES:(hd + 1) * LANES], cos, sin).astype(k_ref.dtype)
    res = _dot(h, w_ref[:, nq + nk:nq + 2 * nk])
    for hd in range(SWA_KV_HEADS):
        v_ref[hd, :, :LANES] = res[:, hd * LANES:(hd + 1) * LANES].astype(v_ref.dtype)
        v_ref[hd, :, LANES:] = jnp.ones((v_ref.shape[1], LANES), v_ref.dtype)
    for c0 in range(0, nq, nk):
        sg_ref[:, c0:c0 + nk] = _silu(_dot(h, w_ref[:, nq + 2 * nk + c0:nq + 2 * nk + c0 + nk])).astype(sg_ref.dtype)


def _swa_proj(o, sgp, x, modp, w_out, g, mod, w, cos, sin, n_lat_tiles):
    b, t, d = x.shape
    nt = t // ROW_TILE
    nq = SWA_Q_HEADS * SWA_HEAD_DIM
    full = lambda a: _once(a.shape, lambda bi, i: (0,) * a.ndim)
    row = lambda bi, i: (bi, i, 0)
    kv = lambda n: pl.BlockSpec((None, SWA_KV_HEADS, ROW_TILE, n), lambda bi, i: (bi, 0, i, 0))
    return pl.pallas_call(
        _swa_proj_kernel,
        out_shape=(jax.ShapeDtypeStruct((b, t, d), F32),
                   jax.ShapeDtypeStruct((b, t, nq), BF16),
                   jax.ShapeDtypeStruct((b, SWA_KV_HEADS, t, LANES), BF16),
                   jax.ShapeDtypeStruct((b, SWA_KV_HEADS, t, 2 * LANES), BF16),
                   jax.ShapeDtypeStruct((b, t, nq), BF16)),
        grid=(b, nt),
        in_specs=_residual_specs(o, d, n_lat_tiles) +
                 [full(g),
                  pl.BlockSpec((None, None, 3, d), lambda bi, i: (bi, i // n_lat_tiles, 0, 0)),
                  full(w),
                  pl.BlockSpec((ROW_TILE, LANES), lambda bi, i: (i, 0)),
                  pl.BlockSpec((ROW_TILE, LANES), lambda bi, i: (i, 0))],
        out_specs=(pl.BlockSpec((None, ROW_TILE, d), row), pl.BlockSpec((None, ROW_TILE, nq), row),
                   kv(LANES), kv(2 * LANES), pl.BlockSpec((None, ROW_TILE, nq), row)),
        compiler_params=_cparams(2), name="swa_proj",
    )(o, sgp, x, modp, w_out, g, mod, w, cos, sin)


SWA_GROUPS_PER_STEP = 4


def _swa_attn_kernel(sink_ref, q_ref, k_ref, v_ref, o_ref, *, n_lat_tiles, n_lat, n_ctx):
    g = pl.program_id(1)
    qi = pl.program_id(2)
    tq = SWA_WINDOW
    span = tq + 2 * SWA_WINDOW
    cols = q_ref.shape[1] // LANES
    lane_half = lax.broadcasted_iota(jnp.int32, (tq, LANES), 1) // SWA_HEAD_DIM
    first = lax.broadcasted_iota(jnp.int32, (2 * tq, 1), 0) < tq
    kv_of = lambda col: col // (SWA_GROUP // 2)

    def pair_query(r0, col):
        qcol = q_ref[r0:r0 + tq, col * LANES:(col + 1) * LANES]
        zero = jnp.zeros_like(qcol)
        return jnp.concatenate([jnp.where(lane_half == 0, qcol, zero), jnp.where(lane_half == 1, qcol, zero)], axis=0)

    def pair_sink(col):
        base = g * (2 * cols) + 2 * col
        return jnp.where(first, sink_ref[base], sink_ref[base + 1]) * LOG2E

    def finish(r0, col, ov, m, sink):
        o = ov[:, :LANES] / (ov[:, LANES:] + jnp.exp2(sink - m))
        o_ref[r0:r0 + tq, col * LANES:(col + 1) * LANES] = jnp.where(lane_half == 0, o[:tq], o[tq:]).astype(o_ref.dtype)

    @pl.when(qi < n_lat_tiles)
    def _():
        chains = []
        for r0 in range(0, q_ref.shape[0], tq):
            q0 = qi * q_ref.shape[0] + r0
            start = pl.multiple_of(jnp.clip(q0 - SWA_WINDOW, 0, n_lat - span), SWA_WINDOW)
            qpos = q0 + lax.broadcasted_iota(jnp.int32, (tq, 1), 0)
            kpos = start + lax.broadcasted_iota(jnp.int32, (1, span), 1)
            band = jnp.abs(kpos - qpos) <= SWA_WINDOW
            for col in range(cols):
                kv = kv_of(col)
                q2, sink = pair_query(r0, col), pair_sink(col)
                sw = _dot_nt(q2, k_ref[kv, pl.ds(start, span), :])
                sw = jnp.concatenate([jnp.where(band, sw[:tq], NEG_INF), jnp.where(band, sw[tq:], NEG_INF)], axis=0)
                sc = _dot_nt(q2, k_ref[kv, pl.ds(n_lat, n_ctx), :])
                m = jnp.maximum(jnp.maximum(jnp.max(sw, axis=-1, keepdims=True),
                                            jnp.max(sc, axis=-1, keepdims=True)), sink)
                chains.append((r0, col, start, jnp.exp2(sw - m).astype(BF16), jnp.exp2(sc - m).astype(BF16), m, sink))
        for r0, col, start, pw, pc, m, sink in chains:
            kv = kv_of(col)
            ov = _dot(pw, v_ref[kv, pl.ds(start, span), :]) + _dot(pc, v_ref[kv, pl.ds(n_lat, n_ctx), :])
            finish(r0, col, ov, m, sink)

    @pl.when(qi >= n_lat_tiles)
    def _():
        o_ref[n_ctx:, :] = jnp.zeros((o_ref.shape[0] - n_ctx, o_ref.shape[1]), o_ref.dtype)
        for r0 in range(0, n_ctx, tq):
            for col in range(cols):
                kv = kv_of(col)
                q2, sink = pair_query(r0, col), pair_sink(col)
                sc = _dot_nt(q2, k_ref[kv, pl.ds(n_lat, n_ctx), :])
                m = jnp.maximum(jnp.max(sc, axis=-1, keepdims=True), sink)
                finish(r0, col, _dot(jnp.exp2(sc - m).astype(BF16), v_ref[kv, pl.ds(n_lat, n_ctx), :]), m, sink)


def _swa_attn(sink, q, k, v, n_lat, n_ctx):
    b, t, nq = q.shape
    nt = t // ROW_TILE
    gps = SWA_GROUPS_PER_STEP
    gw = gps * SWA_GROUP * SWA_HEAD_DIM
    kern = functools.partial(_swa_attn_kernel, n_lat_tiles=n_lat // ROW_TILE, n_lat=n_lat, n_ctx=n_ctx)
    kv = lambda n: _once((None, gps, t, n), lambda bi, gi, i: (bi, gi, 0, 0))
    return pl.pallas_call(
        kern,
        out_shape=jax.ShapeDtypeStruct((b, t, nq), BF16),
        grid=(b, SWA_KV_HEADS // gps, nt),
        in_specs=[pl.BlockSpec(memory_space=pltpu.SMEM),
                  pl.BlockSpec((None, ROW_TILE, gw), lambda bi, gi, i: (bi, i, gi)), kv(LANES), kv(2 * LANES)],
        out_specs=pl.BlockSpec((None, ROW_TILE, gw), lambda bi, gi, i: (bi, i, gi)),
        compiler_params=_cparams(3), name="swa_attn",
    )(sink, q, k, v)


def _cf_proj_kernel(o_ref, sgp_ref, x_ref, modp_ref, wout_ref, g_ref, mod_ref, w_ref, xo_ref, u_ref, sg_ref,
                    *, n_lat_tiles):
    x = _residual_in(o_ref, sgp_ref, x_ref, modp_ref, wout_ref, xo_ref)

    @pl.when(pl.program_id(1) < n_lat_tiles)
    def _():
        h = _norm_mod(x, g_ref[...], mod_ref[...]).astype(BF16)
        c = u_ref.shape[-1]
        tn = c // 2
        for c0 in range(0, c, tn):
            a = _dot(h, w_ref[:, c0:c0 + tn])
            gl = _dot(h, w_ref[:, c + c0:c + c0 + tn])
            u_ref[:, c0:c0 + tn] = a * jax.nn.sigmoid(gl)
            sg_ref[:, c0:c0 + tn] = _silu(_dot(h, w_ref[:, 2 * c + c0:2 * c + c0 + tn])).astype(sg_ref.dtype)


def _cf_proj(o, sgp, x, modp, w_out, g, mod, w, n_lat_tiles):
    b, t, d = x.shape
    c = w.shape[1] // 3
    n_lat = n_lat_tiles * ROW_TILE
    full = lambda a: _once(a.shape, lambda bi, i: (0,) * a.ndim)
    row = lambda bi, i: (bi, i, 0)
    lat_row = lambda bi, i: (bi, jnp.minimum(i, n_lat_tiles - 1), 0)
    return pl.pallas_call(
        functools.partial(_cf_proj_kernel, n_lat_tiles=n_lat_tiles),
        out_shape=(jax.ShapeDtypeStruct((b, t, d), F32),
                   jax.ShapeDtypeStruct((b, n_lat, c), F32), jax.ShapeDtypeStruct((b, n_lat, c), BF16)),
        grid=(b, t // ROW_TILE),
        in_specs=_residual_specs(o, d, n_lat_tiles) +
                 [full(g), pl.BlockSpec((None, None, 3, d), lambda bi, i: (bi, 0, 0, 0)), full(w)],
        out_specs=(pl.BlockSpec((None, ROW_TILE, d), row),
                   pl.BlockSpec((None, ROW_TILE, c), lat_row), pl.BlockSpec((None, ROW_TILE, c), lat_row)),
        compiler_params=_cparams(2), name="cf_proj",
    )(o, sgp, x, modp, w_out, g, mod, w)


def _cf_spec_matrix(n_lat):
    n = 2 * n_lat
    n2 = n_lat // FFT_N1H
    k = (np.arange(FFT_K1)[:, None] + FFT_N1 * np.arange(n2)[None, :]).reshape(-1, 1)
    lag = ((CF_KERNEL - 1) // 2 - np.arange(CF_TAPS_PAD))[None, :]
    ang = 2.0 * np.pi * k * lag / n
    live = (np.arange(CF_TAPS_PAD) < CF_KERNEL)[None, :]
    m = np.concatenate([np.cos(ang) * live, -np.sin(ang) * live], axis=0)
    return jnp.asarray(m.astype(np.float32)).astype(BF16)


def _cf_spec_kernel(e_ref, w_ref, h_ref):
    h_ref[...] = _dot(e_ref[...], w_ref[...].astype(BF16)).reshape(h_ref.shape).astype(h_ref.dtype)


def _cf_spec(dw, n_lat):
    c = dw.shape[1]
    n2 = n_lat // FFT_N1H
    e = _cf_spec_matrix(n_lat)
    wp = jnp.pad(dw, ((0, CF_TAPS_PAD - CF_KERNEL), (0, 0)))
    tc = HY_SLAB
    return pl.pallas_call(
        _cf_spec_kernel,
        out_shape=jax.ShapeDtypeStruct((1, 2, FFT_K1, n2, c), BF16),
        grid=(c // tc,),
        in_specs=[pl.BlockSpec(e.shape, lambda ci: (0, 0)), pl.BlockSpec((CF_TAPS_PAD, tc), lambda ci: (0, ci))],
        out_specs=pl.BlockSpec((None, 2, FFT_K1, n2, tc), lambda ci: (0, 0, 0, 0, ci)),
        compiler_params=_cparams(1), name="cf_spec",
    )(e, wp)


def _cf_tail_kernel(v_ref, sg_ref, x_ref, mod_ref, lg_ref, lb_ref, w_ref, fg_ref, o_ref):
    v = v_ref[...]
    mu = jnp.mean(v, axis=-1, keepdims=True)
    xc = v - mu
    var = jnp.mean(xc * xc, axis=-1, keepdims=True)
    ln = xc * lax.rsqrt(var + NORM_EPS) * lg_ref[...] + lb_ref[...]
    og = (_silu(ln) * sg_ref[...].astype(F32)).astype(BF16)
    xo = x_ref[...] + mod_ref[2:3, :] * _dot(og, w_ref[...])
    o_ref[...] = _rms(xo, fg_ref[...])


def _cf_tail(v, sg, x, mod, lg, lb, w_out, fg):
    b, n_lat, c = v.shape
    d = x.shape[-1]
    full = lambda a: pl.BlockSpec(a.shape, lambda bi, i: (0,) * a.ndim)
    row = lambda bi, i: (bi, i, 0)
    return pl.pallas_call(
        _cf_tail_kernel,
        out_shape=jax.ShapeDtypeStruct((b, n_lat, d), F32),
        grid=(b, n_lat // ROW_TILE),
        in_specs=[pl.BlockSpec((None, ROW_TILE, c), row),
                  pl.BlockSpec((None, ROW_TILE, c), row),
                  pl.BlockSpec((None, ROW_TILE, d), row),
                  pl.BlockSpec((None, None, 3, d), lambda bi, i: (bi, 0, 0, 0)),
                  full(lg), full(lb), full(w_out), full(fg)],
        out_specs=pl.BlockSpec((None, ROW_TILE, d), row),
        compiler_params=_cparams(2), name="cf_tail",
    )(v, sg, x, mod, lg, lb, w_out, fg)


def _rope_tables(n_lat, n_ctx):
    rows = n_lat // GRID_W
    row = jnp.repeat(jnp.arange(rows, dtype=F32), GRID_W)
    col = jnp.tile(jnp.arange(GRID_W, dtype=F32), rows)
    n_freq = ROPE_DIM // 4
    inv = ROPE_BASE ** (-jnp.arange(n_freq, dtype=F32) / n_freq)
    ang = jnp.concatenate([row[:, None] * inv, col[:, None] * inv], axis=-1)
    cos = jnp.concatenate([jnp.cos(ang), jnp.ones((n_ctx, ROPE_DIM // 2), F32)], axis=0)
    sin = jnp.concatenate([jnp.sin(ang), jnp.zeros((n_ctx, ROPE_DIM // 2), F32)], axis=0)
    return jnp.tile(cos, (1, 2)), jnp.tile(sin, (1, 2))


def kernel(x, c, ctx, c_ctx, norm_g, ada_w, ada_b, final_g, mla_w_in, mla_q_norm_g, mla_kv_norm_g, mla_w_uq, mla_w_ukv, mla_w_out, hy_w_in, hy_conv_w, hy_conv_b, hy_filt_w_in, hy_filt_w_hid, hy_filt_b, hy_filt_freq, hy_filt_w_out, hy_bias, hy_w_out, swa_w_in, swa_sink, swa_w_out, cf_w_in, cf_dw_w, cf_dw_b, cf_ln_g, cf_ln_b, cf_w_out):
    b, n_lat, d = x.shape
    n_ctx = ctx.shape[1]
    depth = norm_g.shape[0]
    assert depth == 4 and n_lat % ROW_TILE == 0 and n_ctx == ATTN_TILE and ROW_TILE % n_ctx == 0
    assert n_lat % (FFT_N1H * SUBLANES) == 0 and n_lat % GRID_W == 0
    n_lat_tiles = n_lat // ROW_TILE
    bf = lambda a: a.astype(BF16)

    cs = jnp.concatenate([c, c_ctx[None, :], jnp.zeros((SUBLANES - b - 1, d), F32)], axis=0)
    mods = _adaln(cs, ada_w, ada_b)[:, :b + 1].reshape(depth, b + 1, 3, d)
    mods = jnp.stack([mods[:, :b], jnp.broadcast_to(mods[:, b:], (depth, b, 3, d))], axis=2)
    cos64, sin64 = _rope_tables(n_lat, ROW_TILE)
    cos128, sin128 = jnp.tile(cos64, (1, 2)), jnp.tile(sin64, (1, 2))

    c2 = MLA_Q_RANK + MLA_KV_RANK + MLA_ROPE
    w_lat = bf(jnp.pad(mla_w_in[0][:, :c2], ((0, 0), (0, LANES - MLA_ROPE))))
    w_gate = bf(mla_w_in[0][:, c2:])
    wuq = bf(jnp.pad(mla_w_uq[0].reshape(MLA_Q_RANK, MLA_HEADS, MLA_QK), ((0, 0), (0, 0), (0, LANES - MLA_ROPE)))
             .reshape(MLA_Q_RANK, MLA_HEADS * 2 * LANES))
    q, k, v, sg = _mla_proj(x, ctx, norm_g[0:1], mods[0], w_lat, w_gate, mla_q_norm_g[0:1], mla_kv_norm_g[0:1],
                            wuq, bf(mla_w_ukv[0]), cos128, sin128, n_lat_tiles)
    o = _mla_attn(q, k, v, n_lat, n_ctx)

    xs, z0, g0, g1, sg = _hy_proj(o, sg, x, ctx, mods[0], bf(mla_w_out[0]), norm_g[1:2], mods[1], bf(hy_w_in[0]),
                                  hy_conv_w[0], hy_conv_b[0][None, :], n_lat_tiles)
    cch = z0.shape[-1]
    filt = (hy_filt_w_in[0], hy_filt_w_hid[0], hy_filt_b[0], hy_filt_freq[0], hy_filt_w_out[0])
    assert cch % HY_SLAB == 0
    mats = _fft_matrices(n_lat)
    spec = _hy_spec(_hy_filter_taps(n_lat, *filt), mats[1], mats[2])
    cfwd, cinv, nfp = _dense_dft_matrices(n_ctx)
    cspec = _hy_ctx_spec(_hy_filter_taps(n_ctx, *filt), cfwd)
    gates = (g0, g1)
    z = z0
    for order in range(HY_ORDER):
        bias = hy_bias[0][order][None, :]
        z = _long_conv(z, spec, order, mats, bias, n_lat, gate=gates[order])
        z = _hy_ctx_conv(z, gates[order], cspec, order, cfwd[:, :n_ctx], cinv, nfp, bias, n_lat, n_ctx)

    nq = SWA_Q_HEADS * SWA_HEAD_DIM
    nkv = SWA_KV_HEADS * SWA_HEAD_DIM
    w_in = swa_w_in[0]
    dup = lambda w: jnp.tile(w.reshape(d, SWA_KV_HEADS, 1, SWA_HEAD_DIM), (1, 1, 2, 1)).reshape(d, 2 * nkv)
    w2 = bf(jnp.concatenate([w_in[:, :nq], dup(w_in[:, nq:nq + nkv]), dup(w_in[:, nq + nkv:nq + 2 * nkv]),
                             w_in[:, nq + 2 * nkv:]], axis=1))
    xs, q, k, v, sg = _swa_proj(z, sg, xs, mods[1], bf(hy_w_out[0]), norm_g[2:3], mods[2], w2, cos128, sin128,
                                n_lat_tiles)
    o = _swa_attn(swa_sink[0], q, k, v, n_lat, n_ctx)

    xs, u, sg = _cf_proj(o, sg, xs, mods[2], bf(swa_w_out[0]), norm_g[3:4], mods[3], bf(cf_w_in[0]), n_lat_tiles)
    v = _long_conv(u, _cf_spec(cf_dw_w[0], n_lat), 0, mats, cf_dw_b[0][None, :], n_lat)
    return _cf_tail(v, sg, xs, mods[3], cf_ln_g[0][None, :], cf_ln_b[0][None, :], bf(cf_w_out[0]), final_g[None, :])
```

```python
import functools
import math

import numpy as np
import jax
import jax.numpy as jnp
from jax import lax
from jax.experimental import pallas as pl
from jax.experimental.pallas import tpu as pltpu

F32 = jnp.float32
BF16 = jnp.bfloat16

GRID_W = 64
NORM_EPS = 1e-6
NEG_INF = -1e30
LOG2E = math.log2(math.e)
ROPE_BASE = 10000.0
ROPE_DIM = 64

MLA_HEADS = 8
MLA_Q_RANK = 384
MLA_KV_RANK = 256
MLA_NOPE = 128
MLA_ROPE = ROPE_DIM
MLA_V = 128
MLA_QK = MLA_NOPE + MLA_ROPE

HY_ORDER = 2
HY_EMB = 33
HY_EMB_PAD = 64
HY_BANDS = (HY_EMB - 1) // 2
HY_DECAY_TARGET = 1e-2
HY_FAST_DECAY = 0.3
HY_SLOW_DECAY = 1.5

SWA_Q_HEADS = 16
SWA_KV_HEADS = 4
SWA_GROUP = SWA_Q_HEADS // SWA_KV_HEADS
SWA_HEAD_DIM = ROPE_DIM
SWA_WINDOW = 128

CF_KERNEL = 31
CF_TAPS_PAD = 32

ROW_TILE = 512
ATTN_TILE = 256
SUBLANES = 8
LANES = 128
FFT_N1 = 64
FFT_N1H = FFT_N1 // 2
FFT_K1 = FFT_N1 // 2 + 1
HY_SLAB = 256
HY_HALO = 16
VMEM_LIMIT = 48 * 1024 * 1024
VMEM_LIMIT_BIG = 56 * 1024 * 1024


def _cparams(n_axes):
    return pltpu.CompilerParams(dimension_semantics=("arbitrary",) * n_axes,
                                vmem_limit_bytes=VMEM_LIMIT)


def _cparams_big(n_axes):
    return pltpu.CompilerParams(dimension_semantics=("arbitrary",) * n_axes,
                                vmem_limit_bytes=VMEM_LIMIT_BIG)


def _once(shape, index_map):
    return pl.BlockSpec(shape, index_map, pipeline_mode=pl.Buffered(1))


def _dot(a, b):
    return jnp.dot(a, b, preferred_element_type=F32)


def _dot_nt(a, b):
    return lax.dot_general(a, b, (((1,), (1,)), ((), ())), preferred_element_type=F32)


def _rms(x, g):
    return x * lax.rsqrt(jnp.mean(x * x, axis=-1, keepdims=True) + NORM_EPS) * g


def _norm_mod(x, g, mod):
    return _rms(x, g) * (1.0 + mod[1:2, :]) + mod[0:1, :]


def _silu(x):
    return x * jax.nn.sigmoid(x)


def _adaln_kernel(c_ref, w_ref, b_ref, o_ref):
    s = _silu(c_ref[...]).astype(BF16)
    o_ref[...] = _dot(s, w_ref[...].astype(BF16)) + b_ref[...]


def _adaln(cs, ada_w, ada_b):
    depth, d, d3 = ada_w.shape
    tn = 2 * ROW_TILE
    return pl.pallas_call(
        _adaln_kernel,
        out_shape=jax.ShapeDtypeStruct((depth, cs.shape[0], d3), F32),
        grid=(depth, d3 // tn),
        in_specs=[pl.BlockSpec(cs.shape, lambda i, j: (0, 0)),
                  pl.BlockSpec((None, d, tn), lambda i, j: (i, 0, j)),
                  pl.BlockSpec((None, 1, tn), lambda i, j: (i, 0, j))],
        out_specs=pl.BlockSpec((None, cs.shape[0], tn), lambda i, j: (i, 0, j)),
        compiler_params=_cparams(2), name="adaln",
    )(cs, ada_w, ada_b.reshape(depth, 1, d3))


def _input_tile(x_ref, ctx_ref, is_ctx):
    pad = jnp.zeros((x_ref.shape[0] - ctx_ref.shape[0], x_ref.shape[1]), F32)
    return jnp.where(is_ctx, jnp.concatenate([ctx_ref[...], pad], axis=0), x_ref[...])


def _input_specs(d, n_ctx, n_lat_tiles):
    return [pl.BlockSpec((None, ROW_TILE, d), lambda bi, i: (bi, jnp.minimum(i, n_lat_tiles - 1), 0)),
            pl.BlockSpec((None, n_ctx, d), lambda bi, i: (bi, 0, 0))]


def _residual_in(o_ref, sgp_ref, x_ref, modp_ref, wout_ref, xo_ref):
    og = (o_ref[...].astype(F32) * sgp_ref[...].astype(F32)).astype(BF16)
    x = x_ref[...] + modp_ref[2:3, :] * _dot(og, wout_ref[...])
    xo_ref[...] = x
    return x


def _residual_specs(o, d, n_lat_tiles):
    width = o.shape[-1]
    row = lambda bi, i: (bi, i, 0)
    return [pl.BlockSpec((None, ROW_TILE, width), row), pl.BlockSpec((None, ROW_TILE, width), row),
            pl.BlockSpec((None, ROW_TILE, d), row),
            pl.BlockSpec((None, None, 3, d), lambda bi, i: (bi, i // n_lat_tiles, 0, 0)),
            _once((width, d), lambda bi, i: (0, 0))]


def _rope_lanes(x, cos, sin):
    half = ROPE_DIM // 2
    lane = lax.broadcasted_iota(jnp.int32, x.shape, 1) % ROPE_DIM
    rot = jnp.where(lane < half, -pltpu.roll(x, LANES - half, 1), pltpu.roll(x, half, 1))
    return x * cos + rot * sin


def _mla_proj_kernel(x_ref, ctx_ref, g_ref, mod_ref, wl_ref, wg_ref, qg_ref, kvg_ref, wuq_ref, wukv_ref,
                     cos_ref, sin_ref, q_ref, k_ref, v_ref, sg_ref, *, n_lat_tiles):
    x = _input_tile(x_ref, ctx_ref, pl.program_id(1) >= n_lat_tiles)
    h = _norm_mod(x, g_ref[...], mod_ref[...]).astype(BF16)
    sg_ref[...] = _silu(_dot(h, wg_ref[...])).astype(sg_ref.dtype)
    res = _dot(h, wl_ref[...])
    c0, c1 = MLA_Q_RANK, MLA_Q_RANK + MLA_KV_RANK
    cqn = _rms(res[:, :c0], qg_ref[...]).astype(BF16)
    ckvn = _rms(res[:, c0:c1], kvg_ref[...]).astype(BF16)
    cos, sin = cos_ref[...], sin_ref[...]

    def rope(slab):
        return _rope_lanes(slab, cos, sin)[:, :MLA_ROPE]

    k_rope_t = rope(res[:, c1:c1 + LANES]).T.astype(k_ref.dtype)
    scale = MLA_QK ** -0.5 * LOG2E
    per = 2 * LANES
    for h0 in range(0, MLA_HEADS, 2):
        q2 = _dot(cqn, wuq_ref[:, h0 * per:(h0 + 2) * per])
        kv2 = _dot(ckvn, wukv_ref[:, h0 * per:(h0 + 2) * per])
        for j in range(2):
            hd = h0 + j
            q, kv = q2[:, j * per:(j + 1) * per], kv2[:, j * per:(j + 1) * per]
            q_ref[hd, :, :MLA_NOPE] = (q[:, :MLA_NOPE] * scale).astype(q_ref.dtype)
            q_ref[hd, :, MLA_NOPE:] = (rope(q[:, MLA_NOPE:]) * scale).astype(q_ref.dtype)
            k_ref[hd, :MLA_NOPE, :] = kv[:, :MLA_NOPE].T.astype(k_ref.dtype)
            k_ref[hd, MLA_NOPE:, :] = k_rope_t
            v_ref[hd, :, :MLA_V] = kv[:, MLA_NOPE:].astype(v_ref.dtype)
            v_ref[hd, :, MLA_V:] = jnp.ones((v_ref.shape[1], MLA_V), v_ref.dtype)


def _mla_proj(x, ctx, g, mod, wl, wg, qg, kvg, wuq, wukv, cos, sin, n_lat_tiles):
    b, n_lat, d = x.shape
    nt = n_lat_tiles + 1
    t = nt * ROW_TILE
    width = MLA_HEADS * MLA_V
    full = lambda a: _once(a.shape, lambda bi, i: (0,) * a.ndim)
    head_out = lambda n: pl.BlockSpec((None, MLA_HEADS, ROW_TILE, n), lambda bi, i: (bi, 0, i, 0))
    return pl.pallas_call(
        functools.partial(_mla_proj_kernel, n_lat_tiles=n_lat_tiles),
        out_shape=(jax.ShapeDtypeStruct((b, MLA_HEADS, t, MLA_QK), BF16),
                   jax.ShapeDtypeStruct((b, MLA_HEADS, MLA_QK, t), BF16),
                   jax.ShapeDtypeStruct((b, MLA_HEADS, t, 2 * MLA_V), BF16),
                   jax.ShapeDtypeStruct((b, t, width), BF16)),
        grid=(b, nt),
        in_specs=_input_specs(d, ctx.shape[1], n_lat_tiles) +
                 [full(g),
                  pl.BlockSpec((None, None, 3, d), lambda bi, i: (bi, i // n_lat_tiles, 0, 0)),
                  full(wl), full(wg), full(qg), full(kvg), full(wuq), full(wukv),
                  pl.BlockSpec((ROW_TILE, LANES), lambda bi, i: (i, 0)),
                  pl.BlockSpec((ROW_TILE, LANES), lambda bi, i: (i, 0))],
        out_specs=(head_out(MLA_QK),
                   pl.BlockSpec((None, MLA_HEADS, MLA_QK, ROW_TILE), lambda bi, i: (bi, 0, 0, i)),
                   head_out(2 * MLA_V),
                   pl.BlockSpec((None, ROW_TILE, width), lambda bi, i: (bi, i, 0))),
        compiler_params=_cparams(2), name="mla_proj",
    )(x, ctx, g, mod, wl, wg, qg, kvg, wuq, wukv, cos, sin)


def _attend(q, k_parts, v_parts):
    s = [_dot(q, k) for k in k_parts]
    m = functools.reduce(jnp.maximum, [jnp.max(x, axis=-1, keepdims=True) for x in s])
    ov = functools.reduce(jnp.add, [_dot(jnp.exp2(x - m).astype(BF16), v) for x, v in zip(s, v_parts)])
    n = ov.shape[-1] // 2
    return ov[:, :n] / ov[:, n:]


MLA_HEADS_PER_STEP = 2


def _mla_attn_kernel(q_ref, k_ref, v_ref, o_ref, *, n_lat_tiles, n_lat, n_ctx):
    qi = pl.program_id(2)

    n_keys = n_lat + n_ctx

    @pl.when(qi < n_lat_tiles)
    def _():
        chains = [(hd, r0) for hd in range(MLA_HEADS_PER_STEP) for r0 in range(0, q_ref.shape[1], ATTN_TILE)]
        s = [_dot(q_ref[hd, r0:r0 + ATTN_TILE, :], k_ref[hd, :, 0:n_keys]) for hd, r0 in chains]
        p = [jnp.exp2(x - jnp.max(x, axis=-1, keepdims=True)).astype(BF16) for x in s]
        for (hd, r0), pc in zip(chains, p):
            ov = _dot(pc, v_ref[hd, 0:n_keys, :])
            o = ov[:, :MLA_V] / ov[:, MLA_V:]
            o_ref[r0:r0 + ATTN_TILE, hd * MLA_V:(hd + 1) * MLA_V] = o.astype(o_ref.dtype)

    @pl.when(qi >= n_lat_tiles)
    def _():
        o_ref[n_ctx:, :] = jnp.zeros((o_ref.shape[0] - n_ctx, o_ref.shape[1]), o_ref.dtype)
        for hd in range(MLA_HEADS_PER_STEP):
            k = k_ref[hd, :, n_lat:n_keys]
            v = v_ref[hd, n_lat:n_keys, :]
            o = _attend(q_ref[hd, 0:n_ctx, :], [k], [v])
            o_ref[0:n_ctx, hd * MLA_V:(hd + 1) * MLA_V] = o.astype(o_ref.dtype)


def _mla_attn(q, k, v, n_lat, n_ctx):
    b, hds, t, _ = q.shape
    nt = t // ROW_TILE
    hps = MLA_HEADS_PER_STEP
    kern = functools.partial(_mla_attn_kernel, n_lat_tiles=n_lat // ROW_TILE, n_lat=n_lat, n_ctx=n_ctx)
    return pl.pallas_call(
        kern,
        out_shape=jax.ShapeDtypeStruct((b, t, hds * MLA_V), BF16),
        grid=(b, hds // hps, nt),
        in_specs=[pl.BlockSpec((None, hps, ROW_TILE, MLA_QK), lambda bi, h, i: (bi, h, i, 0)),
                  pl.BlockSpec((None, hps, MLA_QK, t), lambda bi, h, i: (bi, h, 0, 0)),
                  pl.BlockSpec((None, hps, t, 2 * MLA_V), lambda bi, h, i: (bi, h, 0, 0))],
        out_specs=pl.BlockSpec((None, ROW_TILE, hps * MLA_V), lambda bi, h, i: (bi, i, h)),
        compiler_params=_cparams_big(3), name="mla_attn",
    )(q, k, v)


def _hy_proj_kernel(op_ref, o_ref, on_ref, sp_ref, s_ref, sn_ref, xp_ref, x_ref, ctx_ref, xn_ref, modp_ref, wout_ref,
                    g_ref, mod_ref, w_ref, cw_ref, cb_ref, xo_ref, z_ref, g0_ref, g1_ref, sg_ref,
                    *, n_lat_tiles, n_ctx):
    i = pl.program_id(1)
    tm = x_ref.shape[0]
    c = z_ref.shape[-1]
    n = tm + 2 * SUBLANES
    halo = xp_ref.shape[0]
    cat = lambda a, m, z: jnp.concatenate([a[...], m, z[...]], axis=0)
    og = (cat(op_ref, o_ref[...], on_ref).astype(F32) * cat(sp_ref, s_ref[...], sn_ref).astype(F32)).astype(BF16)
    xe = cat(xp_ref, _input_tile(x_ref, ctx_ref, i >= n_lat_tiles), xn_ref)
    xe = xe + modp_ref[2:3, :] * _dot(og, wout_ref[...])
    xo_ref[...] = xe[halo:halo + tm]
    xa = xe[halo - SUBLANES:halo + tm + SUBLANES]
    ctx_tile = i == n_lat_tiles
    first = jnp.logical_or(i == 0, ctx_tile)
    last = jnp.logical_or(i == n_lat_tiles - 1, ctx_tile)
    rows = lax.broadcasted_iota(jnp.int32, (n, 1), 0)
    end = jnp.where(ctx_tile, n_ctx + SUBLANES, jnp.where(last, tm + SUBLANES, n))
    valid = jnp.logical_and(jnp.logical_or(rows >= SUBLANES, jnp.logical_not(first)), rows < end)
    hf = jnp.where(valid, _norm_mod(xa, g_ref[...], mod_ref[...]), 0.0)
    h = hf.astype(BF16)
    outs = (z_ref, g0_ref, g1_ref)
    for j in range(len(outs)):
        u = _dot(h, w_ref[:, j * c:(j + 1) * c])
        cw = cw_ref[:, j * c:(j + 1) * c]
        cv = cw[0:1, :] * pltpu.roll(u, 1, 0) + cw[1:2, :] * u + cw[2:3, :] * pltpu.roll(u, n - 1, 0)
        outs[j][...] = cv[SUBLANES:SUBLANES + tm] + cb_ref[:, j * c:(j + 1) * c]
    h_mid = hf[SUBLANES:SUBLANES + tm].astype(BF16)
    sg_ref[...] = _silu(_dot(h_mid, w_ref[:, len(outs) * c:])).astype(sg_ref.dtype)


def _hy_proj(o, sgp, x, ctx, modp, w_out, g, mod, w, cw, cb, n_lat_tiles):
    b, t, width = o.shape
    n_lat, d = x.shape[1:]
    n_ctx = ctx.shape[1]
    c = w.shape[1] // (HY_ORDER + 2)
    per = ROW_TILE // HY_HALO
    last = n_lat // HY_HALO - 1
    full = lambda a: _once(a.shape, lambda bi, i: (0,) * a.ndim)
    row = lambda bi, i: (bi, i, 0)
    prev = lambda bi, i: (bi, jnp.maximum(i * per - 1, 0), 0)
    nxt = lambda bi, i: (bi, jnp.minimum((i + 1) * per, last), 0)
    x_main, x_ctx = _input_specs(d, n_ctx, n_lat_tiles)
    kern = functools.partial(_hy_proj_kernel, n_lat_tiles=n_lat_tiles, n_ctx=n_ctx)
    halo = lambda n, m: pl.BlockSpec((None, HY_HALO, n), m)
    return pl.pallas_call(
        kern,
        out_shape=(jax.ShapeDtypeStruct((b, t, d), F32),) + (jax.ShapeDtypeStruct((b, t, c), F32),) * 3 +
                  (jax.ShapeDtypeStruct((b, t, c), BF16),),
        grid=(b, t // ROW_TILE),
        in_specs=[halo(width, prev), pl.BlockSpec((None, ROW_TILE, width), row), halo(width, nxt),
                  halo(width, prev), pl.BlockSpec((None, ROW_TILE, width), row), halo(width, nxt),
                  halo(d, prev), x_main, x_ctx, halo(d, nxt),
                  pl.BlockSpec((None, None, 3, d), lambda bi, i: (bi, i // n_lat_tiles, 0, 0)),
                  full(w_out), full(g),
                  pl.BlockSpec((None, None, 3, d), lambda bi, i: (bi, i // n_lat_tiles, 0, 0)),
                  full(w), full(cw), full(cb)],
        out_specs=(pl.BlockSpec((None, ROW_TILE, d), row),) + (pl.BlockSpec((None, ROW_TILE, c), row),) * 4,
        compiler_params=_cparams(2), name="hy_proj",
    )(o, o, o, sgp, sgp, sgp, x, x, ctx, x, modp, w_out, g, mod, w, cw, cb)


def _hy_filter_kernel(ef_ref, eb_ref, w0_ref, wh_ref, b_ref, fr_ref, wf_ref, wb_ref, dl_ref, o_ref):
    i = pl.program_id(0)
    c = dl_ref.shape[-1]
    half = ef_ref.shape[0]

    def taps(e, wo):
        hdn = e
        ws = (w0_ref[...], wh_ref[0], wh_ref[1])
        for k in range(3):
            hdn = jnp.sin(fr_ref[k:k + 1, :] * (_dot(hdn.astype(BF16), ws[k].astype(BF16)) + b_ref[k:k + 1, :]))
        hh = _dot(hdn.astype(BF16), wo.astype(BF16))
        out = []
        for part in range(2):
            decay = jnp.exp(-e[:, part * HY_EMB_PAD:part * HY_EMB_PAD + 1] * dl_ref[...])
            out.append([hh[:, (part * HY_ORDER + o) * c:(part * HY_ORDER + o + 1) * c] * decay
                        for o in range(HY_ORDER)])
        return out

    hf = taps(ef_ref[...], wf_ref[...])
    hb = taps(eb_ref[...], wb_ref[...])
    rows = lax.broadcasted_iota(jnp.int32, (half, 1), 0)
    lag0 = jnp.logical_and(rows == 0, i == 0)
    for o in range(HY_ORDER):
        o_ref[o, 0, 0:half] = hf[0][o] + jnp.where(lag0, hb[0][o], 0.0)
        o_ref[o, 1, 0:half] = jnp.where(lag0, 0.0, hb[0][o])
        o_ref[o, 0, half:] = hf[1][o]
        o_ref[o, 1, half:] = hb[1][o]


def _hy_position_features(n):
    t = jnp.linspace(0.0, 1.0, n, dtype=F32)[:, None]
    wpos = (2.0 * math.pi / n) * jnp.arange(n, dtype=F32)[:, None]
    bands = jnp.linspace(1e-4, HY_BANDS - 1, HY_BANDS, dtype=F32)[None, :]
    hdn = jnp.concatenate([t, jnp.cos(bands * wpos), -jnp.sin(bands * wpos)], axis=-1)
    return jnp.pad(hdn, ((0, 0), (0, HY_EMB_PAD - HY_EMB)))


def _hy_filter_taps(n, w_in, w_hid, bias, freq, w_out):
    c = w_out.shape[1] // (2 * HY_ORDER)
    ef = _hy_position_features(n)
    rev = (n - jnp.arange(n)) % n
    eb = ef[rev]
    w0 = jnp.pad(w_in, ((0, HY_EMB_PAD - HY_EMB), (0, 0)))
    wo = w_out.reshape(w_out.shape[0], HY_ORDER, 2, c)
    wf = wo[:, :, 0, :].reshape(w_out.shape[0], HY_ORDER * c)
    wb = wo[:, :, 1, :].reshape(w_out.shape[0], HY_ORDER * c)
    deltas = jnp.abs(jnp.linspace(math.log(HY_DECAY_TARGET) / HY_FAST_DECAY,
                                  math.log(HY_DECAY_TARGET) / HY_SLOW_DECAY, c, dtype=F32))[None, :]
    tl = min(n, ROW_TILE)
    half = tl // 2
    pair = lambda e: e.reshape(n // tl, 2, half, HY_EMB_PAD).transpose(0, 2, 1, 3).reshape(n // 2, 2 * HY_EMB_PAD)
    both = lambda w: jnp.kron(jnp.eye(2, dtype=F32), w)
    twice = lambda v: jnp.tile(v, (1, 2))
    w_hid2 = jnp.stack([both(w_hid[0]), both(w_hid[1])])
    full = lambda a: pl.BlockSpec(a.shape, lambda i: (0,) * a.ndim)
    args = (pair(ef), pair(eb), both(w0), w_hid2, twice(bias), twice(freq), both(wf), both(wb), deltas)
    return pl.pallas_call(
        _hy_filter_kernel,
        out_shape=jax.ShapeDtypeStruct((HY_ORDER, 2, n, c), F32),
        grid=(n // tl,),
        in_specs=[pl.BlockSpec((half, 2 * HY_EMB_PAD), lambda i: (i, 0)),
                  pl.BlockSpec((half, 2 * HY_EMB_PAD), lambda i: (i, 0))] + [full(a) for a in args[2:]],
        out_specs=pl.BlockSpec((HY_ORDER, 2, tl, c), lambda i: (0, 0, i, 0)),
        compiler_params=_cparams(1), name="hy_filter",
    )(*args)


def _fft_matrices(n_lat):
    n = 2 * n_lat
    n2 = n_lat // FFT_N1H
    k1 = np.arange(FFT_K1)
    n1 = np.arange(FFT_N1H)
    ang = 2.0 * np.pi * np.outer(k1, n1) / FFT_N1
    eye = np.eye(SUBLANES)
    f1 = np.kron(np.concatenate([np.cos(ang), -np.sin(ang)], axis=0), eye)
    wgt = np.full(FFT_K1, 2.0)
    wgt[0] = wgt[-1] = 1.0
    g1 = np.kron(np.concatenate([wgt * np.cos(ang.T), (-wgt * np.sin(ang.T))[:, 1:-1]], axis=1) / n, eye)
    idx = np.arange(n2)
    f2 = np.zeros((FFT_K1, 2 * n2, 2 * n2))
    f2i = np.zeros((FFT_K1, 2 * n2, 2 * n2))
    for a in range(FFT_K1):
        ph = 2.0 * np.pi * (np.outer(idx, idx) / n2 + (a * idx)[None, :] / n)
        f2[a] = np.block([[np.cos(ph), np.sin(ph)], [-np.sin(ph), np.cos(ph)]])
        f2i[a] = np.block([[np.cos(ph.T), -np.sin(ph.T)], [np.sin(ph.T), np.cos(ph.T)]])
    sign = np.kron(np.tile((-1.0) ** k1, 2), np.ones(SUBLANES))[:, None]
    f1s = np.concatenate([f1, sign * f1], axis=1)
    as_bf16 = lambda m: jnp.asarray(m.astype(np.float32)).astype(BF16)
    return as_bf16(f1), as_bf16(f1s), as_bf16(f2), as_bf16(f2i), as_bf16(g1)


def _dense_dft_matrices(n_ctx):
    n = 2 * n_ctx
    nf = n_ctx + 1
    nfp = -(-nf // SUBLANES) * SUBLANES
    k = np.arange(nf)
    ang = 2.0 * np.pi * np.outer(k, np.arange(n)) / n
    fwd = np.zeros((2 * nfp, n))
    fwd[:nf] = np.cos(ang)
    fwd[nfp:nfp + nf] = -np.sin(ang)
    wgt = np.full(nf, 2.0)
    wgt[0] = wgt[-1] = 1.0
    angi = 2.0 * np.pi * np.outer(np.arange(n_ctx), k) / n
    inv = np.zeros((n_ctx, 2 * nfp))
    inv[:, :nf] = wgt * np.cos(angi) / n
    inv[:, nfp:nfp + nf] = -wgt * np.sin(angi) / n
    as_bf16 = lambda m: jnp.asarray(m.astype(np.float32)).astype(BF16)
    return as_bf16(fwd), as_bf16(inv), nfp


def _hy_spec_kernel(t_ref, f1s_ref, f2_ref, h_ref, a_ref):
    _, n1h, nj, s, tc = t_ref.shape
    n2 = nj * s
    for j in range(nj):
        sig = jnp.concatenate([t_ref[0, :, j].reshape(n1h * s, tc), t_ref[1, :, j].reshape(n1h * s, tc)], axis=0)
        a_ref[:, :, j] = _dot(f1s_ref[...], sig.astype(BF16)).reshape(2, FFT_K1, s, tc)
    for k in range(FFT_K1):
        a = a_ref[:, k].reshape(2 * n2, tc).astype(BF16)
        h_ref[:, k] = _dot(f2_ref[k], a).reshape(2, n2, tc).astype(h_ref.dtype)


def _hy_spec(taps, f1s, f2):
    _, _, n_lat, c = taps.shape
    n2 = n_lat // FFT_N1H
    nj = n2 // SUBLANES
    tc = HY_SLAB
    return pl.pallas_call(
        _hy_spec_kernel,
        out_shape=jax.ShapeDtypeStruct((HY_ORDER, 2, FFT_K1, n2, c), BF16),
        grid=(c // tc, HY_ORDER),
        in_specs=[pl.BlockSpec((None, 2, FFT_N1H, nj, SUBLANES, tc), lambda ci, o: (o, 0, 0, 0, 0, ci)),
                  _once(f1s.shape, lambda ci, o: (0, 0)),
                  _once(f2.shape, lambda ci, o: (0, 0, 0))],
        out_specs=pl.BlockSpec((None, 2, FFT_K1, n2, tc), lambda ci, o: (o, 0, 0, 0, ci)),
        scratch_shapes=[pltpu.VMEM((2, FFT_K1, nj, SUBLANES, tc), F32)],
        compiler_params=_cparams_big(2), name="hy_spec",
    )(taps.reshape(HY_ORDER, 2, FFT_N1H, nj, SUBLANES, c), f1s, f2)


def _conv_stages(z_ref, h_ref, f1_ref, f2_ref, f2i_ref, g1_ref, a_ref, emit):
    n1h, nj, s, tc = z_ref.shape
    n2 = nj * s
    for j in range(nj):
        zz = z_ref[:, j].reshape(n1h * s, tc).astype(BF16)
        a_ref[:, :, j] = _dot(f1_ref[...], zz).reshape(2, FFT_K1, s, tc)
    for k in range(FFT_K1):
        x = _dot(f2_ref[k], a_ref[:, k].reshape(2 * n2, tc).astype(BF16))
        xr, xi = x[:n2], x[n2:]
        hr, hi = h_ref[0, k].astype(F32), h_ref[1, k].astype(F32)
        y = jnp.concatenate([xr * hr - xi * hi, xr * hi + xi * hr], axis=0).astype(BF16)
        a_ref[:, k] = _dot(f2i_ref[k], y).reshape(2, nj, s, tc)
    for j in range(nj):
        bp = a_ref[:, :, j].reshape(2 * FFT_K1 * s, tc)
        bp = jnp.concatenate([bp[:FFT_K1 * s], bp[(FFT_K1 + 1) * s:(2 * FFT_K1 - 1) * s]], axis=0)
        emit(j, _dot(g1_ref[...], bp.astype(BF16)).reshape(n1h, s, tc))


def _hy_conv_kernel(z_ref, gt_ref, bias_ref, h_ref, f1_ref, f2_ref, f2i_ref, g1_ref, o_ref, a_ref):
    def emit(j, y):
        o_ref[:, j] = gt_ref[:, j] * (y + z_ref[:, j] * bias_ref[...])
    _conv_stages(z_ref, h_ref, f1_ref, f2_ref, f2i_ref, g1_ref, a_ref, emit)


def _cf_conv_kernel(z_ref, bias_ref, h_ref, f1_ref, f2_ref, f2i_ref, g1_ref, o_ref, a_ref):
    def emit(j, y):
        o_ref[:, j] = y + bias_ref[...]
    _conv_stages(z_ref, h_ref, f1_ref, f2_ref, f2i_ref, g1_ref, a_ref, emit)


def _long_conv(z, h, order, mats, bias, n_lat, gate=None):
    f1, _, f2, f2i, g1 = mats
    bz, t, c = z.shape
    n2 = n_lat // FFT_N1H
    nj = n2 // SUBLANES
    tc = HY_SLAB
    view = (bz, t // n2, nj, SUBLANES, c)
    slab = pl.BlockSpec((None, FFT_N1H, nj, SUBLANES, tc), lambda ci, bi: (bi, 0, 0, 0, ci))
    data, data_specs = [z.reshape(view)], [slab]
    if gate is not None:
        data, data_specs = data + [gate.reshape(view)], data_specs + [slab]
    out = pl.pallas_call(
        _hy_conv_kernel if gate is not None else _cf_conv_kernel,
        out_shape=jax.ShapeDtypeStruct(view, F32),
        grid=(c // tc, bz),
        in_specs=data_specs +
                 [pl.BlockSpec((1, tc), lambda ci, bi: (0, ci)),
                  _once((None, 2, FFT_K1, n2, tc), lambda ci, bi: (order, 0, 0, 0, ci)),
                  _once(f1.shape, lambda ci, bi: (0, 0)),
                  _once(f2.shape, lambda ci, bi: (0, 0, 0)),
                  _once(f2i.shape, lambda ci, bi: (0, 0, 0)),
                  _once(g1.shape, lambda ci, bi: (0, 0))],
        out_specs=slab,
        input_output_aliases={0: 0},
        scratch_shapes=[pltpu.VMEM((2, FFT_K1, nj, SUBLANES, tc), F32)],
        compiler_params=_cparams_big(2), name="long_conv",
    )(*data, bias, h, f1, f2, f2i, g1)
    return out.reshape(bz, t, c)


def _hy_ctx_spec_kernel(t_ref, f_ref, h_ref):
    sig = jnp.concatenate([t_ref[0], t_ref[1]], axis=0).astype(BF16)
    h_ref[...] = _dot(f_ref[...], sig)


def _hy_ctx_spec(taps, fwd):
    _, _, n_ctx, c = taps.shape
    return pl.pallas_call(
        _hy_ctx_spec_kernel,
        out_shape=jax.ShapeDtypeStruct((HY_ORDER, fwd.shape[0], c), F32),
        grid=(HY_ORDER,),
        in_specs=[pl.BlockSpec((None, 2, n_ctx, c), lambda o: (o, 0, 0, 0)),
                  pl.BlockSpec(fwd.shape, lambda o: (0, 0))],
        out_specs=pl.BlockSpec((None, fwd.shape[0], c), lambda o: (o, 0, 0)),
        compiler_params=_cparams(1), name="hy_ctx_spec",
    )(taps, fwd)


def _hy_ctx_conv_kernel(zin_ref, gt_ref, h_ref, f_ref, fi_ref, bias_ref, o_ref, *, nfp):
    z = zin_ref[...]
    x = _dot(f_ref[...], z.astype(BF16))
    xr, xi = x[:nfp], x[nfp:]
    hr, hi = h_ref[:nfp, :], h_ref[nfp:, :]
    y = jnp.concatenate([xr * hr - xi * hi, xr * hi + xi * hr], axis=0).astype(BF16)
    n_ctx = z.shape[0]
    o_ref[0:n_ctx, :] = gt_ref[...] * (_dot(fi_ref[...], y) + z * bias_ref[...])
    o_ref[n_ctx:, :] = jnp.zeros((o_ref.shape[0] - n_ctx, o_ref.shape[1]), o_ref.dtype)


def _hy_ctx_conv(z, gate, h, order, fwd_half, inv, nfp, bias, n_lat, n_ctx):
    bz, t, c = z.shape
    blk = pl.BlockSpec((None, n_ctx, c), lambda bi: (bi, n_lat // n_ctx, 0))
    kern = functools.partial(_hy_ctx_conv_kernel, nfp=nfp)
    return pl.pallas_call(
        kern,
        out_shape=jax.ShapeDtypeStruct(z.shape, F32),
        grid=(bz,),
        in_specs=[blk, blk,
                  pl.BlockSpec((None, 2 * nfp, c), lambda bi: (order, 0, 0)),
                  pl.BlockSpec(fwd_half.shape, lambda bi: (0, 0)),
                  pl.BlockSpec(inv.shape, lambda bi: (0, 0)),
                  pl.BlockSpec((1, c), lambda bi: (0, 0))],
        out_specs=pl.BlockSpec((None, t - n_lat, c), lambda bi: (bi, n_lat // (t - n_lat), 0)),
        input_output_aliases={0: 0},
        compiler_params=_cparams(1), name="hy_ctx_conv",
    )(z, gate, h, fwd_half, inv, bias)


def _swa_proj_kernel(o_ref, sgp_ref, x_ref, modp_ref, wout_ref, g_ref, mod_ref, w_ref, cos_ref, sin_ref,
                     xo_ref, q_ref, k_ref, v_ref, sg_ref):
    x = _residual_in(o_ref, sgp_ref, x_ref, modp_ref, wout_ref, xo_ref)
    h = _norm_mod(x, g_ref[...], mod_ref[...]).astype(BF16)
    cos, sin = cos_ref[...], sin_ref[...]
    nq = q_ref.shape[-1]
    nk = SWA_KV_HEADS * LANES
    scale = SWA_HEAD_DIM ** -0.5 * LOG2E
    for c0 in range(0, nq, nk):
        res = _dot(h, w_ref[:, c0:c0 + nk])
        for cb in range(nk // LANES):
            sl = slice(cb * LANES, (cb + 1) * LANES)
            q_ref[:, c0 + cb * LANES:c0 + (cb + 1) * LANES] = (_rope_lanes(res[:, sl], cos, sin) * scale).astype(q_ref.dtype)
    nkv = SWA_KV_HEADS * SWA_HEAD_DIM
    res = _dot(h, w_ref[:, nq:nq + 2 * nkv])
    low = lax.broadcasted_iota(jnp.int32, (res.shape[0], LANES), 1) < SWA_HEAD_DIM
    for slab in range(nkv // LANES):
        kk = _rope_lanes(res[:, slab * LANES:(slab + 1) * LANES], cos, sin)
        vv = res[:, nkv + slab * LANES:nkv + (slab + 1) * LANES]
        kr, vr = pltpu.roll(kk, SWA_HEAD_DIM, 1), pltpu.roll(vv, SWA_HEAD_DIM, 1)
        for half in range(2):
            hd = 2 * slab + half
            pick = low if half == 0 else jnp.logical_not(low)
            k_ref[hd] = jnp.where(pick, kk, kr).astype(k_ref.dtype)
            v_ref[hd, :, :LANES] = jnp.where(pick, vv, vr).astype(v_ref.dtype)
            v_ref[hd, :, LANES:] = jnp.ones((v_ref.shape[1], LANES), v_ref.dtype)
    for c0 in range(0, nq, nk):
        sg_ref[:, c0:c0 + nk] = _silu(_dot(h, w_ref[:, nq + 2 * nkv + c0:nq + 2 * nkv + c0 + nk])).astype(sg_ref.dtype)


def _swa_proj(o, sgp, x, modp, w_out, g, mod, w, cos, sin, n_lat_tiles):
    b, t, d = x.shape
    nt = t // ROW_TILE
    nq = SWA_Q_HEADS * SWA_HEAD_DIM
    full = lambda a: _once(a.shape, lambda bi, i: (0,) * a.ndim)
    row = lambda bi, i: (bi, i, 0)
    kv = lambda n: pl.BlockSpec((None, SWA_KV_HEADS, ROW_TILE, n), lambda bi, i: (bi, 0, i, 0))
    return pl.pallas_call(
        _swa_proj_kernel,
        out_shape=(jax.ShapeDtypeStruct((b, t, d), F32),
                   jax.ShapeDtypeStruct((b, t, nq), BF16),
                   jax.ShapeDtypeStruct((b, SWA_KV_HEADS, t, LANES), BF16),
                   jax.ShapeDtypeStruct((b, SWA_KV_HEADS, t, 2 * LANES), BF16),
                   jax.ShapeDtypeStruct((b, t, nq), BF16)),
        grid=(b, nt),
        in_specs=_residual_specs(o, d, n_lat_tiles) +
                 [full(g),
                  pl.BlockSpec((None, None, 3, d), lambda bi, i: (bi, i // n_lat_tiles, 0, 0)),
                  full(w),
                  pl.BlockSpec((ROW_TILE, LANES), lambda bi, i: (i, 0)),
                  pl.BlockSpec((ROW_TILE, LANES), lambda bi, i: (i, 0))],
        out_specs=(pl.BlockSpec((None, ROW_TILE, d), row), pl.BlockSpec((None, ROW_TILE, nq), row),
                   kv(LANES), kv(2 * LANES), pl.BlockSpec((None, ROW_TILE, nq), row)),
        compiler_params=_cparams(2), name="swa_proj",
    )(o, sgp, x, modp, w_out, g, mod, w, cos, sin)


SWA_GROUPS_PER_STEP = 4


def _swa_attn_kernel(sink_ref, q_ref, k_ref, v_ref, o_ref, *, n_lat_tiles, n_lat, n_ctx):
    g = pl.program_id(1)
    qi = pl.program_id(2)
    tq = SWA_WINDOW
    span = tq + 2 * SWA_WINDOW
    cols = q_ref.shape[1] // LANES
    lane_half = lax.broadcasted_iota(jnp.int32, (tq, LANES), 1) // SWA_HEAD_DIM
    first = lax.broadcasted_iota(jnp.int32, (2 * tq, 1), 0) < tq
    kv_of = lambda col: col // (SWA_GROUP // 2)

    def pair_query(r0, col):
        qcol = q_ref[r0:r0 + tq, col * LANES:(col + 1) * LANES]
        zero = jnp.zeros_like(qcol)
        return jnp.concatenate([jnp.where(lane_half == 0, qcol, zero), jnp.where(lane_half == 1, qcol, zero)], axis=0)

    def pair_sink(col):
        base = g * (2 * cols) + 2 * col
        return jnp.where(first, sink_ref[base], sink_ref[base + 1]) * LOG2E

    def finish(r0, col, ov, m, sink):
        o = ov[:, :LANES] / (ov[:, LANES:] + jnp.exp2(sink - m))
        o_ref[r0:r0 + tq, col * LANES:(col + 1) * LANES] = jnp.where(lane_half == 0, o[:tq], o[tq:]).astype(o_ref.dtype)

    @pl.when(qi < n_lat_tiles)
    def _():
        chains = []
        for r0 in range(0, q_ref.shape[0], tq):
            q0 = qi * q_ref.shape[0] + r0
            start = pl.multiple_of(jnp.clip(q0 - SWA_WINDOW, 0, n_lat - span), SWA_WINDOW)
            qpos = q0 + lax.broadcasted_iota(jnp.int32, (tq, 1), 0)
            kpos = start + lax.broadcasted_iota(jnp.int32, (1, span), 1)
            band = jnp.abs(kpos - qpos) <= SWA_WINDOW
            for col in range(cols):
                kv = kv_of(col)
                q2, sink = pair_query(r0, col), pair_sink(col)
                sw = _dot_nt(q2, k_ref[kv, pl.ds(start, span), :])
                sw = jnp.concatenate([jnp.where(band, sw[:tq], NEG_INF), jnp.where(band, sw[tq:], NEG_INF)], axis=0)
                sc = _dot_nt(q2, k_ref[kv, pl.ds(n_lat, n_ctx), :])
                m = jnp.maximum(jnp.maximum(jnp.max(sw, axis=-1, keepdims=True),
                                            jnp.max(sc, axis=-1, keepdims=True)), sink)
                chains.append((r0, col, start, jnp.exp2(sw - m).astype(BF16), jnp.exp2(sc - m).astype(BF16), m, sink))
        for r0, col, start, pw, pc, m, sink in chains:
            kv = kv_of(col)
            ov = _dot(pw, v_ref[kv, pl.ds(start, span), :]) + _dot(pc, v_ref[kv, pl.ds(n_lat, n_ctx), :])
            finish(r0, col, ov, m, sink)

    @pl.when(qi >= n_lat_tiles)
    def _():
        o_ref[n_ctx:, :] = jnp.zeros((o_ref.shape[0] - n_ctx, o_ref.shape[1]), o_ref.dtype)
        for r0 in range(0, n_ctx, tq):
            for col in range(cols):
                kv = kv_of(col)
                q2, sink = pair_query(r0, col), pair_sink(col)
                sc = _dot_nt(q2, k_ref[kv, pl.ds(n_lat, n_ctx), :])
                m = jnp.maximum(jnp.max(sc, axis=-1, keepdims=True), sink)
                finish(r0, col, _dot(jnp.exp2(sc - m).astype(BF16), v_ref[kv, pl.ds(n_lat, n_ctx), :]), m, sink)


def _swa_attn(sink, q, k, v, n_lat, n_ctx):
    b, t, nq = q.shape
    nt = t // ROW_TILE
    gps = SWA_GROUPS_PER_STEP
    gw = gps * SWA_GROUP * SWA_HEAD_DIM
    kern = functools.partial(_swa_attn_kernel, n_lat_tiles=n_lat // ROW_TILE, n_lat=n_lat, n_ctx=n_ctx)
    kv = lambda n: _once((None, gps, t, n), lambda bi, gi, i: (bi, gi, 0, 0))
    return pl.pallas_call(
        kern,
        out_shape=jax.ShapeDtypeStruct((b, t, nq), BF16),
        grid=(b, SWA_KV_HEADS // gps, nt),
        in_specs=[pl.BlockSpec(memory_space=pltpu.SMEM),
                  pl.BlockSpec((None, ROW_TILE, gw), lambda bi, gi, i: (bi, i, gi)), kv(LANES), kv(2 * LANES)],
        out_specs=pl.BlockSpec((None, ROW_TILE, gw), lambda bi, gi, i: (bi, i, gi)),
        compiler_params=_cparams(3), name="swa_attn",
    )(sink, q, k, v)


def _cf_proj_kernel(o_ref, sgp_ref, x_ref, modp_ref, wout_ref, g_ref, mod_ref, w_ref, xo_ref, u_ref, sg_ref,
                    *, n_lat_tiles):
    x = _residual_in(o_ref, sgp_ref, x_ref, modp_ref, wout_ref, xo_ref)

    @pl.when(pl.program_id(1) < n_lat_tiles)
    def _():
        h = _norm_mod(x, g_ref[...], mod_ref[...]).astype(BF16)
        c = u_ref.shape[-1]
        tn = c // 2
        for c0 in range(0, c, tn):
            a = _dot(h, w_ref[:, c0:c0 + tn])
            gl = _dot(h, w_ref[:, c + c0:c + c0 + tn])
            u_ref[:, c0:c0 + tn] = a * jax.nn.sigmoid(gl)
            sg_ref[:, c0:c0 + tn] = _silu(_dot(h, w_ref[:, 2 * c + c0:2 * c + c0 + tn])).astype(sg_ref.dtype)


def _cf_proj(o, sgp, x, modp, w_out, g, mod, w, n_lat_tiles):
    b, t, d = x.shape
    c = w.shape[1] // 3
    n_lat = n_lat_tiles * ROW_TILE
    full = lambda a: _once(a.shape, lambda bi, i: (0,) * a.ndim)
    row = lambda bi, i: (bi, i, 0)
    lat_row = lambda bi, i: (bi, jnp.minimum(i, n_lat_tiles - 1), 0)
    return pl.pallas_call(
        functools.partial(_cf_proj_kernel, n_lat_tiles=n_lat_tiles),
        out_shape=(jax.ShapeDtypeStruct((b, t, d), F32),
                   jax.ShapeDtypeStruct((b, n_lat, c), F32), jax.ShapeDtypeStruct((b, n_lat, c), BF16)),
        grid=(b, t // ROW_TILE),
        in_specs=_residual_specs(o, d, n_lat_tiles) +
                 [full(g), pl.BlockSpec((None, None, 3, d), lambda bi, i: (bi, 0, 0, 0)), full(w)],
        out_specs=(pl.BlockSpec((None, ROW_TILE, d), row),
                   pl.BlockSpec((None, ROW_TILE, c), lat_row), pl.BlockSpec((None, ROW_TILE, c), lat_row)),
        compiler_params=_cparams(2), name="cf_proj",
    )(o, sgp, x, modp, w_out, g, mod, w)


def _cf_spec_matrix(n_lat):
    n = 2 * n_lat
    n2 = n_lat // FFT_N1H
    k = (np.arange(FFT_K1)[:, None] + FFT_N1 * np.arange(n2)[None, :]).reshape(-1, 1)
    lag = ((CF_KERNEL - 1) // 2 - np.arange(CF_TAPS_PAD))[None, :]
    ang = 2.0 * np.pi * k * lag / n
    live = (np.arange(CF_TAPS_PAD) < CF_KERNEL)[None, :]
    m = np.concatenate([np.cos(ang) * live, -np.sin(ang) * live], axis=0)
    return jnp.asarray(m.astype(np.float32)).astype(BF16)


def _cf_spec_kernel(e_ref, w_ref, h_ref):
    h_ref[...] = _dot(e_ref[...], w_ref[...].astype(BF16)).reshape(h_ref.shape).astype(h_ref.dtype)


def _cf_spec(dw, n_lat):
    c = dw.shape[1]
    n2 = n_lat // FFT_N1H
    e = _cf_spec_matrix(n_lat)
    wp = jnp.pad(dw, ((0, CF_TAPS_PAD - CF_KERNEL), (0, 0)))
    tc = HY_SLAB
    return pl.pallas_call(
        _cf_spec_kernel,
        out_shape=jax.ShapeDtypeStruct((1, 2, FFT_K1, n2, c), BF16),
        grid=(c // tc,),
        in_specs=[pl.BlockSpec(e.shape, lambda ci: (0, 0)), pl.BlockSpec((CF_TAPS_PAD, tc), lambda ci: (0, ci))],
        out_specs=pl.BlockSpec((None, 2, FFT_K1, n2, tc), lambda ci: (0, 0, 0, 0, ci)),
        compiler_params=_cparams(1), name="cf_spec",
    )(e, wp)


def _cf_tail_kernel(v_ref, sg_ref, x_ref, mod_ref, lg_ref, lb_ref, w_ref, fg_ref, o_ref):
    v = v_ref[...]
    mu = jnp.mean(v, axis=-1, keepdims=True)
    xc = v - mu
    var = jnp.mean(xc * xc, axis=-1, keepdims=True)
    ln = xc * lax.rsqrt(var + NORM_EPS) * lg_ref[...] + lb_ref[...]
    og = (_silu(ln) * sg_ref[...].astype(F32)).astype(BF16)
    xo = x_ref[...] + mod_ref[2:3, :] * _dot(og, w_ref[...])
    o_ref[...] = _rms(xo, fg_ref[...])


def _cf_tail(v, sg, x, mod, lg, lb, w_out, fg):
    b, n_lat, c = v.shape
    d = x.shape[-1]
    full = lambda a: pl.BlockSpec(a.shape, lambda bi, i: (0,) * a.ndim)
    row = lambda bi, i: (bi, i, 0)
    return pl.pallas_call(
        _cf_tail_kernel,
        out_shape=jax.ShapeDtypeStruct((b, n_lat, d), F32),
        grid=(b, n_lat // ROW_TILE),
        in_specs=[pl.BlockSpec((None, ROW_TILE, c), row),
                  pl.BlockSpec((None, ROW_TILE, c), row),
                  pl.BlockSpec((None, ROW_TILE, d), row),
                  pl.BlockSpec((None, None, 3, d), lambda bi, i: (bi, 0, 0, 0)),
                  full(lg), full(lb), full(w_out), full(fg)],
        out_specs=pl.BlockSpec((None, ROW_TILE, d), row),
        compiler_params=_cparams(2), name="cf_tail",
    )(v, sg, x, mod, lg, lb, w_out, fg)


def _rope_tables(n_lat, n_ctx):
    rows = n_lat // GRID_W
    row = jnp.repeat(jnp.arange(rows, dtype=F32), GRID_W)
    col = jnp.tile(jnp.arange(GRID_W, dtype=F32), rows)
    n_freq = ROPE_DIM // 4
    inv = ROPE_BASE ** (-jnp.arange(n_freq, dtype=F32) / n_freq)
    ang = jnp.concatenate([row[:, None] * inv, col[:, None] * inv], axis=-1)
    cos = jnp.concatenate([jnp.cos(ang), jnp.ones((n_ctx, ROPE_DIM // 2), F32)], axis=0)
    sin = jnp.concatenate([jnp.sin(ang), jnp.zeros((n_ctx, ROPE_DIM // 2), F32)], axis=0)
    return jnp.tile(cos, (1, 2)), jnp.tile(sin, (1, 2))


def kernel(x, c, ctx, c_ctx, norm_g, ada_w, ada_b, final_g, mla_w_in, mla_q_norm_g, mla_kv_norm_g, mla_w_uq, mla_w_ukv, mla_w_out, hy_w_in, hy_conv_w, hy_conv_b, hy_filt_w_in, hy_filt_w_hid, hy_filt_b, hy_filt_freq, hy_filt_w_out, hy_bias, hy_w_out, swa_w_in, swa_sink, swa_w_out, cf_w_in, cf_dw_w, cf_dw_b, cf_ln_g, cf_ln_b, cf_w_out):
    b, n_lat, d = x.shape
    n_ctx = ctx.shape[1]
    depth = norm_g.shape[0]
    assert depth == 4 and n_lat % ROW_TILE == 0 and n_ctx == ATTN_TILE and ROW_TILE % n_ctx == 0
    assert n_lat % (FFT_N1H * SUBLANES) == 0 and n_lat % GRID_W == 0
    n_lat_tiles = n_lat // ROW_TILE
    bf = lambda a: a.astype(BF16)

    cs = jnp.concatenate([c, c_ctx[None, :], jnp.zeros((SUBLANES - b - 1, d), F32)], axis=0)
    mods = _adaln(cs, ada_w, ada_b)[:, :b + 1].reshape(depth, b + 1, 3, d)
    mods = jnp.stack([mods[:, :b], jnp.broadcast_to(mods[:, b:], (depth, b, 3, d))], axis=2)
    cos64, sin64 = _rope_tables(n_lat, ROW_TILE)
    cos128, sin128 = jnp.tile(cos64, (1, 2)), jnp.tile(sin64, (1, 2))

    c2 = MLA_Q_RANK + MLA_KV_RANK + MLA_ROPE
    w_lat = bf(jnp.pad(mla_w_in[0][:, :c2], ((0, 0), (0, LANES - MLA_ROPE))))
    w_gate = bf(mla_w_in[0][:, c2:])
    wuq = bf(jnp.pad(mla_w_uq[0].reshape(MLA_Q_RANK, MLA_HEADS, MLA_QK), ((0, 0), (0, 0), (0, LANES - MLA_ROPE)))
             .reshape(MLA_Q_RANK, MLA_HEADS * 2 * LANES))
    q, k, v, sg = _mla_proj(x, ctx, norm_g[0:1], mods[0], w_lat, w_gate, mla_q_norm_g[0:1], mla_kv_norm_g[0:1],
                            wuq, bf(mla_w_ukv[0]), cos128, sin128, n_lat_tiles)
    o = _mla_attn(q, k, v, n_lat, n_ctx)

    xs, z0, g0, g1, sg = _hy_proj(o, sg, x, ctx, mods[0], bf(mla_w_out[0]), norm_g[1:2], mods[1], bf(hy_w_in[0]),
                                  hy_conv_w[0], hy_conv_b[0][None, :], n_lat_tiles)
    cch = z0.shape[-1]
    filt = (hy_filt_w_in[0], hy_filt_w_hid[0], hy_filt_b[0], hy_filt_freq[0], hy_filt_w_out[0])
    assert cch % HY_SLAB == 0
    mats = _fft_matrices(n_lat)
    spec = _hy_spec(_hy_filter_taps(n_lat, *filt), mats[1], mats[2])
    cfwd, cinv, nfp = _dense_dft_matrices(n_ctx)
    cspec = _hy_ctx_spec(_hy_filter_taps(n_ctx, *filt), cfwd)
    gates = (g0, g1)
    z = z0
    for order in range(HY_ORDER):
        bias = hy_bias[0][order][None, :]
        z = _long_conv(z, spec, order, mats, bias, n_lat, gate=gates[order])
        z = _hy_ctx_conv(z, gates[order], cspec, order, cfwd[:, :n_ctx], cinv, nfp, bias, n_lat, n_ctx)

    nq = SWA_Q_HEADS * SWA_HEAD_DIM
    nkv = SWA_KV_HEADS * SWA_HEAD_DIM
    w2 = bf(swa_w_in[0])
    xs, q, k, v, sg = _swa_proj(z, sg, xs, mods[1], bf(hy_w_out[0]), norm_g[2:3], mods[2], w2, cos128, sin128,
                                n_lat_tiles)
    o = _swa_attn(swa_sink[0], q, k, v, n_lat, n_ctx)

    xs, u, sg = _cf_proj(o, sg, xs, mods[2], bf(swa_w_out[0]), norm_g[3:4], mods[3], bf(cf_w_in[0]), n_lat_tiles)
    v = _long_conv(u, _cf_spec(cf_dw_w[0], n_lat), 0, mats, cf_dw_b[0][None, :], n_lat)
    return _cf_tail(v, sg, xs, mods[3], cf_ln_g[0][None, :], cf_ln_b[0][None, :], bf(cf_w_out[0]), final_g[None, :])
```

```python
import functools
import math

import numpy as np
import jax
import jax.numpy as jnp
from jax import lax
from jax.experimental import pallas as pl
from jax.experimental.pallas import tpu as pltpu

F32 = jnp.float32
BF16 = jnp.bfloat16

GRID_W = 64
NORM_EPS = 1e-6
NEG_INF = -1e30
LOG2E = math.log2(math.e)
ROPE_BASE = 10000.0
ROPE_DIM = 64

MLA_HEADS = 8
MLA_Q_RANK = 384
MLA_KV_RANK = 256
MLA_NOPE = 128
MLA_ROPE = ROPE_DIM
MLA_V = 128
MLA_QK = MLA_NOPE + MLA_ROPE

HY_ORDER = 2
HY_EMB = 33
HY_EMB_PAD = 64
HY_BANDS = (HY_EMB - 1) // 2
HY_DECAY_TARGET = 1e-2
HY_FAST_DECAY = 0.3
HY_SLOW_DECAY = 1.5

SWA_Q_HEADS = 16
SWA_KV_HEADS = 4
SWA_GROUP = SWA_Q_HEADS // SWA_KV_HEADS
SWA_HEAD_DIM = ROPE_DIM
SWA_WINDOW = 128

CF_KERNEL = 31
CF_TAPS_PAD = 32

ROW_TILE = 512
ATTN_TILE = 256
SUBLANES = 8
LANES = 128
FFT_N1 = 64
FFT_N1H = FFT_N1 // 2
FFT_K1 = FFT_N1 // 2 + 1
HY_SLAB = 256
HY_HALO = 16
VMEM_LIMIT = 48 * 1024 * 1024
VMEM_LIMIT_BIG = 56 * 1024 * 1024


def _cparams(n_axes):
    return pltpu.CompilerParams(dimension_semantics=("arbitrary",) * n_axes,
                                vmem_limit_bytes=VMEM_LIMIT)


def _cparams_big(n_axes):
    return pltpu.CompilerParams(dimension_semantics=("arbitrary",) * n_axes,
                                vmem_limit_bytes=VMEM_LIMIT_BIG)


def _once(shape, index_map):
    return pl.BlockSpec(shape, index_map, pipeline_mode=pl.Buffered(1))


def _dot(a, b):
    return jnp.dot(a, b, preferred_element_type=F32)


def _dot_nt(a, b):
    return lax.dot_general(a, b, (((1,), (1,)), ((), ())), preferred_element_type=F32)


def _rms(x, g):
    return x * lax.rsqrt(jnp.mean(x * x, axis=-1, keepdims=True) + NORM_EPS) * g


def _norm_mod(x, g, mod):
    return _rms(x, g) * (1.0 + mod[1:2, :]) + mod[0:1, :]


def _silu(x):
    return x * jax.nn.sigmoid(x)


def _adaln_kernel(c_ref, w_ref, b_ref, o_ref):
    s = _silu(c_ref[...]).astype(BF16)
    o_ref[...] = _dot(s, w_ref[...].astype(BF16)) + b_ref[...]


def _adaln(cs, ada_w, ada_b):
    depth, d, d3 = ada_w.shape
    tn = 2 * ROW_TILE
    return pl.pallas_call(
        _adaln_kernel,
        out_shape=jax.ShapeDtypeStruct((depth, cs.shape[0], d3), F32),
        grid=(depth, d3 // tn),
        in_specs=[pl.BlockSpec(cs.shape, lambda i, j: (0, 0)),
                  pl.BlockSpec((None, d, tn), lambda i, j: (i, 0, j)),
                  pl.BlockSpec((None, 1, tn), lambda i, j: (i, 0, j))],
        out_specs=pl.BlockSpec((None, cs.shape[0], tn), lambda i, j: (i, 0, j)),
        compiler_params=_cparams(2), name="adaln",
    )(cs, ada_w, ada_b.reshape(depth, 1, d3))


def _input_tile(x_ref, ctx_ref, is_ctx):
    pad = jnp.zeros((x_ref.shape[0] - ctx_ref.shape[0], x_ref.shape[1]), F32)
    return jnp.where(is_ctx, jnp.concatenate([ctx_ref[...], pad], axis=0), x_ref[...])


def _input_specs(d, n_ctx, n_lat_tiles):
    return [pl.BlockSpec((None, ROW_TILE, d), lambda bi, i: (bi, jnp.minimum(i, n_lat_tiles - 1), 0)),
            pl.BlockSpec((None, n_ctx, d), lambda bi, i: (bi, 0, 0))]


def _residual_in(o_ref, sgp_ref, x_ref, modp_ref, wout_ref, xo_ref):
    og = (o_ref[...].astype(F32) * sgp_ref[...].astype(F32)).astype(BF16)
    x = x_ref[...] + modp_ref[2:3, :] * _dot(og, wout_ref[...])
    xo_ref[...] = x
    return x


def _residual_specs(o, d, n_lat_tiles):
    width = o.shape[-1]
    row = lambda bi, i: (bi, i, 0)
    return [pl.BlockSpec((None, ROW_TILE, width), row), pl.BlockSpec((None, ROW_TILE, width), row),
            pl.BlockSpec((None, ROW_TILE, d), row),
            pl.BlockSpec((None, None, 3, d), lambda bi, i: (bi, i // n_lat_tiles, 0, 0)),
            _once((width, d), lambda bi, i: (0, 0))]


def _rope_lanes(x, cos, sin):
    half = ROPE_DIM // 2
    lane = lax.broadcasted_iota(jnp.int32, x.shape, 1) % ROPE_DIM
    rot = jnp.where(lane < half, -pltpu.roll(x, LANES - half, 1), pltpu.roll(x, half, 1))
    return x * cos + rot * sin


def _mla_proj_kernel(x_ref, ctx_ref, g_ref, mod_ref, wl_ref, wg_ref, qg_ref, kvg_ref, wuq_ref, wukv_ref,
                     cos_ref, sin_ref, q_ref, k_ref, v_ref, sg_ref, *, n_lat_tiles):
    x = _input_tile(x_ref, ctx_ref, pl.program_id(1) >= n_lat_tiles)
    h = _norm_mod(x, g_ref[...], mod_ref[...]).astype(BF16)
    res = _dot(h, wl_ref[...])
    half = sg_ref.shape[-1] // 2
    for c0 in (0, half):
        sg_ref[:, c0:c0 + half] = _silu(_dot(h, wg_ref[:, c0:c0 + half])).astype(sg_ref.dtype)
    c0, c1 = MLA_Q_RANK, MLA_Q_RANK + MLA_KV_RANK
    cqn = _rms(res[:, :c0], qg_ref[...]).astype(BF16)
    ckvn = _rms(res[:, c0:c1], kvg_ref[...]).astype(BF16)
    cos, sin = cos_ref[...], sin_ref[...]

    def rope(slab):
        return _rope_lanes(slab, cos, sin)[:, :MLA_ROPE]

    k_rope_t = rope(res[:, c1:c1 + LANES]).T.astype(k_ref.dtype)
    scale = MLA_QK ** -0.5 * LOG2E
    per = 2 * LANES
    for h0 in range(0, MLA_HEADS, 2):
        q2 = _dot(cqn, wuq_ref[:, h0 * per:(h0 + 2) * per])
        kv2 = _dot(ckvn, wukv_ref[:, h0 * per:(h0 + 2) * per])
        for j in range(2):
            hd = h0 + j
            q, kv = q2[:, j * per:(j + 1) * per], kv2[:, j * per:(j + 1) * per]
            q_ref[hd, :, :MLA_NOPE] = (q[:, :MLA_NOPE] * scale).astype(q_ref.dtype)
            q_ref[hd, :, MLA_NOPE:] = (rope(q[:, MLA_NOPE:]) * scale).astype(q_ref.dtype)
            k_ref[hd, :MLA_NOPE, :] = kv[:, :MLA_NOPE].T.astype(k_ref.dtype)
            k_ref[hd, MLA_NOPE:, :] = k_rope_t
            v_ref[hd, :, :MLA_V] = kv[:, MLA_NOPE:].astype(v_ref.dtype)
            v_ref[hd, :, MLA_V:] = jnp.ones((v_ref.shape[1], MLA_V), v_ref.dtype)


def _mla_proj(x, ctx, g, mod, wl, wg, qg, kvg, wuq, wukv, cos, sin, n_lat_tiles):
    b, n_lat, d = x.shape
    nt = n_lat_tiles + 1
    t = nt * ROW_TILE
    width = MLA_HEADS * MLA_V
    full = lambda a: _once(a.shape, lambda bi, i: (0,) * a.ndim)
    head_out = lambda n: pl.BlockSpec((None, MLA_HEADS, ROW_TILE, n), lambda bi, i: (bi, 0, i, 0))
    return pl.pallas_call(
        functools.partial(_mla_proj_kernel, n_lat_tiles=n_lat_tiles),
        out_shape=(jax.ShapeDtypeStruct((b, MLA_HEADS, t, MLA_QK), BF16),
                   jax.ShapeDtypeStruct((b, MLA_HEADS, MLA_QK, t), BF16),
                   jax.ShapeDtypeStruct((b, MLA_HEADS, t, 2 * MLA_V), BF16),
                   jax.ShapeDtypeStruct((b, t, width), BF16)),
        grid=(b, nt),
        in_specs=_input_specs(d, ctx.shape[1], n_lat_tiles) +
                 [full(g),
                  pl.BlockSpec((None, None, 3, d), lambda bi, i: (bi, i // n_lat_tiles, 0, 0)),
                  full(wl), full(wg), full(qg), full(kvg), full(wuq), full(wukv),
                  pl.BlockSpec((ROW_TILE, LANES), lambda bi, i: (i, 0)),
                  pl.BlockSpec((ROW_TILE, LANES), lambda bi, i: (i, 0))],
        out_specs=(head_out(MLA_QK),
                   pl.BlockSpec((None, MLA_HEADS, MLA_QK, ROW_TILE), lambda bi, i: (bi, 0, 0, i)),
                   head_out(2 * MLA_V),
                   pl.BlockSpec((None, ROW_TILE, width), lambda bi, i: (bi, i, 0))),
        compiler_params=_cparams(2), name="mla_proj",
    )(x, ctx, g, mod, wl, wg, qg, kvg, wuq, wukv, cos, sin)


def _attend(q, k_parts, v_parts):
    s = [_dot(q, k) for k in k_parts]
    m = functools.reduce(jnp.maximum, [jnp.max(x, axis=-1, keepdims=True) for x in s])
    ov = functools.reduce(jnp.add, [_dot(jnp.exp2(x - m).astype(BF16), v) for x, v in zip(s, v_parts)])
    n = ov.shape[-1] // 2
    return ov[:, :n] / ov[:, n:]


MLA_HEADS_PER_STEP = 2


def _mla_attn_kernel(q_ref, k_ref, v_ref, o_ref, *, n_lat_tiles, n_lat, n_ctx):
    qi = pl.program_id(2)

    n_keys = n_lat + n_ctx

    @pl.when(qi < n_lat_tiles)
    def _():
        chains = [(hd, r0) for hd in range(MLA_HEADS_PER_STEP) for r0 in range(0, q_ref.shape[1], ATTN_TILE)]
        s = [_dot(q_ref[hd, r0:r0 + ATTN_TILE, :], k_ref[hd, :, 0:n_keys]) for hd, r0 in chains]
        p = [jnp.exp2(x - jnp.max(x, axis=-1, keepdims=True)).astype(BF16) for x in s]
        for (hd, r0), pc in zip(chains, p):
            ov = _dot(pc, v_ref[hd, 0:n_keys, :])
            o = ov[:, :MLA_V] / ov[:, MLA_V:]
            o_ref[r0:r0 + ATTN_TILE, hd * MLA_V:(hd + 1) * MLA_V] = o.astype(o_ref.dtype)

    @pl.when(qi >= n_lat_tiles)
    def _():
        o_ref[n_ctx:, :] = jnp.zeros((o_ref.shape[0] - n_ctx, o_ref.shape[1]), o_ref.dtype)
        for hd in range(MLA_HEADS_PER_STEP):
            k = k_ref[hd, :, n_lat:n_keys]
            v = v_ref[hd, n_lat:n_keys, :]
            o = _attend(q_ref[hd, 0:n_ctx, :], [k], [v])
            o_ref[0:n_ctx, hd * MLA_V:(hd + 1) * MLA_V] = o.astype(o_ref.dtype)


def _mla_attn(q, k, v, n_lat, n_ctx):
    b, hds, t, _ = q.shape
    nt = t // ROW_TILE
    hps = MLA_HEADS_PER_STEP
    kern = functools.partial(_mla_attn_kernel, n_lat_tiles=n_lat // ROW_TILE, n_lat=n_lat, n_ctx=n_ctx)
    return pl.pallas_call(
        kern,
        out_shape=jax.ShapeDtypeStruct((b, t, hds * MLA_V), BF16),
        grid=(b, hds // hps, nt),
        in_specs=[pl.BlockSpec((None, hps, ROW_TILE, MLA_QK), lambda bi, h, i: (bi, h, i, 0)),
                  _once((None, hps, MLA_QK, t), lambda bi, h, i: (bi, h, 0, 0)),
                  _once((None, hps, t, 2 * MLA_V), lambda bi, h, i: (bi, h, 0, 0))],
        out_specs=pl.BlockSpec((None, ROW_TILE, hps * MLA_V), lambda bi, h, i: (bi, i, h)),
        compiler_params=_cparams_big(3), name="mla_attn",
    )(q, k, v)


def _hy_proj_kernel(op_ref, o_ref, on_ref, sp_ref, s_ref, sn_ref, xp_ref, x_ref, ctx_ref, xn_ref, modp_ref, wout_ref,
                    g_ref, mod_ref, w_ref, cw_ref, cb_ref, xo_ref, z_ref, g0_ref, g1_ref, sg_ref,
                    *, n_lat_tiles, n_ctx):
    i = pl.program_id(1)
    tm = x_ref.shape[0]
    c = z_ref.shape[-1]
    n = tm + 2 * SUBLANES
    halo = xp_ref.shape[0]
    cat = lambda a, m, z: jnp.concatenate([a[...], m, z[...]], axis=0)
    og = (cat(op_ref, o_ref[...], on_ref).astype(F32) * cat(sp_ref, s_ref[...], sn_ref).astype(F32)).astype(BF16)
    xe = cat(xp_ref, _input_tile(x_ref, ctx_ref, i >= n_lat_tiles), xn_ref)
    xe = xe + modp_ref[2:3, :] * _dot(og, wout_ref[...])
    xo_ref[...] = xe[halo:halo + tm]
    xa = xe[halo - SUBLANES:halo + tm + SUBLANES]
    ctx_tile = i == n_lat_tiles
    first = jnp.logical_or(i == 0, ctx_tile)
    last = jnp.logical_or(i == n_lat_tiles - 1, ctx_tile)
    rows = lax.broadcasted_iota(jnp.int32, (n, 1), 0)
    end = jnp.where(ctx_tile, n_ctx + SUBLANES, jnp.where(last, tm + SUBLANES, n))
    valid = jnp.logical_and(jnp.logical_or(rows >= SUBLANES, jnp.logical_not(first)), rows < end)
    hf = jnp.where(valid, _norm_mod(xa, g_ref[...], mod_ref[...]), 0.0)
    h = hf.astype(BF16)
    outs = (z_ref, g0_ref, g1_ref)
    for j in range(len(outs)):
        u = _dot(h, w_ref[:, j * c:(j + 1) * c])
        cw = cw_ref[:, j * c:(j + 1) * c]
        cv = cw[0:1, :] * pltpu.roll(u, 1, 0) + cw[1:2, :] * u + cw[2:3, :] * pltpu.roll(u, n - 1, 0)
        outs[j][...] = cv[SUBLANES:SUBLANES + tm] + cb_ref[:, j * c:(j + 1) * c]
    h_mid = hf[SUBLANES:SUBLANES + tm].astype(BF16)
    sg_ref[...] = _silu(_dot(h_mid, w_ref[:, len(outs) * c:])).astype(sg_ref.dtype)


def _hy_proj(o, sgp, x, ctx, modp, w_out, g, mod, w, cw, cb, n_lat_tiles):
    b, t, width = o.shape
    n_lat, d = x.shape[1:]
    n_ctx = ctx.shape[1]
    c = w.shape[1] // (HY_ORDER + 2)
    per = ROW_TILE // HY_HALO
    last = n_lat // HY_HALO - 1
    full = lambda a: _once(a.shape, lambda bi, i: (0,) * a.ndim)
    row = lambda bi, i: (bi, i, 0)
    prev = lambda bi, i: (bi, jnp.maximum(i * per - 1, 0), 0)
    nxt = lambda bi, i: (bi, jnp.minimum((i + 1) * per, last), 0)
    x_main, x_ctx = _input_specs(d, n_ctx, n_lat_tiles)
    kern = functools.partial(_hy_proj_kernel, n_lat_tiles=n_lat_tiles, n_ctx=n_ctx)
    halo = lambda n, m: pl.BlockSpec((None, HY_HALO, n), m)
    return pl.pallas_call(
        kern,
        out_shape=(jax.ShapeDtypeStruct((b, t, d), F32),) + (jax.ShapeDtypeStruct((b, t, c), F32),) * 3 +
                  (jax.ShapeDtypeStruct((b, t, c), BF16),),
        grid=(b, t // ROW_TILE),
        in_specs=[halo(width, prev), pl.BlockSpec((None, ROW_TILE, width), row), halo(width, nxt),
                  halo(width, prev), pl.BlockSpec((None, ROW_TILE, width), row), halo(width, nxt),
                  halo(d, prev), x_main, x_ctx, halo(d, nxt),
                  pl.BlockSpec((None, None, 3, d), lambda bi, i: (bi, i // n_lat_tiles, 0, 0)),
                  full(w_out), full(g),
                  pl.BlockSpec((None, None, 3, d), lambda bi, i: (bi, i // n_lat_tiles, 0, 0)),
                  full(w), full(cw), full(cb)],
        out_specs=(pl.BlockSpec((None, ROW_TILE, d), row),) + (pl.BlockSpec((None, ROW_TILE, c), row),) * 4,
        compiler_params=_cparams(2), name="hy_proj",
    )(o, o, o, sgp, sgp, sgp, x, x, ctx, x, modp, w_out, g, mod, w, cw, cb)


def _hy_filter_kernel(ef_ref, eb_ref, w0_ref, wh_ref, b_ref, fr_ref, wf_ref, wb_ref, dl_ref, o_ref):
    i = pl.program_id(0)
    c = dl_ref.shape[-1]
    half = ef_ref.shape[0]

    def taps(e, wo):
        hdn = e
        ws = (w0_ref[...], wh_ref[0], wh_ref[1])
        for k in range(3):
            hdn = jnp.sin(fr_ref[k:k + 1, :] * (_dot(hdn.astype(BF16), ws[k].astype(BF16)) + b_ref[k:k + 1, :]))
        hh = _dot(hdn.astype(BF16), wo.astype(BF16))
        out = []
        for part in range(2):
            decay = jnp.exp(-e[:, part * HY_EMB_PAD:part * HY_EMB_PAD + 1] * dl_ref[...])
            out.append([hh[:, (part * HY_ORDER + o) * c:(part * HY_ORDER + o + 1) * c] * decay
                        for o in range(HY_ORDER)])
        return out

    hf = taps(ef_ref[...], wf_ref[...])
    hb = taps(eb_ref[...], wb_ref[...])
    rows = lax.broadcasted_iota(jnp.int32, (half, 1), 0)
    lag0 = jnp.logical_and(rows == 0, i == 0)
    for o in range(HY_ORDER):
        o_ref[o, 0, 0:half] = hf[0][o] + jnp.where(lag0, hb[0][o], 0.0)
        o_ref[o, 1, 0:half] = jnp.where(lag0, 0.0, hb[0][o])
        o_ref[o, 0, half:] = hf[1][o]
        o_ref[o, 1, half:] = hb[1][o]


def _hy_position_features(n):
    t = jnp.linspace(0.0, 1.0, n, dtype=F32)[:, None]
    wpos = (2.0 * math.pi / n) * jnp.arange(n, dtype=F32)[:, None]
    bands = jnp.linspace(1e-4, HY_BANDS - 1, HY_BANDS, dtype=F32)[None, :]
    hdn = jnp.concatenate([t, jnp.cos(bands * wpos), -jnp.sin(bands * wpos)], axis=-1)
    return jnp.pad(hdn, ((0, 0), (0, HY_EMB_PAD - HY_EMB)))


def _hy_filter_taps(n, w_in, w_hid, bias, freq, w_out):
    c = w_out.shape[1] // (2 * HY_ORDER)
    ef = _hy_position_features(n)
    rev = (n - jnp.arange(n)) % n
    eb = ef[rev]
    w0 = jnp.pad(w_in, ((0, HY_EMB_PAD - HY_EMB), (0, 0)))
    wo = w_out.reshape(w_out.shape[0], HY_ORDER, 2, c)
    wf = wo[:, :, 0, :].reshape(w_out.shape[0], HY_ORDER * c)
    wb = wo[:, :, 1, :].reshape(w_out.shape[0], HY_ORDER * c)
    deltas = jnp.abs(jnp.linspace(math.log(HY_DECAY_TARGET) / HY_FAST_DECAY,
                                  math.log(HY_DECAY_TARGET) / HY_SLOW_DECAY, c, dtype=F32))[None, :]
    tl = min(n, ROW_TILE)
    half = tl // 2
    pair = lambda e: e.reshape(n // tl, 2, half, HY_EMB_PAD).transpose(0, 2, 1, 3).reshape(n // 2, 2 * HY_EMB_PAD)
    both = lambda w: jnp.kron(jnp.eye(2, dtype=F32), w)
    twice = lambda v: jnp.tile(v, (1, 2))
    w_hid2 = jnp.stack([both(w_hid[0]), both(w_hid[1])])
    full = lambda a: pl.BlockSpec(a.shape, lambda i: (0,) * a.ndim)
    args = (pair(ef), pair(eb), both(w0), w_hid2, twice(bias), twice(freq), both(wf), both(wb), deltas)
    return pl.pallas_call(
        _hy_filter_kernel,
        out_shape=jax.ShapeDtypeStruct((HY_ORDER, 2, n, c), F32),
        grid=(n // tl,),
        in_specs=[pl.BlockSpec((half, 2 * HY_EMB_PAD), lambda i: (i, 0)),
                  pl.BlockSpec((half, 2 * HY_EMB_PAD), lambda i: (i, 0))] + [full(a) for a in args[2:]],
        out_specs=pl.BlockSpec((HY_ORDER, 2, tl, c), lambda i: (0, 0, i, 0)),
        compiler_params=_cparams(1), name="hy_filter",
    )(*args)


def _fft_matrices(n_lat):
    n = 2 * n_lat
    n2 = n_lat // FFT_N1H
    k1 = np.arange(FFT_K1)
    n1 = np.arange(FFT_N1H)
    ang = 2.0 * np.pi * np.outer(k1, n1) / FFT_N1
    eye = np.eye(SUBLANES)
    f1 = np.kron(np.concatenate([np.cos(ang), -np.sin(ang)], axis=0), eye)
    wgt = np.full(FFT_K1, 2.0)
    wgt[0] = wgt[-1] = 1.0
    g1 = np.kron(np.concatenate([wgt * np.cos(ang.T), (-wgt * np.sin(ang.T))[:, 1:-1]], axis=1) / n, eye)
    idx = np.arange(n2)
    f2 = np.zeros((FFT_K1, 2 * n2, 2 * n2))
    f2i = np.zeros((FFT_K1, 2 * n2, 2 * n2))
    for a in range(FFT_K1):
        ph = 2.0 * np.pi * (np.outer(idx, idx) / n2 + (a * idx)[None, :] / n)
        f2[a] = np.block([[np.cos(ph), np.sin(ph)], [-np.sin(ph), np.cos(ph)]])
        f2i[a] = np.block([[np.cos(ph.T), -np.sin(ph.T)], [np.sin(ph.T), np.cos(ph.T)]])
    sign = np.kron(np.tile((-1.0) ** k1, 2), np.ones(SUBLANES))[:, None]
    f1s = np.concatenate([f1, sign * f1], axis=1)
    as_bf16 = lambda m: jnp.asarray(m.astype(np.float32)).astype(BF16)
    return as_bf16(f1), as_bf16(f1s), as_bf16(f2), as_bf16(f2i), as_bf16(g1)


def _dense_dft_matrices(n_ctx):
    n = 2 * n_ctx
    nf = n_ctx + 1
    nfp = -(-nf // SUBLANES) * SUBLANES
    k = np.arange(nf)
    ang = 2.0 * np.pi * np.outer(k, np.arange(n)) / n
    fwd = np.zeros((2 * nfp, n))
    fwd[:nf] = np.cos(ang)
    fwd[nfp:nfp + nf] = -np.sin(ang)
    wgt = np.full(nf, 2.0)
    wgt[0] = wgt[-1] = 1.0
    angi = 2.0 * np.pi * np.outer(np.arange(n_ctx), k) / n
    inv = np.zeros((n_ctx, 2 * nfp))
    inv[:, :nf] = wgt * np.cos(angi) / n
    inv[:, nfp:nfp + nf] = -wgt * np.sin(angi) / n
    as_bf16 = lambda m: jnp.asarray(m.astype(np.float32)).astype(BF16)
    return as_bf16(fwd), as_bf16(inv), nfp


def _hy_spec_kernel(t_ref, f1s_ref, f2_ref, h_ref, a_ref):
    _, n1h, nj, s, tc = t_ref.shape
    n2 = nj * s
    for j in range(nj):
        sig = jnp.concatenate([t_ref[0, :, j].reshape(n1h * s, tc), t_ref[1, :, j].reshape(n1h * s, tc)], axis=0)
        a_ref[:, :, j] = _dot(f1s_ref[...], sig.astype(BF16)).reshape(2, FFT_K1, s, tc)
    for k in range(FFT_K1):
        a = a_ref[:, k].reshape(2 * n2, tc).astype(BF16)
        h_ref[:, k] = _dot(f2_ref[k], a).reshape(2, n2, tc).astype(h_ref.dtype)


def _hy_spec(taps, f1s, f2):
    _, _, n_lat, c = taps.shape
    n2 = n_lat // FFT_N1H
    nj = n2 // SUBLANES
    tc = HY_SLAB
    return pl.pallas_call(
        _hy_spec_kernel,
        out_shape=jax.ShapeDtypeStruct((HY_ORDER, 2, FFT_K1, n2, c), BF16),
        grid=(c // tc, HY_ORDER),
        in_specs=[pl.BlockSpec((None, 2, FFT_N1H, nj, SUBLANES, tc), lambda ci, o: (o, 0, 0, 0, 0, ci)),
                  _once(f1s.shape, lambda ci, o: (0, 0)),
                  _once(f2.shape, lambda ci, o: (0, 0, 0))],
        out_specs=pl.BlockSpec((None, 2, FFT_K1, n2, tc), lambda ci, o: (o, 0, 0, 0, ci)),
        scratch_shapes=[pltpu.VMEM((2, FFT_K1, nj, SUBLANES, tc), F32)],
        compiler_params=_cparams_big(2), name="hy_spec",
    )(taps.reshape(HY_ORDER, 2, FFT_N1H, nj, SUBLANES, c), f1s, f2)


def _conv_stages(z_ref, h_ref, f1_ref, f2_ref, f2i_ref, g1_ref, a_ref, emit):
    n1h, nj, s, tc = z_ref.shape
    n2 = nj * s
    for j in range(nj):
        zz = z_ref[:, j].reshape(n1h * s, tc).astype(BF16)
        a_ref[:, :, j] = _dot(f1_ref[...], zz).reshape(2, FFT_K1, s, tc)
    for k in range(FFT_K1):
        x = _dot(f2_ref[k], a_ref[:, k].reshape(2 * n2, tc).astype(BF16))
        xr, xi = x[:n2], x[n2:]
        hr, hi = h_ref[0, k].astype(F32), h_ref[1, k].astype(F32)
        y = jnp.concatenate([xr * hr - xi * hi, xr * hi + xi * hr], axis=0).astype(BF16)
        a_ref[:, k] = _dot(f2i_ref[k], y).reshape(2, nj, s, tc)
    for j in range(nj):
        bp = a_ref[:, :, j].reshape(2 * FFT_K1 * s, tc)
        bp = jnp.concatenate([bp[:FFT_K1 * s], bp[(FFT_K1 + 1) * s:(2 * FFT_K1 - 1) * s]], axis=0)
        emit(j, _dot(g1_ref[...], bp.astype(BF16)).reshape(n1h, s, tc))


def _hy_conv_kernel(z_ref, gt_ref, bias_ref, h_ref, f1_ref, f2_ref, f2i_ref, g1_ref, o_ref, a_ref):
    def emit(j, y):
        o_ref[:, j] = gt_ref[:, j] * (y + z_ref[:, j] * bias_ref[...])
    _conv_stages(z_ref, h_ref, f1_ref, f2_ref, f2i_ref, g1_ref, a_ref, emit)


def _cf_conv_kernel(z_ref, bias_ref, h_ref, f1_ref, f2_ref, f2i_ref, g1_ref, o_ref, a_ref):
    def emit(j, y):
        o_ref[:, j] = y + bias_ref[...]
    _conv_stages(z_ref, h_ref, f1_ref, f2_ref, f2i_ref, g1_ref, a_ref, emit)


def _long_conv(z, h, order, mats, bias, n_lat, gate=None):
    f1, _, f2, f2i, g1 = mats
    bz, t, c = z.shape
    n2 = n_lat // FFT_N1H
    nj = n2 // SUBLANES
    tc = HY_SLAB
    view = (bz, t // n2, nj, SUBLANES, c)
    slab = pl.BlockSpec((None, FFT_N1H, nj, SUBLANES, tc), lambda ci, bi: (bi, 0, 0, 0, ci))
    data, data_specs = [z.reshape(view)], [slab]
    if gate is not None:
        data, data_specs = data + [gate.reshape(view)], data_specs + [slab]
    out = pl.pallas_call(
        _hy_conv_kernel if gate is not None else _cf_conv_kernel,
        out_shape=jax.ShapeDtypeStruct(view, F32),
        grid=(c // tc, bz),
        in_specs=data_specs +
                 [pl.BlockSpec((1, tc), lambda ci, bi: (0, ci)),
                  _once((None, 2, FFT_K1, n2, tc), lambda ci, bi: (order, 0, 0, 0, ci)),
                  _once(f1.shape, lambda ci, bi: (0, 0)),
                  _once(f2.shape, lambda ci, bi: (0, 0, 0)),
                  _once(f2i.shape, lambda ci, bi: (0, 0, 0)),
                  _once(g1.shape, lambda ci, bi: (0, 0))],
        out_specs=slab,
        input_output_aliases={0: 0},
        scratch_shapes=[pltpu.VMEM((2, FFT_K1, nj, SUBLANES, tc), F32)],
        compiler_params=_cparams_big(2), name="long_conv",
    )(*data, bias, h, f1, f2, f2i, g1)
    return out.reshape(bz, t, c)


def _hy_ctx_spec_kernel(t_ref, f_ref, h_ref):
    sig = jnp.concatenate([t_ref[0], t_ref[1]], axis=0).astype(BF16)
    h_ref[...] = _dot(f_ref[...], sig)


def _hy_ctx_spec(taps, fwd):
    _, _, n_ctx, c = taps.shape
    return pl.pallas_call(
        _hy_ctx_spec_kernel,
        out_shape=jax.ShapeDtypeStruct((HY_ORDER, fwd.shape[0], c), F32),
        grid=(HY_ORDER,),
        in_specs=[pl.BlockSpec((None, 2, n_ctx, c), lambda o: (o, 0, 0, 0)),
                  pl.BlockSpec(fwd.shape, lambda o: (0, 0))],
        out_specs=pl.BlockSpec((None, fwd.shape[0], c), lambda o: (o, 0, 0)),
        compiler_params=_cparams(1), name="hy_ctx_spec",
    )(taps, fwd)


def _hy_ctx_conv_kernel(zin_ref, gt_ref, h_ref, f_ref, fi_ref, bias_ref, o_ref, *, nfp):
    z = zin_ref[...]
    x = _dot(f_ref[...], z.astype(BF16))
    xr, xi = x[:nfp], x[nfp:]
    hr, hi = h_ref[:nfp, :], h_ref[nfp:, :]
    y = jnp.concatenate([xr * hr - xi * hi, xr * hi + xi * hr], axis=0).astype(BF16)
    n_ctx = z.shape[0]
    o_ref[0:n_ctx, :] = gt_ref[...] * (_dot(fi_ref[...], y) + z * bias_ref[...])
    o_ref[n_ctx:, :] = jnp.zeros((o_ref.shape[0] - n_ctx, o_ref.shape[1]), o_ref.dtype)


def _hy_ctx_conv(z, gate, h, order, fwd_half, inv, nfp, bias, n_lat, n_ctx):
    bz, t, c = z.shape
    blk = pl.BlockSpec((None, n_ctx, c), lambda bi: (bi, n_lat // n_ctx, 0))
    kern = functools.partial(_hy_ctx_conv_kernel, nfp=nfp)
    return pl.pallas_call(
        kern,
        out_shape=jax.ShapeDtypeStruct(z.shape, F32),
        grid=(bz,),
        in_specs=[blk, blk,
                  pl.BlockSpec((None, 2 * nfp, c), lambda bi: (order, 0, 0)),
                  pl.BlockSpec(fwd_half.shape, lambda bi: (0, 0)),
                  pl.BlockSpec(inv.shape, lambda bi: (0, 0)),
                  pl.BlockSpec((1, c), lambda bi: (0, 0))],
        out_specs=pl.BlockSpec((None, t - n_lat, c), lambda bi: (bi, n_lat // (t - n_lat), 0)),
        input_output_aliases={0: 0},
        compiler_params=_cparams(1), name="hy_ctx_conv",
    )(z, gate, h, fwd_half, inv, bias)


def _swa_proj_kernel(o_ref, sgp_ref, x_ref, modp_ref, wout_ref, g_ref, mod_ref, w_ref, cos_ref, sin_ref,
                     xo_ref, q_ref, k_ref, v_ref, sg_ref):
    x = _residual_in(o_ref, sgp_ref, x_ref, modp_ref, wout_ref, xo_ref)
    h = _norm_mod(x, g_ref[...], mod_ref[...]).astype(BF16)
    cos, sin = cos_ref[...], sin_ref[...]
    nq = q_ref.shape[-1]
    nk = SWA_KV_HEADS * LANES
    scale = SWA_HEAD_DIM ** -0.5 * LOG2E
    for c0 in range(0, nq, nk):
        res = _dot(h, w_ref[:, c0:c0 + nk])
        for cb in range(nk // LANES):
            sl = slice(cb * LANES, (cb + 1) * LANES)
            q_ref[:, c0 + cb * LANES:c0 + (cb + 1) * LANES] = (_rope_lanes(res[:, sl], cos, sin) * scale).astype(q_ref.dtype)
    nkv = SWA_KV_HEADS * SWA_HEAD_DIM
    res = _dot(h, w_ref[:, nq:nq + 2 * nkv])
    low = lax.broadcasted_iota(jnp.int32, (res.shape[0], LANES), 1) < SWA_HEAD_DIM
    for slab in range(nkv // LANES):
        kk = _rope_lanes(res[:, slab * LANES:(slab + 1) * LANES], cos, sin)
        vv = res[:, nkv + slab * LANES:nkv + (slab + 1) * LANES]
        kr, vr = pltpu.roll(kk, SWA_HEAD_DIM, 1), pltpu.roll(vv, SWA_HEAD_DIM, 1)
        for half in range(2):
            hd = 2 * slab + half
            pick = low if half == 0 else jnp.logical_not(low)
            k_ref[hd] = jnp.where(pick, kk, kr).astype(k_ref.dtype)
            v_ref[hd, :, :LANES] = jnp.where(pick, vv, vr).astype(v_ref.dtype)
            v_ref[hd, :, LANES:] = jnp.ones((v_ref.shape[1], LANES), v_ref.dtype)
    for c0 in range(0, nq, nk):
        sg_ref[:, c0:c0 + nk] = _silu(_dot(h, w_ref[:, nq + 2 * nkv + c0:nq + 2 * nkv + c0 + nk])).astype(sg_ref.dtype)


def _swa_proj(o, sgp, x, modp, w_out, g, mod, w, cos, sin, n_lat_tiles):
    b, t, d = x.shape
    nt = t // ROW_TILE
    nq = SWA_Q_HEADS * SWA_HEAD_DIM
    full = lambda a: _once(a.shape, lambda bi, i: (0,) * a.ndim)
    row = lambda bi, i: (bi, i, 0)
    kv = lambda n: pl.BlockSpec((None, SWA_KV_HEADS, ROW_TILE, n), lambda bi, i: (bi, 0, i, 0))
    return pl.pallas_call(
        _swa_proj_kernel,
        out_shape=(jax.ShapeDtypeStruct((b, t, d), F32),
                   jax.ShapeDtypeStruct((b, t, nq), BF16),
                   jax.ShapeDtypeStruct((b, SWA_KV_HEADS, t, LANES), BF16),
                   jax.ShapeDtypeStruct((b, SWA_KV_HEADS, t, 2 * LANES), BF16),
                   jax.ShapeDtypeStruct((b, t, nq), BF16)),
        grid=(b, nt),
        in_specs=_residual_specs(o, d, n_lat_tiles) +
                 [full(g),
                  pl.BlockSpec((None, None, 3, d), lambda bi, i: (bi, i // n_lat_tiles, 0, 0)),
                  full(w),
                  pl.BlockSpec((ROW_TILE, LANES), lambda bi, i: (i, 0)),
                  pl.BlockSpec((ROW_TILE, LANES), lambda bi, i: (i, 0))],
        out_specs=(pl.BlockSpec((None, ROW_TILE, d), row), pl.BlockSpec((None, ROW_TILE, nq), row),
                   kv(LANES), kv(2 * LANES), pl.BlockSpec((None, ROW_TILE, nq), row)),
        compiler_params=_cparams(2), name="swa_proj",
    )(o, sgp, x, modp, w_out, g, mod, w, cos, sin)


SWA_GROUPS_PER_STEP = 4


def _swa_attn_kernel(sink_ref, q_ref, k_ref, v_ref, o_ref, *, n_lat_tiles, n_lat, n_ctx):
    g = pl.program_id(1)
    qi = pl.program_id(2)
    tq = SWA_WINDOW
    span = tq + 2 * SWA_WINDOW
    cols = q_ref.shape[1] // LANES
    lane_half = lax.broadcasted_iota(jnp.int32, (tq, LANES), 1) // SWA_HEAD_DIM
    first = lax.broadcasted_iota(jnp.int32, (2 * tq, 1), 0) < tq
    kv_of = lambda col: col // (SWA_GROUP // 2)

    def pair_query(r0, col):
        qcol = q_ref[r0:r0 + tq, col * LANES:(col + 1) * LANES]
        zero = jnp.zeros_like(qcol)
        return jnp.concatenate([jnp.where(lane_half == 0, qcol, zero), jnp.where(lane_half == 1, qcol, zero)], axis=0)

    def pair_sink(col):
        base = g * (2 * cols) + 2 * col
        return jnp.where(first, sink_ref[base], sink_ref[base + 1]) * LOG2E

    def finish(r0, col, ov, m, sink):
        o = ov[:, :LANES] / (ov[:, LANES:] + jnp.exp2(sink - m))
        o_ref[r0:r0 + tq, col * LANES:(col + 1) * LANES] = jnp.where(lane_half == 0, o[:tq], o[tq:]).astype(o_ref.dtype)

    @pl.when(qi < n_lat_tiles)
    def _():
        chains = []
        for r0 in range(0, q_ref.shape[0], tq):
            q0 = qi * q_ref.shape[0] + r0
            start = pl.multiple_of(jnp.clip(q0 - SWA_WINDOW, 0, n_lat - span), SWA_WINDOW)
            qpos = q0 + lax.broadcasted_iota(jnp.int32, (tq, 1), 0)
            kpos = start + lax.broadcasted_iota(jnp.int32, (1, span), 1)
            band = jnp.abs(kpos - qpos) <= SWA_WINDOW
            for col in range(cols):
                kv = kv_of(col)
                q2, sink = pair_query(r0, col), pair_sink(col)
                sw = _dot_nt(q2, k_ref[kv, pl.ds(start, span), :])
                sw = jnp.concatenate([jnp.where(band, sw[:tq], NEG_INF), jnp.where(band, sw[tq:], NEG_INF)], axis=0)
                sc = _dot_nt(q2, k_ref[kv, pl.ds(n_lat, n_ctx), :])
                m = jnp.maximum(jnp.maximum(jnp.max(sw, axis=-1, keepdims=True),
                                            jnp.max(sc, axis=-1, keepdims=True)), sink)
                chains.append((r0, col, start, jnp.exp2(sw - m).astype(BF16), jnp.exp2(sc - m).astype(BF16), m, sink))
        for r0, col, start, pw, pc, m, sink in chains:
            kv = kv_of(col)
            ov = _dot(pw, v_ref[kv, pl.ds(start, span), :]) + _dot(pc, v_ref[kv, pl.ds(n_lat, n_ctx), :])
            finish(r0, col, ov, m, sink)

    @pl.when(qi >= n_lat_tiles)
    def _():
        o_ref[n_ctx:, :] = jnp.zeros((o_ref.shape[0] - n_ctx, o_ref.shape[1]), o_ref.dtype)
        for r0 in range(0, n_ctx, tq):
            for col in range(cols):
                kv = kv_of(col)
                q2, sink = pair_query(r0, col), pair_sink(col)
                sc = _dot_nt(q2, k_ref[kv, pl.ds(n_lat, n_ctx), :])
                m = jnp.maximum(jnp.max(sc, axis=-1, keepdims=True), sink)
                finish(r0, col, _dot(jnp.exp2(sc - m).astype(BF16), v_ref[kv, pl.ds(n_lat, n_ctx), :]), m, sink)


def _swa_attn(sink, q, k, v, n_lat, n_ctx):
    b, t, nq = q.shape
    nt = t // ROW_TILE
    gps = SWA_GROUPS_PER_STEP
    gw = gps * SWA_GROUP * SWA_HEAD_DIM
    kern = functools.partial(_swa_attn_kernel, n_lat_tiles=n_lat // ROW_TILE, n_lat=n_lat, n_ctx=n_ctx)
    kv = lambda n: _once((None, gps, t, n), lambda bi, gi, i: (bi, gi, 0, 0))
    return pl.pallas_call(
        kern,
        out_shape=jax.ShapeDtypeStruct((b, t, nq), BF16),
        grid=(b, SWA_KV_HEADS // gps, nt),
        in_specs=[pl.BlockSpec(memory_space=pltpu.SMEM),
                  pl.BlockSpec((None, ROW_TILE, gw), lambda bi, gi, i: (bi, i, gi)), kv(LANES), kv(2 * LANES)],
        out_specs=pl.BlockSpec((None, ROW_TILE, gw), lambda bi, gi, i: (bi, i, gi)),
        compiler_params=_cparams(3), name="swa_attn",
    )(sink, q, k, v)


def _cf_proj_kernel(o_ref, sgp_ref, x_ref, modp_ref, wout_ref, g_ref, mod_ref, w_ref, xo_ref, u_ref, sg_ref,
                    *, n_lat_tiles):
    x = _residual_in(o_ref, sgp_ref, x_ref, modp_ref, wout_ref, xo_ref)

    @pl.when(pl.program_id(1) < n_lat_tiles)
    def _():
        h = _norm_mod(x, g_ref[...], mod_ref[...]).astype(BF16)
        c = u_ref.shape[-1]
        tn = c // 2
        for c0 in range(0, c, tn):
            a = _dot(h, w_ref[:, c0:c0 + tn])
            gl = _dot(h, w_ref[:, c + c0:c + c0 + tn])
            u_ref[:, c0:c0 + tn] = a * jax.nn.sigmoid(gl)
            sg_ref[:, c0:c0 + tn] = _silu(_dot(h, w_ref[:, 2 * c + c0:2 * c + c0 + tn])).astype(sg_ref.dtype)


def _cf_proj(o, sgp, x, modp, w_out, g, mod, w, n_lat_tiles):
    b, t, d = x.shape
    c = w.shape[1] // 3
    n_lat = n_lat_tiles * ROW_TILE
    full = lambda a: _once(a.shape, lambda bi, i: (0,) * a.ndim)
    row = lambda bi, i: (bi, i, 0)
    lat_row = lambda bi, i: (bi, jnp.minimum(i, n_lat_tiles - 1), 0)
    return pl.pallas_call(
        functools.partial(_cf_proj_kernel, n_lat_tiles=n_lat_tiles),
        out_shape=(jax.ShapeDtypeStruct((b, t, d), F32),
                   jax.ShapeDtypeStruct((b, n_lat, c), F32), jax.ShapeDtypeStruct((b, n_lat, c), BF16)),
        grid=(b, t // ROW_TILE),
        in_specs=_residual_specs(o, d, n_lat_tiles) +
                 [full(g), pl.BlockSpec((None, None, 3, d), lambda bi, i: (bi, 0, 0, 0)), full(w)],
        out_specs=(pl.BlockSpec((None, ROW_TILE, d), row),
                   pl.BlockSpec((None, ROW_TILE, c), lat_row), pl.BlockSpec((None, ROW_TILE, c), lat_row)),
        compiler_params=_cparams(2), name="cf_proj",
    )(o, sgp, x, modp, w_out, g, mod, w)


def _cf_spec_matrix(n_lat):
    n = 2 * n_lat
    n2 = n_lat // FFT_N1H
    k = (np.arange(FFT_K1)[:, None] + FFT_N1 * np.arange(n2)[None, :]).reshape(-1, 1)
    lag = ((CF_KERNEL - 1) // 2 - np.arange(CF_TAPS_PAD))[None, :]
    ang = 2.0 * np.pi * k * lag / n
    live = (np.arange(CF_TAPS_PAD) < CF_KERNEL)[None, :]
    m = np.concatenate([np.cos(ang) * live, -np.sin(ang) * live], axis=0)
    return jnp.asarray(m.astype(np.float32)).astype(BF16)


def _cf_spec_kernel(e_ref, w_ref, h_ref):
    h_ref[...] = _dot(e_ref[...], w_ref[...].astype(BF16)).reshape(h_ref.shape).astype(h_ref.dtype)


def _cf_spec(dw, n_lat):
    c = dw.shape[1]
    n2 = n_lat // FFT_N1H
    e = _cf_spec_matrix(n_lat)
    wp = jnp.pad(dw, ((0, CF_TAPS_PAD - CF_KERNEL), (0, 0)))
    tc = HY_SLAB
    return pl.pallas_call(
        _cf_spec_kernel,
        out_shape=jax.ShapeDtypeStruct((1, 2, FFT_K1, n2, c), BF16),
        grid=(c // tc,),
        in_specs=[pl.BlockSpec(e.shape, lambda ci: (0, 0)), pl.BlockSpec((CF_TAPS_PAD, tc), lambda ci: (0, ci))],
        out_specs=pl.BlockSpec((None, 2, FFT_K1, n2, tc), lambda ci: (0, 0, 0, 0, ci)),
        compiler_params=_cparams(1), name="cf_spec",
    )(e, wp)


def _cf_tail_kernel(v_ref, sg_ref, x_ref, mod_ref, lg_ref, lb_ref, w_ref, fg_ref, o_ref):
    v = v_ref[...]
    mu = jnp.mean(v, axis=-1, keepdims=True)
    xc = v - mu
    var = jnp.mean(xc * xc, axis=-1, keepdims=True)
    ln = xc * lax.rsqrt(var + NORM_EPS) * lg_ref[...] + lb_ref[...]
    og = (_silu(ln) * sg_ref[...].astype(F32)).astype(BF16)
    xo = x_ref[...] + mod_ref[2:3, :] * _dot(og, w_ref[...])
    o_ref[...] = _rms(xo, fg_ref[...])


def _cf_tail(v, sg, x, mod, lg, lb, w_out, fg):
    b, n_lat, c = v.shape
    d = x.shape[-1]
    full = lambda a: pl.BlockSpec(a.shape, lambda bi, i: (0,) * a.ndim)
    row = lambda bi, i: (bi, i, 0)
    return pl.pallas_call(
        _cf_tail_kernel,
        out_shape=jax.ShapeDtypeStruct((b, n_lat, d), F32),
        grid=(b, n_lat // ROW_TILE),
        in_specs=[pl.BlockSpec((None, ROW_TILE, c), row),
                  pl.BlockSpec((None, ROW_TILE, c), row),
                  pl.BlockSpec((None, ROW_TILE, d), row),
                  pl.BlockSpec((None, None, 3, d), lambda bi, i: (bi, 0, 0, 0)),
                  full(lg), full(lb), full(w_out), full(fg)],
        out_specs=pl.BlockSpec((None, ROW_TILE, d), row),
        compiler_params=_cparams(2), name="cf_tail",
    )(v, sg, x, mod, lg, lb, w_out, fg)


def _rope_tables(n_lat, n_ctx):
    rows = n_lat // GRID_W
    row = jnp.repeat(jnp.arange(rows, dtype=F32), GRID_W)
    col = jnp.tile(jnp.arange(GRID_W, dtype=F32), rows)
    n_freq = ROPE_DIM // 4
    inv = ROPE_BASE ** (-jnp.arange(n_freq, dtype=F32) / n_freq)
    ang = jnp.concatenate([row[:, None] * inv, col[:, None] * inv], axis=-1)
    cos = jnp.concatenate([jnp.cos(ang), jnp.ones((n_ctx, ROPE_DIM // 2), F32)], axis=0)
    sin = jnp.concatenate([jnp.sin(ang), jnp.zeros((n_ctx, ROPE_DIM // 2), F32)], axis=0)
    return jnp.tile(cos, (1, 2)), jnp.tile(sin, (1, 2))


def kernel(x, c, ctx, c_ctx, norm_g, ada_w, ada_b, final_g, mla_w_in, mla_q_norm_g, mla_kv_norm_g, mla_w_uq, mla_w_ukv, mla_w_out, hy_w_in, hy_conv_w, hy_conv_b, hy_filt_w_in, hy_filt_w_hid, hy_filt_b, hy_filt_freq, hy_filt_w_out, hy_bias, hy_w_out, swa_w_in, swa_sink, swa_w_out, cf_w_in, cf_dw_w, cf_dw_b, cf_ln_g, cf_ln_b, cf_w_out):
    b, n_lat, d = x.shape
    n_ctx = ctx.shape[1]
    depth = norm_g.shape[0]
    assert depth == 4 and n_lat % ROW_TILE == 0 and n_ctx == ATTN_TILE and ROW_TILE % n_ctx == 0
    assert n_lat % (FFT_N1H * SUBLANES) == 0 and n_lat % GRID_W == 0
    n_lat_tiles = n_lat // ROW_TILE
    bf = lambda a: a.astype(BF16)

    cs = jnp.concatenate([c, c_ctx[None, :], jnp.zeros((SUBLANES - b - 1, d), F32)], axis=0)
    mods = _adaln(cs, ada_w, ada_b)[:, :b + 1].reshape(depth, b + 1, 3, d)
    mods = jnp.stack([mods[:, :b], jnp.broadcast_to(mods[:, b:], (depth, b, 3, d))], axis=2)
    cos64, sin64 = _rope_tables(n_lat, ROW_TILE)
    cos128, sin128 = jnp.tile(cos64, (1, 2)), jnp.tile(sin64, (1, 2))

    c2 = MLA_Q_RANK + MLA_KV_RANK + MLA_ROPE
    w_lat = bf(jnp.pad(mla_w_in[0][:, :c2], ((0, 0), (0, LANES - MLA_ROPE))))
    w_gate = bf(mla_w_in[0][:, c2:])
    wuq = bf(jnp.pad(mla_w_uq[0].reshape(MLA_Q_RANK, MLA_HEADS, MLA_QK), ((0, 0), (0, 0), (0, LANES - MLA_ROPE)))
             .reshape(MLA_Q_RANK, MLA_HEADS * 2 * LANES))
    q, k, v, sg = _mla_proj(x, ctx, norm_g[0:1], mods[0], w_lat, w_gate, mla_q_norm_g[0:1], mla_kv_norm_g[0:1],
                            wuq, bf(mla_w_ukv[0]), cos128, sin128, n_lat_tiles)
    o = _mla_attn(q, k, v, n_lat, n_ctx)

    xs, z0, g0, g1, sg = _hy_proj(o, sg, x, ctx, mods[0], bf(mla_w_out[0]), norm_g[1:2], mods[1], bf(hy_w_in[0]),
                                  hy_conv_w[0], hy_conv_b[0][None, :], n_lat_tiles)
    cch = z0.shape[-1]
    filt = (hy_filt_w_in[0], hy_filt_w_hid[0], hy_filt_b[0], hy_filt_freq[0], hy_filt_w_out[0])
    assert cch % HY_SLAB == 0
    mats = _fft_matrices(n_lat)
    spec = _hy_spec(_hy_filter_taps(n_lat, *filt), mats[1], mats[2])
    cfwd, cinv, nfp = _dense_dft_matrices(n_ctx)
    cspec = _hy_ctx_spec(_hy_filter_taps(n_ctx, *filt), cfwd)
    gates = (g0, g1)
    z = z0
    for order in range(HY_ORDER):
        bias = hy_bias[0][order][None, :]
        z = _long_conv(z, spec, order, mats, bias, n_lat, gate=gates[order])
        z = _hy_ctx_conv(z, gates[order], cspec, order, cfwd[:, :n_ctx], cinv, nfp, bias, n_lat, n_ctx)

    nq = SWA_Q_HEADS * SWA_HEAD_DIM
    nkv = SWA_KV_HEADS * SWA_HEAD_DIM
    w2 = bf(swa_w_in[0])
    xs, q, k, v, sg = _swa_proj(z, sg, xs, mods[1], bf(hy_w_out[0]), norm_g[2:3], mods[2], w2, cos128, sin128,
                                n_lat_tiles)
    o = _swa_attn(swa_sink[0], q, k, v, n_lat, n_ctx)

    xs, u, sg = _cf_proj(o, sg, xs, mods[2], bf(swa_w_out[0]), norm_g[3:4], mods[3], bf(cf_w_in[0]), n_lat_tiles)
    v = _long_conv(u, _cf_spec(cf_dw_w[0], n_lat), 0, mats, cf_dw_b[0][None, :], n_lat)
    return _cf_tail(v, sg, xs, mods[3], cf_ln_g[0][None, :], cf_ln_b[0][None, :], bf(cf_w_out[0]), final_g[None, :])
```
